```python
import jax
import jax.numpy as jnp
from jax import lax
import numpy as np

D_MODEL = 2048
BATCH = 4
SEQ = 2048
DEPTH = 4

A_HEADS = 8
A_KV_HEADS = 2
A_HEAD_DIM = 128
IDX_HEADS = 16
IDX_DIM = 64
TOPK_MAX = 256
M_HEADS = 8
M_Q_LORA = 512
M_KV_LORA = 256
M_NOPE = 128
M_ROPE = 64
M_V = 128
D_FF = 5632
CONV_W = 3
ROPE_THETA = 500000.0
ROT_FRACTION = 4
EPS = 1e-6
Q_BLOCK = 128
N_ADA = 6
IN_SIZES = (A_HEADS * A_HEAD_DIM, A_KV_HEADS * A_HEAD_DIM, A_KV_HEADS * A_HEAD_DIM,
            IDX_HEADS * IDX_DIM, IDX_DIM, IDX_HEADS,
            M_Q_LORA, M_KV_LORA, M_ROPE,
            D_MODEL, D_MODEL)
IN_DIM = sum(IN_SIZES)
IDX_W_SCALE = (IDX_HEADS * IDX_DIM) ** -0.5

kernel_name = 'hybrid_dsa_mla_convffn_adaln'


def rms_norm(t, g):
    tf = t.astype(jnp.float32)
    y = tf * lax.rsqrt(jnp.mean(tf * tf, axis=-1, keepdims=True) + EPS)
    return (y * g.astype(jnp.float32)).astype(t.dtype)


def rope_tables(pos, rot_dim):
    inv = ROPE_THETA ** (-jnp.arange(0, rot_dim, 2, dtype=jnp.float32) / rot_dim)
    ang = pos[..., None] * inv
    return jnp.cos(ang)[:, :, None, :], jnp.sin(ang)[:, :, None, :]


def apply_rope(t, cos, sin):
    r = cos.shape[-1] * 2
    tf = t[..., :r].astype(jnp.float32)
    t1, t2 = tf[..., : r // 2], tf[..., r // 2:]
    rot = jnp.concatenate([t1 * cos - t2 * sin, t2 * cos + t1 * sin], axis=-1).astype(t.dtype)
    return jnp.concatenate([rot, t[..., r:]], axis=-1)


def split_cols(t, sizes):
    offsets = np.cumsum(sizes)[:-1]
    return jnp.split(t, [int(o) for o in offsets], axis=-1)


def to_blocks(t, nb):
    b = t.shape[0]
    return jnp.moveaxis(t.reshape((b, nb, Q_BLOCK) + t.shape[2:]), 1, 0)


def from_blocks(t):
    nb, b, tq = t.shape[:3]
    return jnp.moveaxis(t, 0, 1).reshape((b, nb * tq) + t.shape[3:])


def dsa_attention(q, k, v, qi, ki, wi):
    B, S, H, dh = q.shape
    G = k.shape[2]
    n_sel = min(TOPK_MAX, S // 4)
    nb = S // Q_BLOCK
    key_pos = jnp.arange(S)
    scale = dh ** -0.5

    def block(args):
        q_b, qi_b, wi_b, start = args
        q_pos = start + jnp.arange(Q_BLOCK)
        causal = key_pos[None, :] <= q_pos[:, None]
        dots = jnp.einsum('bthd,bsd->bths', qi_b, ki, preferred_element_type=jnp.float32)
        score = jnp.einsum('bths,bth->bts', jax.nn.relu(dots), wi_b.astype(jnp.float32))
        score = jnp.where(causal[None], score, -jnp.inf)
        _, sel = lax.top_k(score, n_sel)
        valid = sel <= q_pos[None, :, None]
        k_sel = jax.vmap(lambda kk, ii: kk[ii])(k, sel)
        v_sel = jax.vmap(lambda vv, ii: vv[ii])(v, sel)
        qg = q_b.reshape(B, Q_BLOCK, G, H // G, dh)
        logits = jnp.einsum('btgrd,btkgd->btgrk', qg, k_sel,
                            preferred_element_type=jnp.float32) * scale
        logits = jnp.where(valid[:, :, None, None, :], logits, -jnp.inf)
        p = jax.nn.softmax(logits, axis=-1).astype(v.dtype)
        o = jnp.einsum('btgrk,btkgd->btgrd', p, v_sel)
        return o.reshape(B, Q_BLOCK, H * dh)

    starts = jnp.arange(nb) * Q_BLOCK
    out = lax.map(block, (to_blocks(q, nb), to_blocks(qi, nb), to_blocks(wi, nb), starts))
    return from_blocks(out)


def causal_dense_attention(q, k, v):
    B, S, H, dqk = q.shape
    dv = v.shape[-1]
    nb = S // Q_BLOCK
    key_pos = jnp.arange(S)
    scale = dqk ** -0.5

    def block(args):
        q_b, start = args
        q_pos = start + jnp.arange(Q_BLOCK)
        causal = key_pos[None, :] <= q_pos[:, None]
        logits = jnp.einsum('bthd,bshd->bhts', q_b, k, preferred_element_type=jnp.float32) * scale
        logits = jnp.where(causal[None, None], logits, -jnp.inf)
        p = jax.nn.softmax(logits, axis=-1).astype(v.dtype)
        return jnp.einsum('bhts,bshd->bthd', p, v).reshape(B, Q_BLOCK, H * dv)

    starts = jnp.arange(nb) * Q_BLOCK
    out = lax.map(block, (to_blocks(q, nb), starts))
    return from_blocks(out)


def parallel_mixers(h, w_in, g_qa, g_ka, g_mq_lat, w_mq_up, g_mkv_lat, w_mkv_up,
                    g_qm, g_km, w_pa, w_pb, w_o, rope_a, rope_i, rope_m):
    B, S, _ = h.shape
    proj = h @ w_in
    (qa, ka, va, qi, ki, wi, mq_lat, mkv_lat, mk_rope, gate_a, gate_b) = split_cols(proj, IN_SIZES)

    qa = apply_rope(rms_norm(qa.reshape(B, S, A_HEADS, A_HEAD_DIM), g_qa), *rope_a)
    ka = apply_rope(rms_norm(ka.reshape(B, S, A_KV_HEADS, A_HEAD_DIM), g_ka), *rope_a)
    va = va.reshape(B, S, A_KV_HEADS, A_HEAD_DIM)
    qi = apply_rope(qi.reshape(B, S, IDX_HEADS, IDX_DIM), *rope_i)
    ki = apply_rope(ki[:, :, None, :], *rope_i)[:, :, 0, :]
    o_a = dsa_attention(qa, ka, va, qi, ki, wi * IDX_W_SCALE)

    mq = (rms_norm(mq_lat, g_mq_lat) @ w_mq_up).reshape(B, S, M_HEADS, M_NOPE + M_ROPE)
    mkv = (rms_norm(mkv_lat, g_mkv_lat) @ w_mkv_up).reshape(B, S, M_HEADS, M_NOPE + M_V)
    mk_nope, mv = mkv[..., :M_NOPE], mkv[..., M_NOPE:]
    mk_r = jnp.broadcast_to(mk_rope[:, :, None, :], (B, S, M_HEADS, M_ROPE))
    mk = rms_norm(jnp.concatenate([mk_nope, mk_r], axis=-1), g_km)
    mq = rms_norm(mq, g_qm)
    mq = jnp.concatenate([mq[..., :M_NOPE], apply_rope(mq[..., M_NOPE:], *rope_m)], axis=-1)
    mk = jnp.concatenate([mk[..., :M_NOPE], apply_rope(mk[..., M_NOPE:], *rope_m)], axis=-1)
    o_b = causal_dense_attention(mq, mk, mv)

    merged = jax.nn.sigmoid(gate_a) * (o_a @ w_pa) + jax.nn.sigmoid(gate_b) * (o_b @ w_pb)
    return merged @ w_o


def conv_ffn(h, w_up, w_conv, b_conv, w_down):
    S = h.shape[1]
    u = h @ w_up
    up = jnp.pad(u, ((0, 0), (CONV_W - 1, 0), (0, 0)))
    conv = b_conv
    for j in range(CONV_W):
        conv = conv + w_conv[j] * up[:, j:j + S]
    gate, val = jnp.split(conv, 2, axis=-1)
    return (jax.nn.silu(gate) * val) @ w_down


def setup_inputs(seed: int = 0) -> dict:
    key = jax.random.key(seed)
    ks = jax.random.split(key, 24)
    f32 = jnp.float32

    def nrm(k, shape, scale):
        return jax.random.normal(k, shape, f32) * scale

    def gain(k, shape):
        return 1.0 + 0.02 * jax.random.normal(k, shape, f32)

    L, D = DEPTH, D_MODEL
    x = nrm(ks[0], (BATCH, SEQ, D), 1.0)
    c = nrm(ks[1], (BATCH, D), 1.0)
    offset = jax.random.randint(ks[2], (BATCH, 1), 0, 1024, dtype=jnp.int32)
    positions = (jnp.arange(SEQ, dtype=jnp.int32)[None, :] + offset).astype(jnp.int32)
    return {
        'x': x,
        'c': c,
        'positions': positions,
        'g_attn': gain(ks[3], (L, D)),
        'g_ffn': gain(ks[4], (L, D)),
        'w_ada': nrm(ks[5], (L, D, N_ADA * D), 0.5 * D ** -0.5),
        'b_ada': nrm(ks[6], (L, N_ADA * D), 0.01),
        'w_in': nrm(ks[7], (L, D, IN_DIM), D ** -0.5),
        'g_qa': gain(ks[8], (L, A_HEAD_DIM)),
        'g_ka': gain(ks[9], (L, A_HEAD_DIM)),
        'g_mq_lat': gain(ks[10], (L, M_Q_LORA)),
        'w_mq_up': nrm(ks[11], (L, M_Q_LORA, M_HEADS * (M_NOPE + M_ROPE)), M_Q_LORA ** -0.5),
        'g_mkv_lat': gain(ks[12], (L, M_KV_LORA)),
        'w_mkv_up': nrm(ks[13], (L, M_KV_LORA, M_HEADS * (M_NOPE + M_V)), M_KV_LORA ** -0.5),
        'g_qm': gain(ks[14], (L, M_NOPE + M_ROPE)),
        'g_km': gain(ks[15], (L, M_NOPE + M_ROPE)),
        'w_pa': nrm(ks[16], (L, A_HEADS * A_HEAD_DIM, D), (A_HEADS * A_HEAD_DIM) ** -0.5),
        'w_pb': nrm(ks[17], (L, M_HEADS * M_V, D), (M_HEADS * M_V) ** -0.5),
        'w_o': nrm(ks[18], (L, D, D), D ** -0.5),
        'w_up': nrm(ks[19], (L, D, 2 * D_FF), D ** -0.5),
        'w_conv': nrm(ks[20], (L, CONV_W, 2 * D_FF), CONV_W ** -0.5),
        'b_conv': nrm(ks[21], (L, 2 * D_FF), 0.01),
        'w_down': nrm(ks[22], (L, D_FF, D), D_FF ** -0.5),
    }


def reference(x, c, positions, g_attn, g_ffn, w_ada, b_ada, w_in, g_qa, g_ka,
              g_mq_lat, w_mq_up, g_mkv_lat, w_mkv_up, g_qm, g_km, w_pa, w_pb, w_o,
              w_up, w_conv, b_conv, w_down):
    pos = positions.astype(jnp.float32)
    rope_a = rope_tables(pos, A_HEAD_DIM // ROT_FRACTION)
    rope_i = rope_tables(pos, IDX_DIM // ROT_FRACTION)
    rope_m = rope_tables(pos, M_ROPE)
    c_act = jax.nn.silu(c)
    for l in range(DEPTH):
        mod = (c_act @ w_ada[l] + b_ada[l])[:, None, :]
        sh_a, sc_a, gt_a, sh_f, sc_f, gt_f = jnp.split(mod, N_ADA, axis=-1)
        h = rms_norm(x, g_attn[l]) * (1.0 + sc_a) + sh_a
        y = parallel_mixers(h, w_in[l], g_qa[l], g_ka[l], g_mq_lat[l], w_mq_up[l],
                            g_mkv_lat[l], w_mkv_up[l], g_qm[l], g_km[l], w_pa[l], w_pb[l],
                            w_o[l], rope_a, rope_i, rope_m)
        x = x + gt_a * y
        h = rms_norm(x, g_ffn[l]) * (1.0 + sc_f) + sh_f
        x = x + gt_f * conv_ffn(h, w_up[l], w_conv[l], b_conv[l], w_down[l])
    return x
```

```python
import functools

import jax
import jax.numpy as jnp
import numpy as np
from jax import lax
from jax.experimental import pallas as pl
from jax.experimental.pallas import tpu as pltpu

D_MODEL = 2048
BATCH = 4
SEQ = 2048
DEPTH = 4
A_HEADS = 8
A_KV_HEADS = 2
A_HEAD_DIM = 128
IDX_HEADS = 16
IDX_DIM = 64
TOPK_MAX = 256
M_HEADS = 8
M_Q_LORA = 512
M_KV_LORA = 256
M_NOPE = 128
M_ROPE = 64
M_V = 128
D_FF = 5632
CONV_W = 3
ROPE_THETA = 500000.0
ROT_FRACTION = 4
EPS = 1e-6
N_ADA = 6
IDX_W_SCALE = (IDX_HEADS * IDX_DIM) ** -0.5
N_SEL = min(TOPK_MAX, SEQ // 4)
TOKENS = BATCH * SEQ
LANES = 128
M_QK = M_NOPE + M_ROPE
M_QK_PAD = 256

BF16 = jnp.bfloat16
F32 = jnp.float32

COL_GATE_A = 0
COL_GATE_B = 2048
COL_QA = 4096
COL_QI = 5120
COL_MQL = 6144
COL_KV = 6656
COL_MKVL = 7168
COL_KIWI = 7424
COL_MKR = 7552
IN_PACKED = 7680

VMEM_LIMIT = 56 * 1024 * 1024


def _params(*sem):
    return pltpu.CompilerParams(dimension_semantics=sem, vmem_limit_bytes=VMEM_LIMIT)


def _dot(a, b):
    return jnp.dot(a, b, preferred_element_type=F32)


def _dot_nt(a, b):
    return lax.dot_general(a, b, (((1,), (1,)), ((), ())), preferred_element_type=F32)


def _ada_kernel(c_ref, w_ref, b_ref, o_ref):
    c = c_ref[...]
    c_act = (c * jax.nn.sigmoid(c)).astype(BF16)
    o_ref[0] = _dot(c_act, w_ref[0].astype(BF16)) + b_ref[0]


def _ada(c8, w_ada, b_ada):
    tn = 1024
    n = N_ADA * D_MODEL
    return pl.pallas_call(
        _ada_kernel,
        grid=(DEPTH, n // tn),
        in_specs=[
            pl.BlockSpec((8, D_MODEL), lambda l, j: (0, 0)),
            pl.BlockSpec((1, D_MODEL, tn), lambda l, j: (l, 0, j)),
            pl.BlockSpec((1, 1, tn), lambda l, j: (l, 0, j)),
        ],
        out_specs=pl.BlockSpec((1, 8, tn), lambda l, j: (l, 0, j)),
        out_shape=jax.ShapeDtypeStruct((DEPTH, 8, n), F32),
        compiler_params=_params("arbitrary", "arbitrary"),
        name="ada",
    )(c8, w_ada, b_ada.reshape(DEPTH, 1, n))


def _rope_table_kernel(pos_ref, inv_ref, sgn_ref, cos_ref, sin_ref):
    ang = pos_ref[...] * inv_ref[0]
    cos_ref[0] = jnp.cos(ang)
    sin_ref[0] = jnp.sin(ang) * sgn_ref[0]


def _rope_tables(pos_col, inv3, sgn3):
    tm = 512
    return pl.pallas_call(
        _rope_table_kernel,
        grid=(3, TOKENS // tm),
        in_specs=[
            pl.BlockSpec((tm, 1), lambda t, i: (i, 0)),
            pl.BlockSpec((1, 1, LANES), lambda t, i: (t, 0, 0)),
            pl.BlockSpec((1, 1, LANES), lambda t, i: (t, 0, 0)),
        ],
        out_specs=[
            pl.BlockSpec((1, tm, LANES), lambda t, i: (t, i, 0)),
            pl.BlockSpec((1, tm, LANES), lambda t, i: (t, i, 0)),
        ],
        out_shape=[jax.ShapeDtypeStruct((3, TOKENS, LANES), F32)] * 2,
        compiler_params=_params("arbitrary", "arbitrary"),
        name="rope_tables",
    )(pos_col, inv3, sgn3)


def _rope(t, cos, sin_signed, half, lane):
    fwd = pltpu.roll(t, LANES - half, 1)
    bwd = pltpu.roll(t, half, 1)
    swapped = jnp.where((lane % (2 * half)) < half, fwd, bwd)
    return t * cos + swapped * sin_signed


def _rms(t, g, width):
    ms = jnp.sum(t * t, axis=-1, keepdims=True) * (1.0 / width)
    return t * lax.rsqrt(ms + EPS) * g


def _modnorm(x, g, sc, sh):
    ms = jnp.mean(x * x, axis=-1, keepdims=True)
    y = x * lax.rsqrt(ms + EPS) * g
    return y * (1.0 + sc) + sh


def _in_proj_kernel(x_ref, g_ref, sc_ref, sh_ref, w_ref, o_ref, h_ref):
    @pl.when(pl.program_id(1) == 0)
    def _():
        h_ref[...] = _modnorm(x_ref[...], g_ref[...], sc_ref[...], sh_ref[...]).astype(BF16)

    o_ref[...] = _dot(h_ref[...], w_ref[...])


def _in_proj(x, g, mod_l, w, k_scale, k_shift):
    tm, tn = 512, 512
    per_b = SEQ // tm
    n = w.shape[1]
    return pl.pallas_call(
        _in_proj_kernel,
        grid=(TOKENS // tm, n // tn),
        in_specs=[
            pl.BlockSpec((tm, D_MODEL), lambda i, j: (i, 0)),
            pl.BlockSpec((1, D_MODEL), lambda i, j: (0, 0)),
            pl.BlockSpec((None, None, 1, D_MODEL), lambda i, j: (i // per_b, k_scale, 0, 0)),
            pl.BlockSpec((None, None, 1, D_MODEL), lambda i, j: (i // per_b, k_shift, 0, 0)),
            pl.BlockSpec((D_MODEL, tn), lambda i, j: (0, j)),
        ],
        out_specs=pl.BlockSpec((tm, tn), lambda i, j: (i, j)),
        out_shape=jax.ShapeDtypeStruct((TOKENS, n), F32),
        scratch_shapes=[pltpu.VMEM((tm, D_MODEL), BF16)],
        compiler_params=_params("arbitrary", "arbitrary"),
        name="in_proj",
    )(x, g, mod_l, mod_l, w)


def _prep_kernel(qa_ref, qi_ref, mql_ref, kv_ref, mkvl_ref, kiwi_ref,
                 ca_ref, sa_ref, ci_ref, si_ref,
                 gqa_ref, gka_ref, gmq_ref, gmkv_ref,
                 qa_o, ka_o, va_o, qi_o, ki2_o, wi_o, mqn_o, mkvn_o):
    tm = qa_ref.shape[0]
    lane = lax.broadcasted_iota(jnp.int32, (tm, LANES), 1)
    ca, sa = ca_ref[0], sa_ref[0]
    ci, si = ci_ref[0], si_ref[0]
    rot_a = A_HEAD_DIM // ROT_FRACTION // 2
    rot_i = IDX_DIM // ROT_FRACTION // 2
    scale_a = A_HEAD_DIM ** -0.5

    for h in range(A_HEADS):
        t = qa_ref[:, h * LANES:(h + 1) * LANES]
        r = _rope(_rms(t, gqa_ref[...], A_HEAD_DIM), ca, sa, rot_a, lane)
        qa_o[h] = (r * scale_a).astype(BF16)
    for g in range(A_KV_HEADS):
        t = kv_ref[:, g * LANES:(g + 1) * LANES]
        r = _rope(_rms(t, gka_ref[...], A_HEAD_DIM), ca, sa, rot_a, lane)
        ka_o[g] = r.astype(BF16)
        va_o[g] = kv_ref[:, (A_KV_HEADS + g) * LANES:(A_KV_HEADS + g + 1) * LANES].astype(BF16)
    for p in range(IDX_HEADS // 2):
        t = qi_ref[:, p * LANES:(p + 1) * LANES]
        r = _rope(t, ci, si, rot_i, lane)
        qi_o[:, (2 * p) * LANES:(2 * p + 1) * LANES] = jnp.where(lane < IDX_DIM, r, 0.0).astype(BF16)
        qi_o[:, (2 * p + 1) * LANES:(2 * p + 2) * LANES] = jnp.where(lane < IDX_DIM, 0.0, r).astype(BF16)

    kiwi = kiwi_ref[...]
    ki = _rope(kiwi, ci, si, rot_i, lane)
    ki2_o[...] = jnp.where(lane < IDX_DIM, ki, pltpu.roll(ki, IDX_DIM, 1)).astype(BF16)
    wi_o[...] = kiwi * IDX_W_SCALE

    mqn_o[...] = _rms(mql_ref[...], gmq_ref[...], M_Q_LORA).astype(BF16)
    mkvn_o[...] = _rms(mkvl_ref[...], gmkv_ref[...], M_KV_LORA).astype(BF16)


def _prep(proj, tabs_cos, tabs_sin, g_qa, g_ka, g_mq_lat, g_mkv_lat):
    tm = 256
    per_b = SEQ // tm

    def col(width, start):
        idx = start // width
        return pl.BlockSpec((tm, width), lambda i: (i, idx))

    def tab(t):
        return pl.BlockSpec((1, tm, LANES), lambda i: (t, i, 0))

    def vec(width):
        return pl.BlockSpec((1, width), lambda i: (0, 0))

    def heads(n):
        return pl.BlockSpec((None, n, tm, LANES), lambda i: (i // per_b, 0, i % per_b, 0))

    def seq(width):
        return pl.BlockSpec((None, tm, width), lambda i: (i // per_b, i % per_b, 0))

    def tok(width):
        return pl.BlockSpec((tm, width), lambda i: (i, 0))

    return pl.pallas_call(
        _prep_kernel,
        grid=(TOKENS // tm,),
        in_specs=[
            col(1024, COL_QA), col(1024, COL_QI), col(512, COL_MQL), col(512, COL_KV),
            col(256, COL_MKVL), col(128, COL_KIWI),
            tab(0), tab(0), tab(1), tab(1),
            vec(A_HEAD_DIM), vec(A_HEAD_DIM), vec(M_Q_LORA), vec(M_KV_LORA),
        ],
        out_specs=[
            heads(A_HEADS), heads(A_KV_HEADS), heads(A_KV_HEADS),
            seq(IDX_HEADS * LANES), seq(LANES), seq(LANES),
            tok(M_Q_LORA), tok(M_KV_LORA),
        ],
        out_shape=[
            jax.ShapeDtypeStruct((BATCH, A_HEADS, SEQ, LANES), BF16),
            jax.ShapeDtypeStruct((BATCH, A_KV_HEADS, SEQ, LANES), BF16),
            jax.ShapeDtypeStruct((BATCH, A_KV_HEADS, SEQ, LANES), BF16),
            jax.ShapeDtypeStruct((BATCH, SEQ, IDX_HEADS * LANES), BF16),
            jax.ShapeDtypeStruct((BATCH, SEQ, LANES), BF16),
            jax.ShapeDtypeStruct((BATCH, SEQ, LANES), F32),
            jax.ShapeDtypeStruct((TOKENS, M_Q_LORA), BF16),
            jax.ShapeDtypeStruct((TOKENS, M_KV_LORA), BF16),
        ],
        compiler_params=_params("arbitrary"),
        name="prep",
    )(proj, proj, proj, proj, proj, proj, tabs_cos, tabs_sin, tabs_cos, tabs_sin,
      g_qa, g_ka, g_mq_lat, g_mkv_lat)


def _mla_up_kernel(mqn_ref, mkvn_ref, mkr_ref, wq_ref, wkv_ref, cm_ref, sm_ref,
                   gqn_ref, gqr_ref, gkn_ref, gkr_ref, q_o, k_o, v_o):
    tm = mqn_ref.shape[0]
    lane = lax.broadcasted_iota(jnp.int32, (tm, LANES), 1)
    low = lane < M_ROPE
    cm, sm = cm_ref[0], sm_ref[0]
    half = M_ROPE // 2
    scale_m = M_QK ** -0.5
    nope_w = M_HEADS * M_NOPE

    q = _dot(mqn_ref[...], wq_ref[...])
    kv = _dot(mkvn_ref[...], wkv_ref[...])

    for p in range(M_HEADS // 2):
        rp = q[:, nope_w + p * LANES: nope_w + (p + 1) * LANES]
        sq = rp * rp
        s_all = jnp.sum(sq, axis=-1, keepdims=True)
        s_lo = jnp.sum(jnp.where(low, sq, 0.0), axis=-1, keepdims=True)
        for e in range(2):
            h = 2 * p + e
            nope = q[:, h * M_NOPE:(h + 1) * M_NOPE]
            ss = jnp.sum(nope * nope, axis=-1, keepdims=True) + (s_lo if e == 0 else s_all - s_lo)
            rs = lax.rsqrt(ss * (1.0 / M_QK) + EPS)
            roped = _rope(rp * rs * gqr_ref[...], cm, sm, half, lane)
            if e == 1:
                roped = pltpu.roll(roped, M_ROPE, 1)
            q_o[h, :, 0:LANES] = (nope * rs * gqn_ref[...] * scale_m).astype(BF16)
            q_o[h, :, LANES:2 * LANES] = (jnp.where(low, roped, 0.0) * scale_m).astype(BF16)

    kr = mkr_ref[...]
    kr_ss = jnp.sum(kr * kr, axis=-1, keepdims=True)
    kr_roped = _rope(kr * gkr_ref[...], cm, sm, half, lane)
    kr_roped = jnp.where(low, kr_roped, 0.0)
    for h in range(M_HEADS):
        nope = kv[:, h * 2 * LANES: h * 2 * LANES + M_NOPE]
        ss = jnp.sum(nope * nope, axis=-1, keepdims=True) + kr_ss
        rs = lax.rsqrt(ss * (1.0 / M_QK) + EPS)
        k_o[h, :, 0:LANES] = (nope * rs * gkn_ref[...]).astype(BF16)
        k_o[h, :, LANES:2 * LANES] = (kr_roped * rs).astype(BF16)
        v_o[h] = kv[:, h * 2 * LANES + M_NOPE:(h + 1) * 2 * LANES].astype(BF16)


def _mla_up(mqn, mkvn, proj, wq, wkv, tabs_cos, tabs_sin, gqn, gqr, gkn, gkr):
    tm = 256
    per_b = SEQ // tm

    def vec():
        return pl.BlockSpec((1, LANES), lambda i: (0, 0))

    def heads(width):
        return pl.BlockSpec((None, M_HEADS, tm, width), lambda i: (i // per_b, 0, i % per_b, 0))

    return pl.pallas_call(
        _mla_up_kernel,
        grid=(TOKENS // tm,),
        in_specs=[
            pl.BlockSpec((tm, M_Q_LORA), lambda i: (i, 0)),
            pl.BlockSpec((tm, M_KV_LORA), lambda i: (i, 0)),
            pl.BlockSpec((tm, LANES), lambda i: (i, COL_MKR // LANES)),
            pl.BlockSpec(wq.shape, lambda i: (0, 0)),
            pl.BlockSpec(wkv.shape, lambda i: (0, 0)),
            pl.BlockSpec((1, tm, LANES), lambda i: (2, i, 0)),
            pl.BlockSpec((1, tm, LANES), lambda i: (2, i, 0)),
            vec(), vec(), vec(), vec(),
        ],
        out_specs=[heads(M_QK_PAD), heads(M_QK_PAD), heads(M_V)],
        out_shape=[
            jax.ShapeDtypeStruct((BATCH, M_HEADS, SEQ, M_QK_PAD), BF16),
            jax.ShapeDtypeStruct((BATCH, M_HEADS, SEQ, M_QK_PAD), BF16),
            jax.ShapeDtypeStruct((BATCH, M_HEADS, SEQ, M_V), BF16),
        ],
        compiler_params=_params("arbitrary"),
        name="mla_up",
    )(mqn, mkvn, proj, wq, wkv, tabs_cos, tabs_sin, gqn, gqr, gkn, gkr)


INT_MIN = -(2 ** 31)


def _dsa_kernel(qa_ref, ka_ref, va_ref, qi_ref, ki2_ref, wi_ref, o_ref, bias_ref):
    tq = qi_ref.shape[0]
    i = pl.program_id(1)
    row = i * tq + lax.broadcasted_iota(jnp.int32, (tq, SEQ), 0)
    colk = lax.broadcasted_iota(jnp.int32, (tq, SEQ), 1)
    causal = colk <= row

    kk = ki2_ref[...]
    w = wi_ref[...]
    score = jnp.zeros((tq, SEQ), F32)
    for h in range(IDX_HEADS):
        d = _dot_nt(qi_ref[:, h * LANES:(h + 1) * LANES], kk)
        score = score + jnp.maximum(d, 0.0) * w[:, IDX_DIM + h:IDX_DIM + h + 1]
    score = jnp.where(causal, score, -jnp.inf)

    bits = pltpu.bitcast(score, jnp.int32)
    key = bits ^ ((bits >> 31) & jnp.int32(0x7FFFFFFF))

    def count_ge(c):
        return jnp.sum(jnp.where(key >= c, 1.0, 0.0), axis=-1, keepdims=True)

    n_sel = float(N_SEL)
    thr = jnp.where(count_ge(jnp.zeros((tq, 1), jnp.int32)) >= n_sel,
                    jnp.int32(0), jnp.int32(INT_MIN))

    def search(b, thr):
        cand = thr | lax.shift_left(jnp.int32(1), jnp.int32(30) - b)
        return jnp.where(count_ge(cand) >= n_sel, cand, thr)

    thr = lax.fori_loop(0, 31, search, thr)
    bias_ref[...] = jnp.where((key >= thr) & causal, 0.0, -jnp.inf)

    rep = A_HEADS // A_KV_HEADS
    for g in range(A_KV_HEADS):
        k = ka_ref[g]
        v = va_ref[g]
        for r in range(rep):
            h = g * rep + r
            logits = _dot_nt(qa_ref[h], k) + bias_ref[...]
            m = jnp.max(logits, axis=-1, keepdims=True)
            p = jnp.exp(logits - m)
            l = jnp.sum(p, axis=-1, keepdims=True)
            o = _dot(p.astype(BF16), v) / l
            o_ref[:, h * LANES:(h + 1) * LANES] = o.astype(BF16)


def _dsa(qa, ka, va, qi, ki2, wi):
    tq = 128
    nq = SEQ // tq
    return pl.pallas_call(
        _dsa_kernel,
        grid=(BATCH, nq),
        in_specs=[
            pl.BlockSpec((None, A_HEADS, tq, LANES), lambda b, i: (b, 0, i, 0)),
            pl.BlockSpec((None, A_KV_HEADS, SEQ, LANES), lambda b, i: (b, 0, 0, 0)),
            pl.BlockSpec((None, A_KV_HEADS, SEQ, LANES), lambda b, i: (b, 0, 0, 0)),
            pl.BlockSpec((None, tq, IDX_HEADS * LANES), lambda b, i: (b, i, 0)),
            pl.BlockSpec((None, SEQ, LANES), lambda b, i: (b, 0, 0)),
            pl.BlockSpec((None, tq, LANES), lambda b, i: (b, i, 0)),
        ],
        out_specs=pl.BlockSpec((tq, A_HEADS * A_HEAD_DIM), lambda b, i: (b * nq + i, 0)),
        out_shape=jax.ShapeDtypeStruct((TOKENS, A_HEADS * A_HEAD_DIM), BF16),
        scratch_shapes=[pltpu.VMEM((tq, SEQ), F32)],
        compiler_params=_params("arbitrary", "arbitrary"),
        name="dsa",
    )(qa, ka, va, qi, ki2, wi)


def _mla_kernel(q_ref, k_ref, v_ref, o_ref):
    tq = q_ref.shape[0]
    tk = tq
    i = pl.program_id(2)
    q = q_ref[...]
    row = i * tq + lax.broadcasted_iota(jnp.int32, (tq, tk), 0)
    col0 = lax.broadcasted_iota(jnp.int32, (tq, tk), 1)

    def body(kb, carry):
        m, l, acc = carry
        start = pl.multiple_of(kb * tk, tk)
        k = k_ref[pl.ds(start, tk), :]
        v = v_ref[pl.ds(start, tk), :]
        s = _dot_nt(q, k)
        s = jnp.where(col0 + kb * tk <= row, s, -jnp.inf)
        m_new = jnp.maximum(m, jnp.max(s, axis=-1, keepdims=True))
        alpha = jnp.exp(m - m_new)
        p = jnp.exp(s - m_new)
        l = alpha * l + jnp.sum(p, axis=-1, keepdims=True)
        acc = alpha * acc + _dot(p.astype(BF16), v)
        return m_new, l, acc

    init = (jnp.full((tq, 1), -jnp.inf, F32), jnp.zeros((tq, 1), F32), jnp.zeros((tq, M_V), F32))
    _, l, acc = lax.fori_loop(0, i + 1, body, init)
    o_ref[...] = (acc / l).astype(BF16)


def _mla(q, k, v):
    tq = 256
    nq = SEQ // tq
    return pl.pallas_call(
        _mla_kernel,
        grid=(BATCH, M_HEADS, nq),
        in_specs=[
            pl.BlockSpec((None, None, tq, M_QK_PAD), lambda b, h, i: (b, h, i, 0)),
            pl.BlockSpec((None, None, SEQ, M_QK_PAD), lambda b, h, i: (b, h, 0, 0)),
            pl.BlockSpec((None, None, SEQ, M_V), lambda b, h, i: (b, h, 0, 0)),
        ],
        out_specs=pl.BlockSpec((tq, M_V), lambda b, h, i: (b * nq + i, h)),
        out_shape=jax.ShapeDtypeStruct((TOKENS, M_HEADS * M_V), BF16),
        compiler_params=_params("arbitrary", "arbitrary", "arbitrary"),
        name="mla",
    )(q, k, v)


def _merge_kernel(oa_ref, ob_ref, ga_ref, gb_ref, wpa_ref, wpb_ref, wo_ref, x_ref, gt_ref, o_ref):
    a = _dot(oa_ref[...], wpa_ref[...])
    b = _dot(ob_ref[...], wpb_ref[...])
    merged = jax.nn.sigmoid(ga_ref[...]) * a + jax.nn.sigmoid(gb_ref[...]) * b
    y = _dot(merged.astype(BF16), wo_ref[...])
    o_ref[...] = x_ref[...] + gt_ref[...] * y


def _merge(oa, ob, proj, wpa, wpb, wo, x, mod_l):
    tm = 256
    per_b = SEQ // tm

    def resident(shape):
        return pl.BlockSpec(shape, lambda i: (0, 0), pipeline_mode=pl.Buffered(1))

    return pl.pallas_call(
        _merge_kernel,
        grid=(TOKENS // tm,),
        in_specs=[
            pl.BlockSpec((tm, A_HEADS * A_HEAD_DIM), lambda i: (i, 0)),
            pl.BlockSpec((tm, M_HEADS * M_V), lambda i: (i, 0)),
            pl.BlockSpec((tm, D_MODEL), lambda i: (i, COL_GATE_A // D_MODEL)),
            pl.BlockSpec((tm, D_MODEL), lambda i: (i, COL_GATE_B // D_MODEL)),
            resident(wpa.shape), resident(wpb.shape), resident(wo.shape),
            pl.BlockSpec((tm, D_MODEL), lambda i: (i, 0)),
            pl.BlockSpec((None, None, 1, D_MODEL), lambda i: (i // per_b, 2, 0, 0)),
        ],
        out_specs=pl.BlockSpec((tm, D_MODEL), lambda i: (i, 0)),
        out_shape=jax.ShapeDtypeStruct((TOKENS, D_MODEL), F32),
        compiler_params=_params("arbitrary"),
        name="merge",
    )(oa, ob, proj, proj, wpa, wpb, wo, x, mod_l)


HALO = 8


def _ffn_up_kernel(x_ref, g_ref, sc_ref, sh_ref, wg_ref, wv_ref, cg_ref, cv_ref, bg_ref, bv_ref,
                   o_ref, h_ref, halo_ref, buf_ref, *, tiles_per_seq):
    i, j = pl.program_id(0), pl.program_id(1)
    tm, tn = o_ref.shape

    @pl.when(j == 0)
    def _():
        h_ref[...] = _modnorm(x_ref[...], g_ref[...], sc_ref[...], sh_ref[...]).astype(BF16)

    seq_start = (i % tiles_per_seq) == 0

    @pl.when(seq_start)
    def _():
        buf_ref[0:HALO, :] = jnp.zeros((HALO, 2 * tn), F32)

    @pl.when(jnp.logical_not(seq_start))
    def _():
        buf_ref[0:HALO, :] = halo_ref[j]

    h = h_ref[...]
    buf_ref[HALO:HALO + tm, 0:tn] = _dot(h, wg_ref[...])
    buf_ref[HALO:HALO + tm, tn:2 * tn] = _dot(h, wv_ref[...])
    halo_ref[j] = buf_ref[tm:tm + HALO, :]

    def conv(lo, c_ref, b_ref):
        u0 = buf_ref[HALO:HALO + tm, lo:lo + tn]
        u1 = buf_ref[HALO - 1:HALO - 1 + tm, lo:lo + tn]
        u2 = buf_ref[HALO - 2:HALO - 2 + tm, lo:lo + tn]
        return b_ref[...] + c_ref[0:1, :] * u2 + c_ref[1:2, :] * u1 + c_ref[2:3, :] * u0

    gate = conv(0, cg_ref, bg_ref)
    val = conv(tn, cv_ref, bv_ref)
    o_ref[...] = (gate * jax.nn.sigmoid(gate) * val).astype(BF16)


def _ffn_up(x, g, mod_l, w_up, w_conv, b_conv):
    tm, tn = 512, 512
    per_b = SEQ // tm
    nj = D_FF // tn
    kern = functools.partial(_ffn_up_kernel, tiles_per_seq=per_b)
    return pl.pallas_call(
        kern,
        grid=(TOKENS // tm, nj),
        in_specs=[
            pl.BlockSpec((tm, D_MODEL), lambda i, j: (i, 0)),
            pl.BlockSpec((1, D_MODEL), lambda i, j: (0, 0)),
            pl.BlockSpec((None, None, 1, D_MODEL), lambda i, j: (i // per_b, 4, 0, 0)),
            pl.BlockSpec((None, None, 1, D_MODEL), lambda i, j: (i // per_b, 3, 0, 0)),
            pl.BlockSpec((D_MODEL, tn), lambda i, j: (0, j)),
            pl.BlockSpec((D_MODEL, tn), lambda i, j: (0, nj + j)),
            pl.BlockSpec((CONV_W, tn), lambda i, j: (0, j)),
            pl.BlockSpec((CONV_W, tn), lambda i, j: (0, nj + j)),
            pl.BlockSpec((1, tn), lambda i, j: (0, j)),
            pl.BlockSpec((1, tn), lambda i, j: (0, nj + j)),
        ],
        out_specs=pl.BlockSpec((tm, tn), lambda i, j: (i, j)),
        out_shape=jax.ShapeDtypeStruct((TOKENS, D_FF), BF16),
        scratch_shapes=[
            pltpu.VMEM((tm, D_MODEL), BF16),
            pltpu.VMEM((nj, HALO, 2 * tn), F32),
            pltpu.VMEM((HALO + tm, 2 * tn), F32),
        ],
        compiler_params=_params("arbitrary", "arbitrary"),
        name="ffn_up",
    )(x, g, mod_l, mod_l, w_up, w_up, w_conv, w_conv, b_conv, b_conv)


def _ffn_down_kernel(a_ref, w_ref, x_ref, gt_ref, o_ref):
    o_ref[...] = x_ref[...] + gt_ref[...] * _dot(a_ref[...], w_ref[...])


def _ffn_down(act, w_down, x, mod_l):
    tm, tn = 512, 512
    per_b = SEQ // tm
    return pl.pallas_call(
        _ffn_down_kernel,
        grid=(TOKENS // tm, D_MODEL // tn),
        in_specs=[
            pl.BlockSpec((tm, D_FF), lambda i, j: (i, 0)),
            pl.BlockSpec((D_FF, tn), lambda i, j: (0, j)),
            pl.BlockSpec((tm, tn), lambda i, j: (i, j)),
            pl.BlockSpec((None, None, 1, tn), lambda i, j: (i // per_b, 5, 0, j)),
        ],
        out_specs=pl.BlockSpec((tm, tn), lambda i, j: (i, j)),
        out_shape=jax.ShapeDtypeStruct((TOKENS, D_MODEL), F32),
        compiler_params=_params("arbitrary", "arbitrary"),
        name="ffn_down",
    )(act, w_down, x, mod_l)


def _pack_w_in(w_in):
    o = np.cumsum((0,) + (1024, 256, 256, 1024, 64, 16, 512, 256, 64, 2048, 2048))
    qa, ka, va, qi, ki, wi, mql, mkvl, mkr, ga, gb = (
        w_in[..., int(o[k]):int(o[k + 1])] for k in range(11))
    z = lambda n: jnp.zeros(w_in.shape[:-1] + (n,), w_in.dtype)
    packed = jnp.concatenate(
        [ga, gb, qa, qi, mql, ka, va, mkvl, ki, wi, z(48), mkr, z(64)], axis=-1)
    assert packed.shape[-1] == IN_PACKED
    return packed.astype(BF16)


def _pack_w_mq(w_mq_up):
    w = w_mq_up.reshape(DEPTH, M_Q_LORA, M_HEADS, M_QK)
    nope = w[..., :M_NOPE].reshape(DEPTH, M_Q_LORA, M_HEADS * M_NOPE)
    rope = w[..., M_NOPE:].reshape(DEPTH, M_Q_LORA, M_HEADS * M_ROPE)
    return jnp.concatenate([nope, rope], axis=-1).astype(BF16)


def _rope_consts():
    def inv(rot):
        return ROPE_THETA ** (-jnp.arange(0, rot, 2, dtype=F32) / rot)

    def lanes(v, group):
        rep = jnp.concatenate([v, v])
        pad = jnp.zeros((group - rep.shape[0],), F32)
        return jnp.tile(jnp.concatenate([rep, pad]), LANES // group)

    def sgn(half, group):
        v = jnp.concatenate([-jnp.ones((half,), F32), jnp.ones((half,), F32)])
        pad = jnp.zeros((group - 2 * half,), F32)
        return jnp.tile(jnp.concatenate([v, pad]), LANES // group)

    rot_a = A_HEAD_DIM // ROT_FRACTION
    rot_i = IDX_DIM // ROT_FRACTION
    inv3 = jnp.stack([lanes(inv(rot_a), LANES), lanes(inv(rot_i), IDX_DIM), lanes(inv(M_ROPE), M_ROPE)])
    sgn3 = jnp.stack([sgn(rot_a // 2, LANES), sgn(rot_i // 2, IDX_DIM), sgn(M_ROPE // 2, M_ROPE)])
    return inv3.reshape(3, 1, LANES), sgn3.reshape(3, 1, LANES)


def kernel(x, c, positions, g_attn, g_ffn, w_ada, b_ada, w_in, g_qa, g_ka, g_mq_lat, w_mq_up,
           g_mkv_lat, w_mkv_up, g_qm, g_km, w_pa, w_pb, w_o, w_up, w_conv, b_conv, w_down):
    pos_col = positions.astype(F32).reshape(TOKENS, 1)
    inv3, sgn3 = _rope_consts()
    tabs_cos, tabs_sin = _rope_tables(pos_col, inv3, sgn3)

    c8 = jnp.pad(c, ((0, 8 - BATCH), (0, 0)))
    mod = _ada(c8, w_ada, b_ada)[:, :BATCH].reshape(DEPTH, BATCH, N_ADA, 1, D_MODEL)

    w_in_p = _pack_w_in(w_in)
    w_mq_p = _pack_w_mq(w_mq_up)
    w_mkv_b = w_mkv_up.astype(BF16)
    w_pa_b, w_pb_b, w_o_b = w_pa.astype(BF16), w_pb.astype(BF16), w_o.astype(BF16)
    w_up_b, w_down_b = w_up.astype(BF16), w_down.astype(BF16)

    xf = x.reshape(TOKENS, D_MODEL)
    for l in range(DEPTH):
        row = lambda v: v[l].reshape(1, -1)
        proj = _in_proj(xf, row(g_attn), mod[l], w_in_p[l], 1, 0)
        qa, ka, va, qi, ki2, wi, mqn, mkvn = _prep(
            proj, tabs_cos, tabs_sin, row(g_qa), row(g_ka), row(g_mq_lat), row(g_mkv_lat))
        gqm, gkm = row(g_qm), row(g_km)
        pad_r = lambda v: jnp.pad(jnp.tile(v[:, M_NOPE:], (1, 2)), ((0, 0), (0, 0)))
        mq, mk, mv = _mla_up(mqn, mkvn, proj, w_mq_p[l], w_mkv_b[l], tabs_cos, tabs_sin,
                             gqm[:, :M_NOPE], pad_r(gqm), gkm[:, :M_NOPE], pad_r(gkm))
        o_a = _dsa(qa, ka, va, qi, ki2, wi)
        o_b = _mla(mq, mk, mv)
        xf = _merge(o_a, o_b, proj, w_pa_b[l], w_pb_b[l], w_o_b[l], xf, mod[l])
        act = _ffn_up(xf, row(g_ffn), mod[l], w_up_b[l], w_conv[l], row(b_conv))
        xf = _ffn_down(act, w_down_b[l], xf, mod[l])
    return xf.reshape(BATCH, SEQ, D_MODEL)
```

```python
import functools

import jax
import jax.numpy as jnp
import numpy as np
from jax import lax
from jax.experimental import pallas as pl
from jax.experimental.pallas import tpu as pltpu

D_MODEL = 2048
BATCH = 4
SEQ = 2048
DEPTH = 4
A_HEADS = 8
A_KV_HEADS = 2
A_HEAD_DIM = 128
IDX_HEADS = 16
IDX_DIM = 64
TOPK_MAX = 256
M_HEADS = 8
M_Q_LORA = 512
M_KV_LORA = 256
M_NOPE = 128
M_ROPE = 64
M_V = 128
D_FF = 5632
CONV_W = 3
ROPE_THETA = 500000.0
ROT_FRACTION = 4
EPS = 1e-6
N_ADA = 6
IDX_W_SCALE = (IDX_HEADS * IDX_DIM) ** -0.5
N_SEL = min(TOPK_MAX, SEQ // 4)
TOKENS = BATCH * SEQ
LANES = 128
M_QK = M_NOPE + M_ROPE
M_QK_PAD = 256

BF16 = jnp.bfloat16
F32 = jnp.float32

COL_GATE_A = 0
COL_GATE_B = 2048
COL_QA = 4096
COL_QI = 5120
COL_MQL = 6144
COL_KV = 6656
COL_MKVL = 7168
COL_KIWI = 7424
COL_MKR = 7552
IN_PACKED = 7680

VMEM_LIMIT = 56 * 1024 * 1024


def _params(*sem):
    return pltpu.CompilerParams(dimension_semantics=sem, vmem_limit_bytes=VMEM_LIMIT)


def _dot(a, b):
    return jnp.dot(a, b, preferred_element_type=F32)


def _dot_nt(a, b):
    return lax.dot_general(a, b, (((1,), (1,)), ((), ())), preferred_element_type=F32)


def _ada_kernel(c_ref, w_ref, b_ref, o_ref):
    c = c_ref[...]
    c_act = (c * jax.nn.sigmoid(c)).astype(BF16)
    o_ref[0] = _dot(c_act, w_ref[0].astype(BF16)) + b_ref[0]


def _ada(c8, w_ada, b_ada):
    tn = 1024
    n = N_ADA * D_MODEL
    return pl.pallas_call(
        _ada_kernel,
        grid=(DEPTH, n // tn),
        in_specs=[
            pl.BlockSpec((8, D_MODEL), lambda l, j: (0, 0)),
            pl.BlockSpec((1, D_MODEL, tn), lambda l, j: (l, 0, j)),
            pl.BlockSpec((1, 1, tn), lambda l, j: (l, 0, j)),
        ],
        out_specs=pl.BlockSpec((1, 8, tn), lambda l, j: (l, 0, j)),
        out_shape=jax.ShapeDtypeStruct((DEPTH, 8, n), F32),
        compiler_params=_params("arbitrary", "arbitrary"),
        name="ada",
    )(c8, w_ada, b_ada.reshape(DEPTH, 1, n))


def _rope_table_kernel(pos_ref, inv_ref, sgn_ref, cos_ref, sin_ref):
    ang = pos_ref[...] * inv_ref[0]
    cos_ref[0] = jnp.cos(ang)
    sin_ref[0] = jnp.sin(ang) * sgn_ref[0]


def _rope_tables(pos_col, inv3, sgn3):
    tm = 512
    return pl.pallas_call(
        _rope_table_kernel,
        grid=(3, TOKENS // tm),
        in_specs=[
            pl.BlockSpec((tm, 1), lambda t, i: (i, 0)),
            pl.BlockSpec((1, 1, LANES), lambda t, i: (t, 0, 0)),
            pl.BlockSpec((1, 1, LANES), lambda t, i: (t, 0, 0)),
        ],
        out_specs=[
            pl.BlockSpec((1, tm, LANES), lambda t, i: (t, i, 0)),
            pl.BlockSpec((1, tm, LANES), lambda t, i: (t, i, 0)),
        ],
        out_shape=[jax.ShapeDtypeStruct((3, TOKENS, LANES), F32)] * 2,
        compiler_params=_params("arbitrary", "arbitrary"),
        name="rope_tables",
    )(pos_col, inv3, sgn3)


def _rope(t, cos, sin_signed, half, lane):
    fwd = pltpu.roll(t, LANES - half, 1)
    bwd = pltpu.roll(t, half, 1)
    swapped = jnp.where((lane % (2 * half)) < half, fwd, bwd)
    return t * cos + swapped * sin_signed


def _rms(t, g, width):
    ms = jnp.sum(t * t, axis=-1, keepdims=True) * (1.0 / width)
    return t * lax.rsqrt(ms + EPS) * g


def _modnorm(x, g, sc, sh):
    ms = jnp.mean(x * x, axis=-1, keepdims=True)
    y = x * lax.rsqrt(ms + EPS) * g
    return y * (1.0 + sc) + sh


def _in_proj_kernel(x_ref, g_ref, sc_ref, sh_ref, w_ref, o_ref, h_ref):
    @pl.when(pl.program_id(1) == 0)
    def _():
        h_ref[...] = _modnorm(x_ref[...], g_ref[...], sc_ref[...], sh_ref[...]).astype(BF16)

    o_ref[...] = _dot(h_ref[...], w_ref[...])


def _mod_spec(l, k, per_b, width=D_MODEL):
    if width == D_MODEL:
        return pl.BlockSpec((None, None, None, 1, D_MODEL), lambda i, *_: (l, i // per_b, k, 0, 0))
    return pl.BlockSpec((None, None, None, 1, width), lambda i, j: (l, i // per_b, k, 0, j))


def _layer_row_spec(l, width):
    return pl.BlockSpec((None, 1, width), lambda *_: (l, 0, 0))


def _in_proj(l, x, g, mod, w):
    tm, tn = 1024, 512
    per_b = SEQ // tm
    n = w.shape[-1]
    return pl.pallas_call(
        _in_proj_kernel,
        grid=(TOKENS // tm, n // tn),
        in_specs=[
            pl.BlockSpec((tm, D_MODEL), lambda i, j: (i, 0)),
            _layer_row_spec(l, D_MODEL),
            _mod_spec(l, 1, per_b),
            _mod_spec(l, 0, per_b),
            pl.BlockSpec((None, D_MODEL, tn), lambda i, j: (l, 0, j)),
        ],
        out_specs=pl.BlockSpec((tm, tn), lambda i, j: (i, j)),
        out_shape=jax.ShapeDtypeStruct((TOKENS, n), F32),
        scratch_shapes=[pltpu.VMEM((tm, D_MODEL), BF16)],
        compiler_params=_params("arbitrary", "arbitrary"),
        name="in_proj",
    )(x, g, mod, mod, w)


def _prep_kernel(qa_ref, qi_ref, mql_ref, kv_ref, mkvl_ref, kiwi_ref,
                 ca_ref, sa_ref, ci_ref, si_ref,
                 gqa_ref, gka_ref, gmq_ref, gmkv_ref,
                 qa_o, ka_o, va_o, qi_o, ki2_o, wi_o, mqn_o, mkvn_o):
    tm = qa_ref.shape[0]
    lane = lax.broadcasted_iota(jnp.int32, (tm, LANES), 1)
    ca, sa = ca_ref[0], sa_ref[0]
    ci, si = ci_ref[0], si_ref[0]
    rot_a = A_HEAD_DIM // ROT_FRACTION // 2
    rot_i = IDX_DIM // ROT_FRACTION // 2
    scale_a = A_HEAD_DIM ** -0.5

    for h in range(A_HEADS):
        t = qa_ref[:, h * LANES:(h + 1) * LANES]
        r = _rope(_rms(t, gqa_ref[...], A_HEAD_DIM), ca, sa, rot_a, lane)
        qa_o[h] = (r * scale_a).astype(BF16)
    for g in range(A_KV_HEADS):
        t = kv_ref[:, g * LANES:(g + 1) * LANES]
        r = _rope(_rms(t, gka_ref[...], A_HEAD_DIM), ca, sa, rot_a, lane)
        ka_o[g] = r.astype(BF16)
        va_o[g] = kv_ref[:, (A_KV_HEADS + g) * LANES:(A_KV_HEADS + g + 1) * LANES].astype(BF16)
    for p in range(IDX_HEADS // 2):
        t = qi_ref[:, p * LANES:(p + 1) * LANES]
        r = _rope(t, ci, si, rot_i, lane)
        qi_o[2 * p] = jnp.where(lane < IDX_DIM, r, 0.0).astype(BF16)
        qi_o[2 * p + 1] = jnp.where(lane < IDX_DIM, 0.0, r).astype(BF16)

    kiwi = kiwi_ref[...]
    ki = _rope(kiwi, ci, si, rot_i, lane)
    ki2_o[...] = jnp.where(lane < IDX_DIM, ki, pltpu.roll(ki, IDX_DIM, 1)).astype(BF16)
    wi_o[...] = kiwi * IDX_W_SCALE

    mqn_o[...] = _rms(mql_ref[...], gmq_ref[...], M_Q_LORA).astype(BF16)
    mkvn_o[...] = _rms(mkvl_ref[...], gmkv_ref[...], M_KV_LORA).astype(BF16)


def _prep(l, proj, tabs_cos, tabs_sin, g_qa, g_ka, g_mq_lat, g_mkv_lat):
    tm = 256
    per_b = SEQ // tm

    def col(width, start):
        idx = start // width
        return pl.BlockSpec((tm, width), lambda i: (i, idx))

    def tab(t):
        return pl.BlockSpec((1, tm, LANES), lambda i: (t, i, 0))

    def vec(width):
        return _layer_row_spec(l, width)

    def heads(n):
        return pl.BlockSpec((None, n, tm, LANES), lambda i: (i // per_b, 0, i % per_b, 0))

    def seq(width):
        return pl.BlockSpec((None, tm, width), lambda i: (i // per_b, i % per_b, 0))

    def tok(width):
        return pl.BlockSpec((tm, width), lambda i: (i, 0))

    return pl.pallas_call(
        _prep_kernel,
        grid=(TOKENS // tm,),
        in_specs=[
            col(1024, COL_QA), col(1024, COL_QI), col(512, COL_MQL), col(512, COL_KV),
            col(256, COL_MKVL), col(128, COL_KIWI),
            tab(0), tab(0), tab(1), tab(1),
            vec(A_HEAD_DIM), vec(A_HEAD_DIM), vec(M_Q_LORA), vec(M_KV_LORA),
        ],
        out_specs=[
            heads(A_HEADS), heads(A_KV_HEADS), heads(A_KV_HEADS),
            heads(IDX_HEADS), seq(LANES), seq(LANES),
            tok(M_Q_LORA), tok(M_KV_LORA),
        ],
        out_shape=[
            jax.ShapeDtypeStruct((BATCH, A_HEADS, SEQ, LANES), BF16),
            jax.ShapeDtypeStruct((BATCH, A_KV_HEADS, SEQ, LANES), BF16),
            jax.ShapeDtypeStruct((BATCH, A_KV_HEADS, SEQ, LANES), BF16),
            jax.ShapeDtypeStruct((BATCH, IDX_HEADS, SEQ, LANES), BF16),
            jax.ShapeDtypeStruct((BATCH, SEQ, LANES), BF16),
            jax.ShapeDtypeStruct((BATCH, SEQ, LANES), F32),
            jax.ShapeDtypeStruct((TOKENS, M_Q_LORA), BF16),
            jax.ShapeDtypeStruct((TOKENS, M_KV_LORA), BF16),
        ],
        compiler_params=_params("arbitrary"),
        name="prep",
    )(proj, proj, proj, proj, proj, proj, tabs_cos, tabs_sin, tabs_cos, tabs_sin,
      g_qa, g_ka, g_mq_lat, g_mkv_lat)


def _mla_up_kernel(mqn_ref, mkvn_ref, mkr_ref, wq_ref, wkv_ref, cm_ref, sm_ref,
                   gqn_ref, gqr_ref, gkn_ref, gkr_ref, q_o, k_o, v_o):
    tm = mqn_ref.shape[0]
    lane = lax.broadcasted_iota(jnp.int32, (tm, LANES), 1)
    low = lane < M_ROPE
    cm, sm = cm_ref[0], sm_ref[0]
    half = M_ROPE // 2
    scale_m = M_QK ** -0.5
    nope_w = M_HEADS * M_NOPE

    q = _dot(mqn_ref[...], wq_ref[...])
    kv = _dot(mkvn_ref[...], wkv_ref[...])

    for p in range(M_HEADS // 2):
        rp = q[:, nope_w + p * LANES: nope_w + (p + 1) * LANES]
        sq = rp * rp
        s_all = jnp.sum(sq, axis=-1, keepdims=True)
        s_lo = jnp.sum(jnp.where(low, sq, 0.0), axis=-1, keepdims=True)
        for e in range(2):
            h = 2 * p + e
            nope = q[:, h * M_NOPE:(h + 1) * M_NOPE]
            ss = jnp.sum(nope * nope, axis=-1, keepdims=True) + (s_lo if e == 0 else s_all - s_lo)
            rs = lax.rsqrt(ss * (1.0 / M_QK) + EPS)
            roped = _rope(rp * rs * gqr_ref[...], cm, sm, half, lane)
            if e == 1:
                roped = pltpu.roll(roped, M_ROPE, 1)
            q_o[h, :, 0:LANES] = (nope * rs * gqn_ref[...] * scale_m).astype(BF16)
            q_o[h, :, LANES:2 * LANES] = (jnp.where(low, roped, 0.0) * scale_m).astype(BF16)

    kr = mkr_ref[...]
    kr_ss = jnp.sum(kr * kr, axis=-1, keepdims=True)
    kr_roped = _rope(kr * gkr_ref[...], cm, sm, half, lane)
    kr_roped = jnp.where(low, kr_roped, 0.0)
    for h in range(M_HEADS):
        nope = kv[:, h * 2 * LANES: h * 2 * LANES + M_NOPE]
        ss = jnp.sum(nope * nope, axis=-1, keepdims=True) + kr_ss
        rs = lax.rsqrt(ss * (1.0 / M_QK) + EPS)
        k_o[h, :, 0:LANES] = (nope * rs * gkn_ref[...]).astype(BF16)
        k_o[h, :, LANES:2 * LANES] = (kr_roped * rs).astype(BF16)
        v_o[h] = kv[:, h * 2 * LANES + M_NOPE:(h + 1) * 2 * LANES].astype(BF16)


def _mla_up(l, mqn, mkvn, proj, wq, wkv, tabs_cos, tabs_sin, gqn, gqr, gkn, gkr):
    tm = 256
    per_b = SEQ // tm

    def vec():
        return _layer_row_spec(l, LANES)

    def heads(width):
        return pl.BlockSpec((None, M_HEADS, tm, width), lambda i: (i // per_b, 0, i % per_b, 0))

    return pl.pallas_call(
        _mla_up_kernel,
        grid=(TOKENS // tm,),
        in_specs=[
            pl.BlockSpec((tm, M_Q_LORA), lambda i: (i, 0)),
            pl.BlockSpec((tm, M_KV_LORA), lambda i: (i, 0)),
            pl.BlockSpec((tm, LANES), lambda i: (i, COL_MKR // LANES)),
            pl.BlockSpec((None,) + wq.shape[1:], lambda i: (l, 0, 0)),
            pl.BlockSpec((None,) + wkv.shape[1:], lambda i: (l, 0, 0)),
            pl.BlockSpec((1, tm, LANES), lambda i: (2, i, 0)),
            pl.BlockSpec((1, tm, LANES), lambda i: (2, i, 0)),
            vec(), vec(), vec(), vec(),
        ],
        out_specs=[heads(M_QK_PAD), heads(M_QK_PAD), heads(M_V)],
        out_shape=[
            jax.ShapeDtypeStruct((BATCH, M_HEADS, SEQ, M_QK_PAD), BF16),
            jax.ShapeDtypeStruct((BATCH, M_HEADS, SEQ, M_QK_PAD), BF16),
            jax.ShapeDtypeStruct((BATCH, M_HEADS, SEQ, M_V), BF16),
        ],
        compiler_params=_params("arbitrary"),
        name="mla_up",
    )(mqn, mkvn, proj, wq, wkv, tabs_cos, tabs_sin, gqn, gqr, gkn, gkr)


INT_MIN = -(2 ** 31)


DSA_TQ = 256
KEY_CHUNK = 512
SEARCH_ROWS = 128
IDX_CHUNK = 256
IDX_ROWS = 64


def _dsa_body(nk, search, i, qa_ref, ka_ref, va_ref, qi_ref, ki2_ref, wi_ref, o_ref,
              key_ref, bias_ref, d_ref, wb_ref):
    tq = qi_ref.shape[1]
    row = i * tq + lax.broadcasted_iota(jnp.int32, (tq, nk), 0)
    col = lax.broadcasted_iota(jnp.int32, (tq, nk), 1)
    causal = col <= row

    if search:
        w = wi_ref[...]
        for h in range(IDX_HEADS):
            wb_ref[h] = jnp.broadcast_to(w[:, IDX_DIM + h:IDX_DIM + h + 1], (tq, LANES))
        q_all = qi_ref[...].reshape(IDX_HEADS * tq, LANES)
        for c0 in range(0, nk, IDX_CHUNK):
            d_ref[...] = _dot_nt(q_all, ki2_ref[c0:c0 + IDX_CHUNK, :])
            for r0 in range(0, tq, IDX_ROWS):
                row_t = i * tq + r0 + lax.broadcasted_iota(jnp.int32, (IDX_ROWS, LANES), 0)
                for j0 in range(0, IDX_CHUNK, LANES):
                    acc = jnp.zeros((IDX_ROWS, LANES), F32)
                    for h in range(IDX_HEADS):
                        d = d_ref[h * tq + r0:h * tq + r0 + IDX_ROWS, j0:j0 + LANES]
                        acc = acc + jnp.maximum(d, 0.0) * wb_ref[h, r0:r0 + IDX_ROWS, :]
                    col_t = c0 + j0 + lax.broadcasted_iota(jnp.int32, (IDX_ROWS, LANES), 1)
                    score = jnp.where(col_t <= row_t, acc, -jnp.inf)
                    bits = pltpu.bitcast(score, jnp.int32)
                    key_ref[r0:r0 + IDX_ROWS, c0 + j0:c0 + j0 + LANES] = (
                        bits ^ ((bits >> 31) & jnp.int32(0x7FFFFFFF)))

        groups = [(r, SEARCH_ROWS) for r in range(0, tq, SEARCH_ROWS)]
        n_sel = float(N_SEL)

        def count_ge(r0, c):
            k = key_ref[r0:r0 + SEARCH_ROWS, 0:nk]
            return jnp.sum(jnp.where(k >= c, 1.0, 0.0), axis=-1, keepdims=True)

        zero = jnp.zeros((SEARCH_ROWS, 1), jnp.int32)
        thr0 = tuple(jnp.where(count_ge(r0, zero) >= n_sel, jnp.int32(0), jnp.int32(INT_MIN))
                     for r0, _ in groups)

        def step(b, thrs):
            bit = lax.shift_left(jnp.int32(1), jnp.int32(30) - b)
            out = []
            for (r0, _), thr in zip(groups, thrs):
                cand = thr | bit
                out.append(jnp.where(count_ge(r0, cand) >= n_sel, cand, thr))
            return tuple(out)

        thrs = lax.fori_loop(0, 31, step, thr0)
        for (r0, n), thr in zip(groups, thrs):
            sel = (key_ref[r0:r0 + n, 0:nk] >= thr) & causal[r0:r0 + n]
            bias_ref[r0:r0 + n, 0:nk] = jnp.where(sel, 0.0, -jnp.inf)
    else:
        bias_ref[:, 0:nk] = jnp.where(causal, 0.0, -jnp.inf)

    rep = A_HEADS // A_KV_HEADS
    for g in range(A_KV_HEADS):
        k = ka_ref[g, 0:nk, :]
        v = va_ref[g, 0:nk, :]
        q = qa_ref[g * rep:(g + 1) * rep].reshape(rep * tq, LANES)
        logits = _dot_nt(q, k).reshape(rep, tq, nk) + bias_ref[:, 0:nk][None]
        m = jnp.max(logits, axis=-1, keepdims=True)
        p = jnp.exp(logits - m)
        l = jnp.sum(p, axis=-1, keepdims=True)
        o = _dot(p.astype(BF16).reshape(rep * tq, nk), v).reshape(rep, tq, LANES) / l
        for r in range(rep):
            h = g * rep + r
            o_ref[:, h * LANES:(h + 1) * LANES] = o[r].astype(BF16)


def _dsa_kernel(*refs):
    i = pl.program_id(1)
    tq = DSA_TQ
    n_free = N_SEL // tq

    for v in range(n_free):
        @pl.when(i == v)
        def _(v=v):
            _dsa_body((v + 1) * tq, False, i, *refs)

    per_chunk = KEY_CHUNK // tq
    for v in range(SEQ // KEY_CHUNK):
        @pl.when((i >= n_free) & (i // per_chunk == v))
        def _(v=v):
            _dsa_body((v + 1) * KEY_CHUNK, True, i, *refs)


def _dsa(qa, ka, va, qi, ki2, wi):
    tq = DSA_TQ
    nq = SEQ // tq
    return pl.pallas_call(
        _dsa_kernel,
        grid=(BATCH, nq),
        in_specs=[
            pl.BlockSpec((None, A_HEADS, tq, LANES), lambda b, i: (b, 0, i, 0)),
            pl.BlockSpec((None, A_KV_HEADS, SEQ, LANES), lambda b, i: (b, 0, 0, 0)),
            pl.BlockSpec((None, A_KV_HEADS, SEQ, LANES), lambda b, i: (b, 0, 0, 0)),
            pl.BlockSpec((None, IDX_HEADS, tq, LANES), lambda b, i: (b, 0, i, 0)),
            pl.BlockSpec((None, SEQ, LANES), lambda b, i: (b, 0, 0)),
            pl.BlockSpec((None, tq, LANES), lambda b, i: (b, i, 0)),
        ],
        out_specs=pl.BlockSpec((tq, A_HEADS * A_HEAD_DIM), lambda b, i: (b * nq + i, 0)),
        out_shape=jax.ShapeDtypeStruct((TOKENS, A_HEADS * A_HEAD_DIM), BF16),
        scratch_shapes=[
            pltpu.VMEM((tq, SEQ), jnp.int32),
            pltpu.VMEM((tq, SEQ), F32),
            pltpu.VMEM((IDX_HEADS * tq, IDX_CHUNK), F32),
            pltpu.VMEM((IDX_HEADS, tq, LANES), F32),
        ],
        compiler_params=_params("arbitrary", "arbitrary"),
        name="dsa",
    )(qa, ka, va, qi, ki2, wi)


MLA_TQ = 256
MLA_HEADS_PER_STEP = 4


def _mla_body(nk, i, q_ref, k_ref, v_ref, o_ref):
    tq = q_ref.shape[1]
    row = i * tq + lax.broadcasted_iota(jnp.int32, (tq, nk), 0)
    col = lax.broadcasted_iota(jnp.int32, (tq, nk), 1)
    causal = col <= row
    for h in range(q_ref.shape[0]):
        s = jnp.where(causal, _dot_nt(q_ref[h], k_ref[h, 0:nk, :]), -jnp.inf)
        m = jnp.max(s, axis=-1, keepdims=True)
        p = jnp.exp(s - m)
        l = jnp.sum(p, axis=-1, keepdims=True)
        o = _dot(p.astype(BF16), v_ref[h, 0:nk, :]) / l
        o_ref[:, h * M_V:(h + 1) * M_V] = o.astype(BF16)


def _mla_kernel(q_ref, k_ref, v_ref, o_ref):
    i = pl.program_id(2)
    per_chunk = KEY_CHUNK // MLA_TQ
    for v in range(SEQ // KEY_CHUNK):
        @pl.when(i // per_chunk == v)
        def _(v=v):
            _mla_body((v + 1) * KEY_CHUNK, i, q_ref, k_ref, v_ref, o_ref)


def _mla(q, k, v):
    tq, hg = MLA_TQ, MLA_HEADS_PER_STEP
    nq = SEQ // tq
    return pl.pallas_call(
        _mla_kernel,
        grid=(BATCH, M_HEADS // hg, nq),
        in_specs=[
            pl.BlockSpec((None, hg, tq, M_QK_PAD), lambda b, g, i: (b, g, i, 0)),
            pl.BlockSpec((None, hg, SEQ, M_QK_PAD), lambda b, g, i: (b, g, 0, 0)),
            pl.BlockSpec((None, hg, SEQ, M_V), lambda b, g, i: (b, g, 0, 0)),
        ],
        out_specs=pl.BlockSpec((tq, hg * M_V), lambda b, g, i: (b * nq + i, g)),
        out_shape=jax.ShapeDtypeStruct((TOKENS, M_HEADS * M_V), BF16),
        compiler_params=_params("arbitrary", "arbitrary", "arbitrary"),
        name="mla",
    )(q, k, v)


def _merge_kernel(oa_ref, ob_ref, ga_ref, gb_ref, wpa_ref, wpb_ref, wo_ref, x_ref, gt_ref, o_ref):
    a = _dot(oa_ref[...], wpa_ref[...])
    b = _dot(ob_ref[...], wpb_ref[...])
    merged = jax.nn.sigmoid(ga_ref[...]) * a + jax.nn.sigmoid(gb_ref[...]) * b
    y = _dot(merged.astype(BF16), wo_ref[...])
    o_ref[...] = x_ref[...] + gt_ref[...] * y


def _merge(l, oa, ob, proj, wpa, wpb, wo, x, mod):
    tm = 256
    per_b = SEQ // tm

    def resident(shape):
        return pl.BlockSpec((None,) + shape[1:], lambda i: (l, 0, 0), pipeline_mode=pl.Buffered(1))

    return pl.pallas_call(
        _merge_kernel,
        grid=(TOKENS // tm,),
        in_specs=[
            pl.BlockSpec((tm, A_HEADS * A_HEAD_DIM), lambda i: (i, 0)),
            pl.BlockSpec((tm, M_HEADS * M_V), lambda i: (i, 0)),
            pl.BlockSpec((tm, D_MODEL), lambda i: (i, COL_GATE_A // D_MODEL)),
            pl.BlockSpec((tm, D_MODEL), lambda i: (i, COL_GATE_B // D_MODEL)),
            resident(wpa.shape), resident(wpb.shape), resident(wo.shape),
            pl.BlockSpec((tm, D_MODEL), lambda i: (i, 0)),
            _mod_spec(l, 2, per_b),
        ],
        out_specs=pl.BlockSpec((tm, D_MODEL), lambda i: (i, 0)),
        out_shape=jax.ShapeDtypeStruct((TOKENS, D_MODEL), F32),
        compiler_params=_params("arbitrary"),
        name="merge",
    )(oa, ob, proj, proj, wpa, wpb, wo, x, mod)


HALO = 8


def _ffn_up_kernel(x_ref, g_ref, sc_ref, sh_ref, wg_ref, wv_ref, cg_ref, cv_ref, bg_ref, bv_ref,
                   o_ref, h_ref, halo_ref, buf_ref, *, tiles_per_seq):
    i, j = pl.program_id(0), pl.program_id(1)
    tm, tn = o_ref.shape

    @pl.when(j == 0)
    def _():
        h_ref[...] = _modnorm(x_ref[...], g_ref[...], sc_ref[...], sh_ref[...]).astype(BF16)

    seq_start = (i % tiles_per_seq) == 0

    @pl.when(seq_start)
    def _():
        buf_ref[0:HALO, :] = jnp.zeros((HALO, 2 * tn), F32)

    @pl.when(jnp.logical_not(seq_start))
    def _():
        buf_ref[0:HALO, :] = halo_ref[j]

    h = h_ref[...]
    buf_ref[HALO:HALO + tm, 0:tn] = _dot(h, wg_ref[...])
    buf_ref[HALO:HALO + tm, tn:2 * tn] = _dot(h, wv_ref[...])
    halo_ref[j] = buf_ref[tm:tm + HALO, :]

    def conv(lo, c_ref, b_ref):
        u0 = buf_ref[HALO:HALO + tm, lo:lo + tn]
        u1 = buf_ref[HALO - 1:HALO - 1 + tm, lo:lo + tn]
        u2 = buf_ref[HALO - 2:HALO - 2 + tm, lo:lo + tn]
        return b_ref[...] + c_ref[0:1, :] * u2 + c_ref[1:2, :] * u1 + c_ref[2:3, :] * u0

    gate = conv(0, cg_ref, bg_ref)
    val = conv(tn, cv_ref, bv_ref)
    o_ref[...] = (gate * jax.nn.sigmoid(gate) * val).astype(BF16)


def _ffn_up(l, x, g, mod, w_up, w_conv, b_conv):
    tm, tn = 1024, 512
    per_b = SEQ // tm
    nj = D_FF // tn
    kern = functools.partial(_ffn_up_kernel, tiles_per_seq=per_b)
    return pl.pallas_call(
        kern,
        grid=(TOKENS // tm, nj),
        in_specs=[
            pl.BlockSpec((tm, D_MODEL), lambda i, j: (i, 0)),
            _layer_row_spec(l, D_MODEL),
            _mod_spec(l, 4, per_b),
            _mod_spec(l, 3, per_b),
            pl.BlockSpec((None, D_MODEL, tn), lambda i, j: (l, 0, j)),
            pl.BlockSpec((None, D_MODEL, tn), lambda i, j: (l, 0, nj + j)),
            pl.BlockSpec((None, CONV_W, tn), lambda i, j: (l, 0, j)),
            pl.BlockSpec((None, CONV_W, tn), lambda i, j: (l, 0, nj + j)),
            pl.BlockSpec((None, 1, tn), lambda i, j: (l, 0, j)),
            pl.BlockSpec((None, 1, tn), lambda i, j: (l, 0, nj + j)),
        ],
        out_specs=pl.BlockSpec((tm, tn), lambda i, j: (i, j)),
        out_shape=jax.ShapeDtypeStruct((TOKENS, D_FF), BF16),
        scratch_shapes=[
            pltpu.VMEM((tm, D_MODEL), BF16),
            pltpu.VMEM((nj, HALO, 2 * tn), F32),
            pltpu.VMEM((HALO + tm, 2 * tn), F32),
        ],
        compiler_params=_params("arbitrary", "arbitrary"),
        name="ffn_up",
    )(x, g, mod, mod, w_up, w_up, w_conv, w_conv, b_conv, b_conv)


def _ffn_down_kernel(a_ref, w_ref, x_ref, gt_ref, o_ref):
    o_ref[...] = x_ref[...] + gt_ref[...] * _dot(a_ref[...], w_ref[...])


def _ffn_down(l, act, w_down, x, mod):
    tm, tn = 1024, 256
    per_b = SEQ // tm
    return pl.pallas_call(
        _ffn_down_kernel,
        grid=(TOKENS // tm, D_MODEL // tn),
        in_specs=[
            pl.BlockSpec((tm, D_FF), lambda i, j: (i, 0)),
            pl.BlockSpec((None, D_FF, tn), lambda i, j: (l, 0, j)),
            pl.BlockSpec((tm, tn), lambda i, j: (i, j)),
            _mod_spec(l, 5, per_b, tn),
        ],
        out_specs=pl.BlockSpec((tm, tn), lambda i, j: (i, j)),
        out_shape=jax.ShapeDtypeStruct((TOKENS, D_MODEL), F32),
        compiler_params=_params("arbitrary", "arbitrary"),
        name="ffn_down",
    )(act, w_down, x, mod)


def _pack_w_in(w_in):
    o = np.cumsum((0,) + (1024, 256, 256, 1024, 64, 16, 512, 256, 64, 2048, 2048))
    qa, ka, va, qi, ki, wi, mql, mkvl, mkr, ga, gb = (
        w_in[..., int(o[k]):int(o[k + 1])] for k in range(11))
    z = lambda n: jnp.zeros(w_in.shape[:-1] + (n,), w_in.dtype)
    packed = jnp.concatenate(
        [ga, gb, qa, qi, mql, ka, va, mkvl, ki, wi, z(48), mkr, z(64)], axis=-1)
    assert packed.shape[-1] == IN_PACKED
    return packed.astype(BF16)


def _pack_w_mq(w_mq_up):
    w = w_mq_up.reshape(DEPTH, M_Q_LORA, M_HEADS, M_QK)
    nope = w[..., :M_NOPE].reshape(DEPTH, M_Q_LORA, M_HEADS * M_NOPE)
    rope = w[..., M_NOPE:].reshape(DEPTH, M_Q_LORA, M_HEADS * M_ROPE)
    return jnp.concatenate([nope, rope], axis=-1).astype(BF16)


def _rope_consts():
    def inv(rot):
        return ROPE_THETA ** (-jnp.arange(0, rot, 2, dtype=F32) / rot)

    def lanes(v, group):
        rep = jnp.concatenate([v, v])
        pad = jnp.zeros((group - rep.shape[0],), F32)
        return jnp.tile(jnp.concatenate([rep, pad]), LANES // group)

    def sgn(half, group):
        v = jnp.concatenate([-jnp.ones((half,), F32), jnp.ones((half,), F32)])
        pad = jnp.zeros((group - 2 * half,), F32)
        return jnp.tile(jnp.concatenate([v, pad]), LANES // group)

    rot_a = A_HEAD_DIM // ROT_FRACTION
    rot_i = IDX_DIM // ROT_FRACTION
    inv3 = jnp.stack([lanes(inv(rot_a), LANES), lanes(inv(rot_i), IDX_DIM), lanes(inv(M_ROPE), M_ROPE)])
    sgn3 = jnp.stack([sgn(rot_a // 2, LANES), sgn(rot_i // 2, IDX_DIM), sgn(M_ROPE // 2, M_ROPE)])
    return inv3.reshape(3, 1, LANES), sgn3.reshape(3, 1, LANES)


def kernel(x, c, positions, g_attn, g_ffn, w_ada, b_ada, w_in, g_qa, g_ka, g_mq_lat, w_mq_up,
           g_mkv_lat, w_mkv_up, g_qm, g_km, w_pa, w_pb, w_o, w_up, w_conv, b_conv, w_down):
    pos_col = positions.astype(F32).reshape(TOKENS, 1)
    inv3, sgn3 = _rope_consts()
    tabs_cos, tabs_sin = _rope_tables(pos_col, inv3, sgn3)

    c8 = jnp.pad(c, ((0, 8 - BATCH), (0, 0)))
    mod = _ada(c8, w_ada, b_ada)[:, :BATCH].reshape(DEPTH, BATCH, N_ADA, 1, D_MODEL)

    w_in_p = _pack_w_in(w_in)
    w_mq_p = _pack_w_mq(w_mq_up)
    w_mkv_b = w_mkv_up.astype(BF16)
    w_pa_b, w_pb_b, w_o_b = w_pa.astype(BF16), w_pb.astype(BF16), w_o.astype(BF16)
    w_up_b, w_down_b = w_up.astype(BF16), w_down.astype(BF16)

    rows = lambda v: v.reshape(DEPTH, 1, -1)
    rope_gain = lambda v: rows(jnp.tile(v[:, M_NOPE:], (1, 2)))
    g_attn_r, g_ffn_r, b_conv_r = rows(g_attn), rows(g_ffn), rows(b_conv)
    g_qa_r, g_ka_r, g_mq_r, g_mkv_r = rows(g_qa), rows(g_ka), rows(g_mq_lat), rows(g_mkv_lat)
    gqn, gqr = rows(g_qm[:, :M_NOPE]), rope_gain(g_qm)
    gkn, gkr = rows(g_km[:, :M_NOPE]), rope_gain(g_km)

    xf = x.reshape(TOKENS, D_MODEL)
    for l in range(DEPTH):
        proj = _in_proj(l, xf, g_attn_r, mod, w_in_p)
        qa, ka, va, qi, ki2, wi, mqn, mkvn = _prep(
            l, proj, tabs_cos, tabs_sin, g_qa_r, g_ka_r, g_mq_r, g_mkv_r)
        mq, mk, mv = _mla_up(l, mqn, mkvn, proj, w_mq_p, w_mkv_b, tabs_cos, tabs_sin,
                             gqn, gqr, gkn, gkr)
        o_a = _dsa(qa, ka, va, qi, ki2, wi)
        o_b = _mla(mq, mk, mv)
        xf = _merge(l, o_a, o_b, proj, w_pa_b, w_pb_b, w_o_b, xf, mod)
        act = _ffn_up(l, xf, g_ffn_r, mod, w_up_b, w_conv, b_conv_r)
        xf = _ffn_down(l, act, w_down_b, xf, mod)
    return xf.reshape(BATCH, SEQ, D_MODEL)
```

```python
import functools

import jax
import jax.numpy as jnp
import numpy as np
from jax import lax
from jax.experimental import pallas as pl
from jax.experimental.pallas import tpu as pltpu

D_MODEL = 2048
BATCH = 4
SEQ = 2048
DEPTH = 4
A_HEADS = 8
A_KV_HEADS = 2
A_HEAD_DIM = 128
IDX_HEADS = 16
IDX_DIM = 64
TOPK_MAX = 256
M_HEADS = 8
M_Q_LORA = 512
M_KV_LORA = 256
M_NOPE = 128
M_ROPE = 64
M_V = 128
D_FF = 5632
CONV_W = 3
ROPE_THETA = 500000.0
ROT_FRACTION = 4
EPS = 1e-6
N_ADA = 6
IDX_W_SCALE = (IDX_HEADS * IDX_DIM) ** -0.5
N_SEL = min(TOPK_MAX, SEQ // 4)
TOKENS = BATCH * SEQ
LANES = 128
M_QK = M_NOPE + M_ROPE
M_QK_PAD = 256

BF16 = jnp.bfloat16
F32 = jnp.float32

COL_GATE_A = 0
COL_GATE_B = 2048
COL_QA = 4096
COL_QI = 5120
COL_MQL = 6144
COL_KV = 6656
COL_MKVL = 7168
COL_KIWI = 7424
COL_MKR = 7552
IN_PACKED = 7680

VMEM_LIMIT = 56 * 1024 * 1024


def _params(*sem):
    return pltpu.CompilerParams(dimension_semantics=sem, vmem_limit_bytes=VMEM_LIMIT)


def _dot(a, b):
    return jnp.dot(a, b, preferred_element_type=F32)


def _dot_nt(a, b):
    return lax.dot_general(a, b, (((1,), (1,)), ((), ())), preferred_element_type=F32)


def _ada_kernel(c_ref, w_ref, b_ref, o_ref):
    c = c_ref[...]
    c_act = (c * jax.nn.sigmoid(c)).astype(BF16)
    o_ref[0] = _dot(c_act, w_ref[0].astype(BF16)) + b_ref[0]


def _ada(c8, w_ada, b_ada):
    tn = 1024
    n = N_ADA * D_MODEL
    return pl.pallas_call(
        _ada_kernel,
        grid=(DEPTH, n // tn),
        in_specs=[
            pl.BlockSpec((8, D_MODEL), lambda l, j: (0, 0)),
            pl.BlockSpec((1, D_MODEL, tn), lambda l, j: (l, 0, j)),
            pl.BlockSpec((1, 1, tn), lambda l, j: (l, 0, j)),
        ],
        out_specs=pl.BlockSpec((1, 8, tn), lambda l, j: (l, 0, j)),
        out_shape=jax.ShapeDtypeStruct((DEPTH, 8, n), F32),
        compiler_params=_params("arbitrary", "arbitrary"),
        name="ada",
    )(c8, w_ada, b_ada.reshape(DEPTH, 1, n))


def _rope_table_kernel(pos_ref, inv_ref, sgn_ref, cos_ref, sin_ref):
    ang = pos_ref[...] * inv_ref[0]
    cos_ref[0] = jnp.cos(ang)
    sin_ref[0] = jnp.sin(ang) * sgn_ref[0]


def _rope_tables(pos_col, inv3, sgn3):
    tm = 512
    return pl.pallas_call(
        _rope_table_kernel,
        grid=(3, TOKENS // tm),
        in_specs=[
            pl.BlockSpec((tm, 1), lambda t, i: (i, 0)),
            pl.BlockSpec((1, 1, LANES), lambda t, i: (t, 0, 0)),
            pl.BlockSpec((1, 1, LANES), lambda t, i: (t, 0, 0)),
        ],
        out_specs=[
            pl.BlockSpec((1, tm, LANES), lambda t, i: (t, i, 0)),
            pl.BlockSpec((1, tm, LANES), lambda t, i: (t, i, 0)),
        ],
        out_shape=[jax.ShapeDtypeStruct((3, TOKENS, LANES), F32)] * 2,
        compiler_params=_params("arbitrary", "arbitrary"),
        name="rope_tables",
    )(pos_col, inv3, sgn3)


def _rope(t, cos, sin_signed, half, lane):
    fwd = pltpu.roll(t, LANES - half, 1)
    bwd = pltpu.roll(t, half, 1)
    swapped = jnp.where((lane % (2 * half)) < half, fwd, bwd)
    return t * cos + swapped * sin_signed


def _rms(t, g, width):
    ms = jnp.sum(t * t, axis=-1, keepdims=True) * (1.0 / width)
    return t * lax.rsqrt(ms + EPS) * g


def _modnorm(x, g, sc, sh):
    ms = jnp.mean(x * x, axis=-1, keepdims=True)
    y = x * lax.rsqrt(ms + EPS) * g
    return y * (1.0 + sc) + sh


def _in_proj_kernel(x_ref, g_ref, sc_ref, sh_ref, w_ref, o_ref, h_ref):
    @pl.when(pl.program_id(1) == 0)
    def _():
        h_ref[...] = _modnorm(x_ref[...], g_ref[...], sc_ref[...], sh_ref[...]).astype(BF16)

    o_ref[...] = _dot(h_ref[...], w_ref[...])


def _mod_spec(l, k, per_b, width=D_MODEL):
    if width == D_MODEL:
        return pl.BlockSpec((None, None, None, 1, D_MODEL), lambda i, *_: (l, i // per_b, k, 0, 0))
    return pl.BlockSpec((None, None, None, 1, width), lambda i, j: (l, i // per_b, k, 0, j))


def _layer_row_spec(l, width):
    return pl.BlockSpec((None, 1, width), lambda *_: (l, 0, 0))


def _in_proj(l, x, g, mod, w):
    tm, tn = 1024, 512
    per_b = SEQ // tm
    n = w.shape[-1]
    return pl.pallas_call(
        _in_proj_kernel,
        grid=(TOKENS // tm, n // tn),
        in_specs=[
            pl.BlockSpec((tm, D_MODEL), lambda i, j: (i, 0)),
            _layer_row_spec(l, D_MODEL),
            _mod_spec(l, 1, per_b),
            _mod_spec(l, 0, per_b),
            pl.BlockSpec((None, D_MODEL, tn), lambda i, j: (l, 0, j)),
        ],
        out_specs=pl.BlockSpec((tm, tn), lambda i, j: (i, j)),
        out_shape=jax.ShapeDtypeStruct((TOKENS, n), F32),
        scratch_shapes=[pltpu.VMEM((tm, D_MODEL), BF16)],
        compiler_params=_params("arbitrary", "arbitrary"),
        name="in_proj",
    )(x, g, mod, mod, w)


def _prep_kernel(qa_ref, qi_ref, mql_ref, kv_ref, mkvl_ref, kiwi_ref,
                 ca_ref, sa_ref, ci_ref, si_ref,
                 gqa_ref, gka_ref, gmq_ref, gmkv_ref,
                 qa_o, ka_o, va_o, qi_o, ki2_o, wi_o, mqn_o, mkvn_o):
    tm = qa_ref.shape[0]
    lane = lax.broadcasted_iota(jnp.int32, (tm, LANES), 1)
    ca, sa = ca_ref[0], sa_ref[0]
    ci, si = ci_ref[0], si_ref[0]
    rot_a = A_HEAD_DIM // ROT_FRACTION // 2
    rot_i = IDX_DIM // ROT_FRACTION // 2
    scale_a = A_HEAD_DIM ** -0.5

    for h in range(A_HEADS):
        t = qa_ref[:, h * LANES:(h + 1) * LANES]
        r = _rope(_rms(t, gqa_ref[...], A_HEAD_DIM), ca, sa, rot_a, lane)
        qa_o[h] = (r * scale_a).astype(BF16)
    for g in range(A_KV_HEADS):
        t = kv_ref[:, g * LANES:(g + 1) * LANES]
        r = _rope(_rms(t, gka_ref[...], A_HEAD_DIM), ca, sa, rot_a, lane)
        ka_o[g] = r.astype(BF16)
        va_o[g] = kv_ref[:, (A_KV_HEADS + g) * LANES:(A_KV_HEADS + g + 1) * LANES].astype(BF16)
    for p in range(IDX_HEADS // 2):
        t = qi_ref[:, p * LANES:(p + 1) * LANES]
        r = _rope(t, ci, si, rot_i, lane)
        qi_o[2 * p] = jnp.where(lane < IDX_DIM, r, 0.0).astype(BF16)
        qi_o[2 * p + 1] = jnp.where(lane < IDX_DIM, 0.0, r).astype(BF16)

    kiwi = kiwi_ref[...]
    ki = _rope(kiwi, ci, si, rot_i, lane)
    ki2_o[...] = jnp.where(lane < IDX_DIM, ki, pltpu.roll(ki, IDX_DIM, 1)).astype(BF16)
    wi_o[...] = kiwi * IDX_W_SCALE

    mqn_o[...] = _rms(mql_ref[...], gmq_ref[...], M_Q_LORA).astype(BF16)
    mkvn_o[...] = _rms(mkvl_ref[...], gmkv_ref[...], M_KV_LORA).astype(BF16)


def _prep(l, proj, tabs_cos, tabs_sin, g_qa, g_ka, g_mq_lat, g_mkv_lat):
    tm = 256
    per_b = SEQ // tm

    def col(width, start):
        idx = start // width
        return pl.BlockSpec((tm, width), lambda i: (i, idx))

    def tab(t):
        return pl.BlockSpec((1, tm, LANES), lambda i: (t, i, 0))

    def vec(width):
        return _layer_row_spec(l, width)

    def heads(n):
        return pl.BlockSpec((None, n, tm, LANES), lambda i: (i // per_b, 0, i % per_b, 0))

    def seq(width):
        return pl.BlockSpec((None, tm, width), lambda i: (i // per_b, i % per_b, 0))

    def tok(width):
        return pl.BlockSpec((tm, width), lambda i: (i, 0))

    return pl.pallas_call(
        _prep_kernel,
        grid=(TOKENS // tm,),
        in_specs=[
            col(1024, COL_QA), col(1024, COL_QI), col(512, COL_MQL), col(512, COL_KV),
            col(256, COL_MKVL), col(128, COL_KIWI),
            tab(0), tab(0), tab(1), tab(1),
            vec(A_HEAD_DIM), vec(A_HEAD_DIM), vec(M_Q_LORA), vec(M_KV_LORA),
        ],
        out_specs=[
            heads(A_HEADS), heads(A_KV_HEADS), heads(A_KV_HEADS),
            heads(IDX_HEADS), seq(LANES), seq(LANES),
            tok(M_Q_LORA), tok(M_KV_LORA),
        ],
        out_shape=[
            jax.ShapeDtypeStruct((BATCH, A_HEADS, SEQ, LANES), BF16),
            jax.ShapeDtypeStruct((BATCH, A_KV_HEADS, SEQ, LANES), BF16),
            jax.ShapeDtypeStruct((BATCH, A_KV_HEADS, SEQ, LANES), BF16),
            jax.ShapeDtypeStruct((BATCH, IDX_HEADS, SEQ, LANES), BF16),
            jax.ShapeDtypeStruct((BATCH, SEQ, LANES), BF16),
            jax.ShapeDtypeStruct((BATCH, SEQ, LANES), F32),
            jax.ShapeDtypeStruct((TOKENS, M_Q_LORA), BF16),
            jax.ShapeDtypeStruct((TOKENS, M_KV_LORA), BF16),
        ],
        compiler_params=_params("arbitrary"),
        name="prep",
    )(proj, proj, proj, proj, proj, proj, tabs_cos, tabs_sin, tabs_cos, tabs_sin,
      g_qa, g_ka, g_mq_lat, g_mkv_lat)


def _mla_up_kernel(mqn_ref, mkvn_ref, mkr_ref, wq_ref, wkv_ref, cm_ref, sm_ref,
                   gqn_ref, gqr_ref, gkn_ref, gkr_ref, q_o, k_o, v_o):
    tm = mqn_ref.shape[0]
    lane = lax.broadcasted_iota(jnp.int32, (tm, LANES), 1)
    low = lane < M_ROPE
    cm, sm = cm_ref[0], sm_ref[0]
    half = M_ROPE // 2
    scale_m = M_QK ** -0.5
    nope_w = M_HEADS * M_NOPE

    q = _dot(mqn_ref[...], wq_ref[...])
    kv = _dot(mkvn_ref[...], wkv_ref[...])

    for p in range(M_HEADS // 2):
        rp = q[:, nope_w + p * LANES: nope_w + (p + 1) * LANES]
        sq = rp * rp
        s_all = jnp.sum(sq, axis=-1, keepdims=True)
        s_lo = jnp.sum(jnp.where(low, sq, 0.0), axis=-1, keepdims=True)
        for e in range(2):
            h = 2 * p + e
            nope = q[:, h * M_NOPE:(h + 1) * M_NOPE]
            ss = jnp.sum(nope * nope, axis=-1, keepdims=True) + (s_lo if e == 0 else s_all - s_lo)
            rs = lax.rsqrt(ss * (1.0 / M_QK) + EPS)
            roped = _rope(rp * rs * gqr_ref[...], cm, sm, half, lane)
            if e == 1:
                roped = pltpu.roll(roped, M_ROPE, 1)
            q_o[h, :, 0:LANES] = (nope * rs * gqn_ref[...] * scale_m).astype(BF16)
            q_o[h, :, LANES:2 * LANES] = (jnp.where(low, roped, 0.0) * scale_m).astype(BF16)

    kr = mkr_ref[...]
    kr_ss = jnp.sum(kr * kr, axis=-1, keepdims=True)
    kr_roped = _rope(kr * gkr_ref[...], cm, sm, half, lane)
    kr_roped = jnp.where(low, kr_roped, 0.0)
    for h in range(M_HEADS):
        nope = kv[:, h * 2 * LANES: h * 2 * LANES + M_NOPE]
        ss = jnp.sum(nope * nope, axis=-1, keepdims=True) + kr_ss
        rs = lax.rsqrt(ss * (1.0 / M_QK) + EPS)
        k_o[h, :, 0:LANES] = (nope * rs * gkn_ref[...]).astype(BF16)
        k_o[h, :, LANES:2 * LANES] = (kr_roped * rs).astype(BF16)
        v_o[h] = kv[:, h * 2 * LANES + M_NOPE:(h + 1) * 2 * LANES].astype(BF16)


def _mla_up(l, mqn, mkvn, proj, wq, wkv, tabs_cos, tabs_sin, gqn, gqr, gkn, gkr):
    tm = 256
    per_b = SEQ // tm

    def vec():
        return _layer_row_spec(l, LANES)

    def heads(width):
        return pl.BlockSpec((None, M_HEADS, tm, width), lambda i: (i // per_b, 0, i % per_b, 0))

    return pl.pallas_call(
        _mla_up_kernel,
        grid=(TOKENS // tm,),
        in_specs=[
            pl.BlockSpec((tm, M_Q_LORA), lambda i: (i, 0)),
            pl.BlockSpec((tm, M_KV_LORA), lambda i: (i, 0)),
            pl.BlockSpec((tm, LANES), lambda i: (i, COL_MKR // LANES)),
            pl.BlockSpec((None,) + wq.shape[1:], lambda i: (l, 0, 0)),
            pl.BlockSpec((None,) + wkv.shape[1:], lambda i: (l, 0, 0)),
            pl.BlockSpec((1, tm, LANES), lambda i: (2, i, 0)),
            pl.BlockSpec((1, tm, LANES), lambda i: (2, i, 0)),
            vec(), vec(), vec(), vec(),
        ],
        out_specs=[heads(M_QK_PAD), heads(M_QK_PAD), heads(M_V)],
        out_shape=[
            jax.ShapeDtypeStruct((BATCH, M_HEADS, SEQ, M_QK_PAD), BF16),
            jax.ShapeDtypeStruct((BATCH, M_HEADS, SEQ, M_QK_PAD), BF16),
            jax.ShapeDtypeStruct((BATCH, M_HEADS, SEQ, M_V), BF16),
        ],
        compiler_params=_params("arbitrary"),
        name="mla_up",
    )(mqn, mkvn, proj, wq, wkv, tabs_cos, tabs_sin, gqn, gqr, gkn, gkr)


INT_MIN = -(2 ** 31)


DSA_TQ = 256
KEY_CHUNK = 512
SEARCH_ROWS = 128
SEARCH_UNROLL = 5
IDX_CHUNK = 256
IDX_ROWS = 64


def _dsa_body(nk, search, i, qa_ref, ka_ref, va_ref, qi_ref, ki2_ref, wi_ref, o_ref,
              key_ref, bias_ref, d_ref, wb_ref):
    tq = qi_ref.shape[1]
    row = i * tq + lax.broadcasted_iota(jnp.int32, (tq, nk), 0)
    col = lax.broadcasted_iota(jnp.int32, (tq, nk), 1)
    causal = col <= row

    if search:
        w = wi_ref[...]
        for h in range(IDX_HEADS):
            wb_ref[h] = jnp.broadcast_to(w[:, IDX_DIM + h:IDX_DIM + h + 1], (tq, LANES))
        q_all = qi_ref[...].reshape(IDX_HEADS * tq, LANES)
        for c0 in range(0, nk, IDX_CHUNK):
            d_ref[...] = _dot_nt(q_all, ki2_ref[c0:c0 + IDX_CHUNK, :])
            for r0 in range(0, tq, IDX_ROWS):
                row_t = i * tq + r0 + lax.broadcasted_iota(jnp.int32, (IDX_ROWS, LANES), 0)
                for j0 in range(0, IDX_CHUNK, LANES):
                    acc = jnp.zeros((IDX_ROWS, LANES), F32)
                    for h in range(IDX_HEADS):
                        d = d_ref[h * tq + r0:h * tq + r0 + IDX_ROWS, j0:j0 + LANES]
                        acc = acc + jnp.maximum(d, 0.0) * wb_ref[h, r0:r0 + IDX_ROWS, :]
                    col_t = c0 + j0 + lax.broadcasted_iota(jnp.int32, (IDX_ROWS, LANES), 1)
                    score = jnp.where(col_t <= row_t, acc, -jnp.inf)
                    bits = pltpu.bitcast(score, jnp.int32)
                    key_ref[r0:r0 + IDX_ROWS, c0 + j0:c0 + j0 + LANES] = (
                        bits ^ ((bits >> 31) & jnp.int32(0x7FFFFFFF)))

        groups = [(r, SEARCH_ROWS) for r in range(0, tq, SEARCH_ROWS)]
        n_sel = float(N_SEL)

        def count_ge(r0, c):
            k = key_ref[r0:r0 + SEARCH_ROWS, 0:nk]
            return jnp.sum(jnp.where(k >= c, 1.0, 0.0), axis=-1, keepdims=True)

        zero = jnp.zeros((SEARCH_ROWS, 1), jnp.int32)
        thr0 = tuple(jnp.where(count_ge(r0, zero) >= n_sel, jnp.int32(0), jnp.int32(INT_MIN))
                     for r0, _ in groups)

        def step(b, thrs):
            bit = lax.shift_left(jnp.int32(1), jnp.int32(30) - b)
            out = []
            for (r0, _), thr in zip(groups, thrs):
                cand = thr | bit
                out.append(jnp.where(count_ge(r0, cand) >= n_sel, cand, thr))
            return tuple(out)

        thrs = lax.fori_loop(0, 30, step, thr0, unroll=SEARCH_UNROLL)
        thrs = step(jnp.int32(30), thrs)
        for (r0, n), thr in zip(groups, thrs):
            sel = (key_ref[r0:r0 + n, 0:nk] >= thr) & causal[r0:r0 + n]
            bias_ref[r0:r0 + n, 0:nk] = jnp.where(sel, 0.0, -jnp.inf)
    else:
        bias_ref[:, 0:nk] = jnp.where(causal, 0.0, -jnp.inf)

    rep = A_HEADS // A_KV_HEADS
    for g in range(A_KV_HEADS):
        k = ka_ref[g, 0:nk, :]
        v = va_ref[g, 0:nk, :]
        q = qa_ref[g * rep:(g + 1) * rep].reshape(rep * tq, LANES)
        logits = _dot_nt(q, k).reshape(rep, tq, nk) + bias_ref[:, 0:nk][None]
        m = jnp.max(logits, axis=-1, keepdims=True)
        p = jnp.exp(logits - m)
        l = jnp.sum(p, axis=-1, keepdims=True)
        o = _dot(p.astype(BF16).reshape(rep * tq, nk), v).reshape(rep, tq, LANES) / l
        for r in range(rep):
            h = g * rep + r
            o_ref[:, h * LANES:(h + 1) * LANES] = o[r].astype(BF16)


def _dsa_kernel(*refs):
    i = pl.program_id(1)
    tq = DSA_TQ
    n_free = N_SEL // tq

    for v in range(n_free):
        @pl.when(i == v)
        def _(v=v):
            _dsa_body((v + 1) * tq, False, i, *refs)

    per_chunk = KEY_CHUNK // tq
    for v in range(SEQ // KEY_CHUNK):
        @pl.when((i >= n_free) & (i // per_chunk == v))
        def _(v=v):
            _dsa_body((v + 1) * KEY_CHUNK, True, i, *refs)


def _dsa(qa, ka, va, qi, ki2, wi):
    tq = DSA_TQ
    nq = SEQ // tq
    return pl.pallas_call(
        _dsa_kernel,
        grid=(BATCH, nq),
        in_specs=[
            pl.BlockSpec((None, A_HEADS, tq, LANES), lambda b, i: (b, 0, i, 0)),
            pl.BlockSpec((None, A_KV_HEADS, SEQ, LANES), lambda b, i: (b, 0, 0, 0)),
            pl.BlockSpec((None, A_KV_HEADS, SEQ, LANES), lambda b, i: (b, 0, 0, 0)),
            pl.BlockSpec((None, IDX_HEADS, tq, LANES), lambda b, i: (b, 0, i, 0)),
            pl.BlockSpec((None, SEQ, LANES), lambda b, i: (b, 0, 0)),
            pl.BlockSpec((None, tq, LANES), lambda b, i: (b, i, 0)),
        ],
        out_specs=pl.BlockSpec((tq, A_HEADS * A_HEAD_DIM), lambda b, i: (b * nq + i, 0)),
        out_shape=jax.ShapeDtypeStruct((TOKENS, A_HEADS * A_HEAD_DIM), BF16),
        scratch_shapes=[
            pltpu.VMEM((tq, SEQ), jnp.int32),
            pltpu.VMEM((tq, SEQ), F32),
            pltpu.VMEM((IDX_HEADS * tq, IDX_CHUNK), F32),
            pltpu.VMEM((IDX_HEADS, tq, LANES), F32),
        ],
        compiler_params=_params("arbitrary", "arbitrary"),
        name="dsa",
    )(qa, ka, va, qi, ki2, wi)


MLA_TQ = 256
MLA_HEADS_PER_STEP = 4


def _mla_body(nk, i, q_ref, k_ref, v_ref, o_ref):
    tq = q_ref.shape[1]
    row = i * tq + lax.broadcasted_iota(jnp.int32, (tq, nk), 0)
    col = lax.broadcasted_iota(jnp.int32, (tq, nk), 1)
    causal = col <= row
    for h in range(q_ref.shape[0]):
        s = jnp.where(causal, _dot_nt(q_ref[h], k_ref[h, 0:nk, :]), -jnp.inf)
        m = jnp.max(s, axis=-1, keepdims=True)
        p = jnp.exp(s - m)
        l = jnp.sum(p, axis=-1, keepdims=True)
        o = _dot(p.astype(BF16), v_ref[h, 0:nk, :]) / l
        o_ref[:, h * M_V:(h + 1) * M_V] = o.astype(BF16)


def _mla_kernel(q_ref, k_ref, v_ref, o_ref):
    i = pl.program_id(2)
    per_chunk = KEY_CHUNK // MLA_TQ
    for v in range(SEQ // KEY_CHUNK):
        @pl.when(i // per_chunk == v)
        def _(v=v):
            _mla_body((v + 1) * KEY_CHUNK, i, q_ref, k_ref, v_ref, o_ref)


def _mla(q, k, v):
    tq, hg = MLA_TQ, MLA_HEADS_PER_STEP
    nq = SEQ // tq
    return pl.pallas_call(
        _mla_kernel,
        grid=(BATCH, M_HEADS // hg, nq),
        in_specs=[
            pl.BlockSpec((None, hg, tq, M_QK_PAD), lambda b, g, i: (b, g, i, 0)),
            pl.BlockSpec((None, hg, SEQ, M_QK_PAD), lambda b, g, i: (b, g, 0, 0)),
            pl.BlockSpec((None, hg, SEQ, M_V), lambda b, g, i: (b, g, 0, 0)),
        ],
        out_specs=pl.BlockSpec((tq, hg * M_V), lambda b, g, i: (b * nq + i, g)),
        out_shape=jax.ShapeDtypeStruct((TOKENS, M_HEADS * M_V), BF16),
        compiler_params=_params("arbitrary", "arbitrary", "arbitrary"),
        name="mla",
    )(q, k, v)


def _merge_kernel(oa_ref, ob_ref, ga_ref, gb_ref, wpa_ref, wpb_ref, wo_ref, x_ref, gt_ref, o_ref):
    a = _dot(oa_ref[...], wpa_ref[...])
    b = _dot(ob_ref[...], wpb_ref[...])
    merged = jax.nn.sigmoid(ga_ref[...]) * a + jax.nn.sigmoid(gb_ref[...]) * b
    y = _dot(merged.astype(BF16), wo_ref[...])
    o_ref[...] = x_ref[...] + gt_ref[...] * y


def _merge(l, oa, ob, proj, wpa, wpb, wo, x, mod):
    tm = 256
    per_b = SEQ // tm

    def resident(shape):
        return pl.BlockSpec((None,) + shape[1:], lambda i: (l, 0, 0), pipeline_mode=pl.Buffered(1))

    return pl.pallas_call(
        _merge_kernel,
        grid=(TOKENS // tm,),
        in_specs=[
            pl.BlockSpec((tm, A_HEADS * A_HEAD_DIM), lambda i: (i, 0)),
            pl.BlockSpec((tm, M_HEADS * M_V), lambda i: (i, 0)),
            pl.BlockSpec((tm, D_MODEL), lambda i: (i, COL_GATE_A // D_MODEL)),
            pl.BlockSpec((tm, D_MODEL), lambda i: (i, COL_GATE_B // D_MODEL)),
            resident(wpa.shape), resident(wpb.shape), resident(wo.shape),
            pl.BlockSpec((tm, D_MODEL), lambda i: (i, 0)),
            _mod_spec(l, 2, per_b),
        ],
        out_specs=pl.BlockSpec((tm, D_MODEL), lambda i: (i, 0)),
        out_shape=jax.ShapeDtypeStruct((TOKENS, D_MODEL), F32),
        compiler_params=_params("arbitrary"),
        name="merge",
    )(oa, ob, proj, proj, wpa, wpb, wo, x, mod)


HALO = 8


def _ffn_up_kernel(x_ref, g_ref, sc_ref, sh_ref, wg_ref, wv_ref, cg_ref, cv_ref, bg_ref, bv_ref,
                   o_ref, h_ref, halo_ref, buf_ref, *, tiles_per_seq):
    i, j = pl.program_id(0), pl.program_id(1)
    tm, tn = o_ref.shape

    @pl.when(j == 0)
    def _():
        h_ref[...] = _modnorm(x_ref[...], g_ref[...], sc_ref[...], sh_ref[...]).astype(BF16)

    seq_start = (i % tiles_per_seq) == 0

    @pl.when(seq_start)
    def _():
        buf_ref[0:HALO, :] = jnp.zeros((HALO, 2 * tn), F32)

    @pl.when(jnp.logical_not(seq_start))
    def _():
        buf_ref[0:HALO, :] = halo_ref[j]

    h = h_ref[...]
    buf_ref[HALO:HALO + tm, 0:tn] = _dot(h, wg_ref[...])
    buf_ref[HALO:HALO + tm, tn:2 * tn] = _dot(h, wv_ref[...])
    halo_ref[j] = buf_ref[tm:tm + HALO, :]

    def conv(lo, c_ref, b_ref):
        u0 = buf_ref[HALO:HALO + tm, lo:lo + tn]
        u1 = buf_ref[HALO - 1:HALO - 1 + tm, lo:lo + tn]
        u2 = buf_ref[HALO - 2:HALO - 2 + tm, lo:lo + tn]
        return b_ref[...] + c_ref[0:1, :] * u2 + c_ref[1:2, :] * u1 + c_ref[2:3, :] * u0

    gate = conv(0, cg_ref, bg_ref)
    val = conv(tn, cv_ref, bv_ref)
    o_ref[...] = (gate * jax.nn.sigmoid(gate) * val).astype(BF16)


def _ffn_up(l, x, g, mod, w_up, w_conv, b_conv):
    tm, tn = 1024, 512
    per_b = SEQ // tm
    nj = D_FF // tn
    kern = functools.partial(_ffn_up_kernel, tiles_per_seq=per_b)
    return pl.pallas_call(
        kern,
        grid=(TOKENS // tm, nj),
        in_specs=[
            pl.BlockSpec((tm, D_MODEL), lambda i, j: (i, 0)),
            _layer_row_spec(l, D_MODEL),
            _mod_spec(l, 4, per_b),
            _mod_spec(l, 3, per_b),
            pl.BlockSpec((None, D_MODEL, tn), lambda i, j: (l, 0, j)),
            pl.BlockSpec((None, D_MODEL, tn), lambda i, j: (l, 0, nj + j)),
            pl.BlockSpec((None, CONV_W, tn), lambda i, j: (l, 0, j)),
            pl.BlockSpec((None, CONV_W, tn), lambda i, j: (l, 0, nj + j)),
            pl.BlockSpec((None, 1, tn), lambda i, j: (l, 0, j)),
            pl.BlockSpec((None, 1, tn), lambda i, j: (l, 0, nj + j)),
        ],
        out_specs=pl.BlockSpec((tm, tn), lambda i, j: (i, j)),
        out_shape=jax.ShapeDtypeStruct((TOKENS, D_FF), BF16),
        scratch_shapes=[
            pltpu.VMEM((tm, D_MODEL), BF16),
            pltpu.VMEM((nj, HALO, 2 * tn), F32),
            pltpu.VMEM((HALO + tm, 2 * tn), F32),
        ],
        compiler_params=_params("arbitrary", "arbitrary"),
        name="ffn_up",
    )(x, g, mod, mod, w_up, w_up, w_conv, w_conv, b_conv, b_conv)


def _ffn_down_kernel(a_ref, w_ref, x_ref, gt_ref, o_ref):
    o_ref[...] = x_ref[...] + gt_ref[...] * _dot(a_ref[...], w_ref[...])


def _ffn_down(l, act, w_down, x, mod):
    tm, tn = 1024, 256
    per_b = SEQ // tm
    return pl.pallas_call(
        _ffn_down_kernel,
        grid=(TOKENS // tm, D_MODEL // tn),
        in_specs=[
            pl.BlockSpec((tm, D_FF), lambda i, j: (i, 0)),
            pl.BlockSpec((None, D_FF, tn), lambda i, j: (l, 0, j)),
            pl.BlockSpec((tm, tn), lambda i, j: (i, j)),
            _mod_spec(l, 5, per_b, tn),
        ],
        out_specs=pl.BlockSpec((tm, tn), lambda i, j: (i, j)),
        out_shape=jax.ShapeDtypeStruct((TOKENS, D_MODEL), F32),
        compiler_params=_params("arbitrary", "arbitrary"),
        name="ffn_down",
    )(act, w_down, x, mod)


def _pack_w_in(w_in):
    o = np.cumsum((0,) + (1024, 256, 256, 1024, 64, 16, 512, 256, 64, 2048, 2048))
    qa, ka, va, qi, ki, wi, mql, mkvl, mkr, ga, gb = (
        w_in[..., int(o[k]):int(o[k + 1])].astype(BF16) for k in range(11))
    z = lambda n: jnp.zeros(w_in.shape[:-1] + (n,), BF16)
    packed = jnp.concatenate(
        [ga, gb, qa, qi, mql, ka, va, mkvl, ki, wi, z(48), mkr, z(64)], axis=-1)
    assert packed.shape[-1] == IN_PACKED
    return packed


def _pack_w_mq(w_mq_up):
    w = w_mq_up.reshape(DEPTH, M_Q_LORA, M_HEADS, M_QK)
    nope = w[..., :M_NOPE].reshape(DEPTH, M_Q_LORA, M_HEADS * M_NOPE)
    rope = w[..., M_NOPE:].reshape(DEPTH, M_Q_LORA, M_HEADS * M_ROPE)
    return jnp.concatenate([nope, rope], axis=-1).astype(BF16)


def _rope_consts():
    def inv(rot):
        return ROPE_THETA ** (-jnp.arange(0, rot, 2, dtype=F32) / rot)

    def lanes(v, group):
        rep = jnp.concatenate([v, v])
        pad = jnp.zeros((group - rep.shape[0],), F32)
        return jnp.tile(jnp.concatenate([rep, pad]), LANES // group)

    def sgn(half, group):
        v = jnp.concatenate([-jnp.ones((half,), F32), jnp.ones((half,), F32)])
        pad = jnp.zeros((group - 2 * half,), F32)
        return jnp.tile(jnp.concatenate([v, pad]), LANES // group)

    rot_a = A_HEAD_DIM // ROT_FRACTION
    rot_i = IDX_DIM // ROT_FRACTION
    inv3 = jnp.stack([lanes(inv(rot_a), LANES), lanes(inv(rot_i), IDX_DIM), lanes(inv(M_ROPE), M_ROPE)])
    sgn3 = jnp.stack([sgn(rot_a // 2, LANES), sgn(rot_i // 2, IDX_DIM), sgn(M_ROPE // 2, M_ROPE)])
    return inv3.reshape(3, 1, LANES), sgn3.reshape(3, 1, LANES)


def kernel(x, c, positions, g_attn, g_ffn, w_ada, b_ada, w_in, g_qa, g_ka, g_mq_lat, w_mq_up,
           g_mkv_lat, w_mkv_up, g_qm, g_km, w_pa, w_pb, w_o, w_up, w_conv, b_conv, w_down):
    pos_col = positions.astype(F32).reshape(TOKENS, 1)
    inv3, sgn3 = _rope_consts()
    tabs_cos, tabs_sin = _rope_tables(pos_col, inv3, sgn3)

    c8 = jnp.pad(c, ((0, 8 - BATCH), (0, 0)))
    mod = _ada(c8, w_ada, b_ada)[:, :BATCH].reshape(DEPTH, BATCH, N_ADA, 1, D_MODEL)

    w_in_p = _pack_w_in(w_in)
    w_mq_p = _pack_w_mq(w_mq_up)
    w_mkv_b = w_mkv_up.astype(BF16)
    w_pa_b, w_pb_b, w_o_b = w_pa.astype(BF16), w_pb.astype(BF16), w_o.astype(BF16)
    w_up_b, w_down_b = w_up.astype(BF16), w_down.astype(BF16)

    rows = lambda v: v.reshape(DEPTH, 1, -1)
    rope_gain = lambda v: rows(jnp.tile(v[:, M_NOPE:], (1, 2)))
    g_attn_r, g_ffn_r, b_conv_r = rows(g_attn), rows(g_ffn), rows(b_conv)
    g_qa_r, g_ka_r, g_mq_r, g_mkv_r = rows(g_qa), rows(g_ka), rows(g_mq_lat), rows(g_mkv_lat)
    gqn, gqr = rows(g_qm[:, :M_NOPE]), rope_gain(g_qm)
    gkn, gkr = rows(g_km[:, :M_NOPE]), rope_gain(g_km)

    xf = x.reshape(TOKENS, D_MODEL)
    for l in range(DEPTH):
        proj = _in_proj(l, xf, g_attn_r, mod, w_in_p)
        qa, ka, va, qi, ki2, wi, mqn, mkvn = _prep(
            l, proj, tabs_cos, tabs_sin, g_qa_r, g_ka_r, g_mq_r, g_mkv_r)
        mq, mk, mv = _mla_up(l, mqn, mkvn, proj, w_mq_p, w_mkv_b, tabs_cos, tabs_sin,
                             gqn, gqr, gkn, gkr)
        o_a = _dsa(qa, ka, va, qi, ki2, wi)
        o_b = _mla(mq, mk, mv)
        xf = _merge(l, o_a, o_b, proj, w_pa_b, w_pb_b, w_o_b, xf, mod)
        act = _ffn_up(l, xf, g_ffn_r, mod, w_up_b, w_conv, b_conv_r)
        xf = _ffn_down(l, act, w_down_b, xf, mod)
    return xf.reshape(BATCH, SEQ, D_MODEL)
```

```python
import functools

import jax
import jax.numpy as jnp
import numpy as np
from jax import lax
from jax.experimental import pallas as pl
from jax.experimental.pallas import tpu as pltpu

D_MODEL = 2048
BATCH = 4
SEQ = 2048
DEPTH = 4
A_HEADS = 8
A_KV_HEADS = 2
A_HEAD_DIM = 128
IDX_HEADS = 16
IDX_DIM = 64
TOPK_MAX = 256
M_HEADS = 8
M_Q_LORA = 512
M_KV_LORA = 256
M_NOPE = 128
M_ROPE = 64
M_V = 128
D_FF = 5632
CONV_W = 3
ROPE_THETA = 500000.0
ROT_FRACTION = 4
EPS = 1e-6
N_ADA = 6
IDX_W_SCALE = (IDX_HEADS * IDX_DIM) ** -0.5
N_SEL = min(TOPK_MAX, SEQ // 4)
TOKENS = BATCH * SEQ
LANES = 128
M_QK = M_NOPE + M_ROPE
M_QK_PAD = 256

BF16 = jnp.bfloat16
F32 = jnp.float32

COL_GATE_A = 0
COL_GATE_B = 2048
COL_QA = 4096
COL_QI = 5120
COL_MQL = 6144
COL_KV = 6656
COL_MKVL = 7168
COL_KIWI = 7424
COL_MKR = 7552
IN_PACKED = 7680

VMEM_LIMIT = 56 * 1024 * 1024


def _params(*sem):
    return pltpu.CompilerParams(dimension_semantics=sem, vmem_limit_bytes=VMEM_LIMIT)


def _dot(a, b):
    return jnp.dot(a, b, preferred_element_type=F32)


def _dot_nt(a, b):
    return lax.dot_general(a, b, (((1,), (1,)), ((), ())), preferred_element_type=F32)


def _ada_kernel(c_ref, w_ref, b_ref, o_ref):
    c = c_ref[...]
    c_act = (c * jax.nn.sigmoid(c)).astype(BF16)
    o_ref[0] = _dot(c_act, w_ref[0].astype(BF16)) + b_ref[0]


def _ada(c8, w_ada, b_ada):
    tn = 1024
    n = N_ADA * D_MODEL
    return pl.pallas_call(
        _ada_kernel,
        grid=(DEPTH, n // tn),
        in_specs=[
            pl.BlockSpec((8, D_MODEL), lambda l, j: (0, 0)),
            pl.BlockSpec((1, D_MODEL, tn), lambda l, j: (l, 0, j)),
            pl.BlockSpec((1, 1, tn), lambda l, j: (l, 0, j)),
        ],
        out_specs=pl.BlockSpec((1, 8, tn), lambda l, j: (l, 0, j)),
        out_shape=jax.ShapeDtypeStruct((DEPTH, 8, n), F32),
        compiler_params=_params("arbitrary", "arbitrary"),
        name="ada",
    )(c8, w_ada, b_ada.reshape(DEPTH, 1, n))


def _rope_table_kernel(pos_ref, inv_ref, sgn_ref, cos_ref, sin_ref):
    ang = pos_ref[...] * inv_ref[0]
    cos_ref[0] = jnp.cos(ang)
    sin_ref[0] = jnp.sin(ang) * sgn_ref[0]


def _rope_tables(pos_col, inv3, sgn3):
    tm = 512
    return pl.pallas_call(
        _rope_table_kernel,
        grid=(3, TOKENS // tm),
        in_specs=[
            pl.BlockSpec((tm, 1), lambda t, i: (i, 0)),
            pl.BlockSpec((1, 1, LANES), lambda t, i: (t, 0, 0)),
            pl.BlockSpec((1, 1, LANES), lambda t, i: (t, 0, 0)),
        ],
        out_specs=[
            pl.BlockSpec((1, tm, LANES), lambda t, i: (t, i, 0)),
            pl.BlockSpec((1, tm, LANES), lambda t, i: (t, i, 0)),
        ],
        out_shape=[jax.ShapeDtypeStruct((3, TOKENS, LANES), F32)] * 2,
        compiler_params=_params("arbitrary", "arbitrary"),
        name="rope_tables",
    )(pos_col, inv3, sgn3)


def _rope(t, cos, sin_signed, half, lane):
    fwd = pltpu.roll(t, LANES - half, 1)
    bwd = pltpu.roll(t, half, 1)
    swapped = jnp.where((lane % (2 * half)) < half, fwd, bwd)
    return t * cos + swapped * sin_signed


def _rms(t, g, width):
    ms = jnp.sum(t * t, axis=-1, keepdims=True) * (1.0 / width)
    return t * lax.rsqrt(ms + EPS) * g


def _modnorm(x, g, sc, sh):
    ms = jnp.mean(x * x, axis=-1, keepdims=True)
    y = x * lax.rsqrt(ms + EPS) * g
    return y * (1.0 + sc) + sh


def _in_proj_kernel(x_ref, g_ref, sc_ref, sh_ref, w_ref, o_ref, h_ref):
    @pl.when(pl.program_id(1) == 0)
    def _():
        h_ref[...] = _modnorm(x_ref[...], g_ref[...], sc_ref[...], sh_ref[...]).astype(BF16)

    o_ref[...] = _dot_nt(h_ref[...], w_ref[...])


def _mod_spec(l, k, per_b, width=D_MODEL):
    if width == D_MODEL:
        return pl.BlockSpec((None, None, None, 1, D_MODEL), lambda i, *_: (l, i // per_b, k, 0, 0))
    return pl.BlockSpec((None, None, None, 1, width), lambda i, j: (l, i // per_b, k, 0, j))


def _layer_row_spec(l, width):
    return pl.BlockSpec((None, 1, width), lambda *_: (l, 0, 0))


def _in_proj(l, x, g, mod, w):
    tm, tn = 1024, 512
    per_b = SEQ // tm
    n = w.shape[1]
    return pl.pallas_call(
        _in_proj_kernel,
        grid=(TOKENS // tm, n // tn),
        in_specs=[
            pl.BlockSpec((tm, D_MODEL), lambda i, j: (i, 0)),
            _layer_row_spec(l, D_MODEL),
            _mod_spec(l, 1, per_b),
            _mod_spec(l, 0, per_b),
            pl.BlockSpec((None, tn, D_MODEL), lambda i, j: (l, j, 0)),
        ],
        out_specs=pl.BlockSpec((tm, tn), lambda i, j: (i, j)),
        out_shape=jax.ShapeDtypeStruct((TOKENS, n), F32),
        scratch_shapes=[pltpu.VMEM((tm, D_MODEL), BF16)],
        compiler_params=_params("arbitrary", "arbitrary"),
        name="in_proj",
    )(x, g, mod, mod, w)


def _prep_kernel(qa_ref, qi_ref, mql_ref, kv_ref, mkvl_ref, kiwi_ref,
                 ca_ref, sa_ref, ci_ref, si_ref,
                 gqa_ref, gka_ref, gmq_ref, gmkv_ref,
                 qa_o, ka_o, va_o, qi_o, ki2_o, wi_o, mqn_o, mkvn_o):
    tm = qa_ref.shape[0]
    lane = lax.broadcasted_iota(jnp.int32, (tm, LANES), 1)
    ca, sa = ca_ref[0], sa_ref[0]
    ci, si = ci_ref[0], si_ref[0]
    rot_a = A_HEAD_DIM // ROT_FRACTION // 2
    rot_i = IDX_DIM // ROT_FRACTION // 2
    scale_a = A_HEAD_DIM ** -0.5

    for h in range(A_HEADS):
        t = qa_ref[:, h * LANES:(h + 1) * LANES]
        r = _rope(_rms(t, gqa_ref[...], A_HEAD_DIM), ca, sa, rot_a, lane)
        qa_o[h] = (r * scale_a).astype(BF16)
    for g in range(A_KV_HEADS):
        t = kv_ref[:, g * LANES:(g + 1) * LANES]
        r = _rope(_rms(t, gka_ref[...], A_HEAD_DIM), ca, sa, rot_a, lane)
        ka_o[g] = r.astype(BF16)
        va_o[g] = kv_ref[:, (A_KV_HEADS + g) * LANES:(A_KV_HEADS + g + 1) * LANES].astype(BF16)
    for p in range(IDX_HEADS // 2):
        t = qi_ref[:, p * LANES:(p + 1) * LANES]
        r = _rope(t, ci, si, rot_i, lane)
        qi_o[2 * p] = jnp.where(lane < IDX_DIM, r, 0.0).astype(BF16)
        qi_o[2 * p + 1] = jnp.where(lane < IDX_DIM, 0.0, r).astype(BF16)

    kiwi = kiwi_ref[...]
    ki = _rope(kiwi, ci, si, rot_i, lane)
    ki2_o[...] = jnp.where(lane < IDX_DIM, ki, pltpu.roll(ki, IDX_DIM, 1)).astype(BF16)
    wi_o[...] = kiwi * IDX_W_SCALE

    mqn_o[...] = _rms(mql_ref[...], gmq_ref[...], M_Q_LORA).astype(BF16)
    mkvn_o[...] = _rms(mkvl_ref[...], gmkv_ref[...], M_KV_LORA).astype(BF16)


def _prep(l, proj, tabs_cos, tabs_sin, g_qa, g_ka, g_mq_lat, g_mkv_lat):
    tm = 256
    per_b = SEQ // tm

    def col(width, start):
        idx = start // width
        return pl.BlockSpec((tm, width), lambda i: (i, idx))

    def tab(t):
        return pl.BlockSpec((1, tm, LANES), lambda i: (t, i, 0))

    def vec(width):
        return _layer_row_spec(l, width)

    def heads(n):
        return pl.BlockSpec((None, n, tm, LANES), lambda i: (i // per_b, 0, i % per_b, 0))

    def seq(width):
        return pl.BlockSpec((None, tm, width), lambda i: (i // per_b, i % per_b, 0))

    def tok(width):
        return pl.BlockSpec((tm, width), lambda i: (i, 0))

    return pl.pallas_call(
        _prep_kernel,
        grid=(TOKENS // tm,),
        in_specs=[
            col(1024, COL_QA), col(1024, COL_QI), col(512, COL_MQL), col(512, COL_KV),
            col(256, COL_MKVL), col(128, COL_KIWI),
            tab(0), tab(0), tab(1), tab(1),
            vec(A_HEAD_DIM), vec(A_HEAD_DIM), vec(M_Q_LORA), vec(M_KV_LORA),
        ],
        out_specs=[
            heads(A_HEADS), heads(A_KV_HEADS), heads(A_KV_HEADS),
            heads(IDX_HEADS), seq(LANES), seq(LANES),
            tok(M_Q_LORA), tok(M_KV_LORA),
        ],
        out_shape=[
            jax.ShapeDtypeStruct((BATCH, A_HEADS, SEQ, LANES), BF16),
            jax.ShapeDtypeStruct((BATCH, A_KV_HEADS, SEQ, LANES), BF16),
            jax.ShapeDtypeStruct((BATCH, A_KV_HEADS, SEQ, LANES), BF16),
            jax.ShapeDtypeStruct((BATCH, IDX_HEADS, SEQ, LANES), BF16),
            jax.ShapeDtypeStruct((BATCH, SEQ, LANES), BF16),
            jax.ShapeDtypeStruct((BATCH, SEQ, LANES), F32),
            jax.ShapeDtypeStruct((TOKENS, M_Q_LORA), BF16),
            jax.ShapeDtypeStruct((TOKENS, M_KV_LORA), BF16),
        ],
        compiler_params=_params("arbitrary"),
        name="prep",
    )(proj, proj, proj, proj, proj, proj, tabs_cos, tabs_sin, tabs_cos, tabs_sin,
      g_qa, g_ka, g_mq_lat, g_mkv_lat)


def _mla_up_kernel(mqn_ref, mkvn_ref, mkr_ref, wq_ref, wkv_ref, cm_ref, sm_ref,
                   gqn_ref, gqr_ref, gkn_ref, gkr_ref, q_o, k_o, v_o):
    tm = mqn_ref.shape[0]
    lane = lax.broadcasted_iota(jnp.int32, (tm, LANES), 1)
    low = lane < M_ROPE
    cm, sm = cm_ref[0], sm_ref[0]
    half = M_ROPE // 2
    scale_m = M_QK ** -0.5
    nope_w = M_HEADS * M_NOPE

    q = _dot(mqn_ref[...], wq_ref[...])
    kv = _dot(mkvn_ref[...], wkv_ref[...])

    for p in range(M_HEADS // 2):
        rp = q[:, nope_w + p * LANES: nope_w + (p + 1) * LANES]
        sq = rp * rp
        s_all = jnp.sum(sq, axis=-1, keepdims=True)
        s_lo = jnp.sum(jnp.where(low, sq, 0.0), axis=-1, keepdims=True)
        for e in range(2):
            h = 2 * p + e
            nope = q[:, h * M_NOPE:(h + 1) * M_NOPE]
            ss = jnp.sum(nope * nope, axis=-1, keepdims=True) + (s_lo if e == 0 else s_all - s_lo)
            rs = lax.rsqrt(ss * (1.0 / M_QK) + EPS)
            roped = _rope(rp * rs * gqr_ref[...], cm, sm, half, lane)
            if e == 1:
                roped = pltpu.roll(roped, M_ROPE, 1)
            q_o[h, :, 0:LANES] = (nope * rs * gqn_ref[...] * scale_m).astype(BF16)
            q_o[h, :, LANES:2 * LANES] = (jnp.where(low, roped, 0.0) * scale_m).astype(BF16)

    kr = mkr_ref[...]
    kr_ss = jnp.sum(kr * kr, axis=-1, keepdims=True)
    kr_roped = _rope(kr * gkr_ref[...], cm, sm, half, lane)
    kr_roped = jnp.where(low, kr_roped, 0.0)
    for h in range(M_HEADS):
        nope = kv[:, h * 2 * LANES: h * 2 * LANES + M_NOPE]
        ss = jnp.sum(nope * nope, axis=-1, keepdims=True) + kr_ss
        rs = lax.rsqrt(ss * (1.0 / M_QK) + EPS)
        k_o[h, :, 0:LANES] = (nope * rs * gkn_ref[...]).astype(BF16)
        k_o[h, :, LANES:2 * LANES] = (kr_roped * rs).astype(BF16)
        v_o[h] = kv[:, h * 2 * LANES + M_NOPE:(h + 1) * 2 * LANES].astype(BF16)


def _mla_up(l, mqn, mkvn, proj, wq, wkv, tabs_cos, tabs_sin, gqn, gqr, gkn, gkr):
    tm = 256
    per_b = SEQ // tm

    def vec():
        return _layer_row_spec(l, LANES)

    def heads(width):
        return pl.BlockSpec((None, M_HEADS, tm, width), lambda i: (i // per_b, 0, i % per_b, 0))

    return pl.pallas_call(
        _mla_up_kernel,
        grid=(TOKENS // tm,),
        in_specs=[
            pl.BlockSpec((tm, M_Q_LORA), lambda i: (i, 0)),
            pl.BlockSpec((tm, M_KV_LORA), lambda i: (i, 0)),
            pl.BlockSpec((tm, LANES), lambda i: (i, COL_MKR // LANES)),
            pl.BlockSpec((None,) + wq.shape[1:], lambda i: (l, 0, 0)),
            pl.BlockSpec((None,) + wkv.shape[1:], lambda i: (l, 0, 0)),
            pl.BlockSpec((1, tm, LANES), lambda i: (2, i, 0)),
            pl.BlockSpec((1, tm, LANES), lambda i: (2, i, 0)),
            vec(), vec(), vec(), vec(),
        ],
        out_specs=[heads(M_QK_PAD), heads(M_QK_PAD), heads(M_V)],
        out_shape=[
            jax.ShapeDtypeStruct((BATCH, M_HEADS, SEQ, M_QK_PAD), BF16),
            jax.ShapeDtypeStruct((BATCH, M_HEADS, SEQ, M_QK_PAD), BF16),
            jax.ShapeDtypeStruct((BATCH, M_HEADS, SEQ, M_V), BF16),
        ],
        compiler_params=_params("arbitrary"),
        name="mla_up",
    )(mqn, mkvn, proj, wq, wkv, tabs_cos, tabs_sin, gqn, gqr, gkn, gkr)


INT_MIN = -(2 ** 31)
KEY_MASKED = INT_MIN

DSA_TQ = 256
KEY_CHUNK = 512
SEARCH_ROWS = 128
SEARCH_UNROLL = 5
IDX_CHUNK = 256
IDX_ROWS = 64


def _dsa_index(i, n_chunks, qi_ref, ki2_ref, wi_ref, key_ref, d_refs, wb_ref):
    tq = qi_ref.shape[1]
    w = wi_ref[...]
    for h in range(IDX_HEADS):
        wb_ref[h] = jnp.broadcast_to(w[:, IDX_DIM + h:IDX_DIM + h + 1], (tq, LANES))

    def matmul(c, d_ref):
        start = pl.multiple_of(c * IDX_CHUNK, IDX_CHUNK)
        q_all = qi_ref[...].reshape(IDX_HEADS * tq, LANES)
        d_ref[...] = _dot_nt(q_all, ki2_ref[pl.ds(start, IDX_CHUNK), :])

    def head_sum(c, d_ref):
        for r0 in range(0, tq, IDX_ROWS):
            row_t = i * tq + r0 + lax.broadcasted_iota(jnp.int32, (IDX_ROWS, LANES), 0)
            for j0 in range(0, IDX_CHUNK, LANES):
                acc = jnp.zeros((IDX_ROWS, LANES), F32)
                for h in range(IDX_HEADS):
                    d = d_ref[h * tq + r0:h * tq + r0 + IDX_ROWS, j0:j0 + LANES]
                    acc = acc + jnp.maximum(d, 0.0) * wb_ref[h, r0:r0 + IDX_ROWS, :]
                bits = pltpu.bitcast(acc, jnp.int32)
                img = bits ^ ((bits >> 31) & jnp.int32(0x7FFFFFFF))
                col_t = c * IDX_CHUNK + j0 + lax.broadcasted_iota(jnp.int32, (IDX_ROWS, LANES), 1)
                key_ref[c, r0:r0 + IDX_ROWS, j0:j0 + LANES] = jnp.where(col_t <= row_t, img, KEY_MASKED)

    d0_ref, d1_ref = d_refs
    matmul(0, d0_ref)

    def pair(k, carry):
        c = 2 * k
        matmul(c + 1, d1_ref)
        head_sum(c, d0_ref)
        matmul(jnp.minimum(c + 2, n_chunks - 1), d0_ref)
        head_sum(c + 1, d1_ref)
        return carry

    lax.fori_loop(0, n_chunks // 2, pair, 0)


def _dsa_select(nk, key_ref, bias_ref):
    chunks = range(nk // IDX_CHUNK)
    n_sel = float(N_SEL)
    groups = list(range(0, key_ref.shape[1], SEARCH_ROWS))

    def count_ge(r0, cand):
        hits = None
        for c in chunks:
            hit = jnp.where(key_ref[c, r0:r0 + SEARCH_ROWS, :] >= cand, 1.0, 0.0)
            hits = hit if hits is None else hits + hit
        return jnp.sum(hits, axis=-1, keepdims=True)

    zero = jnp.zeros((SEARCH_ROWS, 1), jnp.int32)
    thr0 = tuple(jnp.where(count_ge(r0, zero) >= n_sel, jnp.int32(0), jnp.int32(INT_MIN))
                 for r0 in groups)

    def step(b, thrs):
        bit = lax.shift_left(jnp.int32(1), jnp.int32(30) - b)
        out = []
        for r0, thr in zip(groups, thrs):
            cand = thr | bit
            out.append(jnp.where(count_ge(r0, cand) >= n_sel, cand, thr))
        return tuple(out)

    thrs = lax.fori_loop(0, 30, step, thr0, unroll=SEARCH_UNROLL)
    thrs = step(jnp.int32(30), thrs)
    for r0, thr in zip(groups, thrs):
        floor = jnp.maximum(thr, jnp.int32(KEY_MASKED + 1))
        for c in chunks:
            sel = key_ref[c, r0:r0 + SEARCH_ROWS, :] >= floor
            bias_ref[r0:r0 + SEARCH_ROWS, c * IDX_CHUNK:(c + 1) * IDX_CHUNK] = jnp.where(sel, 0.0, -jnp.inf)


def _dsa_attend(nk, qa_ref, ka_ref, va_ref, bias_ref, o_ref, oh_ref):
    tq = qa_ref.shape[1]
    rep = A_HEADS // A_KV_HEADS

    def group(g, carry):
        k = ka_ref[g, 0:nk, :]
        v = va_ref[g, 0:nk, :]
        q = qa_ref[pl.ds(g * rep, rep)].reshape(rep * tq, LANES)
        logits = _dot_nt(q, k).reshape(rep, tq, nk) + bias_ref[:, 0:nk][None]
        m = jnp.max(logits, axis=-1, keepdims=True)
        p = jnp.exp(logits - m)
        l = jnp.sum(p, axis=-1, keepdims=True)
        o = _dot(p.astype(BF16).reshape(rep * tq, nk), v).reshape(rep, tq, LANES) / l
        oh_ref[pl.ds(g * rep, rep)] = o.astype(BF16)
        return carry

    lax.fori_loop(0, A_KV_HEADS, group, 0)
    for h in range(A_HEADS):
        o_ref[:, h * LANES:(h + 1) * LANES] = oh_ref[h]


def _dsa_kernel(qa_ref, ka_ref, va_ref, qi_ref, ki2_ref, wi_ref, o_ref,
                key_ref, bias_ref, d0_ref, d1_ref, wb_ref, oh_ref):
    i = pl.program_id(1)
    tq = DSA_TQ
    n_free = N_SEL // tq
    variant = i // (KEY_CHUNK // tq)

    for v in range(n_free):
        @pl.when(i == v)
        def _(v=v):
            nk = (v + 1) * tq
            row = i * tq + lax.broadcasted_iota(jnp.int32, (tq, nk), 0)
            col = lax.broadcasted_iota(jnp.int32, (tq, nk), 1)
            bias_ref[:, 0:nk] = jnp.where(col <= row, 0.0, -jnp.inf)
            _dsa_attend(nk, qa_ref, ka_ref, va_ref, bias_ref, o_ref, oh_ref)

    @pl.when(i >= n_free)
    def _():
        n_chunks = (variant + 1) * (KEY_CHUNK // IDX_CHUNK)
        _dsa_index(i, n_chunks, qi_ref, ki2_ref, wi_ref, key_ref, (d0_ref, d1_ref), wb_ref)

    for v in range(SEQ // KEY_CHUNK):
        @pl.when((i >= n_free) & (variant == v))
        def _(v=v):
            nk = (v + 1) * KEY_CHUNK
            _dsa_select(nk, key_ref, bias_ref)
            _dsa_attend(nk, qa_ref, ka_ref, va_ref, bias_ref, o_ref, oh_ref)


def _dsa(qa, ka, va, qi, ki2, wi):
    tq = DSA_TQ
    nq = SEQ // tq
    return pl.pallas_call(
        _dsa_kernel,
        grid=(BATCH, nq),
        in_specs=[
            pl.BlockSpec((None, A_HEADS, tq, LANES), lambda b, i: (b, 0, i, 0)),
            pl.BlockSpec((None, A_KV_HEADS, SEQ, LANES), lambda b, i: (b, 0, 0, 0)),
            pl.BlockSpec((None, A_KV_HEADS, SEQ, LANES), lambda b, i: (b, 0, 0, 0)),
            pl.BlockSpec((None, IDX_HEADS, tq, LANES), lambda b, i: (b, 0, i, 0)),
            pl.BlockSpec((None, SEQ, LANES), lambda b, i: (b, 0, 0)),
            pl.BlockSpec((None, tq, LANES), lambda b, i: (b, i, 0)),
        ],
        out_specs=pl.BlockSpec((tq, A_HEADS * A_HEAD_DIM), lambda b, i: (b * nq + i, 0)),
        out_shape=jax.ShapeDtypeStruct((TOKENS, A_HEADS * A_HEAD_DIM), BF16),
        scratch_shapes=[
            pltpu.VMEM((SEQ // IDX_CHUNK, tq, IDX_CHUNK), jnp.int32),
            pltpu.VMEM((tq, SEQ), F32),
            pltpu.VMEM((IDX_HEADS * tq, IDX_CHUNK), F32),
            pltpu.VMEM((IDX_HEADS * tq, IDX_CHUNK), F32),
            pltpu.VMEM((IDX_HEADS, tq, LANES), F32),
            pltpu.VMEM((A_HEADS, tq, LANES), BF16),
        ],
        compiler_params=_params("arbitrary", "arbitrary"),
        name="dsa",
    )(qa, ka, va, qi, ki2, wi)


MLA_TQ = 256
MLA_HEADS_PER_STEP = 4


def _mla_body(nk, i, q_ref, k_ref, v_ref, o_ref):
    tq = q_ref.shape[1]
    row = i * tq + lax.broadcasted_iota(jnp.int32, (tq, nk), 0)
    col = lax.broadcasted_iota(jnp.int32, (tq, nk), 1)
    causal = col <= row
    for h in range(q_ref.shape[0]):
        s = jnp.where(causal, _dot_nt(q_ref[h], k_ref[h, 0:nk, :]), -jnp.inf)
        m = jnp.max(s, axis=-1, keepdims=True)
        p = jnp.exp(s - m)
        l = jnp.sum(p, axis=-1, keepdims=True)
        o = _dot(p.astype(BF16), v_ref[h, 0:nk, :]) / l
        o_ref[:, h * M_V:(h + 1) * M_V] = o.astype(BF16)


def _mla_kernel(q_ref, k_ref, v_ref, o_ref):
    i = pl.program_id(2)
    per_chunk = KEY_CHUNK // MLA_TQ
    for v in range(SEQ // KEY_CHUNK):
        @pl.when(i // per_chunk == v)
        def _(v=v):
            _mla_body((v + 1) * KEY_CHUNK, i, q_ref, k_ref, v_ref, o_ref)


def _mla(q, k, v):
    tq, hg = MLA_TQ, MLA_HEADS_PER_STEP
    nq = SEQ // tq
    return pl.pallas_call(
        _mla_kernel,
        grid=(BATCH, M_HEADS // hg, nq),
        in_specs=[
            pl.BlockSpec((None, hg, tq, M_QK_PAD), lambda b, g, i: (b, g, i, 0)),
            pl.BlockSpec((None, hg, SEQ, M_QK_PAD), lambda b, g, i: (b, g, 0, 0)),
            pl.BlockSpec((None, hg, SEQ, M_V), lambda b, g, i: (b, g, 0, 0)),
        ],
        out_specs=pl.BlockSpec((tq, hg * M_V), lambda b, g, i: (b * nq + i, g)),
        out_shape=jax.ShapeDtypeStruct((TOKENS, M_HEADS * M_V), BF16),
        compiler_params=_params("arbitrary", "arbitrary", "arbitrary"),
        name="mla",
    )(q, k, v)


def _merge_kernel(oa_ref, ob_ref, ga_ref, gb_ref, wpa_ref, wpb_ref, wo_ref, x_ref, gt_ref, o_ref):
    a = _dot(oa_ref[...], wpa_ref[...])
    b = _dot(ob_ref[...], wpb_ref[...])
    merged = jax.nn.sigmoid(ga_ref[...]) * a + jax.nn.sigmoid(gb_ref[...]) * b
    y = _dot(merged.astype(BF16), wo_ref[...])
    o_ref[...] = x_ref[...] + gt_ref[...] * y


def _merge(l, oa, ob, proj, wpa, wpb, wo, x, mod):
    tm = 256
    per_b = SEQ // tm

    def resident(shape):
        return pl.BlockSpec((None,) + shape[1:], lambda i: (l, 0, 0), pipeline_mode=pl.Buffered(1))

    return pl.pallas_call(
        _merge_kernel,
        grid=(TOKENS // tm,),
        in_specs=[
            pl.BlockSpec((tm, A_HEADS * A_HEAD_DIM), lambda i: (i, 0)),
            pl.BlockSpec((tm, M_HEADS * M_V), lambda i: (i, 0)),
            pl.BlockSpec((tm, D_MODEL), lambda i: (i, COL_GATE_A // D_MODEL)),
            pl.BlockSpec((tm, D_MODEL), lambda i: (i, COL_GATE_B // D_MODEL)),
            resident(wpa.shape), resident(wpb.shape), resident(wo.shape),
            pl.BlockSpec((tm, D_MODEL), lambda i: (i, 0)),
            _mod_spec(l, 2, per_b),
        ],
        out_specs=pl.BlockSpec((tm, D_MODEL), lambda i: (i, 0)),
        out_shape=jax.ShapeDtypeStruct((TOKENS, D_MODEL), F32),
        compiler_params=_params("arbitrary"),
        name="merge",
    )(oa, ob, proj, proj, wpa, wpb, wo, x, mod)


HALO = 8


def _ffn_up_kernel(x_ref, g_ref, sc_ref, sh_ref, wg_ref, wv_ref, cg_ref, cv_ref, bg_ref, bv_ref,
                   o_ref, h_ref, halo_ref, buf_ref, *, tiles_per_seq):
    i, j = pl.program_id(0), pl.program_id(1)
    tm, tn = o_ref.shape

    @pl.when(j == 0)
    def _():
        h_ref[...] = _modnorm(x_ref[...], g_ref[...], sc_ref[...], sh_ref[...]).astype(BF16)

    seq_start = (i % tiles_per_seq) == 0

    @pl.when(seq_start)
    def _():
        buf_ref[0:HALO, :] = jnp.zeros((HALO, 2 * tn), F32)

    @pl.when(jnp.logical_not(seq_start))
    def _():
        buf_ref[0:HALO, :] = halo_ref[j]

    h = h_ref[...]
    buf_ref[HALO:HALO + tm, 0:tn] = _dot(h, wg_ref[...])
    buf_ref[HALO:HALO + tm, tn:2 * tn] = _dot(h, wv_ref[...])
    halo_ref[j] = buf_ref[tm:tm + HALO, :]

    def conv(lo, c_ref, b_ref):
        u0 = buf_ref[HALO:HALO + tm, lo:lo + tn]
        u1 = buf_ref[HALO - 1:HALO - 1 + tm, lo:lo + tn]
        u2 = buf_ref[HALO - 2:HALO - 2 + tm, lo:lo + tn]
        return b_ref[...] + c_ref[0:1, :] * u2 + c_ref[1:2, :] * u1 + c_ref[2:3, :] * u0

    gate = conv(0, cg_ref, bg_ref)
    val = conv(tn, cv_ref, bv_ref)
    o_ref[...] = (gate * jax.nn.sigmoid(gate) * val).astype(BF16)


def _ffn_up(l, x, g, mod, w_up, w_conv, b_conv):
    tm, tn = 1024, 512
    per_b = SEQ // tm
    nj = D_FF // tn
    kern = functools.partial(_ffn_up_kernel, tiles_per_seq=per_b)
    return pl.pallas_call(
        kern,
        grid=(TOKENS // tm, nj),
        in_specs=[
            pl.BlockSpec((tm, D_MODEL), lambda i, j: (i, 0)),
            _layer_row_spec(l, D_MODEL),
            _mod_spec(l, 4, per_b),
            _mod_spec(l, 3, per_b),
            pl.BlockSpec((None, D_MODEL, tn), lambda i, j: (l, 0, j)),
            pl.BlockSpec((None, D_MODEL, tn), lambda i, j: (l, 0, nj + j)),
            pl.BlockSpec((None, CONV_W, tn), lambda i, j: (l, 0, j)),
            pl.BlockSpec((None, CONV_W, tn), lambda i, j: (l, 0, nj + j)),
            pl.BlockSpec((None, 1, tn), lambda i, j: (l, 0, j)),
            pl.BlockSpec((None, 1, tn), lambda i, j: (l, 0, nj + j)),
        ],
        out_specs=pl.BlockSpec((tm, tn), lambda i, j: (i, j)),
        out_shape=jax.ShapeDtypeStruct((TOKENS, D_FF), BF16),
        scratch_shapes=[
            pltpu.VMEM((tm, D_MODEL), BF16),
            pltpu.VMEM((nj, HALO, 2 * tn), F32),
            pltpu.VMEM((HALO + tm, 2 * tn), F32),
        ],
        compiler_params=_params("arbitrary", "arbitrary"),
        name="ffn_up",
    )(x, g, mod, mod, w_up, w_up, w_conv, w_conv, b_conv, b_conv)


def _ffn_down_kernel(a_ref, w_ref, x_ref, gt_ref, o_ref):
    o_ref[...] = x_ref[...] + gt_ref[...] * _dot(a_ref[...], w_ref[...])


def _ffn_down(l, act, w_down, x, mod):
    tm, tn = 1024, 256
    per_b = SEQ // tm
    return pl.pallas_call(
        _ffn_down_kernel,
        grid=(TOKENS // tm, D_MODEL // tn),
        in_specs=[
            pl.BlockSpec((tm, D_FF), lambda i, j: (i, 0)),
            pl.BlockSpec((None, D_FF, tn), lambda i, j: (l, 0, j)),
            pl.BlockSpec((tm, tn), lambda i, j: (i, j)),
            _mod_spec(l, 5, per_b, tn),
        ],
        out_specs=pl.BlockSpec((tm, tn), lambda i, j: (i, j)),
        out_shape=jax.ShapeDtypeStruct((TOKENS, D_MODEL), F32),
        compiler_params=_params("arbitrary", "arbitrary"),
        name="ffn_down",
    )(act, w_down, x, mod)


def _pack_w_in(w_in):
    o = np.cumsum((0,) + (1024, 256, 256, 1024, 64, 16, 512, 256, 64, 2048, 2048))
    w_t = jnp.swapaxes(w_in, 1, 2)
    qa, ka, va, qi, ki, wi, mql, mkvl, mkr, ga, gb = (
        w_t[:, int(o[k]):int(o[k + 1])].astype(BF16) for k in range(11))
    z = lambda n: jnp.zeros((DEPTH, n, D_MODEL), BF16)
    packed = jnp.concatenate(
        [ga, gb, qa, qi, mql, ka, va, mkvl, ki, wi, z(48), mkr, z(64)], axis=1)
    assert packed.shape[1] == IN_PACKED
    return packed


def _pack_w_mq(w_mq_up):
    w = w_mq_up.reshape(DEPTH, M_Q_LORA, M_HEADS, M_QK)
    nope = w[..., :M_NOPE].reshape(DEPTH, M_Q_LORA, M_HEADS * M_NOPE)
    rope = w[..., M_NOPE:].reshape(DEPTH, M_Q_LORA, M_HEADS * M_ROPE)
    return jnp.concatenate([nope, rope], axis=-1).astype(BF16)


def _rope_consts():
    def inv(rot):
        return ROPE_THETA ** (-jnp.arange(0, rot, 2, dtype=F32) / rot)

    def lanes(v, group):
        rep = jnp.concatenate([v, v])
        pad = jnp.zeros((group - rep.shape[0],), F32)
        return jnp.tile(jnp.concatenate([rep, pad]), LANES // group)

    def sgn(half, group):
        v = jnp.concatenate([-jnp.ones((half,), F32), jnp.ones((half,), F32)])
        pad = jnp.zeros((group - 2 * half,), F32)
        return jnp.tile(jnp.concatenate([v, pad]), LANES // group)

    rot_a = A_HEAD_DIM // ROT_FRACTION
    rot_i = IDX_DIM // ROT_FRACTION
    inv3 = jnp.stack([lanes(inv(rot_a), LANES), lanes(inv(rot_i), IDX_DIM), lanes(inv(M_ROPE), M_ROPE)])
    sgn3 = jnp.stack([sgn(rot_a // 2, LANES), sgn(rot_i // 2, IDX_DIM), sgn(M_ROPE // 2, M_ROPE)])
    return inv3.reshape(3, 1, LANES), sgn3.reshape(3, 1, LANES)


def kernel(x, c, positions, g_attn, g_ffn, w_ada, b_ada, w_in, g_qa, g_ka, g_mq_lat, w_mq_up,
           g_mkv_lat, w_mkv_up, g_qm, g_km, w_pa, w_pb, w_o, w_up, w_conv, b_conv, w_down):
    pos_col = positions.astype(F32).reshape(TOKENS, 1)
    inv3, sgn3 = _rope_consts()
    tabs_cos, tabs_sin = _rope_tables(pos_col, inv3, sgn3)

    c8 = jnp.pad(c, ((0, 8 - BATCH), (0, 0)))
    mod = _ada(c8, w_ada, b_ada)[:, :BATCH].reshape(DEPTH, BATCH, N_ADA, 1, D_MODEL)

    w_in_p = _pack_w_in(w_in)
    w_mq_p = _pack_w_mq(w_mq_up)
    w_mkv_b = w_mkv_up.astype(BF16)
    w_pa_b, w_pb_b, w_o_b = w_pa.astype(BF16), w_pb.astype(BF16), w_o.astype(BF16)
    w_up_b, w_down_b = w_up.astype(BF16), w_down.astype(BF16)

    rows = lambda v: v.reshape(DEPTH, 1, -1)
    rope_gain = lambda v: rows(jnp.tile(v[:, M_NOPE:], (1, 2)))
    g_attn_r, g_ffn_r, b_conv_r = rows(g_attn), rows(g_ffn), rows(b_conv)
    g_qa_r, g_ka_r, g_mq_r, g_mkv_r = rows(g_qa), rows(g_ka), rows(g_mq_lat), rows(g_mkv_lat)
    gqn, gqr = rows(g_qm[:, :M_NOPE]), rope_gain(g_qm)
    gkn, gkr = rows(g_km[:, :M_NOPE]), rope_gain(g_km)

    xf = x.reshape(TOKENS, D_MODEL)
    for l in range(DEPTH):
        proj = _in_proj(l, xf, g_attn_r, mod, w_in_p)
        qa, ka, va, qi, ki2, wi, mqn, mkvn = _prep(
            l, proj, tabs_cos, tabs_sin, g_qa_r, g_ka_r, g_mq_r, g_mkv_r)
        mq, mk, mv = _mla_up(l, mqn, mkvn, proj, w_mq_p, w_mkv_b, tabs_cos, tabs_sin,
                             gqn, gqr, gkn, gkr)
        o_a = _dsa(qa, ka, va, qi, ki2, wi)
        o_b = _mla(mq, mk, mv)
        xf = _merge(l, o_a, o_b, proj, w_pa_b, w_pb_b, w_o_b, xf, mod)
        act = _ffn_up(l, xf, g_ffn_r, mod, w_up_b, w_conv, b_conv_r)
        xf = _ffn_down(l, act, w_down_b, xf, mod)
    return xf.reshape(BATCH, SEQ, D_MODEL)
```

```python
import functools

import jax
import jax.numpy as jnp
import numpy as np
from jax import lax
from jax.experimental import pallas as pl
from jax.experimental.pallas import tpu as pltpu

D_MODEL = 2048
BATCH = 4
SEQ = 2048
DEPTH = 4
A_HEADS = 8
A_KV_HEADS = 2
A_HEAD_DIM = 128
IDX_HEADS = 16
IDX_DIM = 64
TOPK_MAX = 256
M_HEADS = 8
M_Q_LORA = 512
M_KV_LORA = 256
M_NOPE = 128
M_ROPE = 64
M_V = 128
D_FF = 5632
CONV_W = 3
ROPE_THETA = 500000.0
ROT_FRACTION = 4
EPS = 1e-6
N_ADA = 6
IDX_W_SCALE = (IDX_HEADS * IDX_DIM) ** -0.5
N_SEL = min(TOPK_MAX, SEQ // 4)
TOKENS = BATCH * SEQ
LANES = 128
M_QK = M_NOPE + M_ROPE
M_QK_PAD = 256

BF16 = jnp.bfloat16
F32 = jnp.float32

COL_GATE_A = 0
COL_GATE_B = 2048
COL_QA = 4096
COL_QI = 5120
COL_MQL = 6144
COL_KV = 6656
COL_MKVL = 7168
COL_KIWI = 7424
COL_MKR = 7552
IN_PACKED = 7680

VMEM_LIMIT = 56 * 1024 * 1024


def _params(*sem):
    return pltpu.CompilerParams(dimension_semantics=sem, vmem_limit_bytes=VMEM_LIMIT)


def _dot(a, b):
    return jnp.dot(a, b, preferred_element_type=F32)


def _dot_nt(a, b):
    return lax.dot_general(a, b, (((1,), (1,)), ((), ())), preferred_element_type=F32)


def _ada_kernel(c_ref, w_ref, b_ref, o_ref):
    c = c_ref[...]
    c_act = (c * jax.nn.sigmoid(c)).astype(BF16)
    o_ref[0] = _dot(c_act, w_ref[0].astype(BF16)) + b_ref[0]


def _ada(c8, w_ada, b_ada):
    tn = 1024
    n = N_ADA * D_MODEL
    return pl.pallas_call(
        _ada_kernel,
        grid=(DEPTH, n // tn),
        in_specs=[
            pl.BlockSpec((8, D_MODEL), lambda l, j: (0, 0)),
            pl.BlockSpec((1, D_MODEL, tn), lambda l, j: (l, 0, j)),
            pl.BlockSpec((1, 1, tn), lambda l, j: (l, 0, j)),
        ],
        out_specs=pl.BlockSpec((1, 8, tn), lambda l, j: (l, 0, j)),
        out_shape=jax.ShapeDtypeStruct((DEPTH, 8, n), F32),
        compiler_params=_params("arbitrary", "arbitrary"),
        name="ada",
    )(c8, w_ada, b_ada.reshape(DEPTH, 1, n))


def _rope_table_kernel(pos_ref, inv_ref, sgn_ref, cos_ref, sin_ref):
    ang = pos_ref[...] * inv_ref[0]
    cos_ref[0] = jnp.cos(ang)
    sin_ref[0] = jnp.sin(ang) * sgn_ref[0]


def _rope_tables(pos_col, inv3, sgn3):
    tm = 512
    return pl.pallas_call(
        _rope_table_kernel,
        grid=(3, TOKENS // tm),
        in_specs=[
            pl.BlockSpec((tm, 1), lambda t, i: (i, 0)),
            pl.BlockSpec((1, 1, LANES), lambda t, i: (t, 0, 0)),
            pl.BlockSpec((1, 1, LANES), lambda t, i: (t, 0, 0)),
        ],
        out_specs=[
            pl.BlockSpec((1, tm, LANES), lambda t, i: (t, i, 0)),
            pl.BlockSpec((1, tm, LANES), lambda t, i: (t, i, 0)),
        ],
        out_shape=[jax.ShapeDtypeStruct((3, TOKENS, LANES), F32)] * 2,
        compiler_params=_params("arbitrary", "arbitrary"),
        name="rope_tables",
    )(pos_col, inv3, sgn3)


def _rope(t, cos, sin_signed, half, lane):
    fwd = pltpu.roll(t, LANES - half, 1)
    bwd = pltpu.roll(t, half, 1)
    swapped = jnp.where((lane % (2 * half)) < half, fwd, bwd)
    return t * cos + swapped * sin_signed


def _rms(t, g, width):
    ms = jnp.sum(t * t, axis=-1, keepdims=True) * (1.0 / width)
    return t * lax.rsqrt(ms + EPS) * g


def _modnorm(x, g, sc, sh):
    ms = jnp.mean(x * x, axis=-1, keepdims=True)
    y = x * lax.rsqrt(ms + EPS) * g
    return y * (1.0 + sc) + sh


def _in_proj_kernel(x_ref, g_ref, sc_ref, sh_ref, w_ref, o_ref, h_ref):
    @pl.when(pl.program_id(1) == 0)
    def _():
        h_ref[...] = _modnorm(x_ref[...], g_ref[...], sc_ref[...], sh_ref[...]).astype(BF16)

    o_ref[...] = _dot_nt(h_ref[...], w_ref[...])


def _mod_spec(l, k, per_b, width=D_MODEL):
    if width == D_MODEL:
        return pl.BlockSpec((None, None, None, 1, D_MODEL), lambda i, *_: (l, i // per_b, k, 0, 0))
    return pl.BlockSpec((None, None, None, 1, width), lambda i, j: (l, i // per_b, k, 0, j))


def _layer_row_spec(l, width):
    return pl.BlockSpec((None, 1, width), lambda *_: (l, 0, 0))


def _in_proj(l, x, g, mod, w):
    tm, tn = 1024, 768
    per_b = SEQ // tm
    n = w.shape[1]
    return pl.pallas_call(
        _in_proj_kernel,
        grid=(TOKENS // tm, n // tn),
        in_specs=[
            pl.BlockSpec((tm, D_MODEL), lambda i, j: (i, 0)),
            _layer_row_spec(l, D_MODEL),
            _mod_spec(l, 1, per_b),
            _mod_spec(l, 0, per_b),
            pl.BlockSpec((None, tn, D_MODEL), lambda i, j: (l, j, 0)),
        ],
        out_specs=pl.BlockSpec((tm, tn), lambda i, j: (i, j)),
        out_shape=jax.ShapeDtypeStruct((TOKENS, n), F32),
        scratch_shapes=[pltpu.VMEM((tm, D_MODEL), BF16)],
        compiler_params=_params("arbitrary", "arbitrary"),
        name="in_proj",
    )(x, g, mod, mod, w)


def _prep_kernel(qa_ref, qi_ref, mql_ref, kv_ref, mkvl_ref, kiwi_ref,
                 ca_ref, sa_ref, ci_ref, si_ref,
                 gqa_ref, gka_ref, gmq_ref, gmkv_ref,
                 qa_o, ka_o, va_o, qi_o, ki2_o, wi_o, mqn_o, mkvn_o):
    tm = qa_ref.shape[0]
    lane = lax.broadcasted_iota(jnp.int32, (tm, LANES), 1)
    ca, sa = ca_ref[0], sa_ref[0]
    ci, si = ci_ref[0], si_ref[0]
    rot_a = A_HEAD_DIM // ROT_FRACTION // 2
    rot_i = IDX_DIM // ROT_FRACTION // 2
    scale_a = A_HEAD_DIM ** -0.5

    for h in range(A_HEADS):
        t = qa_ref[:, h * LANES:(h + 1) * LANES]
        r = _rope(_rms(t, gqa_ref[...], A_HEAD_DIM), ca, sa, rot_a, lane)
        qa_o[h] = (r * scale_a).astype(BF16)
    for g in range(A_KV_HEADS):
        t = kv_ref[:, g * LANES:(g + 1) * LANES]
        r = _rope(_rms(t, gka_ref[...], A_HEAD_DIM), ca, sa, rot_a, lane)
        ka_o[g] = r.astype(BF16)
        va_o[g] = kv_ref[:, (A_KV_HEADS + g) * LANES:(A_KV_HEADS + g + 1) * LANES].astype(BF16)
    for p in range(IDX_HEADS // 2):
        t = qi_ref[:, p * LANES:(p + 1) * LANES]
        r = _rope(t, ci, si, rot_i, lane)
        qi_o[2 * p] = jnp.where(lane < IDX_DIM, r, 0.0).astype(BF16)
        qi_o[2 * p + 1] = jnp.where(lane < IDX_DIM, 0.0, r).astype(BF16)

    kiwi = kiwi_ref[...]
    ki = _rope(kiwi, ci, si, rot_i, lane)
    ki2_o[...] = jnp.where(lane < IDX_DIM, ki, pltpu.roll(ki, IDX_DIM, 1)).astype(BF16)
    wi_o[...] = kiwi * IDX_W_SCALE

    mqn_o[...] = _rms(mql_ref[...], gmq_ref[...], M_Q_LORA).astype(BF16)
    mkvn_o[...] = _rms(mkvl_ref[...], gmkv_ref[...], M_KV_LORA).astype(BF16)


def _prep(l, proj, tabs_cos, tabs_sin, g_qa, g_ka, g_mq_lat, g_mkv_lat):
    tm = 256
    per_b = SEQ // tm

    def col(width, start):
        idx = start // width
        return pl.BlockSpec((tm, width), lambda i: (i, idx))

    def tab(t):
        return pl.BlockSpec((1, tm, LANES), lambda i: (t, i, 0))

    def vec(width):
        return _layer_row_spec(l, width)

    def heads(n):
        return pl.BlockSpec((None, n, tm, LANES), lambda i: (i // per_b, 0, i % per_b, 0))

    def seq(width):
        return pl.BlockSpec((None, tm, width), lambda i: (i // per_b, i % per_b, 0))

    def tok(width):
        return pl.BlockSpec((tm, width), lambda i: (i, 0))

    return pl.pallas_call(
        _prep_kernel,
        grid=(TOKENS // tm,),
        in_specs=[
            col(1024, COL_QA), col(1024, COL_QI), col(512, COL_MQL), col(512, COL_KV),
            col(256, COL_MKVL), col(128, COL_KIWI),
            tab(0), tab(0), tab(1), tab(1),
            vec(A_HEAD_DIM), vec(A_HEAD_DIM), vec(M_Q_LORA), vec(M_KV_LORA),
        ],
        out_specs=[
            heads(A_HEADS), heads(A_KV_HEADS), heads(A_KV_HEADS),
            heads(IDX_HEADS), seq(LANES), seq(LANES),
            tok(M_Q_LORA), tok(M_KV_LORA),
        ],
        out_shape=[
            jax.ShapeDtypeStruct((BATCH, A_HEADS, SEQ, LANES), BF16),
            jax.ShapeDtypeStruct((BATCH, A_KV_HEADS, SEQ, LANES), BF16),
            jax.ShapeDtypeStruct((BATCH, A_KV_HEADS, SEQ, LANES), BF16),
            jax.ShapeDtypeStruct((BATCH, IDX_HEADS, SEQ, LANES), BF16),
            jax.ShapeDtypeStruct((BATCH, SEQ, LANES), BF16),
            jax.ShapeDtypeStruct((BATCH, SEQ, LANES), F32),
            jax.ShapeDtypeStruct((TOKENS, M_Q_LORA), BF16),
            jax.ShapeDtypeStruct((TOKENS, M_KV_LORA), BF16),
        ],
        compiler_params=_params("arbitrary"),
        name="prep",
    )(proj, proj, proj, proj, proj, proj, tabs_cos, tabs_sin, tabs_cos, tabs_sin,
      g_qa, g_ka, g_mq_lat, g_mkv_lat)


def _mla_up_kernel(mqn_ref, mkvn_ref, mkr_ref, wq_ref, wkv_ref, cm_ref, sm_ref,
                   gqn_ref, gqr_ref, gkn_ref, gkr_ref, q_o, k_o, v_o):
    tm = mqn_ref.shape[0]
    lane = lax.broadcasted_iota(jnp.int32, (tm, LANES), 1)
    low = lane < M_ROPE
    cm, sm = cm_ref[0], sm_ref[0]
    half = M_ROPE // 2
    scale_m = M_QK ** -0.5
    nope_w = M_HEADS * M_NOPE

    q = _dot(mqn_ref[...], wq_ref[...])
    kv = _dot(mkvn_ref[...], wkv_ref[...])

    for p in range(M_HEADS // 2):
        rp = q[:, nope_w + p * LANES: nope_w + (p + 1) * LANES]
        sq = rp * rp
        s_all = jnp.sum(sq, axis=-1, keepdims=True)
        s_lo = jnp.sum(jnp.where(low, sq, 0.0), axis=-1, keepdims=True)
        for e in range(2):
            h = 2 * p + e
            nope = q[:, h * M_NOPE:(h + 1) * M_NOPE]
            ss = jnp.sum(nope * nope, axis=-1, keepdims=True) + (s_lo if e == 0 else s_all - s_lo)
            rs = lax.rsqrt(ss * (1.0 / M_QK) + EPS)
            roped = _rope(rp * rs * gqr_ref[...], cm, sm, half, lane)
            if e == 1:
                roped = pltpu.roll(roped, M_ROPE, 1)
            q_o[h, :, 0:LANES] = (nope * rs * gqn_ref[...] * scale_m).astype(BF16)
            q_o[h, :, LANES:2 * LANES] = (jnp.where(low, roped, 0.0) * scale_m).astype(BF16)

    kr = mkr_ref[...]
    kr_ss = jnp.sum(kr * kr, axis=-1, keepdims=True)
    kr_roped = _rope(kr * gkr_ref[...], cm, sm, half, lane)
    kr_roped = jnp.where(low, kr_roped, 0.0)
    for h in range(M_HEADS):
        nope = kv[:, h * 2 * LANES: h * 2 * LANES + M_NOPE]
        ss = jnp.sum(nope * nope, axis=-1, keepdims=True) + kr_ss
        rs = lax.rsqrt(ss * (1.0 / M_QK) + EPS)
        k_o[h, :, 0:LANES] = (nope * rs * gkn_ref[...]).astype(BF16)
        k_o[h, :, LANES:2 * LANES] = (kr_roped * rs).astype(BF16)
        v_o[h] = kv[:, h * 2 * LANES + M_NOPE:(h + 1) * 2 * LANES].astype(BF16)


def _mla_up(l, mqn, mkvn, proj, wq, wkv, tabs_cos, tabs_sin, gqn, gqr, gkn, gkr):
    tm = 256
    per_b = SEQ // tm

    def vec():
        return _layer_row_spec(l, LANES)

    def heads(width):
        return pl.BlockSpec((None, M_HEADS, tm, width), lambda i: (i // per_b, 0, i % per_b, 0))

    return pl.pallas_call(
        _mla_up_kernel,
        grid=(TOKENS // tm,),
        in_specs=[
            pl.BlockSpec((tm, M_Q_LORA), lambda i: (i, 0)),
            pl.BlockSpec((tm, M_KV_LORA), lambda i: (i, 0)),
            pl.BlockSpec((tm, LANES), lambda i: (i, COL_MKR // LANES)),
            pl.BlockSpec((None,) + wq.shape[1:], lambda i: (l, 0, 0)),
            pl.BlockSpec((None,) + wkv.shape[1:], lambda i: (l, 0, 0)),
            pl.BlockSpec((1, tm, LANES), lambda i: (2, i, 0)),
            pl.BlockSpec((1, tm, LANES), lambda i: (2, i, 0)),
            vec(), vec(), vec(), vec(),
        ],
        out_specs=[heads(M_QK_PAD), heads(M_QK_PAD), heads(M_V)],
        out_shape=[
            jax.ShapeDtypeStruct((BATCH, M_HEADS, SEQ, M_QK_PAD), BF16),
            jax.ShapeDtypeStruct((BATCH, M_HEADS, SEQ, M_QK_PAD), BF16),
            jax.ShapeDtypeStruct((BATCH, M_HEADS, SEQ, M_V), BF16),
        ],
        compiler_params=_params("arbitrary"),
        name="mla_up",
    )(mqn, mkvn, proj, wq, wkv, tabs_cos, tabs_sin, gqn, gqr, gkn, gkr)


INT_MIN = -(2 ** 31)
KEY_MASKED = INT_MIN

DSA_TQ = 256
KEY_CHUNK = 512
SEARCH_ROWS = 128
SEARCH_UNROLL = 5
IDX_CHUNK = 256
IDX_ROWS = 64
ATT_SPLIT = 2


def _dsa_index(i, n_chunks, qi_ref, ki2_ref, wi_ref, key_ref, d_refs, wb_ref):
    tq = qi_ref.shape[1]
    w = wi_ref[...]
    for h in range(IDX_HEADS):
        wb_ref[h] = jnp.broadcast_to(w[:, IDX_DIM + h:IDX_DIM + h + 1], (tq, LANES))

    def matmul(c, d_ref):
        start = pl.multiple_of(c * IDX_CHUNK, IDX_CHUNK)
        q_all = qi_ref[...].reshape(IDX_HEADS * tq, LANES)
        d_ref[...] = _dot_nt(q_all, ki2_ref[pl.ds(start, IDX_CHUNK), :])

    def head_sum(c, d_ref):
        for r0 in range(0, tq, IDX_ROWS):
            row_t = i * tq + r0 + lax.broadcasted_iota(jnp.int32, (IDX_ROWS, LANES), 0)
            for j0 in range(0, IDX_CHUNK, LANES):
                acc = jnp.zeros((IDX_ROWS, LANES), F32)
                for h in range(IDX_HEADS):
                    d = d_ref[h * tq + r0:h * tq + r0 + IDX_ROWS, j0:j0 + LANES]
                    acc = acc + jnp.maximum(d, 0.0) * wb_ref[h, r0:r0 + IDX_ROWS, :]
                bits = pltpu.bitcast(acc, jnp.int32)
                img = bits ^ ((bits >> 31) & jnp.int32(0x7FFFFFFF))
                col_t = c * IDX_CHUNK + j0 + lax.broadcasted_iota(jnp.int32, (IDX_ROWS, LANES), 1)
                key_ref[c, r0:r0 + IDX_ROWS, j0:j0 + LANES] = jnp.where(col_t <= row_t, img, KEY_MASKED)

    d0_ref, d1_ref = d_refs
    matmul(0, d0_ref)

    def pair(k, carry):
        c = 2 * k
        matmul(c + 1, d1_ref)
        head_sum(c, d0_ref)
        matmul(jnp.minimum(c + 2, n_chunks - 1), d0_ref)
        head_sum(c + 1, d1_ref)
        return carry

    lax.fori_loop(0, n_chunks // 2, pair, 0)


def _dsa_select(nk, key_ref, bias_ref):
    chunks = range(nk // IDX_CHUNK)
    n_sel = float(N_SEL)
    groups = list(range(0, key_ref.shape[1], SEARCH_ROWS))

    def count_ge(r0, cand):
        hits = None
        for c in chunks:
            hit = jnp.where(key_ref[c, r0:r0 + SEARCH_ROWS, :] >= cand, 1.0, 0.0)
            hits = hit if hits is None else hits + hit
        return jnp.sum(hits, axis=-1, keepdims=True)

    zero = jnp.zeros((SEARCH_ROWS, 1), jnp.int32)
    thr0 = tuple(jnp.where(count_ge(r0, zero) >= n_sel, jnp.int32(0), jnp.int32(INT_MIN))
                 for r0 in groups)

    def step(b, thrs):
        bit = lax.shift_left(jnp.int32(1), jnp.int32(30) - b)
        out = []
        for r0, thr in zip(groups, thrs):
            cand = thr | bit
            out.append(jnp.where(count_ge(r0, cand) >= n_sel, cand, thr))
        return tuple(out)

    thrs = lax.fori_loop(0, 30, step, thr0, unroll=SEARCH_UNROLL)
    thrs = step(jnp.int32(30), thrs)
    for r0, thr in zip(groups, thrs):
        floor = jnp.maximum(thr, jnp.int32(KEY_MASKED + 1))
        for c in chunks:
            sel = key_ref[c, r0:r0 + SEARCH_ROWS, :] >= floor
            bias_ref[r0:r0 + SEARCH_ROWS, c * IDX_CHUNK:(c + 1) * IDX_CHUNK] = jnp.where(sel, 0.0, -jnp.inf)


def _dsa_attend(nk, qa_ref, ka_ref, va_ref, bias_ref, o_ref, oh_ref):
    tq = qa_ref.shape[1]
    rep = A_HEADS // A_KV_HEADS

    sub = rep // ATT_SPLIT

    def group(g, carry):
        k = ka_ref[g, 0:nk, :]
        v = va_ref[g, 0:nk, :]
        for s in range(ATT_SPLIT):
            h0 = g * rep + s * sub
            q = qa_ref[pl.ds(h0, sub)].reshape(sub * tq, LANES)
            logits = _dot_nt(q, k).reshape(sub, tq, nk) + bias_ref[:, 0:nk][None]
            m = jnp.max(logits, axis=-1, keepdims=True)
            p = jnp.exp(logits - m)
            l = jnp.sum(p, axis=-1, keepdims=True)
            o = _dot(p.astype(BF16).reshape(sub * tq, nk), v).reshape(sub, tq, LANES) / l
            oh_ref[pl.ds(h0, sub)] = o.astype(BF16)
        return carry

    lax.fori_loop(0, A_KV_HEADS, group, 0)
    for h in range(A_HEADS):
        o_ref[:, h * LANES:(h + 1) * LANES] = oh_ref[h]


def _dsa_kernel(qa_ref, ka_ref, va_ref, qi_ref, ki2_ref, wi_ref, o_ref,
                key_ref, bias_ref, d0_ref, d1_ref, wb_ref, oh_ref):
    i = pl.program_id(1)
    tq = DSA_TQ
    n_free = N_SEL // tq
    variant = i // (KEY_CHUNK // tq)

    for v in range(n_free):
        @pl.when(i == v)
        def _(v=v):
            nk = (v + 1) * tq
            row = i * tq + lax.broadcasted_iota(jnp.int32, (tq, nk), 0)
            col = lax.broadcasted_iota(jnp.int32, (tq, nk), 1)
            bias_ref[:, 0:nk] = jnp.where(col <= row, 0.0, -jnp.inf)
            _dsa_attend(nk, qa_ref, ka_ref, va_ref, bias_ref, o_ref, oh_ref)

    @pl.when(i >= n_free)
    def _():
        n_chunks = (variant + 1) * (KEY_CHUNK // IDX_CHUNK)
        _dsa_index(i, n_chunks, qi_ref, ki2_ref, wi_ref, key_ref, (d0_ref, d1_ref), wb_ref)

    for v in range(SEQ // KEY_CHUNK):
        @pl.when((i >= n_free) & (variant == v))
        def _(v=v):
            nk = (v + 1) * KEY_CHUNK
            _dsa_select(nk, key_ref, bias_ref)
            _dsa_attend(nk, qa_ref, ka_ref, va_ref, bias_ref, o_ref, oh_ref)


def _dsa(qa, ka, va, qi, ki2, wi):
    tq = DSA_TQ
    nq = SEQ // tq
    return pl.pallas_call(
        _dsa_kernel,
        grid=(BATCH, nq),
        in_specs=[
            pl.BlockSpec((None, A_HEADS, tq, LANES), lambda b, i: (b, 0, i, 0)),
            pl.BlockSpec((None, A_KV_HEADS, SEQ, LANES), lambda b, i: (b, 0, 0, 0)),
            pl.BlockSpec((None, A_KV_HEADS, SEQ, LANES), lambda b, i: (b, 0, 0, 0)),
            pl.BlockSpec((None, IDX_HEADS, tq, LANES), lambda b, i: (b, 0, i, 0)),
            pl.BlockSpec((None, SEQ, LANES), lambda b, i: (b, 0, 0)),
            pl.BlockSpec((None, tq, LANES), lambda b, i: (b, i, 0)),
        ],
        out_specs=pl.BlockSpec((tq, A_HEADS * A_HEAD_DIM), lambda b, i: (b * nq + i, 0)),
        out_shape=jax.ShapeDtypeStruct((TOKENS, A_HEADS * A_HEAD_DIM), BF16),
        scratch_shapes=[
            pltpu.VMEM((SEQ // IDX_CHUNK, tq, IDX_CHUNK), jnp.int32),
            pltpu.VMEM((tq, SEQ), F32),
            pltpu.VMEM((IDX_HEADS * tq, IDX_CHUNK), F32),
            pltpu.VMEM((IDX_HEADS * tq, IDX_CHUNK), F32),
            pltpu.VMEM((IDX_HEADS, tq, LANES), F32),
            pltpu.VMEM((A_HEADS, tq, LANES), BF16),
        ],
        compiler_params=_params("arbitrary", "arbitrary"),
        name="dsa",
    )(qa, ka, va, qi, ki2, wi)


MLA_TQ = 256
MLA_HEADS_PER_STEP = 4


def _mla_body(nk, i, q_ref, k_ref, v_ref, o_ref):
    tq = q_ref.shape[1]
    n0 = nk - KEY_CHUNK
    row = i * tq + lax.broadcasted_iota(jnp.int32, (tq, KEY_CHUNK), 0)
    col = n0 + lax.broadcasted_iota(jnp.int32, (tq, KEY_CHUNK), 1)
    causal = col <= row
    for h in range(q_ref.shape[0]):
        q = q_ref[h]
        s1 = jnp.where(causal, _dot_nt(q, k_ref[h, n0:nk, :]), -jnp.inf)
        m = jnp.max(s1, axis=-1, keepdims=True)
        if n0:
            s0 = _dot_nt(q, k_ref[h, 0:n0, :])
            m = jnp.maximum(m, jnp.max(s0, axis=-1, keepdims=True))
        p1 = jnp.exp(s1 - m)
        l = jnp.sum(p1, axis=-1, keepdims=True)
        o = _dot(p1.astype(BF16), v_ref[h, n0:nk, :])
        if n0:
            p0 = jnp.exp(s0 - m)
            l = l + jnp.sum(p0, axis=-1, keepdims=True)
            o = o + _dot(p0.astype(BF16), v_ref[h, 0:n0, :])
        o_ref[:, h * M_V:(h + 1) * M_V] = (o / l).astype(BF16)


def _mla_kernel(q_ref, k_ref, v_ref, o_ref):
    i = pl.program_id(2)
    per_chunk = KEY_CHUNK // MLA_TQ
    for v in range(SEQ // KEY_CHUNK):
        @pl.when(i // per_chunk == v)
        def _(v=v):
            _mla_body((v + 1) * KEY_CHUNK, i, q_ref, k_ref, v_ref, o_ref)


def _mla(q, k, v):
    tq, hg = MLA_TQ, MLA_HEADS_PER_STEP
    nq = SEQ // tq
    return pl.pallas_call(
        _mla_kernel,
        grid=(BATCH, M_HEADS // hg, nq),
        in_specs=[
            pl.BlockSpec((None, hg, tq, M_QK_PAD), lambda b, g, i: (b, g, i, 0)),
            pl.BlockSpec((None, hg, SEQ, M_QK_PAD), lambda b, g, i: (b, g, 0, 0)),
            pl.BlockSpec((None, hg, SEQ, M_V), lambda b, g, i: (b, g, 0, 0)),
        ],
        out_specs=pl.BlockSpec((tq, hg * M_V), lambda b, g, i: (b * nq + i, g)),
        out_shape=jax.ShapeDtypeStruct((TOKENS, M_HEADS * M_V), BF16),
        compiler_params=_params("arbitrary", "arbitrary", "arbitrary"),
        name="mla",
    )(q, k, v)


def _merge_kernel(oa_ref, ob_ref, ga_ref, gb_ref, wpa_ref, wpb_ref, wo_ref, x_ref, gt_ref, o_ref):
    a = _dot(oa_ref[...], wpa_ref[...])
    b = _dot(ob_ref[...], wpb_ref[...])
    merged = jax.nn.sigmoid(ga_ref[...]) * a + jax.nn.sigmoid(gb_ref[...]) * b
    y = _dot(merged.astype(BF16), wo_ref[...])
    o_ref[...] = x_ref[...] + gt_ref[...] * y


def _merge(l, oa, ob, proj, wpa, wpb, wo, x, mod):
    tm = 256
    per_b = SEQ // tm

    def resident(shape):
        return pl.BlockSpec((None,) + shape[1:], lambda i: (l, 0, 0), pipeline_mode=pl.Buffered(1))

    return pl.pallas_call(
        _merge_kernel,
        grid=(TOKENS // tm,),
        in_specs=[
            pl.BlockSpec((tm, A_HEADS * A_HEAD_DIM), lambda i: (i, 0)),
            pl.BlockSpec((tm, M_HEADS * M_V), lambda i: (i, 0)),
            pl.BlockSpec((tm, D_MODEL), lambda i: (i, COL_GATE_A // D_MODEL)),
            pl.BlockSpec((tm, D_MODEL), lambda i: (i, COL_GATE_B // D_MODEL)),
            resident(wpa.shape), resident(wpb.shape), resident(wo.shape),
            pl.BlockSpec((tm, D_MODEL), lambda i: (i, 0)),
            _mod_spec(l, 2, per_b),
        ],
        out_specs=pl.BlockSpec((tm, D_MODEL), lambda i: (i, 0)),
        out_shape=jax.ShapeDtypeStruct((TOKENS, D_MODEL), F32),
        compiler_params=_params("arbitrary"),
        name="merge",
    )(oa, ob, proj, proj, wpa, wpb, wo, x, mod)


HALO = 8
FFN_SPLIT = 1


def _ffn_up_kernel(x_ref, g_ref, sc_ref, sh_ref, wg_ref, wv_ref, cg_ref, cv_ref, bg_ref, bv_ref,
                   o_ref, h_ref, halo_ref, buf_ref, *, tiles_per_seq):
    i, j = pl.program_id(0), pl.program_id(1)
    tm, tn = o_ref.shape

    @pl.when(j == 0)
    def _():
        h_ref[...] = _modnorm(x_ref[...], g_ref[...], sc_ref[...], sh_ref[...]).astype(BF16)

    seq_start = (i % tiles_per_seq) == 0

    @pl.when(seq_start)
    def _():
        buf_ref[0:HALO, :] = jnp.zeros((HALO, 2 * tn), F32)

    @pl.when(jnp.logical_not(seq_start))
    def _():
        buf_ref[0:HALO, :] = halo_ref[j]

    h = h_ref[...]
    w = tn // FFN_SPLIT
    for s in range(FFN_SPLIT):
        c = slice(s * w, (s + 1) * w)
        buf_ref[HALO:HALO + tm, s * w:(s + 1) * w] = _dot(h, wg_ref[:, c])
        buf_ref[HALO:HALO + tm, tn + s * w:tn + (s + 1) * w] = _dot(h, wv_ref[:, c])

    def conv(lo, c, c_ref, b_ref):
        u0 = buf_ref[HALO:HALO + tm, lo:lo + w]
        u1 = buf_ref[HALO - 1:HALO - 1 + tm, lo:lo + w]
        u2 = buf_ref[HALO - 2:HALO - 2 + tm, lo:lo + w]
        return b_ref[:, c] + c_ref[0:1, c] * u2 + c_ref[1:2, c] * u1 + c_ref[2:3, c] * u0

    for s in range(FFN_SPLIT):
        c = slice(s * w, (s + 1) * w)
        gate = conv(s * w, c, cg_ref, bg_ref)
        val = conv(tn + s * w, c, cv_ref, bv_ref)
        o_ref[:, c] = (gate * jax.nn.sigmoid(gate) * val).astype(BF16)
    halo_ref[j] = buf_ref[tm:tm + HALO, :]


def _ffn_up(l, x, g, mod, w_up, w_conv, b_conv):
    tm, tn = 1024, 512
    per_b = SEQ // tm
    nj = D_FF // tn
    kern = functools.partial(_ffn_up_kernel, tiles_per_seq=per_b)
    return pl.pallas_call(
        kern,
        grid=(TOKENS // tm, nj),
        in_specs=[
            pl.BlockSpec((tm, D_MODEL), lambda i, j: (i, 0)),
            _layer_row_spec(l, D_MODEL),
            _mod_spec(l, 4, per_b),
            _mod_spec(l, 3, per_b),
            pl.BlockSpec((None, D_MODEL, tn), lambda i, j: (l, 0, j)),
            pl.BlockSpec((None, D_MODEL, tn), lambda i, j: (l, 0, nj + j)),
            pl.BlockSpec((None, CONV_W, tn), lambda i, j: (l, 0, j)),
            pl.BlockSpec((None, CONV_W, tn), lambda i, j: (l, 0, nj + j)),
            pl.BlockSpec((None, 1, tn), lambda i, j: (l, 0, j)),
            pl.BlockSpec((None, 1, tn), lambda i, j: (l, 0, nj + j)),
        ],
        out_specs=pl.BlockSpec((tm, tn), lambda i, j: (i, j)),
        out_shape=jax.ShapeDtypeStruct((TOKENS, D_FF), BF16),
        scratch_shapes=[
            pltpu.VMEM((tm, D_MODEL), BF16),
            pltpu.VMEM((nj, HALO, 2 * tn), F32),
            pltpu.VMEM((HALO + tm, 2 * tn), F32),
        ],
        compiler_params=_params("arbitrary", "arbitrary"),
        name="ffn_up",
    )(x, g, mod, mod, w_up, w_up, w_conv, w_conv, b_conv, b_conv)


def _ffn_down_kernel(a_ref, w_ref, x_ref, gt_ref, o_ref):
    o_ref[...] = x_ref[...] + gt_ref[...] * _dot(a_ref[...], w_ref[...].astype(BF16))


def _ffn_down(l, act, w_down, x, mod):
    tm, tn = 1024, 256
    per_b = SEQ // tm
    return pl.pallas_call(
        _ffn_down_kernel,
        grid=(TOKENS // tm, D_MODEL // tn),
        in_specs=[
            pl.BlockSpec((tm, D_FF), lambda i, j: (i, 0)),
            pl.BlockSpec((None, D_FF, tn), lambda i, j: (l, 0, j)),
            pl.BlockSpec((tm, tn), lambda i, j: (i, j)),
            _mod_spec(l, 5, per_b, tn),
        ],
        out_specs=pl.BlockSpec((tm, tn), lambda i, j: (i, j)),
        out_shape=jax.ShapeDtypeStruct((TOKENS, D_MODEL), F32),
        compiler_params=_params("arbitrary", "arbitrary"),
        name="ffn_down",
    )(act, w_down, x, mod)


def _pack_w_in(w_in):
    o = np.cumsum((0,) + (1024, 256, 256, 1024, 64, 16, 512, 256, 64, 2048, 2048))
    w_t = jnp.swapaxes(w_in, 1, 2)
    qa, ka, va, qi, ki, wi, mql, mkvl, mkr, ga, gb = (
        w_t[:, int(o[k]):int(o[k + 1])].astype(BF16) for k in range(11))
    z = lambda n: jnp.zeros((DEPTH, n, D_MODEL), BF16)
    packed = jnp.concatenate(
        [ga, gb, qa, qi, mql, ka, va, mkvl, ki, wi, z(48), mkr, z(64)], axis=1)
    assert packed.shape[1] == IN_PACKED
    return packed


def _pack_w_mq(w_mq_up):
    w = w_mq_up.reshape(DEPTH, M_Q_LORA, M_HEADS, M_QK)
    nope = w[..., :M_NOPE].reshape(DEPTH, M_Q_LORA, M_HEADS * M_NOPE)
    rope = w[..., M_NOPE:].reshape(DEPTH, M_Q_LORA, M_HEADS * M_ROPE)
    return jnp.concatenate([nope, rope], axis=-1).astype(BF16)


def _rope_consts():
    def inv(rot):
        return ROPE_THETA ** (-jnp.arange(0, rot, 2, dtype=F32) / rot)

    def lanes(v, group):
        rep = jnp.concatenate([v, v])
        pad = jnp.zeros((group - rep.shape[0],), F32)
        return jnp.tile(jnp.concatenate([rep, pad]), LANES // group)

    def sgn(half, group):
        v = jnp.concatenate([-jnp.ones((half,), F32), jnp.ones((half,), F32)])
        pad = jnp.zeros((group - 2 * half,), F32)
        return jnp.tile(jnp.concatenate([v, pad]), LANES // group)

    rot_a = A_HEAD_DIM // ROT_FRACTION
    rot_i = IDX_DIM // ROT_FRACTION
    inv3 = jnp.stack([lanes(inv(rot_a), LANES), lanes(inv(rot_i), IDX_DIM), lanes(inv(M_ROPE), M_ROPE)])
    sgn3 = jnp.stack([sgn(rot_a // 2, LANES), sgn(rot_i // 2, IDX_DIM), sgn(M_ROPE // 2, M_ROPE)])
    return inv3.reshape(3, 1, LANES), sgn3.reshape(3, 1, LANES)


def kernel(x, c, positions, g_attn, g_ffn, w_ada, b_ada, w_in, g_qa, g_ka, g_mq_lat, w_mq_up,
           g_mkv_lat, w_mkv_up, g_qm, g_km, w_pa, w_pb, w_o, w_up, w_conv, b_conv, w_down):
    pos_col = positions.astype(F32).reshape(TOKENS, 1)
    inv3, sgn3 = _rope_consts()
    tabs_cos, tabs_sin = _rope_tables(pos_col, inv3, sgn3)

    c8 = jnp.pad(c, ((0, 8 - BATCH), (0, 0)))
    mod = _ada(c8, w_ada, b_ada)[:, :BATCH].reshape(DEPTH, BATCH, N_ADA, 1, D_MODEL)

    w_in_p = _pack_w_in(w_in)
    w_mq_p = _pack_w_mq(w_mq_up)
    w_mkv_b = w_mkv_up.astype(BF16)
    w_pa_b, w_pb_b, w_o_b = w_pa.astype(BF16), w_pb.astype(BF16), w_o.astype(BF16)
    w_up_b = w_up.astype(BF16)

    rows = lambda v: v.reshape(DEPTH, 1, -1)
    rope_gain = lambda v: rows(jnp.tile(v[:, M_NOPE:], (1, 2)))
    g_attn_r, g_ffn_r, b_conv_r = rows(g_attn), rows(g_ffn), rows(b_conv)
    g_qa_r, g_ka_r, g_mq_r, g_mkv_r = rows(g_qa), rows(g_ka), rows(g_mq_lat), rows(g_mkv_lat)
    gqn, gqr = rows(g_qm[:, :M_NOPE]), rope_gain(g_qm)
    gkn, gkr = rows(g_km[:, :M_NOPE]), rope_gain(g_km)

    xf = x.reshape(TOKENS, D_MODEL)
    for l in range(DEPTH):
        proj = _in_proj(l, xf, g_attn_r, mod, w_in_p)
        qa, ka, va, qi, ki2, wi, mqn, mkvn = _prep(
            l, proj, tabs_cos, tabs_sin, g_qa_r, g_ka_r, g_mq_r, g_mkv_r)
        mq, mk, mv = _mla_up(l, mqn, mkvn, proj, w_mq_p, w_mkv_b, tabs_cos, tabs_sin,
                             gqn, gqr, gkn, gkr)
        o_a = _dsa(qa, ka, va, qi, ki2, wi)
        o_b = _mla(mq, mk, mv)
        xf = _merge(l, o_a, o_b, proj, w_pa_b, w_pb_b, w_o_b, xf, mod)
        act = _ffn_up(l, xf, g_ffn_r, mod, w_up_b, w_conv, b_conv_r)
        xf = _ffn_down(l, act, w_down, xf, mod)
    return xf.reshape(BATCH, SEQ, D_MODEL)
```

```python
import functools

import jax
import jax.numpy as jnp
import numpy as np
from jax import lax
from jax.experimental import pallas as pl
from jax.experimental.pallas import tpu as pltpu

D_MODEL = 2048
BATCH = 4
SEQ = 2048
DEPTH = 4
A_HEADS = 8
A_KV_HEADS = 2
A_HEAD_DIM = 128
IDX_HEADS = 16
IDX_DIM = 64
TOPK_MAX = 256
M_HEADS = 8
M_Q_LORA = 512
M_KV_LORA = 256
M_NOPE = 128
M_ROPE = 64
M_V = 128
D_FF = 5632
CONV_W = 3
ROPE_THETA = 500000.0
ROT_FRACTION = 4
EPS = 1e-6
N_ADA = 6
IDX_W_SCALE = (IDX_HEADS * IDX_DIM) ** -0.5
N_SEL = min(TOPK_MAX, SEQ // 4)
TOKENS = BATCH * SEQ
LANES = 128
M_QK = M_NOPE + M_ROPE
M_QK_PAD = 256

BF16 = jnp.bfloat16
F32 = jnp.float32

COL_GATE_A = 0
COL_GATE_B = 2048
COL_QA = 4096
COL_QI = 5120
COL_MQL = 6144
COL_KV = 6656
COL_MKVL = 7168
COL_KIWI = 7424
COL_MKR = 7552
IN_PACKED = 7680

VMEM_LIMIT = 56 * 1024 * 1024


def _params(*sem):
    return pltpu.CompilerParams(dimension_semantics=sem, vmem_limit_bytes=VMEM_LIMIT)


def _dot(a, b):
    return jnp.dot(a, b, preferred_element_type=F32)


def _dot_nt(a, b):
    return lax.dot_general(a, b, (((1,), (1,)), ((), ())), preferred_element_type=F32)


def _ada_kernel(c_ref, w_ref, b_ref, o_ref):
    c = c_ref[...]
    c_act = (c * jax.nn.sigmoid(c)).astype(BF16)
    o_ref[0] = _dot(c_act, w_ref[0].astype(BF16)) + b_ref[0]


def _ada(c8, w_ada, b_ada):
    tn = 1024
    n = N_ADA * D_MODEL
    return pl.pallas_call(
        _ada_kernel,
        grid=(DEPTH, n // tn),
        in_specs=[
            pl.BlockSpec((8, D_MODEL), lambda l, j: (0, 0)),
            pl.BlockSpec((1, D_MODEL, tn), lambda l, j: (l, 0, j)),
            pl.BlockSpec((1, 1, tn), lambda l, j: (l, 0, j)),
        ],
        out_specs=pl.BlockSpec((1, 8, tn), lambda l, j: (l, 0, j)),
        out_shape=jax.ShapeDtypeStruct((DEPTH, 8, n), F32),
        compiler_params=_params("arbitrary", "arbitrary"),
        name="ada",
    )(c8, w_ada, b_ada.reshape(DEPTH, 1, n))


def _rope_table_kernel(pos_ref, inv_ref, sgn_ref, cos_ref, sin_ref):
    ang = pos_ref[...] * inv_ref[0]
    cos_ref[0] = jnp.cos(ang)
    sin_ref[0] = jnp.sin(ang) * sgn_ref[0]


def _rope_tables(pos_col, inv3, sgn3):
    tm = 512
    return pl.pallas_call(
        _rope_table_kernel,
        grid=(3, TOKENS // tm),
        in_specs=[
            pl.BlockSpec((tm, 1), lambda t, i: (i, 0)),
            pl.BlockSpec((1, 1, LANES), lambda t, i: (t, 0, 0)),
            pl.BlockSpec((1, 1, LANES), lambda t, i: (t, 0, 0)),
        ],
        out_specs=[
            pl.BlockSpec((1, tm, LANES), lambda t, i: (t, i, 0)),
            pl.BlockSpec((1, tm, LANES), lambda t, i: (t, i, 0)),
        ],
        out_shape=[jax.ShapeDtypeStruct((3, TOKENS, LANES), F32)] * 2,
        compiler_params=_params("arbitrary", "arbitrary"),
        name="rope_tables",
    )(pos_col, inv3, sgn3)


def _rope(t, cos, sin_signed, half, lane):
    fwd = pltpu.roll(t, LANES - half, 1)
    bwd = pltpu.roll(t, half, 1)
    swapped = jnp.where((lane % (2 * half)) < half, fwd, bwd)
    return t * cos + swapped * sin_signed


def _rms(t, g, width):
    ms = jnp.sum(t * t, axis=-1, keepdims=True) * (1.0 / width)
    return t * lax.rsqrt(ms + EPS) * g


def _modnorm(x, g, sc, sh):
    ms = jnp.mean(x * x, axis=-1, keepdims=True)
    y = x * lax.rsqrt(ms + EPS) * g
    return y * (1.0 + sc) + sh


def _in_proj_kernel(x_ref, g_ref, sc_ref, sh_ref, w_ref, o_ref, h_ref):
    @pl.when(pl.program_id(1) == 0)
    def _():
        h_ref[...] = _modnorm(x_ref[...], g_ref[...], sc_ref[...], sh_ref[...]).astype(BF16)

    o_ref[...] = _dot_nt(h_ref[...], w_ref[...])


def _mod_spec(l, k, per_b, width=D_MODEL):
    if width == D_MODEL:
        return pl.BlockSpec((None, None, None, 1, D_MODEL), lambda i, *_: (l, i // per_b, k, 0, 0))
    return pl.BlockSpec((None, None, None, 1, width), lambda i, j: (l, i // per_b, k, 0, j))


def _layer_row_spec(l, width):
    return pl.BlockSpec((None, 1, width), lambda *_: (l, 0, 0))


def _in_proj(l, x, g, mod, w):
    tm, tn = 1024, 768
    per_b = SEQ // tm
    n = w.shape[1]
    return pl.pallas_call(
        _in_proj_kernel,
        grid=(TOKENS // tm, n // tn),
        in_specs=[
            pl.BlockSpec((tm, D_MODEL), lambda i, j: (i, 0)),
            _layer_row_spec(l, D_MODEL),
            _mod_spec(l, 1, per_b),
            _mod_spec(l, 0, per_b),
            pl.BlockSpec((None, tn, D_MODEL), lambda i, j: (l, j, 0)),
        ],
        out_specs=pl.BlockSpec((tm, tn), lambda i, j: (i, j)),
        out_shape=jax.ShapeDtypeStruct((TOKENS, n), F32),
        scratch_shapes=[pltpu.VMEM((tm, D_MODEL), BF16)],
        compiler_params=_params("arbitrary", "arbitrary"),
        name="in_proj",
    )(x, g, mod, mod, w)


def _prep_kernel(qa_ref, qi_ref, mql_ref, kv_ref, mkvl_ref, kiwi_ref,
                 ca_ref, sa_ref, ci_ref, si_ref,
                 gqa_ref, gka_ref, gmq_ref, gmkv_ref,
                 qa_o, ka_o, va_o, qi_o, ki2_o, wi_o, mqn_o, mkvn_o):
    tm = qa_ref.shape[0]
    lane = lax.broadcasted_iota(jnp.int32, (tm, LANES), 1)
    ca, sa = ca_ref[0], sa_ref[0]
    ci, si = ci_ref[0], si_ref[0]
    rot_a = A_HEAD_DIM // ROT_FRACTION // 2
    rot_i = IDX_DIM // ROT_FRACTION // 2
    scale_a = A_HEAD_DIM ** -0.5

    for h in range(A_HEADS):
        t = qa_ref[:, h * LANES:(h + 1) * LANES]
        r = _rope(_rms(t, gqa_ref[...], A_HEAD_DIM), ca, sa, rot_a, lane)
        qa_o[h] = (r * scale_a).astype(BF16)
    for g in range(A_KV_HEADS):
        t = kv_ref[:, g * LANES:(g + 1) * LANES]
        r = _rope(_rms(t, gka_ref[...], A_HEAD_DIM), ca, sa, rot_a, lane)
        ka_o[g] = r.astype(BF16)
        va_o[g] = kv_ref[:, (A_KV_HEADS + g) * LANES:(A_KV_HEADS + g + 1) * LANES].astype(BF16)
    for p in range(IDX_HEADS // 2):
        t = qi_ref[:, p * LANES:(p + 1) * LANES]
        r = _rope(t, ci, si, rot_i, lane)
        qi_o[2 * p] = jnp.where(lane < IDX_DIM, r, 0.0).astype(BF16)
        qi_o[2 * p + 1] = jnp.where(lane < IDX_DIM, 0.0, r).astype(BF16)

    kiwi = kiwi_ref[...]
    ki = _rope(kiwi, ci, si, rot_i, lane)
    ki2_o[...] = jnp.where(lane < IDX_DIM, ki, pltpu.roll(ki, IDX_DIM, 1)).astype(BF16)
    wi_o[...] = kiwi * IDX_W_SCALE

    mqn_o[...] = _rms(mql_ref[...], gmq_ref[...], M_Q_LORA).astype(BF16)
    mkvn_o[...] = _rms(mkvl_ref[...], gmkv_ref[...], M_KV_LORA).astype(BF16)


def _prep(l, proj, tabs_cos, tabs_sin, g_qa, g_ka, g_mq_lat, g_mkv_lat):
    tm = 256
    per_b = SEQ // tm

    def col(width, start):
        idx = start // width
        return pl.BlockSpec((tm, width), lambda i: (i, idx))

    def tab(t):
        return pl.BlockSpec((1, tm, LANES), lambda i: (t, i, 0))

    def vec(width):
        return _layer_row_spec(l, width)

    def heads(n):
        return pl.BlockSpec((None, n, tm, LANES), lambda i: (i // per_b, 0, i % per_b, 0))

    def seq(width):
        return pl.BlockSpec((None, tm, width), lambda i: (i // per_b, i % per_b, 0))

    def tok(width):
        return pl.BlockSpec((tm, width), lambda i: (i, 0))

    return pl.pallas_call(
        _prep_kernel,
        grid=(TOKENS // tm,),
        in_specs=[
            col(1024, COL_QA), col(1024, COL_QI), col(512, COL_MQL), col(512, COL_KV),
            col(256, COL_MKVL), col(128, COL_KIWI),
            tab(0), tab(0), tab(1), tab(1),
            vec(A_HEAD_DIM), vec(A_HEAD_DIM), vec(M_Q_LORA), vec(M_KV_LORA),
        ],
        out_specs=[
            heads(A_HEADS), heads(A_KV_HEADS), heads(A_KV_HEADS),
            heads(IDX_HEADS), seq(LANES), seq(LANES),
            tok(M_Q_LORA), tok(M_KV_LORA),
        ],
        out_shape=[
            jax.ShapeDtypeStruct((BATCH, A_HEADS, SEQ, LANES), BF16),
            jax.ShapeDtypeStruct((BATCH, A_KV_HEADS, SEQ, LANES), BF16),
            jax.ShapeDtypeStruct((BATCH, A_KV_HEADS, SEQ, LANES), BF16),
            jax.ShapeDtypeStruct((BATCH, IDX_HEADS, SEQ, LANES), BF16),
            jax.ShapeDtypeStruct((BATCH, SEQ, LANES), BF16),
            jax.ShapeDtypeStruct((BATCH, SEQ, LANES), F32),
            jax.ShapeDtypeStruct((TOKENS, M_Q_LORA), BF16),
            jax.ShapeDtypeStruct((TOKENS, M_KV_LORA), BF16),
        ],
        compiler_params=_params("arbitrary"),
        name="prep",
    )(proj, proj, proj, proj, proj, proj, tabs_cos, tabs_sin, tabs_cos, tabs_sin,
      g_qa, g_ka, g_mq_lat, g_mkv_lat)


def _mla_up_kernel(mqn_ref, mkvn_ref, mkr_ref, wq_ref, wkv_ref, cm_ref, sm_ref,
                   gqn_ref, gqr_ref, gkn_ref, gkr_ref, q_o, k_o, v_o):
    tm = mqn_ref.shape[0]
    lane = lax.broadcasted_iota(jnp.int32, (tm, LANES), 1)
    low = lane < M_ROPE
    cm, sm = cm_ref[0], sm_ref[0]
    half = M_ROPE // 2
    scale_m = M_QK ** -0.5
    nope_w = M_HEADS * M_NOPE

    q = _dot(mqn_ref[...], wq_ref[...])
    kv = _dot(mkvn_ref[...], wkv_ref[...])

    for p in range(M_HEADS // 2):
        rp = q[:, nope_w + p * LANES: nope_w + (p + 1) * LANES]
        sq = rp * rp
        s_all = jnp.sum(sq, axis=-1, keepdims=True)
        s_lo = jnp.sum(jnp.where(low, sq, 0.0), axis=-1, keepdims=True)
        for e in range(2):
            h = 2 * p + e
            nope = q[:, h * M_NOPE:(h + 1) * M_NOPE]
            ss = jnp.sum(nope * nope, axis=-1, keepdims=True) + (s_lo if e == 0 else s_all - s_lo)
            rs = lax.rsqrt(ss * (1.0 / M_QK) + EPS)
            roped = _rope(rp * rs * gqr_ref[...], cm, sm, half, lane)
            if e == 1:
                roped = pltpu.roll(roped, M_ROPE, 1)
            q_o[h, :, 0:LANES] = (nope * rs * gqn_ref[...] * scale_m).astype(BF16)
            q_o[h, :, LANES:2 * LANES] = (jnp.where(low, roped, 0.0) * scale_m).astype(BF16)

    kr = mkr_ref[...]
    kr_ss = jnp.sum(kr * kr, axis=-1, keepdims=True)
    kr_roped = _rope(kr * gkr_ref[...], cm, sm, half, lane)
    kr_roped = jnp.where(low, kr_roped, 0.0)
    for h in range(M_HEADS):
        nope = kv[:, h * 2 * LANES: h * 2 * LANES + M_NOPE]
        ss = jnp.sum(nope * nope, axis=-1, keepdims=True) + kr_ss
        rs = lax.rsqrt(ss * (1.0 / M_QK) + EPS)
        k_o[h, :, 0:LANES] = (nope * rs * gkn_ref[...]).astype(BF16)
        k_o[h, :, LANES:2 * LANES] = (kr_roped * rs).astype(BF16)
        v_o[h] = kv[:, h * 2 * LANES + M_NOPE:(h + 1) * 2 * LANES].astype(BF16)


def _mla_up(l, mqn, mkvn, proj, wq, wkv, tabs_cos, tabs_sin, gqn, gqr, gkn, gkr):
    tm = 256
    per_b = SEQ // tm

    def vec():
        return _layer_row_spec(l, LANES)

    def heads(width):
        return pl.BlockSpec((None, M_HEADS, tm, width), lambda i: (i // per_b, 0, i % per_b, 0))

    return pl.pallas_call(
        _mla_up_kernel,
        grid=(TOKENS // tm,),
        in_specs=[
            pl.BlockSpec((tm, M_Q_LORA), lambda i: (i, 0)),
            pl.BlockSpec((tm, M_KV_LORA), lambda i: (i, 0)),
            pl.BlockSpec((tm, LANES), lambda i: (i, COL_MKR // LANES)),
            pl.BlockSpec((None,) + wq.shape[1:], lambda i: (l, 0, 0)),
            pl.BlockSpec((None,) + wkv.shape[1:], lambda i: (l, 0, 0)),
            pl.BlockSpec((1, tm, LANES), lambda i: (2, i, 0)),
            pl.BlockSpec((1, tm, LANES), lambda i: (2, i, 0)),
            vec(), vec(), vec(), vec(),
        ],
        out_specs=[heads(M_QK_PAD), heads(M_QK_PAD), heads(M_V)],
        out_shape=[
            jax.ShapeDtypeStruct((BATCH, M_HEADS, SEQ, M_QK_PAD), BF16),
            jax.ShapeDtypeStruct((BATCH, M_HEADS, SEQ, M_QK_PAD), BF16),
            jax.ShapeDtypeStruct((BATCH, M_HEADS, SEQ, M_V), BF16),
        ],
        compiler_params=_params("arbitrary"),
        name="mla_up",
    )(mqn, mkvn, proj, wq, wkv, tabs_cos, tabs_sin, gqn, gqr, gkn, gkr)


INT_MIN = -(2 ** 31)
KEY_MASKED = INT_MIN

DSA_TQ = 256
KEY_CHUNK = 512
SEARCH_ROWS = 128
SEARCH_UNROLL = 5
IDX_CHUNK = 256
IDX_ROWS = 64
ATT_SPLIT = 4


def _dsa_index(i, n_chunks, qi_ref, ki2_ref, wi_ref, key_ref, d_refs, wb_ref):
    tq = qi_ref.shape[1]
    w = wi_ref[...]
    for h in range(IDX_HEADS):
        wb_ref[h] = jnp.broadcast_to(w[:, IDX_DIM + h:IDX_DIM + h + 1], (tq, LANES))

    def matmul(c, d_ref):
        start = pl.multiple_of(c * IDX_CHUNK, IDX_CHUNK)
        q_all = qi_ref[...].reshape(IDX_HEADS * tq, LANES)
        d_ref[...] = _dot_nt(q_all, ki2_ref[pl.ds(start, IDX_CHUNK), :])

    def head_sum(c, d_ref):
        for r0 in range(0, tq, IDX_ROWS):
            row_t = i * tq + r0 + lax.broadcasted_iota(jnp.int32, (IDX_ROWS, LANES), 0)
            for j0 in range(0, IDX_CHUNK, LANES):
                acc = jnp.zeros((IDX_ROWS, LANES), F32)
                for h in range(IDX_HEADS):
                    d = d_ref[h * tq + r0:h * tq + r0 + IDX_ROWS, j0:j0 + LANES]
                    acc = acc + jnp.maximum(d, 0.0) * wb_ref[h, r0:r0 + IDX_ROWS, :]
                bits = pltpu.bitcast(acc, jnp.int32)
                img = bits ^ ((bits >> 31) & jnp.int32(0x7FFFFFFF))
                col_t = c * IDX_CHUNK + j0 + lax.broadcasted_iota(jnp.int32, (IDX_ROWS, LANES), 1)
                key_ref[c, r0:r0 + IDX_ROWS, j0:j0 + LANES] = jnp.where(col_t <= row_t, img, KEY_MASKED)

    d0_ref, d1_ref = d_refs
    matmul(0, d0_ref)

    def pair(k, carry):
        c = 2 * k
        matmul(c + 1, d1_ref)
        head_sum(c, d0_ref)
        matmul(jnp.minimum(c + 2, n_chunks - 1), d0_ref)
        head_sum(c + 1, d1_ref)
        return carry

    lax.fori_loop(0, n_chunks // 2, pair, 0)


def _dsa_select(nk, key_ref, bias_ref):
    chunks = range(nk // IDX_CHUNK)
    n_sel = float(N_SEL)
    groups = list(range(0, key_ref.shape[1], SEARCH_ROWS))

    def count_ge(r0, cand):
        hits = None
        for c in chunks:
            hit = jnp.where(key_ref[c, r0:r0 + SEARCH_ROWS, :] >= cand, 1.0, 0.0)
            hits = hit if hits is None else hits + hit
        return jnp.sum(hits, axis=-1, keepdims=True)

    zero = jnp.zeros((SEARCH_ROWS, 1), jnp.int32)
    thr0 = tuple(jnp.where(count_ge(r0, zero) >= n_sel, jnp.int32(0), jnp.int32(INT_MIN))
                 for r0 in groups)

    def step(b, thrs):
        bit = lax.shift_left(jnp.int32(1), jnp.int32(30) - b)
        out = []
        for r0, thr in zip(groups, thrs):
            cand = thr | bit
            out.append(jnp.where(count_ge(r0, cand) >= n_sel, cand, thr))
        return tuple(out)

    thrs = lax.fori_loop(0, 30, step, thr0, unroll=SEARCH_UNROLL)
    thrs = step(jnp.int32(30), thrs)
    for r0, thr in zip(groups, thrs):
        floor = jnp.maximum(thr, jnp.int32(KEY_MASKED + 1))
        for c in chunks:
            sel = key_ref[c, r0:r0 + SEARCH_ROWS, :] >= floor
            bias_ref[r0:r0 + SEARCH_ROWS, c * IDX_CHUNK:(c + 1) * IDX_CHUNK] = jnp.where(sel, 0.0, -jnp.inf)


def _dsa_attend(nk, qa_ref, ka_ref, va_ref, bias_ref, o_ref, oh_ref, lg_refs, p_refs):
    tq = qa_ref.shape[1]
    rep = A_HEADS // A_KV_HEADS

    sub = rep // ATT_SPLIT

    def group(g, carry):
        k = ka_ref[g, 0:nk, :]
        v = va_ref[g, 0:nk, :]
        inv_l = [None] * ATT_SPLIT

        def scores(s):
            q = qa_ref[pl.ds(g * rep + s * sub, sub)].reshape(sub * tq, LANES)
            lg_refs[s][:, 0:nk] = _dot_nt(q, k)

        def softmax(s):
            logits = lg_refs[s][:, 0:nk].reshape(sub, tq, nk) + bias_ref[:, 0:nk][None]
            m = jnp.max(logits, axis=-1, keepdims=True)
            p = jnp.exp(logits - m)
            inv_l[s] = 1.0 / jnp.sum(p, axis=-1, keepdims=True)
            p_refs[s][:, 0:nk] = p.astype(BF16).reshape(sub * tq, nk)

        def values(s):
            o = _dot(p_refs[s][:, 0:nk], v).reshape(sub, tq, LANES) * inv_l[s]
            oh_ref[pl.ds(g * rep + s * sub, sub)] = o.astype(BF16)

        for t in range(ATT_SPLIT + 2):
            if t < ATT_SPLIT:
                scores(t)
            if 0 <= t - 1 < ATT_SPLIT:
                softmax(t - 1)
            if 0 <= t - 2 < ATT_SPLIT:
                values(t - 2)
        return carry

    lax.fori_loop(0, A_KV_HEADS, group, 0)
    for h in range(A_HEADS):
        o_ref[:, h * LANES:(h + 1) * LANES] = oh_ref[h]


def _dsa_kernel(qa_ref, ka_ref, va_ref, qi_ref, ki2_ref, wi_ref, o_ref,
                key_ref, bias_ref, d0_ref, d1_ref, wb_ref, oh_ref, *stage_refs):
    lg_refs, p_refs = stage_refs[:ATT_SPLIT], stage_refs[ATT_SPLIT:]
    i = pl.program_id(1)
    tq = DSA_TQ
    n_free = N_SEL // tq
    variant = i // (KEY_CHUNK // tq)

    for v in range(n_free):
        @pl.when(i == v)
        def _(v=v):
            nk = (v + 1) * tq
            row = i * tq + lax.broadcasted_iota(jnp.int32, (tq, nk), 0)
            col = lax.broadcasted_iota(jnp.int32, (tq, nk), 1)
            bias_ref[:, 0:nk] = jnp.where(col <= row, 0.0, -jnp.inf)
            _dsa_attend(nk, qa_ref, ka_ref, va_ref, bias_ref, o_ref, oh_ref, lg_refs, p_refs)

    @pl.when(i >= n_free)
    def _():
        n_chunks = (variant + 1) * (KEY_CHUNK // IDX_CHUNK)
        _dsa_index(i, n_chunks, qi_ref, ki2_ref, wi_ref, key_ref, (d0_ref, d1_ref), wb_ref)

    for v in range(SEQ // KEY_CHUNK):
        @pl.when((i >= n_free) & (variant == v))
        def _(v=v):
            nk = (v + 1) * KEY_CHUNK
            _dsa_select(nk, key_ref, bias_ref)
            _dsa_attend(nk, qa_ref, ka_ref, va_ref, bias_ref, o_ref, oh_ref, lg_refs, p_refs)


def _dsa(qa, ka, va, qi, ki2, wi):
    tq = DSA_TQ
    nq = SEQ // tq
    att_rows = A_HEADS // A_KV_HEADS // ATT_SPLIT * tq
    return pl.pallas_call(
        _dsa_kernel,
        grid=(BATCH, nq),
        in_specs=[
            pl.BlockSpec((None, A_HEADS, tq, LANES), lambda b, i: (b, 0, i, 0)),
            pl.BlockSpec((None, A_KV_HEADS, SEQ, LANES), lambda b, i: (b, 0, 0, 0)),
            pl.BlockSpec((None, A_KV_HEADS, SEQ, LANES), lambda b, i: (b, 0, 0, 0)),
            pl.BlockSpec((None, IDX_HEADS, tq, LANES), lambda b, i: (b, 0, i, 0)),
            pl.BlockSpec((None, SEQ, LANES), lambda b, i: (b, 0, 0)),
            pl.BlockSpec((None, tq, LANES), lambda b, i: (b, i, 0)),
        ],
        out_specs=pl.BlockSpec((tq, A_HEADS * A_HEAD_DIM), lambda b, i: (b * nq + i, 0)),
        out_shape=jax.ShapeDtypeStruct((TOKENS, A_HEADS * A_HEAD_DIM), BF16),
        scratch_shapes=[
            pltpu.VMEM((SEQ // IDX_CHUNK, tq, IDX_CHUNK), jnp.int32),
            pltpu.VMEM((tq, SEQ), F32),
            pltpu.VMEM((IDX_HEADS * tq, IDX_CHUNK), F32),
            pltpu.VMEM((IDX_HEADS * tq, IDX_CHUNK), F32),
            pltpu.VMEM((IDX_HEADS, tq, LANES), F32),
            pltpu.VMEM((A_HEADS, tq, LANES), BF16),
        ] + [pltpu.VMEM((att_rows, SEQ), F32)] * ATT_SPLIT + [pltpu.VMEM((att_rows, SEQ), BF16)] * ATT_SPLIT,
        compiler_params=_params("arbitrary", "arbitrary"),
        name="dsa",
    )(qa, ka, va, qi, ki2, wi)


MLA_TQ = 256
MLA_HEADS_PER_STEP = 4


def _mla_body(nk, i, q_ref, k_ref, v_ref, o_ref):
    tq = q_ref.shape[1]
    n0 = nk - KEY_CHUNK
    row = i * tq + lax.broadcasted_iota(jnp.int32, (tq, KEY_CHUNK), 0)
    col = n0 + lax.broadcasted_iota(jnp.int32, (tq, KEY_CHUNK), 1)
    causal = col <= row
    for h in range(q_ref.shape[0]):
        q = q_ref[h]
        s1 = jnp.where(causal, _dot_nt(q, k_ref[h, n0:nk, :]), -jnp.inf)
        m = jnp.max(s1, axis=-1, keepdims=True)
        if n0:
            s0 = _dot_nt(q, k_ref[h, 0:n0, :])
            m = jnp.maximum(m, jnp.max(s0, axis=-1, keepdims=True))
        p1 = jnp.exp(s1 - m)
        l = jnp.sum(p1, axis=-1, keepdims=True)
        o = _dot(p1.astype(BF16), v_ref[h, n0:nk, :])
        if n0:
            p0 = jnp.exp(s0 - m)
            l = l + jnp.sum(p0, axis=-1, keepdims=True)
            o = o + _dot(p0.astype(BF16), v_ref[h, 0:n0, :])
        o_ref[:, h * M_V:(h + 1) * M_V] = (o / l).astype(BF16)


def _mla_kernel(q_ref, k_ref, v_ref, o_ref):
    i = pl.program_id(2)
    per_chunk = KEY_CHUNK // MLA_TQ
    for v in range(SEQ // KEY_CHUNK):
        @pl.when(i // per_chunk == v)
        def _(v=v):
            _mla_body((v + 1) * KEY_CHUNK, i, q_ref, k_ref, v_ref, o_ref)


def _mla(q, k, v):
    tq, hg = MLA_TQ, MLA_HEADS_PER_STEP
    nq = SEQ // tq
    return pl.pallas_call(
        _mla_kernel,
        grid=(BATCH, M_HEADS // hg, nq),
        in_specs=[
            pl.BlockSpec((None, hg, tq, M_QK_PAD), lambda b, g, i: (b, g, i, 0)),
            pl.BlockSpec((None, hg, SEQ, M_QK_PAD), lambda b, g, i: (b, g, 0, 0)),
            pl.BlockSpec((None, hg, SEQ, M_V), lambda b, g, i: (b, g, 0, 0)),
        ],
        out_specs=pl.BlockSpec((tq, hg * M_V), lambda b, g, i: (b * nq + i, g)),
        out_shape=jax.ShapeDtypeStruct((TOKENS, M_HEADS * M_V), BF16),
        compiler_params=_params("arbitrary", "arbitrary", "arbitrary"),
        name="mla",
    )(q, k, v)


def _merge_kernel(oa_ref, ob_ref, ga_ref, gb_ref, wpa_ref, wpb_ref, wo_ref, x_ref, gt_ref, o_ref):
    a = _dot(oa_ref[...], wpa_ref[...])
    b = _dot(ob_ref[...], wpb_ref[...])
    merged = jax.nn.sigmoid(ga_ref[...]) * a + jax.nn.sigmoid(gb_ref[...]) * b
    y = _dot(merged.astype(BF16), wo_ref[...])
    o_ref[...] = x_ref[...] + gt_ref[...] * y


def _merge(l, oa, ob, proj, wpa, wpb, wo, x, mod):
    tm = 256
    per_b = SEQ // tm

    def resident(shape):
        return pl.BlockSpec((None,) + shape[1:], lambda i: (l, 0, 0), pipeline_mode=pl.Buffered(1))

    return pl.pallas_call(
        _merge_kernel,
        grid=(TOKENS // tm,),
        in_specs=[
            pl.BlockSpec((tm, A_HEADS * A_HEAD_DIM), lambda i: (i, 0)),
            pl.BlockSpec((tm, M_HEADS * M_V), lambda i: (i, 0)),
            pl.BlockSpec((tm, D_MODEL), lambda i: (i, COL_GATE_A // D_MODEL)),
            pl.BlockSpec((tm, D_MODEL), lambda i: (i, COL_GATE_B // D_MODEL)),
            resident(wpa.shape), resident(wpb.shape), resident(wo.shape),
            pl.BlockSpec((tm, D_MODEL), lambda i: (i, 0)),
            _mod_spec(l, 2, per_b),
        ],
        out_specs=pl.BlockSpec((tm, D_MODEL), lambda i: (i, 0)),
        out_shape=jax.ShapeDtypeStruct((TOKENS, D_MODEL), F32),
        compiler_params=_params("arbitrary"),
        name="merge",
    )(oa, ob, proj, proj, wpa, wpb, wo, x, mod)


HALO = 8


def _ffn_up_kernel(x_ref, g_ref, sc_ref, sh_ref, wg_ref, wv_ref, cg_ref, cv_ref, bg_ref, bv_ref,
                   o_ref, h_ref, halo_ref, buf_ref, *, tiles_per_seq):
    i, j = pl.program_id(0), pl.program_id(1)
    tm, tn = o_ref.shape

    @pl.when(j == 0)
    def _():
        h_ref[...] = _modnorm(x_ref[...], g_ref[...], sc_ref[...], sh_ref[...]).astype(BF16)

    seq_start = (i % tiles_per_seq) == 0

    @pl.when(seq_start)
    def _():
        buf_ref[0:HALO, :] = jnp.zeros((HALO, 2 * tn), F32)

    @pl.when(jnp.logical_not(seq_start))
    def _():
        buf_ref[0:HALO, :] = halo_ref[j]

    h = h_ref[...]
    buf_ref[HALO:HALO + tm, 0:tn] = _dot(h, wg_ref[...])
    buf_ref[HALO:HALO + tm, tn:2 * tn] = _dot(h, wv_ref[...])
    halo_ref[j] = buf_ref[tm:tm + HALO, :]

    def conv(lo, c_ref, b_ref):
        u0 = buf_ref[HALO:HALO + tm, lo:lo + tn]
        u1 = buf_ref[HALO - 1:HALO - 1 + tm, lo:lo + tn]
        u2 = buf_ref[HALO - 2:HALO - 2 + tm, lo:lo + tn]
        return b_ref[...] + c_ref[0:1, :] * u2 + c_ref[1:2, :] * u1 + c_ref[2:3, :] * u0

    gate = conv(0, cg_ref, bg_ref)
    val = conv(tn, cv_ref, bv_ref)
    o_ref[...] = (gate * jax.nn.sigmoid(gate) * val).astype(BF16)


def _ffn_up(l, x, g, mod, w_up, w_conv, b_conv):
    tm, tn = 1024, 512
    per_b = SEQ // tm
    nj = D_FF // tn
    kern = functools.partial(_ffn_up_kernel, tiles_per_seq=per_b)
    return pl.pallas_call(
        kern,
        grid=(TOKENS // tm, nj),
        in_specs=[
            pl.BlockSpec((tm, D_MODEL), lambda i, j: (i, 0)),
            _layer_row_spec(l, D_MODEL),
            _mod_spec(l, 4, per_b),
            _mod_spec(l, 3, per_b),
            pl.BlockSpec((None, D_MODEL, tn), lambda i, j: (l, 0, j)),
            pl.BlockSpec((None, D_MODEL, tn), lambda i, j: (l, 0, nj + j)),
            pl.BlockSpec((None, CONV_W, tn), lambda i, j: (l, 0, j)),
            pl.BlockSpec((None, CONV_W, tn), lambda i, j: (l, 0, nj + j)),
            pl.BlockSpec((None, 1, tn), lambda i, j: (l, 0, j)),
            pl.BlockSpec((None, 1, tn), lambda i, j: (l, 0, nj + j)),
        ],
        out_specs=pl.BlockSpec((tm, tn), lambda i, j: (i, j)),
        out_shape=jax.ShapeDtypeStruct((TOKENS, D_FF), BF16),
        scratch_shapes=[
            pltpu.VMEM((tm, D_MODEL), BF16),
            pltpu.VMEM((nj, HALO, 2 * tn), F32),
            pltpu.VMEM((HALO + tm, 2 * tn), F32),
        ],
        compiler_params=_params("arbitrary", "arbitrary"),
        name="ffn_up",
    )(x, g, mod, mod, w_up, w_up, w_conv, w_conv, b_conv, b_conv)


def _ffn_down_kernel(a_ref, w_ref, x_ref, gt_ref, o_ref):
    o_ref[...] = x_ref[...] + gt_ref[...] * _dot(a_ref[...], w_ref[...].astype(BF16))


def _ffn_down(l, act, w_down, x, mod):
    tm, tn = 1024, 256
    per_b = SEQ // tm
    return pl.pallas_call(
        _ffn_down_kernel,
        grid=(TOKENS // tm, D_MODEL // tn),
        in_specs=[
            pl.BlockSpec((tm, D_FF), lambda i, j: (i, 0)),
            pl.BlockSpec((None, D_FF, tn), lambda i, j: (l, 0, j)),
            pl.BlockSpec((tm, tn), lambda i, j: (i, j)),
            _mod_spec(l, 5, per_b, tn),
        ],
        out_specs=pl.BlockSpec((tm, tn), lambda i, j: (i, j)),
        out_shape=jax.ShapeDtypeStruct((TOKENS, D_MODEL), F32),
        compiler_params=_params("arbitrary", "arbitrary"),
        name="ffn_down",
    )(act, w_down, x, mod)


def _pack_w_in(w_in):
    o = np.cumsum((0,) + (1024, 256, 256, 1024, 64, 16, 512, 256, 64, 2048, 2048))
    w_t = jnp.swapaxes(w_in, 1, 2)
    qa, ka, va, qi, ki, wi, mql, mkvl, mkr, ga, gb = (
        w_t[:, int(o[k]):int(o[k + 1])].astype(BF16) for k in range(11))
    z = lambda n: jnp.zeros((DEPTH, n, D_MODEL), BF16)
    packed = jnp.concatenate(
        [ga, gb, qa, qi, mql, ka, va, mkvl, ki, wi, z(48), mkr, z(64)], axis=1)
    assert packed.shape[1] == IN_PACKED
    return packed


def _pack_w_mq(w_mq_up):
    w = w_mq_up.reshape(DEPTH, M_Q_LORA, M_HEADS, M_QK)
    nope = w[..., :M_NOPE].reshape(DEPTH, M_Q_LORA, M_HEADS * M_NOPE)
    rope = w[..., M_NOPE:].reshape(DEPTH, M_Q_LORA, M_HEADS * M_ROPE)
    return jnp.concatenate([nope, rope], axis=-1).astype(BF16)


def _rope_consts():
    def inv(rot):
        return ROPE_THETA ** (-jnp.arange(0, rot, 2, dtype=F32) / rot)

    def lanes(v, group):
        rep = jnp.concatenate([v, v])
        pad = jnp.zeros((group - rep.shape[0],), F32)
        return jnp.tile(jnp.concatenate([rep, pad]), LANES // group)

    def sgn(half, group):
        v = jnp.concatenate([-jnp.ones((half,), F32), jnp.ones((half,), F32)])
        pad = jnp.zeros((group - 2 * half,), F32)
        return jnp.tile(jnp.concatenate([v, pad]), LANES // group)

    rot_a = A_HEAD_DIM // ROT_FRACTION
    rot_i = IDX_DIM // ROT_FRACTION
    inv3 = jnp.stack([lanes(inv(rot_a), LANES), lanes(inv(rot_i), IDX_DIM), lanes(inv(M_ROPE), M_ROPE)])
    sgn3 = jnp.stack([sgn(rot_a // 2, LANES), sgn(rot_i // 2, IDX_DIM), sgn(M_ROPE // 2, M_ROPE)])
    return inv3.reshape(3, 1, LANES), sgn3.reshape(3, 1, LANES)


def kernel(x, c, positions, g_attn, g_ffn, w_ada, b_ada, w_in, g_qa, g_ka, g_mq_lat, w_mq_up,
           g_mkv_lat, w_mkv_up, g_qm, g_km, w_pa, w_pb, w_o, w_up, w_conv, b_conv, w_down):
    pos_col = positions.astype(F32).reshape(TOKENS, 1)
    inv3, sgn3 = _rope_consts()
    tabs_cos, tabs_sin = _rope_tables(pos_col, inv3, sgn3)

    c8 = jnp.pad(c, ((0, 8 - BATCH), (0, 0)))
    mod = _ada(c8, w_ada, b_ada)[:, :BATCH].reshape(DEPTH, BATCH, N_ADA, 1, D_MODEL)

    w_in_p = _pack_w_in(w_in)
    w_mq_p = _pack_w_mq(w_mq_up)
    w_mkv_b = w_mkv_up.astype(BF16)
    w_pa_b, w_pb_b, w_o_b = w_pa.astype(BF16), w_pb.astype(BF16), w_o.astype(BF16)
    w_up_b = w_up.astype(BF16)

    rows = lambda v: v.reshape(DEPTH, 1, -1)
    rope_gain = lambda v: rows(jnp.tile(v[:, M_NOPE:], (1, 2)))
    g_attn_r, g_ffn_r, b_conv_r = rows(g_attn), rows(g_ffn), rows(b_conv)
    g_qa_r, g_ka_r, g_mq_r, g_mkv_r = rows(g_qa), rows(g_ka), rows(g_mq_lat), rows(g_mkv_lat)
    gqn, gqr = rows(g_qm[:, :M_NOPE]), rope_gain(g_qm)
    gkn, gkr = rows(g_km[:, :M_NOPE]), rope_gain(g_km)

    xf = x.reshape(TOKENS, D_MODEL)
    for l in range(DEPTH):
        proj = _in_proj(l, xf, g_attn_r, mod, w_in_p)
        qa, ka, va, qi, ki2, wi, mqn, mkvn = _prep(
            l, proj, tabs_cos, tabs_sin, g_qa_r, g_ka_r, g_mq_r, g_mkv_r)
        mq, mk, mv = _mla_up(l, mqn, mkvn, proj, w_mq_p, w_mkv_b, tabs_cos, tabs_sin,
                             gqn, gqr, gkn, gkr)
        o_a = _dsa(qa, ka, va, qi, ki2, wi)
        o_b = _mla(mq, mk, mv)
        xf = _merge(l, o_a, o_b, proj, w_pa_b, w_pb_b, w_o_b, xf, mod)
        act = _ffn_up(l, xf, g_ffn_r, mod, w_up_b, w_conv, b_conv_r)
        xf = _ffn_down(l, act, w_down, xf, mod)
    return xf.reshape(BATCH, SEQ, D_MODEL)
```

```python
import functools

import jax
import jax.numpy as jnp
import numpy as np
from jax import lax
from jax.experimental import pallas as pl
from jax.experimental.pallas import tpu as pltpu

D_MODEL = 2048
BATCH = 4
SEQ = 2048
DEPTH = 4
A_HEADS = 8
A_KV_HEADS = 2
A_HEAD_DIM = 128
IDX_HEADS = 16
IDX_DIM = 64
TOPK_MAX = 256
M_HEADS = 8
M_Q_LORA = 512
M_KV_LORA = 256
M_NOPE = 128
M_ROPE = 64
M_V = 128
D_FF = 5632
CONV_W = 3
ROPE_THETA = 500000.0
ROT_FRACTION = 4
EPS = 1e-6
N_ADA = 6
IDX_W_SCALE = (IDX_HEADS * IDX_DIM) ** -0.5
N_SEL = min(TOPK_MAX, SEQ // 4)
TOKENS = BATCH * SEQ
LANES = 128
M_QK = M_NOPE + M_ROPE
M_QK_PAD = 256

BF16 = jnp.bfloat16
F32 = jnp.float32

COL_GATE_A = 0
COL_GATE_B = 2048
COL_QA = 4096
COL_QI = 5120
COL_MQL = 6144
COL_KV = 6656
COL_MKVL = 7168
COL_KIWI = 7424
COL_MKR = 7552
IN_PACKED = 7680

VMEM_LIMIT = 56 * 1024 * 1024


def _params(*sem):
    return pltpu.CompilerParams(dimension_semantics=sem, vmem_limit_bytes=VMEM_LIMIT)


def _dot(a, b):
    return jnp.dot(a, b, preferred_element_type=F32)


def _dot_nt(a, b):
    return lax.dot_general(a, b, (((1,), (1,)), ((), ())), preferred_element_type=F32)


def _ada_kernel(c_ref, w_ref, b_ref, o_ref):
    c = c_ref[...]
    c_act = (c * jax.nn.sigmoid(c)).astype(BF16)
    o_ref[0] = _dot(c_act, w_ref[0].astype(BF16)) + b_ref[0]


def _ada(c8, w_ada, b_ada):
    tn = 1024
    n = N_ADA * D_MODEL
    return pl.pallas_call(
        _ada_kernel,
        grid=(DEPTH, n // tn),
        in_specs=[
            pl.BlockSpec((8, D_MODEL), lambda l, j: (0, 0)),
            pl.BlockSpec((1, D_MODEL, tn), lambda l, j: (l, 0, j)),
            pl.BlockSpec((1, 1, tn), lambda l, j: (l, 0, j)),
        ],
        out_specs=pl.BlockSpec((1, 8, tn), lambda l, j: (l, 0, j)),
        out_shape=jax.ShapeDtypeStruct((DEPTH, 8, n), F32),
        compiler_params=_params("arbitrary", "arbitrary"),
        name="ada",
    )(c8, w_ada, b_ada.reshape(DEPTH, 1, n))


def _rope_table_kernel(pos_ref, inv_ref, sgn_ref, cos_ref, sin_ref):
    ang = pos_ref[...] * inv_ref[0]
    cos_ref[0] = jnp.cos(ang)
    sin_ref[0] = jnp.sin(ang) * sgn_ref[0]


def _rope_tables(pos_col, inv3, sgn3):
    tm = 512
    return pl.pallas_call(
        _rope_table_kernel,
        grid=(3, TOKENS // tm),
        in_specs=[
            pl.BlockSpec((tm, 1), lambda t, i: (i, 0)),
            pl.BlockSpec((1, 1, LANES), lambda t, i: (t, 0, 0)),
            pl.BlockSpec((1, 1, LANES), lambda t, i: (t, 0, 0)),
        ],
        out_specs=[
            pl.BlockSpec((1, tm, LANES), lambda t, i: (t, i, 0)),
            pl.BlockSpec((1, tm, LANES), lambda t, i: (t, i, 0)),
        ],
        out_shape=[jax.ShapeDtypeStruct((3, TOKENS, LANES), F32)] * 2,
        compiler_params=_params("arbitrary", "arbitrary"),
        name="rope_tables",
    )(pos_col, inv3, sgn3)


def _rope(t, cos, sin_signed, swap):
    hi = t.astype(BF16)
    lo = (t - hi.astype(F32)).astype(BF16)
    swapped = _dot(jnp.concatenate([hi, lo], axis=1), swap)
    return t * cos + swapped * sin_signed


def _swap_matrix(half, group):
    p = np.zeros((LANES, LANES), np.float32)
    for j in range(LANES):
        g = j % group
        if g < half:
            p[j + half, j] = 1.0
        elif g < 2 * half:
            p[j - half, j] = 1.0
    return np.concatenate([p, p], axis=0)


def _rms(t, g, width):
    ms = jnp.sum(t * t, axis=-1, keepdims=True) * (1.0 / width)
    return t * lax.rsqrt(ms + EPS) * g


def _modnorm(x, g, sc, sh):
    ms = jnp.mean(x * x, axis=-1, keepdims=True)
    y = x * lax.rsqrt(ms + EPS) * g
    return y * (1.0 + sc) + sh


def _in_proj_kernel(x_ref, g_ref, sc_ref, sh_ref, w_ref, o_ref, h_ref):
    @pl.when(pl.program_id(1) == 0)
    def _():
        h_ref[...] = _modnorm(x_ref[...], g_ref[...], sc_ref[...], sh_ref[...]).astype(BF16)

    o_ref[...] = _dot_nt(h_ref[...], w_ref[...])


def _mod_spec(l, k, per_b, width=D_MODEL):
    if width == D_MODEL:
        return pl.BlockSpec((None, None, None, 1, D_MODEL), lambda i, *_: (l, i // per_b, k, 0, 0))
    return pl.BlockSpec((None, None, None, 1, width), lambda i, j: (l, i // per_b, k, 0, j))


def _layer_row_spec(l, width):
    return pl.BlockSpec((None, 1, width), lambda *_: (l, 0, 0))


def _in_proj(l, x, g, mod, w):
    tm, tn = 1024, 768
    per_b = SEQ // tm
    n = w.shape[1]
    return pl.pallas_call(
        _in_proj_kernel,
        grid=(TOKENS // tm, n // tn),
        in_specs=[
            pl.BlockSpec((tm, D_MODEL), lambda i, j: (i, 0)),
            _layer_row_spec(l, D_MODEL),
            _mod_spec(l, 1, per_b),
            _mod_spec(l, 0, per_b),
            pl.BlockSpec((None, tn, D_MODEL), lambda i, j: (l, j, 0)),
        ],
        out_specs=pl.BlockSpec((tm, tn), lambda i, j: (i, j)),
        out_shape=jax.ShapeDtypeStruct((TOKENS, n), F32),
        scratch_shapes=[pltpu.VMEM((tm, D_MODEL), BF16)],
        compiler_params=_params("arbitrary", "arbitrary"),
        name="in_proj",
    )(x, g, mod, mod, w)


def _prep_kernel(qa_ref, qi_ref, mql_ref, kv_ref, mkvl_ref, kiwi_ref,
                 ca_ref, sa_ref, ci_ref, si_ref, pa_ref, pi_ref,
                 gqa_ref, gka_ref, gmq_ref, gmkv_ref,
                 qa_o, ka_o, va_o, qi_o, ki2_o, wi_o, mqn_o, mkvn_o):
    tm = qa_ref.shape[0]
    lane = lax.broadcasted_iota(jnp.int32, (tm, LANES), 1)
    ca, sa, pa = ca_ref[0], sa_ref[0], pa_ref[0]
    ci, si, pi = ci_ref[0], si_ref[0], pi_ref[0]
    scale_a = A_HEAD_DIM ** -0.5

    for h in range(A_HEADS):
        t = qa_ref[:, h * LANES:(h + 1) * LANES]
        r = _rope(_rms(t, gqa_ref[...], A_HEAD_DIM), ca, sa, pa)
        qa_o[h] = (r * scale_a).astype(BF16)
    for g in range(A_KV_HEADS):
        t = kv_ref[:, g * LANES:(g + 1) * LANES]
        r = _rope(_rms(t, gka_ref[...], A_HEAD_DIM), ca, sa, pa)
        ka_o[g] = r.astype(BF16)
        va_o[g] = kv_ref[:, (A_KV_HEADS + g) * LANES:(A_KV_HEADS + g + 1) * LANES].astype(BF16)
    for p in range(IDX_HEADS // 2):
        t = qi_ref[:, p * LANES:(p + 1) * LANES]
        r = _rope(t, ci, si, pi)
        qi_o[2 * p] = jnp.where(lane < IDX_DIM, r, 0.0).astype(BF16)
        qi_o[2 * p + 1] = jnp.where(lane < IDX_DIM, 0.0, r).astype(BF16)

    kiwi = kiwi_ref[...]
    ki = _rope(kiwi, ci, si, pi)
    ki2_o[...] = jnp.where(lane < IDX_DIM, ki, pltpu.roll(ki, IDX_DIM, 1)).astype(BF16)
    wi_o[...] = kiwi * IDX_W_SCALE

    mqn_o[...] = _rms(mql_ref[...], gmq_ref[...], M_Q_LORA).astype(BF16)
    mkvn_o[...] = _rms(mkvl_ref[...], gmkv_ref[...], M_KV_LORA).astype(BF16)


def _prep(l, proj, tabs_cos, tabs_sin, swaps, g_qa, g_ka, g_mq_lat, g_mkv_lat):
    tm = 256
    per_b = SEQ // tm

    def swap(t):
        return pl.BlockSpec((1, 2 * LANES, LANES), lambda i: (t, 0, 0))

    def col(width, start):
        idx = start // width
        return pl.BlockSpec((tm, width), lambda i: (i, idx))

    def tab(t):
        return pl.BlockSpec((1, tm, LANES), lambda i: (t, i, 0))

    def vec(width):
        return _layer_row_spec(l, width)

    def heads(n):
        return pl.BlockSpec((None, n, tm, LANES), lambda i: (i // per_b, 0, i % per_b, 0))

    def seq(width):
        return pl.BlockSpec((None, tm, width), lambda i: (i // per_b, i % per_b, 0))

    def tok(width):
        return pl.BlockSpec((tm, width), lambda i: (i, 0))

    return pl.pallas_call(
        _prep_kernel,
        grid=(TOKENS // tm,),
        in_specs=[
            col(1024, COL_QA), col(1024, COL_QI), col(512, COL_MQL), col(512, COL_KV),
            col(256, COL_MKVL), col(128, COL_KIWI),
            tab(0), tab(0), tab(1), tab(1), swap(0), swap(1),
            vec(A_HEAD_DIM), vec(A_HEAD_DIM), vec(M_Q_LORA), vec(M_KV_LORA),
        ],
        out_specs=[
            heads(A_HEADS), heads(A_KV_HEADS), heads(A_KV_HEADS),
            heads(IDX_HEADS), seq(LANES), seq(LANES),
            tok(M_Q_LORA), tok(M_KV_LORA),
        ],
        out_shape=[
            jax.ShapeDtypeStruct((BATCH, A_HEADS, SEQ, LANES), BF16),
            jax.ShapeDtypeStruct((BATCH, A_KV_HEADS, SEQ, LANES), BF16),
            jax.ShapeDtypeStruct((BATCH, A_KV_HEADS, SEQ, LANES), BF16),
            jax.ShapeDtypeStruct((BATCH, IDX_HEADS, SEQ, LANES), BF16),
            jax.ShapeDtypeStruct((BATCH, SEQ, LANES), BF16),
            jax.ShapeDtypeStruct((BATCH, SEQ, LANES), F32),
            jax.ShapeDtypeStruct((TOKENS, M_Q_LORA), BF16),
            jax.ShapeDtypeStruct((TOKENS, M_KV_LORA), BF16),
        ],
        compiler_params=_params("arbitrary"),
        name="prep",
    )(proj, proj, proj, proj, proj, proj, tabs_cos, tabs_sin, tabs_cos, tabs_sin, swaps, swaps,
      g_qa, g_ka, g_mq_lat, g_mkv_lat)


def _mla_up_kernel(mqn_ref, mkvn_ref, mkr_ref, wq_ref, wkv_ref, cm_ref, sm_ref, pm_ref,
                   gqn_ref, gqr_ref, gkn_ref, gkr_ref, q_o, k_o, v_o):
    tm = mqn_ref.shape[0]
    lane = lax.broadcasted_iota(jnp.int32, (tm, LANES), 1)
    low = lane < M_ROPE
    cm, sm, pm = cm_ref[0], sm_ref[0], pm_ref[0]
    scale_m = M_QK ** -0.5
    nope_w = M_HEADS * M_NOPE

    q = _dot(mqn_ref[...], wq_ref[...])
    kv = _dot(mkvn_ref[...], wkv_ref[...])

    for p in range(M_HEADS // 2):
        rp = q[:, nope_w + p * LANES: nope_w + (p + 1) * LANES]
        sq = rp * rp
        s_all = jnp.sum(sq, axis=-1, keepdims=True)
        s_lo = jnp.sum(jnp.where(low, sq, 0.0), axis=-1, keepdims=True)
        for e in range(2):
            h = 2 * p + e
            nope = q[:, h * M_NOPE:(h + 1) * M_NOPE]
            ss = jnp.sum(nope * nope, axis=-1, keepdims=True) + (s_lo if e == 0 else s_all - s_lo)
            rs = lax.rsqrt(ss * (1.0 / M_QK) + EPS)
            roped = _rope(rp * rs * gqr_ref[...], cm, sm, pm)
            if e == 1:
                roped = pltpu.roll(roped, M_ROPE, 1)
            q_o[h, :, 0:LANES] = (nope * rs * gqn_ref[...] * scale_m).astype(BF16)
            q_o[h, :, LANES:2 * LANES] = (jnp.where(low, roped, 0.0) * scale_m).astype(BF16)

    kr = mkr_ref[...]
    kr_ss = jnp.sum(kr * kr, axis=-1, keepdims=True)
    kr_roped = _rope(kr * gkr_ref[...], cm, sm, pm)
    kr_roped = jnp.where(low, kr_roped, 0.0)
    for h in range(M_HEADS):
        nope = kv[:, h * 2 * LANES: h * 2 * LANES + M_NOPE]
        ss = jnp.sum(nope * nope, axis=-1, keepdims=True) + kr_ss
        rs = lax.rsqrt(ss * (1.0 / M_QK) + EPS)
        k_o[h, :, 0:LANES] = (nope * rs * gkn_ref[...]).astype(BF16)
        k_o[h, :, LANES:2 * LANES] = (kr_roped * rs).astype(BF16)
        v_o[h] = kv[:, h * 2 * LANES + M_NOPE:(h + 1) * 2 * LANES].astype(BF16)


def _mla_up(l, mqn, mkvn, proj, wq, wkv, tabs_cos, tabs_sin, swaps, gqn, gqr, gkn, gkr):
    tm = 256
    per_b = SEQ // tm

    def vec():
        return _layer_row_spec(l, LANES)

    def heads(width):
        return pl.BlockSpec((None, M_HEADS, tm, width), lambda i: (i // per_b, 0, i % per_b, 0))

    return pl.pallas_call(
        _mla_up_kernel,
        grid=(TOKENS // tm,),
        in_specs=[
            pl.BlockSpec((tm, M_Q_LORA), lambda i: (i, 0)),
            pl.BlockSpec((tm, M_KV_LORA), lambda i: (i, 0)),
            pl.BlockSpec((tm, LANES), lambda i: (i, COL_MKR // LANES)),
            pl.BlockSpec((None,) + wq.shape[1:], lambda i: (l, 0, 0)),
            pl.BlockSpec((None,) + wkv.shape[1:], lambda i: (l, 0, 0)),
            pl.BlockSpec((1, tm, LANES), lambda i: (2, i, 0)),
            pl.BlockSpec((1, tm, LANES), lambda i: (2, i, 0)),
            pl.BlockSpec((1, 2 * LANES, LANES), lambda i: (2, 0, 0)),
            vec(), vec(), vec(), vec(),
        ],
        out_specs=[heads(M_QK_PAD), heads(M_QK_PAD), heads(M_V)],
        out_shape=[
            jax.ShapeDtypeStruct((BATCH, M_HEADS, SEQ, M_QK_PAD), BF16),
            jax.ShapeDtypeStruct((BATCH, M_HEADS, SEQ, M_QK_PAD), BF16),
            jax.ShapeDtypeStruct((BATCH, M_HEADS, SEQ, M_V), BF16),
        ],
        compiler_params=_params("arbitrary"),
        name="mla_up",
    )(mqn, mkvn, proj, wq, wkv, tabs_cos, tabs_sin, swaps, gqn, gqr, gkn, gkr)


INT_MIN = -(2 ** 31)
KEY_MASKED = INT_MIN

DSA_TQ = 256
KEY_CHUNK = 512
SEARCH_ROWS = 128
SEARCH_UNROLL = 5
IDX_CHUNK = 256
IDX_ROWS = 64
ATT_SPLIT = 4


def _dsa_index(i, n_chunks, qi_ref, ki2_ref, wi_ref, key_ref, d_refs, wb_ref):
    tq = qi_ref.shape[1]
    w = wi_ref[...]
    for h in range(IDX_HEADS):
        wb_ref[h] = jnp.broadcast_to(w[:, IDX_DIM + h:IDX_DIM + h + 1], (tq, LANES))

    def matmul(c, d_ref):
        start = pl.multiple_of(c * IDX_CHUNK, IDX_CHUNK)
        q_all = qi_ref[...].reshape(IDX_HEADS * tq, LANES)
        d_ref[...] = _dot_nt(q_all, ki2_ref[pl.ds(start, IDX_CHUNK), :])

    def head_sum(c, d_ref):
        for r0 in range(0, tq, IDX_ROWS):
            row_t = i * tq + r0 + lax.broadcasted_iota(jnp.int32, (IDX_ROWS, LANES), 0)
            for j0 in range(0, IDX_CHUNK, LANES):
                acc = jnp.zeros((IDX_ROWS, LANES), F32)
                for h in range(IDX_HEADS):
                    d = d_ref[h * tq + r0:h * tq + r0 + IDX_ROWS, j0:j0 + LANES]
                    acc = acc + jnp.maximum(d, 0.0) * wb_ref[h, r0:r0 + IDX_ROWS, :]
                bits = pltpu.bitcast(acc, jnp.int32)
                img = bits ^ ((bits >> 31) & jnp.int32(0x7FFFFFFF))
                col_t = c * IDX_CHUNK + j0 + lax.broadcasted_iota(jnp.int32, (IDX_ROWS, LANES), 1)
                key_ref[c, r0:r0 + IDX_ROWS, j0:j0 + LANES] = jnp.where(col_t <= row_t, img, KEY_MASKED)

    d0_ref, d1_ref = d_refs
    matmul(0, d0_ref)

    def pair(k, carry):
        c = 2 * k
        matmul(c + 1, d1_ref)
        head_sum(c, d0_ref)
        matmul(jnp.minimum(c + 2, n_chunks - 1), d0_ref)
        head_sum(c + 1, d1_ref)
        return carry

    lax.fori_loop(0, n_chunks // 2, pair, 0)


def _dsa_select(nk, key_ref, bias_ref):
    chunks = range(nk // IDX_CHUNK)
    n_sel = float(N_SEL)
    groups = list(range(0, key_ref.shape[1], SEARCH_ROWS))

    def count_ge(r0, cand):
        hits = None
        for c in chunks:
            hit = jnp.where(key_ref[c, r0:r0 + SEARCH_ROWS, :] >= cand, 1.0, 0.0)
            hits = hit if hits is None else hits + hit
        return jnp.sum(hits, axis=-1, keepdims=True)

    zero = jnp.zeros((SEARCH_ROWS, 1), jnp.int32)
    thr0 = tuple(jnp.where(count_ge(r0, zero) >= n_sel, jnp.int32(0), jnp.int32(INT_MIN))
                 for r0 in groups)

    def step(b, thrs):
        bit = lax.shift_left(jnp.int32(1), jnp.int32(30) - b)
        out = []
        for r0, thr in zip(groups, thrs):
            cand = thr | bit
            out.append(jnp.where(count_ge(r0, cand) >= n_sel, cand, thr))
        return tuple(out)

    thrs = lax.fori_loop(0, 30, step, thr0, unroll=SEARCH_UNROLL)
    thrs = step(jnp.int32(30), thrs)
    for r0, thr in zip(groups, thrs):
        floor = jnp.maximum(thr, jnp.int32(KEY_MASKED + 1))
        for c in chunks:
            sel = key_ref[c, r0:r0 + SEARCH_ROWS, :] >= floor
            bias_ref[r0:r0 + SEARCH_ROWS, c * IDX_CHUNK:(c + 1) * IDX_CHUNK] = jnp.where(sel, 0.0, -jnp.inf)


def _dsa_attend(nk, qa_ref, ka_ref, va_ref, bias_ref, o_ref, oh_ref, lg_refs, p_refs):
    tq = qa_ref.shape[1]
    rep = A_HEADS // A_KV_HEADS

    sub = rep // ATT_SPLIT

    def group(g, carry):
        k = ka_ref[g, 0:nk, :]
        v = va_ref[g, 0:nk, :]
        inv_l = [None] * ATT_SPLIT

        def scores(s):
            q = qa_ref[pl.ds(g * rep + s * sub, sub)].reshape(sub * tq, LANES)
            lg_refs[s][:, 0:nk] = _dot_nt(q, k)

        def softmax(s):
            logits = lg_refs[s][:, 0:nk].reshape(sub, tq, nk) + bias_ref[:, 0:nk][None]
            m = jnp.max(logits, axis=-1, keepdims=True)
            p = jnp.exp(logits - m)
            inv_l[s] = 1.0 / jnp.sum(p, axis=-1, keepdims=True)
            p_refs[s][:, 0:nk] = p.astype(BF16).reshape(sub * tq, nk)

        def values(s):
            o = _dot(p_refs[s][:, 0:nk], v).reshape(sub, tq, LANES) * inv_l[s]
            oh_ref[pl.ds(g * rep + s * sub, sub)] = o.astype(BF16)

        for t in range(ATT_SPLIT + 2):
            if t < ATT_SPLIT:
                scores(t)
            if 0 <= t - 1 < ATT_SPLIT:
                softmax(t - 1)
            if 0 <= t - 2 < ATT_SPLIT:
                values(t - 2)
        return carry

    lax.fori_loop(0, A_KV_HEADS, group, 0)
    for h in range(A_HEADS):
        o_ref[:, h * LANES:(h + 1) * LANES] = oh_ref[h]


def _dsa_kernel(qa_ref, ka_ref, va_ref, qi_ref, ki2_ref, wi_ref, o_ref,
                key_ref, bias_ref, d0_ref, d1_ref, wb_ref, oh_ref, *stage_refs):
    lg_refs, p_refs = stage_refs[:ATT_SPLIT], stage_refs[ATT_SPLIT:]
    i = pl.program_id(1)
    tq = DSA_TQ
    n_free = N_SEL // tq
    variant = i // (KEY_CHUNK // tq)

    for v in range(n_free):
        @pl.when(i == v)
        def _(v=v):
            nk = (v + 1) * tq
            row = i * tq + lax.broadcasted_iota(jnp.int32, (tq, nk), 0)
            col = lax.broadcasted_iota(jnp.int32, (tq, nk), 1)
            bias_ref[:, 0:nk] = jnp.where(col <= row, 0.0, -jnp.inf)
            _dsa_attend(nk, qa_ref, ka_ref, va_ref, bias_ref, o_ref, oh_ref, lg_refs, p_refs)

    @pl.when(i >= n_free)
    def _():
        n_chunks = (variant + 1) * (KEY_CHUNK // IDX_CHUNK)
        _dsa_index(i, n_chunks, qi_ref, ki2_ref, wi_ref, key_ref, (d0_ref, d1_ref), wb_ref)

    for v in range(SEQ // KEY_CHUNK):
        @pl.when((i >= n_free) & (variant == v))
        def _(v=v):
            nk = (v + 1) * KEY_CHUNK
            _dsa_select(nk, key_ref, bias_ref)
            _dsa_attend(nk, qa_ref, ka_ref, va_ref, bias_ref, o_ref, oh_ref, lg_refs, p_refs)


def _dsa(qa, ka, va, qi, ki2, wi):
    tq = DSA_TQ
    nq = SEQ // tq
    att_rows = A_HEADS // A_KV_HEADS // ATT_SPLIT * tq
    return pl.pallas_call(
        _dsa_kernel,
        grid=(BATCH, nq),
        in_specs=[
            pl.BlockSpec((None, A_HEADS, tq, LANES), lambda b, i: (b, 0, i, 0)),
            pl.BlockSpec((None, A_KV_HEADS, SEQ, LANES), lambda b, i: (b, 0, 0, 0)),
            pl.BlockSpec((None, A_KV_HEADS, SEQ, LANES), lambda b, i: (b, 0, 0, 0)),
            pl.BlockSpec((None, IDX_HEADS, tq, LANES), lambda b, i: (b, 0, i, 0)),
            pl.BlockSpec((None, SEQ, LANES), lambda b, i: (b, 0, 0)),
            pl.BlockSpec((None, tq, LANES), lambda b, i: (b, i, 0)),
        ],
        out_specs=pl.BlockSpec((tq, A_HEADS * A_HEAD_DIM), lambda b, i: (b * nq + i, 0)),
        out_shape=jax.ShapeDtypeStruct((TOKENS, A_HEADS * A_HEAD_DIM), BF16),
        scratch_shapes=[
            pltpu.VMEM((SEQ // IDX_CHUNK, tq, IDX_CHUNK), jnp.int32),
            pltpu.VMEM((tq, SEQ), F32),
            pltpu.VMEM((IDX_HEADS * tq, IDX_CHUNK), F32),
            pltpu.VMEM((IDX_HEADS * tq, IDX_CHUNK), F32),
            pltpu.VMEM((IDX_HEADS, tq, LANES), F32),
            pltpu.VMEM((A_HEADS, tq, LANES), BF16),
        ] + [pltpu.VMEM((att_rows, SEQ), F32)] * ATT_SPLIT + [pltpu.VMEM((att_rows, SEQ), BF16)] * ATT_SPLIT,
        compiler_params=_params("arbitrary", "arbitrary"),
        name="dsa",
    )(qa, ka, va, qi, ki2, wi)


MLA_TQ = 256
MLA_HEADS_PER_STEP = 4


def _mla_body(nk, i, q_ref, k_ref, v_ref, o_ref):
    tq = q_ref.shape[1]
    n0 = nk - KEY_CHUNK
    row = i * tq + lax.broadcasted_iota(jnp.int32, (tq, KEY_CHUNK), 0)
    col = n0 + lax.broadcasted_iota(jnp.int32, (tq, KEY_CHUNK), 1)
    causal = col <= row
    for h in range(q_ref.shape[0]):
        q = q_ref[h]
        s1 = jnp.where(causal, _dot_nt(q, k_ref[h, n0:nk, :]), -jnp.inf)
        m = jnp.max(s1, axis=-1, keepdims=True)
        if n0:
            s0 = _dot_nt(q, k_ref[h, 0:n0, :])
            m = jnp.maximum(m, jnp.max(s0, axis=-1, keepdims=True))
        p1 = jnp.exp(s1 - m)
        l = jnp.sum(p1, axis=-1, keepdims=True)
        o = _dot(p1.astype(BF16), v_ref[h, n0:nk, :])
        if n0:
            p0 = jnp.exp(s0 - m)
            l = l + jnp.sum(p0, axis=-1, keepdims=True)
            o = o + _dot(p0.astype(BF16), v_ref[h, 0:n0, :])
        o_ref[:, h * M_V:(h + 1) * M_V] = (o / l).astype(BF16)


def _mla_kernel(q_ref, k_ref, v_ref, o_ref):
    i = pl.program_id(2)
    per_chunk = KEY_CHUNK // MLA_TQ
    for v in range(SEQ // KEY_CHUNK):
        @pl.when(i // per_chunk == v)
        def _(v=v):
            _mla_body((v + 1) * KEY_CHUNK, i, q_ref, k_ref, v_ref, o_ref)


def _mla(q, k, v):
    tq, hg = MLA_TQ, MLA_HEADS_PER_STEP
    nq = SEQ // tq
    return pl.pallas_call(
        _mla_kernel,
        grid=(BATCH, M_HEADS // hg, nq),
        in_specs=[
            pl.BlockSpec((None, hg, tq, M_QK_PAD), lambda b, g, i: (b, g, i, 0)),
            pl.BlockSpec((None, hg, SEQ, M_QK_PAD), lambda b, g, i: (b, g, 0, 0)),
            pl.BlockSpec((None, hg, SEQ, M_V), lambda b, g, i: (b, g, 0, 0)),
        ],
        out_specs=pl.BlockSpec((tq, hg * M_V), lambda b, g, i: (b * nq + i, g)),
        out_shape=jax.ShapeDtypeStruct((TOKENS, M_HEADS * M_V), BF16),
        compiler_params=_params("arbitrary", "arbitrary", "arbitrary"),
        name="mla",
    )(q, k, v)


def _merge_kernel(oa_ref, ob_ref, ga_ref, gb_ref, wpa_ref, wpb_ref, wo_ref, x_ref, gt_ref, o_ref):
    a = _dot(oa_ref[...], wpa_ref[...])
    b = _dot(ob_ref[...], wpb_ref[...])
    merged = jax.nn.sigmoid(ga_ref[...]) * a + jax.nn.sigmoid(gb_ref[...]) * b
    y = _dot(merged.astype(BF16), wo_ref[...])
    o_ref[...] = x_ref[...] + gt_ref[...] * y


def _merge(l, oa, ob, proj, wpa, wpb, wo, x, mod):
    tm = 256
    per_b = SEQ // tm

    def resident(shape):
        return pl.BlockSpec((None,) + shape[1:], lambda i: (l, 0, 0), pipeline_mode=pl.Buffered(1))

    return pl.pallas_call(
        _merge_kernel,
        grid=(TOKENS // tm,),
        in_specs=[
            pl.BlockSpec((tm, A_HEADS * A_HEAD_DIM), lambda i: (i, 0)),
            pl.BlockSpec((tm, M_HEADS * M_V), lambda i: (i, 0)),
            pl.BlockSpec((tm, D_MODEL), lambda i: (i, COL_GATE_A // D_MODEL)),
            pl.BlockSpec((tm, D_MODEL), lambda i: (i, COL_GATE_B // D_MODEL)),
            resident(wpa.shape), resident(wpb.shape), resident(wo.shape),
            pl.BlockSpec((tm, D_MODEL), lambda i: (i, 0)),
            _mod_spec(l, 2, per_b),
        ],
        out_specs=pl.BlockSpec((tm, D_MODEL), lambda i: (i, 0)),
        out_shape=jax.ShapeDtypeStruct((TOKENS, D_MODEL), F32),
        compiler_params=_params("arbitrary"),
        name="merge",
    )(oa, ob, proj, proj, wpa, wpb, wo, x, mod)


HALO = 8
FFN_SPLIT = 1


def _ffn_up_kernel(x_ref, g_ref, sc_ref, sh_ref, wg_ref, wv_ref, cg_ref, cv_ref, bg_ref, bv_ref,
                   o_ref, h_ref, halo_ref, *buf_refs, tiles_per_seq):
    i, j = pl.program_id(0), pl.program_id(1)
    tm, tn = o_ref.shape
    w = tn // FFN_SPLIT

    @pl.when(j == 0)
    def _():
        h_ref[...] = _modnorm(x_ref[...], g_ref[...], sc_ref[...], sh_ref[...]).astype(BF16)

    seq_start = (i % tiles_per_seq) == 0

    @pl.when(seq_start)
    def _():
        for buf_ref in buf_refs:
            buf_ref[0:HALO, :] = jnp.zeros((HALO, 2 * w), F32)

    @pl.when(jnp.logical_not(seq_start))
    def _():
        for s, buf_ref in enumerate(buf_refs):
            buf_ref[0:HALO, :] = halo_ref[j, :, s * 2 * w:(s + 1) * 2 * w]

    h = h_ref[...]
    for s, buf_ref in enumerate(buf_refs):
        c = slice(s * w, (s + 1) * w)
        buf_ref[HALO:HALO + tm, 0:w] = _dot(h, wg_ref[:, c])
        buf_ref[HALO:HALO + tm, w:2 * w] = _dot(h, wv_ref[:, c])
        halo_ref[j, :, s * 2 * w:(s + 1) * 2 * w] = buf_ref[tm:tm + HALO, :]

    def conv(buf_ref, lo, c, c_ref, b_ref):
        u0 = buf_ref[HALO:HALO + tm, lo:lo + w]
        u1 = buf_ref[HALO - 1:HALO - 1 + tm, lo:lo + w]
        u2 = buf_ref[HALO - 2:HALO - 2 + tm, lo:lo + w]
        return b_ref[:, c] + c_ref[0:1, c] * u2 + c_ref[1:2, c] * u1 + c_ref[2:3, c] * u0

    for s, buf_ref in enumerate(buf_refs):
        c = slice(s * w, (s + 1) * w)
        gate = conv(buf_ref, 0, c, cg_ref, bg_ref)
        val = conv(buf_ref, w, c, cv_ref, bv_ref)
        o_ref[:, c] = (gate * jax.nn.sigmoid(gate) * val).astype(BF16)


def _ffn_up(l, x, g, mod, w_up, w_conv, b_conv):
    tm, tn = 1024, 512
    per_b = SEQ // tm
    nj = D_FF // tn
    kern = functools.partial(_ffn_up_kernel, tiles_per_seq=per_b)
    return pl.pallas_call(
        kern,
        grid=(TOKENS // tm, nj),
        in_specs=[
            pl.BlockSpec((tm, D_MODEL), lambda i, j: (i, 0)),
            _layer_row_spec(l, D_MODEL),
            _mod_spec(l, 4, per_b),
            _mod_spec(l, 3, per_b),
            pl.BlockSpec((None, D_MODEL, tn), lambda i, j: (l, 0, j)),
            pl.BlockSpec((None, D_MODEL, tn), lambda i, j: (l, 0, nj + j)),
            pl.BlockSpec((None, CONV_W, tn), lambda i, j: (l, 0, j)),
            pl.BlockSpec((None, CONV_W, tn), lambda i, j: (l, 0, nj + j)),
            pl.BlockSpec((None, 1, tn), lambda i, j: (l, 0, j)),
            pl.BlockSpec((None, 1, tn), lambda i, j: (l, 0, nj + j)),
        ],
        out_specs=pl.BlockSpec((tm, tn), lambda i, j: (i, j)),
        out_shape=jax.ShapeDtypeStruct((TOKENS, D_FF), BF16),
        scratch_shapes=[
            pltpu.VMEM((tm, D_MODEL), BF16),
            pltpu.VMEM((nj, HALO, 2 * tn), F32),
        ] + [pltpu.VMEM((HALO + tm, 2 * tn // FFN_SPLIT), F32)] * FFN_SPLIT,
        compiler_params=_params("arbitrary", "arbitrary"),
        name="ffn_up",
    )(x, g, mod, mod, w_up, w_up, w_conv, w_conv, b_conv, b_conv)


def _ffn_down_kernel(a_ref, w_ref, x_ref, gt_ref, o_ref):
    o_ref[...] = x_ref[...] + gt_ref[...] * _dot(a_ref[...], w_ref[...].astype(BF16))


def _ffn_down(l, act, w_down, x, mod):
    tm, tn = 1024, 256
    per_b = SEQ // tm
    return pl.pallas_call(
        _ffn_down_kernel,
        grid=(TOKENS // tm, D_MODEL // tn),
        in_specs=[
            pl.BlockSpec((tm, D_FF), lambda i, j: (i, 0)),
            pl.BlockSpec((None, D_FF, tn), lambda i, j: (l, 0, j)),
            pl.BlockSpec((tm, tn), lambda i, j: (i, j)),
            _mod_spec(l, 5, per_b, tn),
        ],
        out_specs=pl.BlockSpec((tm, tn), lambda i, j: (i, j)),
        out_shape=jax.ShapeDtypeStruct((TOKENS, D_MODEL), F32),
        compiler_params=_params("arbitrary", "arbitrary"),
        name="ffn_down",
    )(act, w_down, x, mod)


def _pack_w_in(w_in):
    o = np.cumsum((0,) + (1024, 256, 256, 1024, 64, 16, 512, 256, 64, 2048, 2048))
    w_t = jnp.swapaxes(w_in, 1, 2)
    qa, ka, va, qi, ki, wi, mql, mkvl, mkr, ga, gb = (
        w_t[:, int(o[k]):int(o[k + 1])].astype(BF16) for k in range(11))
    z = lambda n: jnp.zeros((DEPTH, n, D_MODEL), BF16)
    packed = jnp.concatenate(
        [ga, gb, qa, qi, mql, ka, va, mkvl, ki, wi, z(48), mkr, z(64)], axis=1)
    assert packed.shape[1] == IN_PACKED
    return packed


def _pack_w_mq(w_mq_up):
    w = w_mq_up.reshape(DEPTH, M_Q_LORA, M_HEADS, M_QK)
    nope = w[..., :M_NOPE].reshape(DEPTH, M_Q_LORA, M_HEADS * M_NOPE)
    rope = w[..., M_NOPE:].reshape(DEPTH, M_Q_LORA, M_HEADS * M_ROPE)
    return jnp.concatenate([nope, rope], axis=-1).astype(BF16)


def _rope_consts():
    def inv(rot):
        return ROPE_THETA ** (-jnp.arange(0, rot, 2, dtype=F32) / rot)

    def lanes(v, group):
        rep = jnp.concatenate([v, v])
        pad = jnp.zeros((group - rep.shape[0],), F32)
        return jnp.tile(jnp.concatenate([rep, pad]), LANES // group)

    def sgn(half, group):
        v = jnp.concatenate([-jnp.ones((half,), F32), jnp.ones((half,), F32)])
        pad = jnp.zeros((group - 2 * half,), F32)
        return jnp.tile(jnp.concatenate([v, pad]), LANES // group)

    rot_a = A_HEAD_DIM // ROT_FRACTION
    rot_i = IDX_DIM // ROT_FRACTION
    inv3 = jnp.stack([lanes(inv(rot_a), LANES), lanes(inv(rot_i), IDX_DIM), lanes(inv(M_ROPE), M_ROPE)])
    sgn3 = jnp.stack([sgn(rot_a // 2, LANES), sgn(rot_i // 2, IDX_DIM), sgn(M_ROPE // 2, M_ROPE)])
    return inv3.reshape(3, 1, LANES), sgn3.reshape(3, 1, LANES)


def kernel(x, c, positions, g_attn, g_ffn, w_ada, b_ada, w_in, g_qa, g_ka, g_mq_lat, w_mq_up,
           g_mkv_lat, w_mkv_up, g_qm, g_km, w_pa, w_pb, w_o, w_up, w_conv, b_conv, w_down):
    pos_col = positions.astype(F32).reshape(TOKENS, 1)
    inv3, sgn3 = _rope_consts()
    tabs_cos, tabs_sin = _rope_tables(pos_col, inv3, sgn3)
    rot_a, rot_i = A_HEAD_DIM // ROT_FRACTION, IDX_DIM // ROT_FRACTION
    swaps = jnp.asarray(np.stack([_swap_matrix(rot_a // 2, LANES), _swap_matrix(rot_i // 2, IDX_DIM),
                                  _swap_matrix(M_ROPE // 2, M_ROPE)]), dtype=BF16)

    c8 = jnp.pad(c, ((0, 8 - BATCH), (0, 0)))
    mod = _ada(c8, w_ada, b_ada)[:, :BATCH].reshape(DEPTH, BATCH, N_ADA, 1, D_MODEL)

    w_in_p = _pack_w_in(w_in)
    w_mq_p = _pack_w_mq(w_mq_up)
    w_mkv_b = w_mkv_up.astype(BF16)
    w_pa_b, w_pb_b, w_o_b = w_pa.astype(BF16), w_pb.astype(BF16), w_o.astype(BF16)
    w_up_b = w_up.astype(BF16)

    rows = lambda v: v.reshape(DEPTH, 1, -1)
    rope_gain = lambda v: rows(jnp.tile(v[:, M_NOPE:], (1, 2)))
    g_attn_r, g_ffn_r, b_conv_r = rows(g_attn), rows(g_ffn), rows(b_conv)
    g_qa_r, g_ka_r, g_mq_r, g_mkv_r = rows(g_qa), rows(g_ka), rows(g_mq_lat), rows(g_mkv_lat)
    gqn, gqr = rows(g_qm[:, :M_NOPE]), rope_gain(g_qm)
    gkn, gkr = rows(g_km[:, :M_NOPE]), rope_gain(g_km)

    xf = x.reshape(TOKENS, D_MODEL)
    for l in range(DEPTH):
        proj = _in_proj(l, xf, g_attn_r, mod, w_in_p)
        qa, ka, va, qi, ki2, wi, mqn, mkvn = _prep(
            l, proj, tabs_cos, tabs_sin, swaps, g_qa_r, g_ka_r, g_mq_r, g_mkv_r)
        mq, mk, mv = _mla_up(l, mqn, mkvn, proj, w_mq_p, w_mkv_b, tabs_cos, tabs_sin, swaps,
                             gqn, gqr, gkn, gkr)
        o_a = _dsa(qa, ka, va, qi, ki2, wi)
        o_b = _mla(mq, mk, mv)
        xf = _merge(l, o_a, o_b, proj, w_pa_b, w_pb_b, w_o_b, xf, mod)
        act = _ffn_up(l, xf, g_ffn_r, mod, w_up_b, w_conv, b_conv_r)
        xf = _ffn_down(l, act, w_down, xf, mod)
    return xf.reshape(BATCH, SEQ, D_MODEL)
```

```python
import functools

import jax
import jax.numpy as jnp
import numpy as np
from jax import lax
from jax.experimental import pallas as pl
from jax.experimental.pallas import tpu as pltpu

D_MODEL = 2048
BATCH = 4
SEQ = 2048
DEPTH = 4
A_HEADS = 8
A_KV_HEADS = 2
A_HEAD_DIM = 128
IDX_HEADS = 16
IDX_DIM = 64
TOPK_MAX = 256
M_HEADS = 8
M_Q_LORA = 512
M_KV_LORA = 256
M_NOPE = 128
M_ROPE = 64
M_V = 128
D_FF = 5632
CONV_W = 3
ROPE_THETA = 500000.0
ROT_FRACTION = 4
EPS = 1e-6
N_ADA = 6
IDX_W_SCALE = (IDX_HEADS * IDX_DIM) ** -0.5
N_SEL = min(TOPK_MAX, SEQ // 4)
TOKENS = BATCH * SEQ
LANES = 128
M_QK = M_NOPE + M_ROPE
M_QK_PAD = 256

BF16 = jnp.bfloat16
F32 = jnp.float32

COL_GATE_A = 0
COL_GATE_B = 2048
COL_QA = 4096
COL_QI = 5120
COL_MQL = 6144
COL_KV = 6656
COL_MKVL = 7168
COL_KIWI = 7424
COL_MKR = 7552
IN_PACKED = 7680

VMEM_LIMIT = 56 * 1024 * 1024


def _params(*sem):
    return pltpu.CompilerParams(dimension_semantics=sem, vmem_limit_bytes=VMEM_LIMIT)


def _dot(a, b):
    return jnp.dot(a, b, preferred_element_type=F32)


def _dot_nt(a, b):
    return lax.dot_general(a, b, (((1,), (1,)), ((), ())), preferred_element_type=F32)


def _ada_kernel(c_ref, w_ref, b_ref, o_ref):
    c = c_ref[...]
    c_act = (c * jax.nn.sigmoid(c)).astype(BF16)
    o_ref[0] = _dot(c_act, w_ref[0].astype(BF16)) + b_ref[0]


def _ada(c8, w_ada, b_ada):
    tn = 1024
    n = N_ADA * D_MODEL
    return pl.pallas_call(
        _ada_kernel,
        grid=(DEPTH, n // tn),
        in_specs=[
            pl.BlockSpec((8, D_MODEL), lambda l, j: (0, 0)),
            pl.BlockSpec((1, D_MODEL, tn), lambda l, j: (l, 0, j)),
            pl.BlockSpec((1, 1, tn), lambda l, j: (l, 0, j)),
        ],
        out_specs=pl.BlockSpec((1, 8, tn), lambda l, j: (l, 0, j)),
        out_shape=jax.ShapeDtypeStruct((DEPTH, 8, n), F32),
        compiler_params=_params("arbitrary", "arbitrary"),
        name="ada",
    )(c8, w_ada, b_ada.reshape(DEPTH, 1, n))


def _rope_table_kernel(pos_ref, inv_ref, sgn_ref, cos_ref, sin_ref):
    ang = pos_ref[...] * inv_ref[0]
    cos_ref[0] = jnp.cos(ang)
    sin_ref[0] = jnp.sin(ang) * sgn_ref[0]


def _rope_tables(pos_col, inv3, sgn3):
    tm = 512
    return pl.pallas_call(
        _rope_table_kernel,
        grid=(3, TOKENS // tm),
        in_specs=[
            pl.BlockSpec((tm, 1), lambda t, i: (i, 0)),
            pl.BlockSpec((1, 1, LANES), lambda t, i: (t, 0, 0)),
            pl.BlockSpec((1, 1, LANES), lambda t, i: (t, 0, 0)),
        ],
        out_specs=[
            pl.BlockSpec((1, tm, LANES), lambda t, i: (t, i, 0)),
            pl.BlockSpec((1, tm, LANES), lambda t, i: (t, i, 0)),
        ],
        out_shape=[jax.ShapeDtypeStruct((3, TOKENS, LANES), F32)] * 2,
        compiler_params=_params("arbitrary", "arbitrary"),
        name="rope_tables",
    )(pos_col, inv3, sgn3)


def _rope(t, cos, sin_signed, swap):
    hi = t.astype(BF16)
    lo = (t - hi.astype(F32)).astype(BF16)
    swapped = _dot(jnp.concatenate([hi, lo], axis=1), swap)
    return t * cos + swapped * sin_signed


def _swap_matrix(half, group):
    p = np.zeros((LANES, LANES), np.float32)
    for j in range(LANES):
        g = j % group
        if g < half:
            p[j + half, j] = 1.0
        elif g < 2 * half:
            p[j - half, j] = 1.0
    return np.concatenate([p, p], axis=0)


def _rms(t, g, width):
    ms = jnp.sum(t * t, axis=-1, keepdims=True) * (1.0 / width)
    return t * lax.rsqrt(ms + EPS) * g


NORM_ROWS = 16


def _modnorm_into(h_ref, x_ref, g_ref, sc_ref, sh_ref):
    g, sc1, sh = g_ref[...], 1.0 + sc_ref[...], sh_ref[...]

    def chunk(c, carry):
        rows = pl.ds(pl.multiple_of(c * NORM_ROWS, NORM_ROWS), NORM_ROWS)
        x = x_ref[rows, :]
        ms = jnp.mean(x * x, axis=-1, keepdims=True)
        y = x * lax.rsqrt(ms + EPS) * g
        h_ref[rows, :] = (y * sc1 + sh).astype(BF16)
        return carry

    lax.fori_loop(0, x_ref.shape[0] // NORM_ROWS, chunk, 0, unroll=8)


def _in_proj_kernel(x_ref, g_ref, sc_ref, sh_ref, w_ref, o_ref, h_ref):
    @pl.when(pl.program_id(1) == 0)
    def _():
        _modnorm_into(h_ref, x_ref, g_ref, sc_ref, sh_ref)

    o_ref[...] = _dot_nt(h_ref[...], w_ref[...])


def _mod_spec(l, k, per_b, width=D_MODEL):
    if width == D_MODEL:
        return pl.BlockSpec((None, None, None, 1, D_MODEL), lambda i, *_: (l, i // per_b, k, 0, 0))
    return pl.BlockSpec((None, None, None, 1, width), lambda i, j: (l, i // per_b, k, 0, j))


def _layer_row_spec(l, width):
    return pl.BlockSpec((None, 1, width), lambda *_: (l, 0, 0))


def _in_proj(l, x, g, mod, w):
    tm, tn = 1024, 768
    per_b = SEQ // tm
    n = w.shape[1]
    return pl.pallas_call(
        _in_proj_kernel,
        grid=(TOKENS // tm, n // tn),
        in_specs=[
            pl.BlockSpec((tm, D_MODEL), lambda i, j: (i, 0)),
            _layer_row_spec(l, D_MODEL),
            _mod_spec(l, 1, per_b),
            _mod_spec(l, 0, per_b),
            pl.BlockSpec((None, tn, D_MODEL), lambda i, j: (l, j, 0)),
        ],
        out_specs=pl.BlockSpec((tm, tn), lambda i, j: (i, j)),
        out_shape=jax.ShapeDtypeStruct((TOKENS, n), F32),
        scratch_shapes=[pltpu.VMEM((tm, D_MODEL), BF16)],
        compiler_params=_params("arbitrary", "arbitrary"),
        name="in_proj",
    )(x, g, mod, mod, w)


def _prep_kernel(qa_ref, qi_ref, mql_ref, kv_ref, mkvl_ref, kiwi_ref,
                 ca_ref, sa_ref, ci_ref, si_ref, pa_ref, pi_ref,
                 gqa_ref, gka_ref, gmq_ref, gmkv_ref,
                 qa_o, ka_o, va_o, qi_o, ki2_o, wi_o, mqn_o, mkvn_o):
    tm = qa_ref.shape[0]
    lane = lax.broadcasted_iota(jnp.int32, (tm, LANES), 1)
    ca, sa, pa = ca_ref[0], sa_ref[0], pa_ref[0]
    ci, si, pi = ci_ref[0], si_ref[0], pi_ref[0]
    scale_a = A_HEAD_DIM ** -0.5

    for h in range(A_HEADS):
        t = qa_ref[:, h * LANES:(h + 1) * LANES]
        r = _rope(_rms(t, gqa_ref[...], A_HEAD_DIM), ca, sa, pa)
        qa_o[h] = (r * scale_a).astype(BF16)
    for g in range(A_KV_HEADS):
        t = kv_ref[:, g * LANES:(g + 1) * LANES]
        r = _rope(_rms(t, gka_ref[...], A_HEAD_DIM), ca, sa, pa)
        ka_o[g] = r.astype(BF16)
        va_o[g] = kv_ref[:, (A_KV_HEADS + g) * LANES:(A_KV_HEADS + g + 1) * LANES].astype(BF16)
    for p in range(IDX_HEADS // 2):
        t = qi_ref[:, p * LANES:(p + 1) * LANES]
        r = _rope(t, ci, si, pi)
        qi_o[2 * p] = jnp.where(lane < IDX_DIM, r, 0.0).astype(BF16)
        qi_o[2 * p + 1] = jnp.where(lane < IDX_DIM, 0.0, r).astype(BF16)

    kiwi = kiwi_ref[...]
    ki = _rope(kiwi, ci, si, pi)
    ki2_o[...] = jnp.where(lane < IDX_DIM, ki, pltpu.roll(ki, IDX_DIM, 1)).astype(BF16)
    wi_o[...] = kiwi * IDX_W_SCALE

    mqn_o[...] = _rms(mql_ref[...], gmq_ref[...], M_Q_LORA).astype(BF16)
    mkvn_o[...] = _rms(mkvl_ref[...], gmkv_ref[...], M_KV_LORA).astype(BF16)


def _prep(l, proj, tabs_cos, tabs_sin, swaps, g_qa, g_ka, g_mq_lat, g_mkv_lat):
    tm = 256
    per_b = SEQ // tm

    def swap(t):
        return pl.BlockSpec((1, 2 * LANES, LANES), lambda i: (t, 0, 0))

    def col(width, start):
        idx = start // width
        return pl.BlockSpec((tm, width), lambda i: (i, idx))

    def tab(t):
        return pl.BlockSpec((1, tm, LANES), lambda i: (t, i, 0))

    def vec(width):
        return _layer_row_spec(l, width)

    def heads(n):
        return pl.BlockSpec((None, n, tm, LANES), lambda i: (i // per_b, 0, i % per_b, 0))

    def seq(width):
        return pl.BlockSpec((None, tm, width), lambda i: (i // per_b, i % per_b, 0))

    def tok(width):
        return pl.BlockSpec((tm, width), lambda i: (i, 0))

    return pl.pallas_call(
        _prep_kernel,
        grid=(TOKENS // tm,),
        in_specs=[
            col(1024, COL_QA), col(1024, COL_QI), col(512, COL_MQL), col(512, COL_KV),
            col(256, COL_MKVL), col(128, COL_KIWI),
            tab(0), tab(0), tab(1), tab(1), swap(0), swap(1),
            vec(A_HEAD_DIM), vec(A_HEAD_DIM), vec(M_Q_LORA), vec(M_KV_LORA),
        ],
        out_specs=[
            heads(A_HEADS), heads(A_KV_HEADS), heads(A_KV_HEADS),
            heads(IDX_HEADS), seq(LANES), seq(LANES),
            tok(M_Q_LORA), tok(M_KV_LORA),
        ],
        out_shape=[
            jax.ShapeDtypeStruct((BATCH, A_HEADS, SEQ, LANES), BF16),
            jax.ShapeDtypeStruct((BATCH, A_KV_HEADS, SEQ, LANES), BF16),
            jax.ShapeDtypeStruct((BATCH, A_KV_HEADS, SEQ, LANES), BF16),
            jax.ShapeDtypeStruct((BATCH, IDX_HEADS, SEQ, LANES), BF16),
            jax.ShapeDtypeStruct((BATCH, SEQ, LANES), BF16),
            jax.ShapeDtypeStruct((BATCH, SEQ, LANES), F32),
            jax.ShapeDtypeStruct((TOKENS, M_Q_LORA), BF16),
            jax.ShapeDtypeStruct((TOKENS, M_KV_LORA), BF16),
        ],
        compiler_params=_params("arbitrary"),
        name="prep",
    )(proj, proj, proj, proj, proj, proj, tabs_cos, tabs_sin, tabs_cos, tabs_sin, swaps, swaps,
      g_qa, g_ka, g_mq_lat, g_mkv_lat)


def _mla_up_kernel(mqn_ref, mkvn_ref, mkr_ref, wq_ref, wkv_ref, cm_ref, sm_ref, pm_ref,
                   gqn_ref, gqr_ref, gkn_ref, gkr_ref, q_o, k_o, v_o):
    tm = mqn_ref.shape[0]
    lane = lax.broadcasted_iota(jnp.int32, (tm, LANES), 1)
    low = lane < M_ROPE
    cm, sm, pm = cm_ref[0], sm_ref[0], pm_ref[0]
    scale_m = M_QK ** -0.5
    nope_w = M_HEADS * M_NOPE

    q = _dot(mqn_ref[...], wq_ref[...])
    kv = _dot(mkvn_ref[...], wkv_ref[...])

    for p in range(M_HEADS // 2):
        rp = q[:, nope_w + p * LANES: nope_w + (p + 1) * LANES]
        sq = rp * rp
        s_all = jnp.sum(sq, axis=-1, keepdims=True)
        s_lo = jnp.sum(jnp.where(low, sq, 0.0), axis=-1, keepdims=True)
        for e in range(2):
            h = 2 * p + e
            nope = q[:, h * M_NOPE:(h + 1) * M_NOPE]
            ss = jnp.sum(nope * nope, axis=-1, keepdims=True) + (s_lo if e == 0 else s_all - s_lo)
            rs = lax.rsqrt(ss * (1.0 / M_QK) + EPS)
            roped = _rope(rp * rs * gqr_ref[...], cm, sm, pm)
            if e == 1:
                roped = pltpu.roll(roped, M_ROPE, 1)
            q_o[h, :, 0:LANES] = (nope * rs * gqn_ref[...] * scale_m).astype(BF16)
            q_o[h, :, LANES:2 * LANES] = (jnp.where(low, roped, 0.0) * scale_m).astype(BF16)

    kr = mkr_ref[...]
    kr_ss = jnp.sum(kr * kr, axis=-1, keepdims=True)
    kr_roped = _rope(kr * gkr_ref[...], cm, sm, pm)
    kr_roped = jnp.where(low, kr_roped, 0.0)
    for h in range(M_HEADS):
        nope = kv[:, h * 2 * LANES: h * 2 * LANES + M_NOPE]
        ss = jnp.sum(nope * nope, axis=-1, keepdims=True) + kr_ss
        rs = lax.rsqrt(ss * (1.0 / M_QK) + EPS)
        k_o[h, :, 0:LANES] = (nope * rs * gkn_ref[...]).astype(BF16)
        k_o[h, :, LANES:2 * LANES] = (kr_roped * rs).astype(BF16)
        v_o[h] = kv[:, h * 2 * LANES + M_NOPE:(h + 1) * 2 * LANES].astype(BF16)


def _mla_up(l, mqn, mkvn, proj, wq, wkv, tabs_cos, tabs_sin, swaps, gqn, gqr, gkn, gkr):
    tm = 256
    per_b = SEQ // tm

    def vec():
        return _layer_row_spec(l, LANES)

    def heads(width):
        return pl.BlockSpec((None, M_HEADS, tm, width), lambda i: (i // per_b, 0, i % per_b, 0))

    return pl.pallas_call(
        _mla_up_kernel,
        grid=(TOKENS // tm,),
        in_specs=[
            pl.BlockSpec((tm, M_Q_LORA), lambda i: (i, 0)),
            pl.BlockSpec((tm, M_KV_LORA), lambda i: (i, 0)),
            pl.BlockSpec((tm, LANES), lambda i: (i, COL_MKR // LANES)),
            pl.BlockSpec((None,) + wq.shape[1:], lambda i: (l, 0, 0)),
            pl.BlockSpec((None,) + wkv.shape[1:], lambda i: (l, 0, 0)),
            pl.BlockSpec((1, tm, LANES), lambda i: (2, i, 0)),
            pl.BlockSpec((1, tm, LANES), lambda i: (2, i, 0)),
            pl.BlockSpec((1, 2 * LANES, LANES), lambda i: (2, 0, 0)),
            vec(), vec(), vec(), vec(),
        ],
        out_specs=[heads(M_QK_PAD), heads(M_QK_PAD), heads(M_V)],
        out_shape=[
            jax.ShapeDtypeStruct((BATCH, M_HEADS, SEQ, M_QK_PAD), BF16),
            jax.ShapeDtypeStruct((BATCH, M_HEADS, SEQ, M_QK_PAD), BF16),
            jax.ShapeDtypeStruct((BATCH, M_HEADS, SEQ, M_V), BF16),
        ],
        compiler_params=_params("arbitrary"),
        name="mla_up",
    )(mqn, mkvn, proj, wq, wkv, tabs_cos, tabs_sin, swaps, gqn, gqr, gkn, gkr)


INT_MIN = -(2 ** 31)
KEY_MASKED = INT_MIN

DSA_TQ = 256
KEY_CHUNK = 512
SEARCH_ROWS = 128
SEARCH_UNROLL = 5
IDX_CHUNK = 256
IDX_ROWS = 64
ATT_SPLIT = 4


def _dsa_index(i, n_chunks, qi_ref, ki2_ref, wi_ref, key_ref, d_refs, wb_ref):
    tq = qi_ref.shape[1]
    w = wi_ref[...]
    for h in range(IDX_HEADS):
        wb_ref[h] = jnp.broadcast_to(w[:, IDX_DIM + h:IDX_DIM + h + 1], (tq, LANES))

    def matmul(c, d_ref):
        start = pl.multiple_of(c * IDX_CHUNK, IDX_CHUNK)
        q_all = qi_ref[...].reshape(IDX_HEADS * tq, LANES)
        d_ref[...] = _dot_nt(q_all, ki2_ref[pl.ds(start, IDX_CHUNK), :])

    def head_sum(c, d_ref):
        for r0 in range(0, tq, IDX_ROWS):
            row_t = i * tq + r0 + lax.broadcasted_iota(jnp.int32, (IDX_ROWS, LANES), 0)
            for j0 in range(0, IDX_CHUNK, LANES):
                acc = jnp.zeros((IDX_ROWS, LANES), F32)
                for h in range(IDX_HEADS):
                    d = d_ref[h * tq + r0:h * tq + r0 + IDX_ROWS, j0:j0 + LANES]
                    acc = acc + jnp.maximum(d, 0.0) * wb_ref[h, r0:r0 + IDX_ROWS, :]
                bits = pltpu.bitcast(acc, jnp.int32)
                img = bits ^ ((bits >> 31) & jnp.int32(0x7FFFFFFF))
                col_t = c * IDX_CHUNK + j0 + lax.broadcasted_iota(jnp.int32, (IDX_ROWS, LANES), 1)
                key_ref[c, r0:r0 + IDX_ROWS, j0:j0 + LANES] = jnp.where(col_t <= row_t, img, KEY_MASKED)

    d0_ref, d1_ref = d_refs
    matmul(0, d0_ref)

    def pair(k, carry):
        c = 2 * k
        matmul(c + 1, d1_ref)
        head_sum(c, d0_ref)
        matmul(jnp.minimum(c + 2, n_chunks - 1), d0_ref)
        head_sum(c + 1, d1_ref)
        return carry

    lax.fori_loop(0, n_chunks // 2, pair, 0)


def _dsa_select(nk, key_ref, bias_ref):
    chunks = range(nk // IDX_CHUNK)
    n_sel = float(N_SEL)
    groups = list(range(0, key_ref.shape[1], SEARCH_ROWS))

    def count_ge(r0, cand):
        hits = None
        for c in chunks:
            hit = jnp.where(key_ref[c, r0:r0 + SEARCH_ROWS, :] >= cand, 1.0, 0.0)
            hits = hit if hits is None else hits + hit
        return jnp.sum(hits, axis=-1, keepdims=True)

    zero = jnp.zeros((SEARCH_ROWS, 1), jnp.int32)
    thr0 = tuple(jnp.where(count_ge(r0, zero) >= n_sel, jnp.int32(0), jnp.int32(INT_MIN))
                 for r0 in groups)

    def step(b, thrs):
        bit = lax.shift_left(jnp.int32(1), jnp.int32(30) - b)
        out = []
        for r0, thr in zip(groups, thrs):
            cand = thr | bit
            out.append(jnp.where(count_ge(r0, cand) >= n_sel, cand, thr))
        return tuple(out)

    thrs = lax.fori_loop(0, 30, step, thr0, unroll=SEARCH_UNROLL)
    thrs = step(jnp.int32(30), thrs)
    for r0, thr in zip(groups, thrs):
        floor = jnp.maximum(thr, jnp.int32(KEY_MASKED + 1))
        for c in chunks:
            sel = key_ref[c, r0:r0 + SEARCH_ROWS, :] >= floor
            bias_ref[r0:r0 + SEARCH_ROWS, c * IDX_CHUNK:(c + 1) * IDX_CHUNK] = jnp.where(sel, 0.0, -jnp.inf)


def _dsa_attend(nk, qa_ref, ka_ref, va_ref, bias_ref, o_ref, oh_ref, lg_refs, p_refs):
    tq = qa_ref.shape[1]
    rep = A_HEADS // A_KV_HEADS

    sub = rep // ATT_SPLIT

    def group(g, carry):
        k = ka_ref[g, 0:nk, :]
        v = va_ref[g, 0:nk, :]
        inv_l = [None] * ATT_SPLIT

        def scores(s):
            q = qa_ref[pl.ds(g * rep + s * sub, sub)].reshape(sub * tq, LANES)
            lg_refs[s][:, 0:nk] = _dot_nt(q, k)

        def softmax(s):
            logits = lg_refs[s][:, 0:nk].reshape(sub, tq, nk) + bias_ref[:, 0:nk][None]
            m = jnp.max(logits, axis=-1, keepdims=True)
            p = jnp.exp(logits - m)
            inv_l[s] = 1.0 / jnp.sum(p, axis=-1, keepdims=True)
            p_refs[s][:, 0:nk] = p.astype(BF16).reshape(sub * tq, nk)

        def values(s):
            o = _dot(p_refs[s][:, 0:nk], v).reshape(sub, tq, LANES) * inv_l[s]
            oh_ref[pl.ds(g * rep + s * sub, sub)] = o.astype(BF16)

        for t in range(ATT_SPLIT + 2):
            if t < ATT_SPLIT:
                scores(t)
            if 0 <= t - 1 < ATT_SPLIT:
                softmax(t - 1)
            if 0 <= t - 2 < ATT_SPLIT:
                values(t - 2)
        return carry

    lax.fori_loop(0, A_KV_HEADS, group, 0)
    for h in range(A_HEADS):
        o_ref[:, h * LANES:(h + 1) * LANES] = oh_ref[h]


def _dsa_kernel(qa_ref, ka_ref, va_ref, qi_ref, ki2_ref, wi_ref, o_ref,
                key_ref, bias_ref, d0_ref, d1_ref, wb_ref, oh_ref, *stage_refs):
    lg_refs, p_refs = stage_refs[:ATT_SPLIT], stage_refs[ATT_SPLIT:]
    i = pl.program_id(1)
    tq = DSA_TQ
    n_free = N_SEL // tq
    variant = i // (KEY_CHUNK // tq)

    for v in range(n_free):
        @pl.when(i == v)
        def _(v=v):
            nk = (v + 1) * tq
            row = i * tq + lax.broadcasted_iota(jnp.int32, (tq, nk), 0)
            col = lax.broadcasted_iota(jnp.int32, (tq, nk), 1)
            bias_ref[:, 0:nk] = jnp.where(col <= row, 0.0, -jnp.inf)
            _dsa_attend(nk, qa_ref, ka_ref, va_ref, bias_ref, o_ref, oh_ref, lg_refs, p_refs)

    @pl.when(i >= n_free)
    def _():
        n_chunks = (variant + 1) * (KEY_CHUNK // IDX_CHUNK)
        _dsa_index(i, n_chunks, qi_ref, ki2_ref, wi_ref, key_ref, (d0_ref, d1_ref), wb_ref)

    for v in range(SEQ // KEY_CHUNK):
        @pl.when((i >= n_free) & (variant == v))
        def _(v=v):
            nk = (v + 1) * KEY_CHUNK
            _dsa_select(nk, key_ref, bias_ref)
            _dsa_attend(nk, qa_ref, ka_ref, va_ref, bias_ref, o_ref, oh_ref, lg_refs, p_refs)


def _dsa(qa, ka, va, qi, ki2, wi):
    tq = DSA_TQ
    nq = SEQ // tq
    att_rows = A_HEADS // A_KV_HEADS // ATT_SPLIT * tq
    return pl.pallas_call(
        _dsa_kernel,
        grid=(BATCH, nq),
        in_specs=[
            pl.BlockSpec((None, A_HEADS, tq, LANES), lambda b, i: (b, 0, i, 0)),
            pl.BlockSpec((None, A_KV_HEADS, SEQ, LANES), lambda b, i: (b, 0, 0, 0)),
            pl.BlockSpec((None, A_KV_HEADS, SEQ, LANES), lambda b, i: (b, 0, 0, 0)),
            pl.BlockSpec((None, IDX_HEADS, tq, LANES), lambda b, i: (b, 0, i, 0)),
            pl.BlockSpec((None, SEQ, LANES), lambda b, i: (b, 0, 0)),
            pl.BlockSpec((None, tq, LANES), lambda b, i: (b, i, 0)),
        ],
        out_specs=pl.BlockSpec((tq, A_HEADS * A_HEAD_DIM), lambda b, i: (b * nq + i, 0)),
        out_shape=jax.ShapeDtypeStruct((TOKENS, A_HEADS * A_HEAD_DIM), BF16),
        scratch_shapes=[
            pltpu.VMEM((SEQ // IDX_CHUNK, tq, IDX_CHUNK), jnp.int32),
            pltpu.VMEM((tq, SEQ), F32),
            pltpu.VMEM((IDX_HEADS * tq, IDX_CHUNK), F32),
            pltpu.VMEM((IDX_HEADS * tq, IDX_CHUNK), F32),
            pltpu.VMEM((IDX_HEADS, tq, LANES), F32),
            pltpu.VMEM((A_HEADS, tq, LANES), BF16),
        ] + [pltpu.VMEM((att_rows, SEQ), F32)] * ATT_SPLIT + [pltpu.VMEM((att_rows, SEQ), BF16)] * ATT_SPLIT,
        compiler_params=_params("arbitrary", "arbitrary"),
        name="dsa",
    )(qa, ka, va, qi, ki2, wi)


MLA_TQ = 256
MLA_HEADS_PER_STEP = 4


def _mla_body(nk, i, q_ref, k_ref, v_ref, o_ref, lg_refs, p_refs):
    n_heads, tq = q_ref.shape[0], q_ref.shape[1]
    n0 = nk - KEY_CHUNK
    row = i * tq + lax.broadcasted_iota(jnp.int32, (tq, KEY_CHUNK), 0)
    col = n0 + lax.broadcasted_iota(jnp.int32, (tq, KEY_CHUNK), 1)
    causal = col <= row
    inv_l = [None] * n_heads

    def scores(h):
        lg_refs[h][:, 0:nk] = _dot_nt(q_ref[h], k_ref[h, 0:nk, :])

    def softmax(h):
        s1 = jnp.where(causal, lg_refs[h][:, n0:nk], -jnp.inf)
        m = jnp.max(s1, axis=-1, keepdims=True)
        if n0:
            s0 = lg_refs[h][:, 0:n0]
            m = jnp.maximum(m, jnp.max(s0, axis=-1, keepdims=True))
        p1 = jnp.exp(s1 - m)
        l = jnp.sum(p1, axis=-1, keepdims=True)
        p_refs[h][:, n0:nk] = p1.astype(BF16)
        if n0:
            p0 = jnp.exp(s0 - m)
            l = l + jnp.sum(p0, axis=-1, keepdims=True)
            p_refs[h][:, 0:n0] = p0.astype(BF16)
        inv_l[h] = 1.0 / l

    def values(h):
        o = _dot(p_refs[h][:, 0:nk], v_ref[h, 0:nk, :]) * inv_l[h]
        o_ref[:, h * M_V:(h + 1) * M_V] = o.astype(BF16)

    for t in range(n_heads + 2):
        if t < n_heads:
            scores(t)
        if 0 <= t - 1 < n_heads:
            softmax(t - 1)
        if 0 <= t - 2 < n_heads:
            values(t - 2)


def _mla_kernel(q_ref, k_ref, v_ref, o_ref, *stage_refs):
    n_heads = q_ref.shape[0]
    lg_refs, p_refs = stage_refs[:n_heads], stage_refs[n_heads:]
    i = pl.program_id(2)
    per_chunk = KEY_CHUNK // MLA_TQ
    for v in range(SEQ // KEY_CHUNK):
        @pl.when(i // per_chunk == v)
        def _(v=v):
            _mla_body((v + 1) * KEY_CHUNK, i, q_ref, k_ref, v_ref, o_ref, lg_refs, p_refs)


def _mla(q, k, v):
    tq, hg = MLA_TQ, MLA_HEADS_PER_STEP
    nq = SEQ // tq
    return pl.pallas_call(
        _mla_kernel,
        grid=(BATCH, M_HEADS // hg, nq),
        in_specs=[
            pl.BlockSpec((None, hg, tq, M_QK_PAD), lambda b, g, i: (b, g, i, 0)),
            pl.BlockSpec((None, hg, SEQ, M_QK_PAD), lambda b, g, i: (b, g, 0, 0)),
            pl.BlockSpec((None, hg, SEQ, M_V), lambda b, g, i: (b, g, 0, 0)),
        ],
        out_specs=pl.BlockSpec((tq, hg * M_V), lambda b, g, i: (b * nq + i, g)),
        out_shape=jax.ShapeDtypeStruct((TOKENS, M_HEADS * M_V), BF16),
        scratch_shapes=[pltpu.VMEM((tq, SEQ), F32)] * hg + [pltpu.VMEM((tq, SEQ), BF16)] * hg,
        compiler_params=_params("arbitrary", "arbitrary", "arbitrary"),
        name="mla",
    )(q, k, v)


def _merge_kernel(oa_ref, ob_ref, ga_ref, gb_ref, wpa_ref, wpb_ref, wo_ref, x_ref, gt_ref, o_ref):
    a = _dot(oa_ref[...], wpa_ref[...])
    b = _dot(ob_ref[...], wpb_ref[...])
    merged = jax.nn.sigmoid(ga_ref[...]) * a + jax.nn.sigmoid(gb_ref[...]) * b
    y = _dot(merged.astype(BF16), wo_ref[...])
    o_ref[...] = x_ref[...] + gt_ref[...] * y


def _merge(l, oa, ob, proj, wpa, wpb, wo, x, mod):
    tm = 256
    per_b = SEQ // tm

    def resident(shape):
        return pl.BlockSpec((None,) + shape[1:], lambda i: (l, 0, 0), pipeline_mode=pl.Buffered(1))

    return pl.pallas_call(
        _merge_kernel,
        grid=(TOKENS // tm,),
        in_specs=[
            pl.BlockSpec((tm, A_HEADS * A_HEAD_DIM), lambda i: (i, 0)),
            pl.BlockSpec((tm, M_HEADS * M_V), lambda i: (i, 0)),
            pl.BlockSpec((tm, D_MODEL), lambda i: (i, COL_GATE_A // D_MODEL)),
            pl.BlockSpec((tm, D_MODEL), lambda i: (i, COL_GATE_B // D_MODEL)),
            resident(wpa.shape), resident(wpb.shape), resident(wo.shape),
            pl.BlockSpec((tm, D_MODEL), lambda i: (i, 0)),
            _mod_spec(l, 2, per_b),
        ],
        out_specs=pl.BlockSpec((tm, D_MODEL), lambda i: (i, 0)),
        out_shape=jax.ShapeDtypeStruct((TOKENS, D_MODEL), F32),
        compiler_params=_params("arbitrary"),
        name="merge",
    )(oa, ob, proj, proj, wpa, wpb, wo, x, mod)


HALO = 8
FFN_SPLIT = 1


def _ffn_up_kernel(x_ref, g_ref, sc_ref, sh_ref, wg_ref, wv_ref, cg_ref, cv_ref, bg_ref, bv_ref,
                   o_ref, h_ref, halo_ref, *buf_refs, tiles_per_seq):
    i, j = pl.program_id(0), pl.program_id(1)
    tm, tn = o_ref.shape
    w = tn // FFN_SPLIT

    @pl.when(j == 0)
    def _():
        _modnorm_into(h_ref, x_ref, g_ref, sc_ref, sh_ref)

    seq_start = (i % tiles_per_seq) == 0

    @pl.when(seq_start)
    def _():
        for buf_ref in buf_refs:
            buf_ref[0:HALO, :] = jnp.zeros((HALO, 2 * w), F32)

    @pl.when(jnp.logical_not(seq_start))
    def _():
        for s, buf_ref in enumerate(buf_refs):
            buf_ref[0:HALO, :] = halo_ref[j, :, s * 2 * w:(s + 1) * 2 * w]

    h = h_ref[...]
    for s, buf_ref in enumerate(buf_refs):
        c = slice(s * w, (s + 1) * w)
        buf_ref[HALO:HALO + tm, 0:w] = _dot(h, wg_ref[:, c])
        buf_ref[HALO:HALO + tm, w:2 * w] = _dot(h, wv_ref[:, c])
        halo_ref[j, :, s * 2 * w:(s + 1) * 2 * w] = buf_ref[tm:tm + HALO, :]

    def conv(buf_ref, lo, c, c_ref, b_ref):
        u0 = buf_ref[HALO:HALO + tm, lo:lo + w]
        u1 = buf_ref[HALO - 1:HALO - 1 + tm, lo:lo + w]
        u2 = buf_ref[HALO - 2:HALO - 2 + tm, lo:lo + w]
        return b_ref[:, c] + c_ref[0:1, c] * u2 + c_ref[1:2, c] * u1 + c_ref[2:3, c] * u0

    for s, buf_ref in enumerate(buf_refs):
        c = slice(s * w, (s + 1) * w)
        gate = conv(buf_ref, 0, c, cg_ref, bg_ref)
        val = conv(buf_ref, w, c, cv_ref, bv_ref)
        o_ref[:, c] = (gate * jax.nn.sigmoid(gate) * val).astype(BF16)


def _ffn_up(l, x, g, mod, w_up, w_conv, b_conv):
    tm, tn = 1024, 512
    per_b = SEQ // tm
    nj = D_FF // tn
    kern = functools.partial(_ffn_up_kernel, tiles_per_seq=per_b)
    return pl.pallas_call(
        kern,
        grid=(TOKENS // tm, nj),
        in_specs=[
            pl.BlockSpec((tm, D_MODEL), lambda i, j: (i, 0)),
            _layer_row_spec(l, D_MODEL),
            _mod_spec(l, 4, per_b),
            _mod_spec(l, 3, per_b),
            pl.BlockSpec((None, D_MODEL, tn), lambda i, j: (l, 0, j)),
            pl.BlockSpec((None, D_MODEL, tn), lambda i, j: (l, 0, nj + j)),
            pl.BlockSpec((None, CONV_W, tn), lambda i, j: (l, 0, j)),
            pl.BlockSpec((None, CONV_W, tn), lambda i, j: (l, 0, nj + j)),
            pl.BlockSpec((None, 1, tn), lambda i, j: (l, 0, j)),
            pl.BlockSpec((None, 1, tn), lambda i, j: (l, 0, nj + j)),
        ],
        out_specs=pl.BlockSpec((tm, tn), lambda i, j: (i, j)),
        out_shape=jax.ShapeDtypeStruct((TOKENS, D_FF), BF16),
        scratch_shapes=[
            pltpu.VMEM((tm, D_MODEL), BF16),
            pltpu.VMEM((nj, HALO, 2 * tn), F32),
        ] + [pltpu.VMEM((HALO + tm, 2 * tn // FFN_SPLIT), F32)] * FFN_SPLIT,
        compiler_params=_params("arbitrary", "arbitrary"),
        name="ffn_up",
    )(x, g, mod, mod, w_up, w_up, w_conv, w_conv, b_conv, b_conv)


def _ffn_down_kernel(a_ref, w_ref, x_ref, gt_ref, o_ref):
    o_ref[...] = x_ref[...] + gt_ref[...] * _dot(a_ref[...], w_ref[...].astype(BF16))


def _ffn_down(l, act, w_down, x, mod):
    tm, tn = 1024, 256
    per_b = SEQ // tm
    return pl.pallas_call(
        _ffn_down_kernel,
        grid=(TOKENS // tm, D_MODEL // tn),
        in_specs=[
            pl.BlockSpec((tm, D_FF), lambda i, j: (i, 0)),
            pl.BlockSpec((None, D_FF, tn), lambda i, j: (l, 0, j)),
            pl.BlockSpec((tm, tn), lambda i, j: (i, j)),
            _mod_spec(l, 5, per_b, tn),
        ],
        out_specs=pl.BlockSpec((tm, tn), lambda i, j: (i, j)),
        out_shape=jax.ShapeDtypeStruct((TOKENS, D_MODEL), F32),
        compiler_params=_params("arbitrary", "arbitrary"),
        name="ffn_down",
    )(act, w_down, x, mod)


def _pack_w_in(w_in):
    o = np.cumsum((0,) + (1024, 256, 256, 1024, 64, 16, 512, 256, 64, 2048, 2048))
    w_t = jnp.swapaxes(w_in, 1, 2)
    qa, ka, va, qi, ki, wi, mql, mkvl, mkr, ga, gb = (
        w_t[:, int(o[k]):int(o[k + 1])].astype(BF16) for k in range(11))
    z = lambda n: jnp.zeros((DEPTH, n, D_MODEL), BF16)
    packed = jnp.concatenate(
        [ga, gb, qa, qi, mql, ka, va, mkvl, ki, wi, z(48), mkr, z(64)], axis=1)
    assert packed.shape[1] == IN_PACKED
    return packed


def _pack_w_mq(w_mq_up):
    w = w_mq_up.reshape(DEPTH, M_Q_LORA, M_HEADS, M_QK)
    nope = w[..., :M_NOPE].reshape(DEPTH, M_Q_LORA, M_HEADS * M_NOPE)
    rope = w[..., M_NOPE:].reshape(DEPTH, M_Q_LORA, M_HEADS * M_ROPE)
    return jnp.concatenate([nope, rope], axis=-1).astype(BF16)


def _rope_consts():
    def inv(rot):
        return ROPE_THETA ** (-jnp.arange(0, rot, 2, dtype=F32) / rot)

    def lanes(v, group):
        rep = jnp.concatenate([v, v])
        pad = jnp.zeros((group - rep.shape[0],), F32)
        return jnp.tile(jnp.concatenate([rep, pad]), LANES // group)

    def sgn(half, group):
        v = jnp.concatenate([-jnp.ones((half,), F32), jnp.ones((half,), F32)])
        pad = jnp.zeros((group - 2 * half,), F32)
        return jnp.tile(jnp.concatenate([v, pad]), LANES // group)

    rot_a = A_HEAD_DIM // ROT_FRACTION
    rot_i = IDX_DIM // ROT_FRACTION
    inv3 = jnp.stack([lanes(inv(rot_a), LANES), lanes(inv(rot_i), IDX_DIM), lanes(inv(M_ROPE), M_ROPE)])
    sgn3 = jnp.stack([sgn(rot_a // 2, LANES), sgn(rot_i // 2, IDX_DIM), sgn(M_ROPE // 2, M_ROPE)])
    return inv3.reshape(3, 1, LANES), sgn3.reshape(3, 1, LANES)


def kernel(x, c, positions, g_attn, g_ffn, w_ada, b_ada, w_in, g_qa, g_ka, g_mq_lat, w_mq_up,
           g_mkv_lat, w_mkv_up, g_qm, g_km, w_pa, w_pb, w_o, w_up, w_conv, b_conv, w_down):
    pos_col = positions.astype(F32).reshape(TOKENS, 1)
    inv3, sgn3 = _rope_consts()
    tabs_cos, tabs_sin = _rope_tables(pos_col, inv3, sgn3)
    rot_a, rot_i = A_HEAD_DIM // ROT_FRACTION, IDX_DIM // ROT_FRACTION
    swaps = jnp.asarray(np.stack([_swap_matrix(rot_a // 2, LANES), _swap_matrix(rot_i // 2, IDX_DIM),
                                  _swap_matrix(M_ROPE // 2, M_ROPE)]), dtype=BF16)

    c8 = jnp.pad(c, ((0, 8 - BATCH), (0, 0)))
    mod = _ada(c8, w_ada, b_ada)[:, :BATCH].reshape(DEPTH, BATCH, N_ADA, 1, D_MODEL)

    w_in_p = _pack_w_in(w_in)
    w_mq_p = _pack_w_mq(w_mq_up)
    w_mkv_b = w_mkv_up.astype(BF16)
    w_pa_b, w_pb_b, w_o_b = w_pa.astype(BF16), w_pb.astype(BF16), w_o.astype(BF16)
    w_up_b = w_up.astype(BF16)

    rows = lambda v: v.reshape(DEPTH, 1, -1)
    rope_gain = lambda v: rows(jnp.tile(v[:, M_NOPE:], (1, 2)))
    g_attn_r, g_ffn_r, b_conv_r = rows(g_attn), rows(g_ffn), rows(b_conv)
    g_qa_r, g_ka_r, g_mq_r, g_mkv_r = rows(g_qa), rows(g_ka), rows(g_mq_lat), rows(g_mkv_lat)
    gqn, gqr = rows(g_qm[:, :M_NOPE]), rope_gain(g_qm)
    gkn, gkr = rows(g_km[:, :M_NOPE]), rope_gain(g_km)

    xf = x.reshape(TOKENS, D_MODEL)
    for l in range(DEPTH):
        proj = _in_proj(l, xf, g_attn_r, mod, w_in_p)
        qa, ka, va, qi, ki2, wi, mqn, mkvn = _prep(
            l, proj, tabs_cos, tabs_sin, swaps, g_qa_r, g_ka_r, g_mq_r, g_mkv_r)
        mq, mk, mv = _mla_up(l, mqn, mkvn, proj, w_mq_p, w_mkv_b, tabs_cos, tabs_sin, swaps,
                             gqn, gqr, gkn, gkr)
        o_a = _dsa(qa, ka, va, qi, ki2, wi)
        o_b = _mla(mq, mk, mv)
        xf = _merge(l, o_a, o_b, proj, w_pa_b, w_pb_b, w_o_b, xf, mod)
        act = _ffn_up(l, xf, g_ffn_r, mod, w_up_b, w_conv, b_conv_r)
        xf = _ffn_down(l, act, w_down, xf, mod)
    return xf.reshape(BATCH, SEQ, D_MODEL)
```

```python
import functools

import jax
import jax.numpy as jnp
import numpy as np
from jax import lax
from jax.experimental import pallas as pl
from jax.experimental.pallas import tpu as pltpu

D_MODEL = 2048
BATCH = 4
SEQ = 2048
DEPTH = 4
A_HEADS = 8
A_KV_HEADS = 2
A_HEAD_DIM = 128
IDX_HEADS = 16
IDX_DIM = 64
TOPK_MAX = 256
M_HEADS = 8
M_Q_LORA = 512
M_KV_LORA = 256
M_NOPE = 128
M_ROPE = 64
M_V = 128
D_FF = 5632
CONV_W = 3
ROPE_THETA = 500000.0
ROT_FRACTION = 4
EPS = 1e-6
N_ADA = 6
IDX_W_SCALE = (IDX_HEADS * IDX_DIM) ** -0.5
N_SEL = min(TOPK_MAX, SEQ // 4)
TOKENS = BATCH * SEQ
LANES = 128
M_QK = M_NOPE + M_ROPE
M_QK_PAD = 256

BF16 = jnp.bfloat16
F32 = jnp.float32

COL_GATE_A = 0
COL_GATE_B = 2048
COL_QA = 4096
COL_QI = 5120
COL_MQL = 6144
COL_KV = 6656
COL_MKVL = 7168
COL_KIWI = 7424
COL_MKR = 7552
IN_PACKED = 7680

VMEM_LIMIT = 56 * 1024 * 1024


def _params(*sem):
    return pltpu.CompilerParams(dimension_semantics=sem, vmem_limit_bytes=VMEM_LIMIT)


def _dot(a, b):
    return jnp.dot(a, b, preferred_element_type=F32)


def _dot_nt(a, b):
    return lax.dot_general(a, b, (((1,), (1,)), ((), ())), preferred_element_type=F32)


def _ada_kernel(c_ref, w_ref, b_ref, o_ref):
    c = c_ref[...]
    c_act = (c * jax.nn.sigmoid(c)).astype(BF16)
    o_ref[0] = _dot(c_act, w_ref[0].astype(BF16)) + b_ref[0]


def _ada(c8, w_ada, b_ada):
    tn = 1024
    n = N_ADA * D_MODEL
    return pl.pallas_call(
        _ada_kernel,
        grid=(DEPTH, n // tn),
        in_specs=[
            pl.BlockSpec((8, D_MODEL), lambda l, j: (0, 0)),
            pl.BlockSpec((1, D_MODEL, tn), lambda l, j: (l, 0, j)),
            pl.BlockSpec((1, 1, tn), lambda l, j: (l, 0, j)),
        ],
        out_specs=pl.BlockSpec((1, 8, tn), lambda l, j: (l, 0, j)),
        out_shape=jax.ShapeDtypeStruct((DEPTH, 8, n), F32),
        compiler_params=_params("arbitrary", "arbitrary"),
        name="ada",
    )(c8, w_ada, b_ada.reshape(DEPTH, 1, n))


def _rope_table_kernel(pos_ref, inv_ref, sgn_ref, cos_ref, sin_ref):
    ang = pos_ref[...] * inv_ref[0]
    cos_ref[0] = jnp.cos(ang)
    sin_ref[0] = jnp.sin(ang) * sgn_ref[0]


def _rope_tables(pos_col, inv3, sgn3):
    tm = 512
    return pl.pallas_call(
        _rope_table_kernel,
        grid=(3, TOKENS // tm),
        in_specs=[
            pl.BlockSpec((tm, 1), lambda t, i: (i, 0)),
            pl.BlockSpec((1, 1, LANES), lambda t, i: (t, 0, 0)),
            pl.BlockSpec((1, 1, LANES), lambda t, i: (t, 0, 0)),
        ],
        out_specs=[
            pl.BlockSpec((1, tm, LANES), lambda t, i: (t, i, 0)),
            pl.BlockSpec((1, tm, LANES), lambda t, i: (t, i, 0)),
        ],
        out_shape=[jax.ShapeDtypeStruct((3, TOKENS, LANES), F32)] * 2,
        compiler_params=_params("arbitrary", "arbitrary"),
        name="rope_tables",
    )(pos_col, inv3, sgn3)


def _rope(t, cos, sin_signed, swap):
    hi = t.astype(BF16)
    lo = (t - hi.astype(F32)).astype(BF16)
    swapped = _dot(jnp.concatenate([hi, lo], axis=1), swap)
    return t * cos + swapped * sin_signed


def _swap_matrix(half, group):
    p = np.zeros((LANES, LANES), np.float32)
    for j in range(LANES):
        g = j % group
        if g < half:
            p[j + half, j] = 1.0
        elif g < 2 * half:
            p[j - half, j] = 1.0
    return np.concatenate([p, p], axis=0)


def _rms(t, g, width):
    ms = jnp.sum(t * t, axis=-1, keepdims=True) * (1.0 / width)
    return t * lax.rsqrt(ms + EPS) * g


NORM_ROWS = 16


def _modnorm_into(h_ref, x_ref, g_ref, sc_ref, sh_ref):
    g, sc1, sh = g_ref[...], 1.0 + sc_ref[...], sh_ref[...]

    def chunk(c, carry):
        rows = pl.ds(pl.multiple_of(c * NORM_ROWS, NORM_ROWS), NORM_ROWS)
        x = x_ref[rows, :]
        ms = jnp.mean(x * x, axis=-1, keepdims=True)
        y = x * lax.rsqrt(ms + EPS) * g
        h_ref[rows, :] = (y * sc1 + sh).astype(BF16)
        return carry

    lax.fori_loop(0, x_ref.shape[0] // NORM_ROWS, chunk, 0, unroll=8)


def _in_proj_kernel(x_ref, g_ref, sc_ref, sh_ref, w_ref, o_ref, h_ref):
    @pl.when(pl.program_id(1) == 0)
    def _():
        _modnorm_into(h_ref, x_ref, g_ref, sc_ref, sh_ref)

    o_ref[...] = _dot_nt(h_ref[...], w_ref[...])


def _mod_spec(l, k, per_b, width=D_MODEL):
    if width == D_MODEL:
        return pl.BlockSpec((None, None, None, 1, D_MODEL), lambda i, *_: (l, i // per_b, k, 0, 0))
    return pl.BlockSpec((None, None, None, 1, width), lambda i, j: (l, i // per_b, k, 0, j))


def _layer_row_spec(l, width):
    return pl.BlockSpec((None, 1, width), lambda *_: (l, 0, 0))


def _in_proj(l, x, g, mod, w):
    tm, tn = 1024, 768
    per_b = SEQ // tm
    n = w.shape[1]
    return pl.pallas_call(
        _in_proj_kernel,
        grid=(TOKENS // tm, n // tn),
        in_specs=[
            pl.BlockSpec((tm, D_MODEL), lambda i, j: (i, 0)),
            _layer_row_spec(l, D_MODEL),
            _mod_spec(l, 1, per_b),
            _mod_spec(l, 0, per_b),
            pl.BlockSpec((None, tn, D_MODEL), lambda i, j: (l, j, 0)),
        ],
        out_specs=pl.BlockSpec((tm, tn), lambda i, j: (i, j)),
        out_shape=jax.ShapeDtypeStruct((TOKENS, n), F32),
        scratch_shapes=[pltpu.VMEM((tm, D_MODEL), BF16)],
        compiler_params=_params("arbitrary", "arbitrary"),
        name="in_proj",
    )(x, g, mod, mod, w)


def _prep_kernel(qa_ref, qi_ref, mql_ref, kv_ref, mkvl_ref, kiwi_ref,
                 ca_ref, sa_ref, ci_ref, si_ref, pa_ref, pi_ref,
                 gqa_ref, gka_ref, gmq_ref, gmkv_ref,
                 qa_o, ka_o, va_o, qi_o, ki2_o, wi_o, mqn_o, mkvn_o):
    tm = qa_ref.shape[0]
    lane = lax.broadcasted_iota(jnp.int32, (tm, LANES), 1)
    ca, sa, pa = ca_ref[0], sa_ref[0], pa_ref[0]
    ci, si, pi = ci_ref[0], si_ref[0], pi_ref[0]
    scale_a = A_HEAD_DIM ** -0.5

    for h in range(A_HEADS):
        t = qa_ref[:, h * LANES:(h + 1) * LANES]
        r = _rope(_rms(t, gqa_ref[...], A_HEAD_DIM), ca, sa, pa)
        qa_o[h] = (r * scale_a).astype(BF16)
    for g in range(A_KV_HEADS):
        t = kv_ref[:, g * LANES:(g + 1) * LANES]
        r = _rope(_rms(t, gka_ref[...], A_HEAD_DIM), ca, sa, pa)
        ka_o[g] = r.astype(BF16)
        va_o[g] = kv_ref[:, (A_KV_HEADS + g) * LANES:(A_KV_HEADS + g + 1) * LANES].astype(BF16)
    for p in range(IDX_HEADS // 2):
        t = qi_ref[:, p * LANES:(p + 1) * LANES]
        r = _rope(t, ci, si, pi)
        qi_o[2 * p] = jnp.where(lane < IDX_DIM, r, 0.0).astype(BF16)
        qi_o[2 * p + 1] = jnp.where(lane < IDX_DIM, 0.0, r).astype(BF16)

    kiwi = kiwi_ref[...]
    ki = _rope(kiwi, ci, si, pi)
    ki2_o[...] = jnp.where(lane < IDX_DIM, ki, pltpu.roll(ki, IDX_DIM, 1)).astype(BF16)
    wi_o[...] = kiwi * IDX_W_SCALE

    mqn_o[...] = _rms(mql_ref[...], gmq_ref[...], M_Q_LORA).astype(BF16)
    mkvn_o[...] = _rms(mkvl_ref[...], gmkv_ref[...], M_KV_LORA).astype(BF16)


def _prep(l, proj, tabs_cos, tabs_sin, swaps, g_qa, g_ka, g_mq_lat, g_mkv_lat):
    tm = 256
    per_b = SEQ // tm

    def swap(t):
        return pl.BlockSpec((1, 2 * LANES, LANES), lambda i: (t, 0, 0))

    def col(width, start):
        idx = start // width
        return pl.BlockSpec((tm, width), lambda i: (i, idx))

    def tab(t):
        return pl.BlockSpec((1, tm, LANES), lambda i: (t, i, 0))

    def vec(width):
        return _layer_row_spec(l, width)

    def heads(n):
        return pl.BlockSpec((None, n, tm, LANES), lambda i: (i // per_b, 0, i % per_b, 0))

    def seq(width):
        return pl.BlockSpec((None, tm, width), lambda i: (i // per_b, i % per_b, 0))

    def tok(width):
        return pl.BlockSpec((tm, width), lambda i: (i, 0))

    return pl.pallas_call(
        _prep_kernel,
        grid=(TOKENS // tm,),
        in_specs=[
            col(1024, COL_QA), col(1024, COL_QI), col(512, COL_MQL), col(512, COL_KV),
            col(256, COL_MKVL), col(128, COL_KIWI),
            tab(0), tab(0), tab(1), tab(1), swap(0), swap(1),
            vec(A_HEAD_DIM), vec(A_HEAD_DIM), vec(M_Q_LORA), vec(M_KV_LORA),
        ],
        out_specs=[
            heads(A_HEADS), heads(A_KV_HEADS), heads(A_KV_HEADS),
            heads(IDX_HEADS), seq(LANES), seq(LANES),
            tok(M_Q_LORA), tok(M_KV_LORA),
        ],
        out_shape=[
            jax.ShapeDtypeStruct((BATCH, A_HEADS, SEQ, LANES), BF16),
            jax.ShapeDtypeStruct((BATCH, A_KV_HEADS, SEQ, LANES), BF16),
            jax.ShapeDtypeStruct((BATCH, A_KV_HEADS, SEQ, LANES), BF16),
            jax.ShapeDtypeStruct((BATCH, IDX_HEADS, SEQ, LANES), BF16),
            jax.ShapeDtypeStruct((BATCH, SEQ, LANES), BF16),
            jax.ShapeDtypeStruct((BATCH, SEQ, LANES), F32),
            jax.ShapeDtypeStruct((TOKENS, M_Q_LORA), BF16),
            jax.ShapeDtypeStruct((TOKENS, M_KV_LORA), BF16),
        ],
        compiler_params=_params("arbitrary"),
        name="prep",
    )(proj, proj, proj, proj, proj, proj, tabs_cos, tabs_sin, tabs_cos, tabs_sin, swaps, swaps,
      g_qa, g_ka, g_mq_lat, g_mkv_lat)


def _mla_up_kernel(mqn_ref, mkvn_ref, mkr_ref, wq_ref, wkv_ref, cm_ref, sm_ref, pm_ref,
                   gqn_ref, gqr_ref, gkn_ref, gkr_ref, q_o, k_o, v_o):
    tm = mqn_ref.shape[0]
    lane = lax.broadcasted_iota(jnp.int32, (tm, LANES), 1)
    low = lane < M_ROPE
    cm, sm, pm = cm_ref[0], sm_ref[0], pm_ref[0]
    scale_m = M_QK ** -0.5
    nope_w = M_HEADS * M_NOPE

    q = _dot(mqn_ref[...], wq_ref[...])
    kv = _dot(mkvn_ref[...], wkv_ref[...])

    for p in range(M_HEADS // 2):
        rp = q[:, nope_w + p * LANES: nope_w + (p + 1) * LANES]
        sq = rp * rp
        s_all = jnp.sum(sq, axis=-1, keepdims=True)
        s_lo = jnp.sum(jnp.where(low, sq, 0.0), axis=-1, keepdims=True)
        for e in range(2):
            h = 2 * p + e
            nope = q[:, h * M_NOPE:(h + 1) * M_NOPE]
            ss = jnp.sum(nope * nope, axis=-1, keepdims=True) + (s_lo if e == 0 else s_all - s_lo)
            rs = lax.rsqrt(ss * (1.0 / M_QK) + EPS)
            roped = _rope(rp * rs * gqr_ref[...], cm, sm, pm)
            if e == 1:
                roped = pltpu.roll(roped, M_ROPE, 1)
            q_o[h, :, 0:LANES] = (nope * rs * gqn_ref[...] * scale_m).astype(BF16)
            q_o[h, :, LANES:2 * LANES] = (jnp.where(low, roped, 0.0) * scale_m).astype(BF16)

    kr = mkr_ref[...]
    kr_ss = jnp.sum(kr * kr, axis=-1, keepdims=True)
    kr_roped = _rope(kr * gkr_ref[...], cm, sm, pm)
    kr_roped = jnp.where(low, kr_roped, 0.0)
    for h in range(M_HEADS):
        nope = kv[:, h * 2 * LANES: h * 2 * LANES + M_NOPE]
        ss = jnp.sum(nope * nope, axis=-1, keepdims=True) + kr_ss
        rs = lax.rsqrt(ss * (1.0 / M_QK) + EPS)
        k_o[h, :, 0:LANES] = (nope * rs * gkn_ref[...]).astype(BF16)
        k_o[h, :, LANES:2 * LANES] = (kr_roped * rs).astype(BF16)
        v_o[h] = kv[:, h * 2 * LANES + M_NOPE:(h + 1) * 2 * LANES].astype(BF16)


def _mla_up(l, mqn, mkvn, proj, wq, wkv, tabs_cos, tabs_sin, swaps, gqn, gqr, gkn, gkr):
    tm = 256
    per_b = SEQ // tm

    def vec():
        return _layer_row_spec(l, LANES)

    def heads(width):
        return pl.BlockSpec((None, M_HEADS, tm, width), lambda i: (i // per_b, 0, i % per_b, 0))

    return pl.pallas_call(
        _mla_up_kernel,
        grid=(TOKENS // tm,),
        in_specs=[
            pl.BlockSpec((tm, M_Q_LORA), lambda i: (i, 0)),
            pl.BlockSpec((tm, M_KV_LORA), lambda i: (i, 0)),
            pl.BlockSpec((tm, LANES), lambda i: (i, COL_MKR // LANES)),
            pl.BlockSpec((None,) + wq.shape[1:], lambda i: (l, 0, 0)),
            pl.BlockSpec((None,) + wkv.shape[1:], lambda i: (l, 0, 0)),
            pl.BlockSpec((1, tm, LANES), lambda i: (2, i, 0)),
            pl.BlockSpec((1, tm, LANES), lambda i: (2, i, 0)),
            pl.BlockSpec((1, 2 * LANES, LANES), lambda i: (2, 0, 0)),
            vec(), vec(), vec(), vec(),
        ],
        out_specs=[heads(M_QK_PAD), heads(M_QK_PAD), heads(M_V)],
        out_shape=[
            jax.ShapeDtypeStruct((BATCH, M_HEADS, SEQ, M_QK_PAD), BF16),
            jax.ShapeDtypeStruct((BATCH, M_HEADS, SEQ, M_QK_PAD), BF16),
            jax.ShapeDtypeStruct((BATCH, M_HEADS, SEQ, M_V), BF16),
        ],
        compiler_params=_params("arbitrary"),
        name="mla_up",
    )(mqn, mkvn, proj, wq, wkv, tabs_cos, tabs_sin, swaps, gqn, gqr, gkn, gkr)


INT_MIN = -(2 ** 31)
KEY_MASKED = INT_MIN

DSA_TQ = 256
KEY_CHUNK = 512
SEARCH_ROWS = 128
SEARCH_UNROLL = 5
IDX_CHUNK = 256
IDX_ROWS = 64
ATT_SPLIT = 4


def _dsa_index(i, n_chunks, qi_ref, ki2_ref, wi_ref, key_ref, d_refs, wb_ref):
    tq = qi_ref.shape[1]
    w = wi_ref[...]
    for h in range(IDX_HEADS):
        wb_ref[h] = jnp.broadcast_to(w[:, IDX_DIM + h:IDX_DIM + h + 1], (tq, LANES))

    def matmul(c, d_ref):
        start = pl.multiple_of(c * IDX_CHUNK, IDX_CHUNK)
        q_all = qi_ref[...].reshape(IDX_HEADS * tq, LANES)
        d_ref[...] = _dot_nt(q_all, ki2_ref[pl.ds(start, IDX_CHUNK), :])

    def head_sum(c, d_ref):
        for r0 in range(0, tq, IDX_ROWS):
            row_t = i * tq + r0 + lax.broadcasted_iota(jnp.int32, (IDX_ROWS, LANES), 0)
            for j0 in range(0, IDX_CHUNK, LANES):
                acc = jnp.zeros((IDX_ROWS, LANES), F32)
                for h in range(IDX_HEADS):
                    d = d_ref[h * tq + r0:h * tq + r0 + IDX_ROWS, j0:j0 + LANES]
                    acc = acc + jnp.maximum(d, 0.0) * wb_ref[h, r0:r0 + IDX_ROWS, :]
                bits = pltpu.bitcast(acc, jnp.int32)
                img = bits ^ ((bits >> 31) & jnp.int32(0x7FFFFFFF))
                col_t = c * IDX_CHUNK + j0 + lax.broadcasted_iota(jnp.int32, (IDX_ROWS, LANES), 1)
                key_ref[c, r0:r0 + IDX_ROWS, j0:j0 + LANES] = jnp.where(col_t <= row_t, img, KEY_MASKED)

    d0_ref, d1_ref = d_refs
    matmul(0, d0_ref)

    def pair(k, carry):
        c = 2 * k
        matmul(c + 1, d1_ref)
        head_sum(c, d0_ref)
        matmul(jnp.minimum(c + 2, n_chunks - 1), d0_ref)
        head_sum(c + 1, d1_ref)
        return carry

    lax.fori_loop(0, n_chunks // 2, pair, 0)


def _dsa_select(nk, key_ref, bias_ref):
    chunks = range(nk // IDX_CHUNK)
    n_sel = float(N_SEL)
    groups = list(range(0, key_ref.shape[1], SEARCH_ROWS))

    def count_ge(r0, cand):
        hits = None
        for c in chunks:
            hit = jnp.where(key_ref[c, r0:r0 + SEARCH_ROWS, :] >= cand, 1.0, 0.0)
            hits = hit if hits is None else hits + hit
        return jnp.sum(hits, axis=-1, keepdims=True)

    zero = jnp.zeros((SEARCH_ROWS, 1), jnp.int32)
    thr0 = tuple(jnp.where(count_ge(r0, zero) >= n_sel, jnp.int32(0), jnp.int32(INT_MIN))
                 for r0 in groups)

    def step(b, thrs):
        bit = lax.shift_left(jnp.int32(1), jnp.int32(30) - b)
        out = []
        for r0, thr in zip(groups, thrs):
            cand = thr | bit
            out.append(jnp.where(count_ge(r0, cand) >= n_sel, cand, thr))
        return tuple(out)

    thrs = lax.fori_loop(0, 30, step, thr0, unroll=SEARCH_UNROLL)
    thrs = step(jnp.int32(30), thrs)
    for r0, thr in zip(groups, thrs):
        floor = jnp.maximum(thr, jnp.int32(KEY_MASKED + 1))
        for c in chunks:
            sel = key_ref[c, r0:r0 + SEARCH_ROWS, :] >= floor
            bias_ref[r0:r0 + SEARCH_ROWS, c * IDX_CHUNK:(c + 1) * IDX_CHUNK] = jnp.where(sel, 0.0, -jnp.inf)


def _dsa_attend(nk, qa_ref, ka_ref, va_ref, bias_ref, o_ref, oh_ref, lg_refs, p_refs):
    tq = qa_ref.shape[1]
    rep = A_HEADS // A_KV_HEADS

    sub = rep // ATT_SPLIT

    def group(g, carry):
        k = ka_ref[g, 0:nk, :]
        v = va_ref[g, 0:nk, :]
        inv_l = [None] * ATT_SPLIT

        def scores(s):
            q = qa_ref[pl.ds(g * rep + s * sub, sub)].reshape(sub * tq, LANES)
            lg_refs[s][:, 0:nk] = _dot_nt(q, k)

        def softmax(s):
            logits = lg_refs[s][:, 0:nk].reshape(sub, tq, nk) + bias_ref[:, 0:nk][None]
            m = jnp.max(logits, axis=-1, keepdims=True)
            p = jnp.exp(logits - m)
            inv_l[s] = 1.0 / jnp.sum(p, axis=-1, keepdims=True)
            p_refs[s][:, 0:nk] = p.astype(BF16).reshape(sub * tq, nk)

        def values(s):
            o = _dot(p_refs[s][:, 0:nk], v).reshape(sub, tq, LANES) * inv_l[s]
            oh_ref[pl.ds(g * rep + s * sub, sub)] = o.astype(BF16)

        for t in range(ATT_SPLIT + 2):
            if t < ATT_SPLIT:
                scores(t)
            if 0 <= t - 1 < ATT_SPLIT:
                softmax(t - 1)
            if 0 <= t - 2 < ATT_SPLIT:
                values(t - 2)
        return carry

    lax.fori_loop(0, A_KV_HEADS, group, 0)
    for h in range(A_HEADS):
        o_ref[:, h * LANES:(h + 1) * LANES] = oh_ref[h]


def _dsa_kernel(qa_ref, ka_ref, va_ref, qi_ref, ki2_ref, wi_ref, o_ref,
                key_ref, bias_ref, d0_ref, d1_ref, wb_ref, oh_ref, *stage_refs):
    lg_refs, p_refs = stage_refs[:ATT_SPLIT], stage_refs[ATT_SPLIT:]
    i = pl.program_id(1)
    tq = DSA_TQ
    n_free = N_SEL // tq
    variant = i // (KEY_CHUNK // tq)

    for v in range(n_free):
        @pl.when(i == v)
        def _(v=v):
            nk = (v + 1) * tq
            row = i * tq + lax.broadcasted_iota(jnp.int32, (tq, nk), 0)
            col = lax.broadcasted_iota(jnp.int32, (tq, nk), 1)
            bias_ref[:, 0:nk] = jnp.where(col <= row, 0.0, -jnp.inf)
            _dsa_attend(nk, qa_ref, ka_ref, va_ref, bias_ref, o_ref, oh_ref, lg_refs, p_refs)

    @pl.when(i >= n_free)
    def _():
        n_chunks = (variant + 1) * (KEY_CHUNK // IDX_CHUNK)
        _dsa_index(i, n_chunks, qi_ref, ki2_ref, wi_ref, key_ref, (d0_ref, d1_ref), wb_ref)

    for v in range(SEQ // KEY_CHUNK):
        @pl.when((i >= n_free) & (variant == v))
        def _(v=v):
            nk = (v + 1) * KEY_CHUNK
            _dsa_select(nk, key_ref, bias_ref)
            _dsa_attend(nk, qa_ref, ka_ref, va_ref, bias_ref, o_ref, oh_ref, lg_refs, p_refs)


def _dsa(qa, ka, va, qi, ki2, wi):
    tq = DSA_TQ
    nq = SEQ // tq
    att_rows = A_HEADS // A_KV_HEADS // ATT_SPLIT * tq
    return pl.pallas_call(
        _dsa_kernel,
        grid=(BATCH, nq),
        in_specs=[
            pl.BlockSpec((None, A_HEADS, tq, LANES), lambda b, i: (b, 0, i, 0)),
            pl.BlockSpec((None, A_KV_HEADS, SEQ, LANES), lambda b, i: (b, 0, 0, 0)),
            pl.BlockSpec((None, A_KV_HEADS, SEQ, LANES), lambda b, i: (b, 0, 0, 0)),
            pl.BlockSpec((None, IDX_HEADS, tq, LANES), lambda b, i: (b, 0, i, 0)),
            pl.BlockSpec((None, SEQ, LANES), lambda b, i: (b, 0, 0)),
            pl.BlockSpec((None, tq, LANES), lambda b, i: (b, i, 0)),
        ],
        out_specs=pl.BlockSpec((tq, A_HEADS * A_HEAD_DIM), lambda b, i: (b * nq + i, 0)),
        out_shape=jax.ShapeDtypeStruct((TOKENS, A_HEADS * A_HEAD_DIM), BF16),
        scratch_shapes=[
            pltpu.VMEM((SEQ // IDX_CHUNK, tq, IDX_CHUNK), jnp.int32),
            pltpu.VMEM((tq, SEQ), F32),
            pltpu.VMEM((IDX_HEADS * tq, IDX_CHUNK), F32),
            pltpu.VMEM((IDX_HEADS * tq, IDX_CHUNK), F32),
            pltpu.VMEM((IDX_HEADS, tq, LANES), F32),
            pltpu.VMEM((A_HEADS, tq, LANES), BF16),
        ] + [pltpu.VMEM((att_rows, SEQ), F32)] * ATT_SPLIT + [pltpu.VMEM((att_rows, SEQ), BF16)] * ATT_SPLIT,
        compiler_params=_params("arbitrary", "arbitrary"),
        name="dsa",
    )(qa, ka, va, qi, ki2, wi)


MLA_TQ = 256
MLA_HEADS_PER_STEP = 4


def _mla_body(nk, i, q_ref, k_ref, v_ref, o_ref, lg_refs, p_refs):
    n_heads, tq = q_ref.shape[0], q_ref.shape[1]
    n0 = nk - KEY_CHUNK
    row = i * tq + lax.broadcasted_iota(jnp.int32, (tq, KEY_CHUNK), 0)
    col = n0 + lax.broadcasted_iota(jnp.int32, (tq, KEY_CHUNK), 1)
    causal = col <= row
    inv_l = [None] * n_heads

    def scores(h):
        lg_refs[h][:, 0:nk] = _dot_nt(q_ref[h], k_ref[h, 0:nk, :])

    def softmax(h):
        s1 = jnp.where(causal, lg_refs[h][:, n0:nk], -jnp.inf)
        m = jnp.max(s1, axis=-1, keepdims=True)
        if n0:
            s0 = lg_refs[h][:, 0:n0]
            m = jnp.maximum(m, jnp.max(s0, axis=-1, keepdims=True))
        p1 = jnp.exp(s1 - m)
        l = jnp.sum(p1, axis=-1, keepdims=True)
        p_refs[h][:, n0:nk] = p1.astype(BF16)
        if n0:
            p0 = jnp.exp(s0 - m)
            l = l + jnp.sum(p0, axis=-1, keepdims=True)
            p_refs[h][:, 0:n0] = p0.astype(BF16)
        inv_l[h] = 1.0 / l

    def values(h):
        o = _dot(p_refs[h][:, 0:nk], v_ref[h, 0:nk, :]) * inv_l[h]
        o_ref[:, h * M_V:(h + 1) * M_V] = o.astype(BF16)

    for t in range(n_heads + 2):
        if t < n_heads:
            scores(t)
        if 0 <= t - 1 < n_heads:
            softmax(t - 1)
        if 0 <= t - 2 < n_heads:
            values(t - 2)


def _mla_kernel(q_ref, k_ref, v_ref, o_ref, *stage_refs):
    n_heads = q_ref.shape[0]
    lg_refs, p_refs = stage_refs[:n_heads], stage_refs[n_heads:]
    i = pl.program_id(2)
    per_chunk = KEY_CHUNK // MLA_TQ
    for v in range(SEQ // KEY_CHUNK):
        @pl.when(i // per_chunk == v)
        def _(v=v):
            _mla_body((v + 1) * KEY_CHUNK, i, q_ref, k_ref, v_ref, o_ref, lg_refs, p_refs)


def _mla(q, k, v):
    tq, hg = MLA_TQ, MLA_HEADS_PER_STEP
    nq = SEQ // tq
    return pl.pallas_call(
        _mla_kernel,
        grid=(BATCH, M_HEADS // hg, nq),
        in_specs=[
            pl.BlockSpec((None, hg, tq, M_QK_PAD), lambda b, g, i: (b, g, i, 0)),
            pl.BlockSpec((None, hg, SEQ, M_QK_PAD), lambda b, g, i: (b, g, 0, 0)),
            pl.BlockSpec((None, hg, SEQ, M_V), lambda b, g, i: (b, g, 0, 0)),
        ],
        out_specs=pl.BlockSpec((tq, hg * M_V), lambda b, g, i: (b * nq + i, g)),
        out_shape=jax.ShapeDtypeStruct((TOKENS, M_HEADS * M_V), BF16),
        scratch_shapes=[pltpu.VMEM((tq, SEQ), F32)] * hg + [pltpu.VMEM((tq, SEQ), BF16)] * hg,
        compiler_params=_params("arbitrary", "arbitrary", "arbitrary"),
        name="mla",
    )(q, k, v)


def _merge_kernel(oa_ref, ob_ref, ga_ref, gb_ref, wpa_ref, wpb_ref, wo_ref, x_ref, gt_ref,
                  gf_ref, scf_ref, shf_ref, o_ref, hf_ref):
    a = _dot(oa_ref[...], wpa_ref[...])
    b = _dot(ob_ref[...], wpb_ref[...])
    merged = jax.nn.sigmoid(ga_ref[...]) * a + jax.nn.sigmoid(gb_ref[...]) * b
    y = _dot(merged.astype(BF16), wo_ref[...])
    xn = x_ref[...] + gt_ref[...] * y
    o_ref[...] = xn
    ms = jnp.mean(xn * xn, axis=-1, keepdims=True)
    hn = xn * lax.rsqrt(ms + EPS) * gf_ref[...]
    hf_ref[...] = (hn * (1.0 + scf_ref[...]) + shf_ref[...]).astype(BF16)


def _merge(l, oa, ob, proj, wpa, wpb, wo, x, mod, g_ffn):
    tm = 256
    per_b = SEQ // tm

    def resident(shape):
        return pl.BlockSpec((None,) + shape[1:], lambda i: (l, 0, 0), pipeline_mode=pl.Buffered(1))

    return pl.pallas_call(
        _merge_kernel,
        grid=(TOKENS // tm,),
        in_specs=[
            pl.BlockSpec((tm, A_HEADS * A_HEAD_DIM), lambda i: (i, 0)),
            pl.BlockSpec((tm, M_HEADS * M_V), lambda i: (i, 0)),
            pl.BlockSpec((tm, D_MODEL), lambda i: (i, COL_GATE_A // D_MODEL)),
            pl.BlockSpec((tm, D_MODEL), lambda i: (i, COL_GATE_B // D_MODEL)),
            resident(wpa.shape), resident(wpb.shape), resident(wo.shape),
            pl.BlockSpec((tm, D_MODEL), lambda i: (i, 0)),
            _mod_spec(l, 2, per_b),
            _layer_row_spec(l, D_MODEL),
            _mod_spec(l, 4, per_b),
            _mod_spec(l, 3, per_b),
        ],
        out_specs=[pl.BlockSpec((tm, D_MODEL), lambda i: (i, 0))] * 2,
        out_shape=[jax.ShapeDtypeStruct((TOKENS, D_MODEL), F32),
                   jax.ShapeDtypeStruct((TOKENS, D_MODEL), BF16)],
        compiler_params=_params("arbitrary"),
        name="merge",
    )(oa, ob, proj, proj, wpa, wpb, wo, x, mod, g_ffn, mod, mod)


HALO = 8


def _ffn_up_kernel(h_ref, wg_ref, wv_ref, cg_ref, cv_ref, bg_ref, bv_ref,
                   o_ref, wgb_ref, wvb_ref, buf_ref, *, tiles_per_seq):
    i = pl.program_id(1)
    tm, tn = o_ref.shape

    @pl.when(i == 0)
    def _():
        wgb_ref[...] = wg_ref[...].astype(BF16)
        wvb_ref[...] = wv_ref[...].astype(BF16)

    seq_start = (i % tiles_per_seq) == 0

    @pl.when(seq_start)
    def _():
        buf_ref[0:HALO, :] = jnp.zeros((HALO, 2 * tn), F32)

    @pl.when(jnp.logical_not(seq_start))
    def _():
        buf_ref[0:HALO, :] = buf_ref[tm:tm + HALO, :]

    h = h_ref[...]
    buf_ref[HALO:HALO + tm, 0:tn] = _dot(h, wgb_ref[...])
    buf_ref[HALO:HALO + tm, tn:2 * tn] = _dot(h, wvb_ref[...])

    def conv(lo, c_ref, b_ref):
        u0 = buf_ref[HALO:HALO + tm, lo:lo + tn]
        u1 = buf_ref[HALO - 1:HALO - 1 + tm, lo:lo + tn]
        u2 = buf_ref[HALO - 2:HALO - 2 + tm, lo:lo + tn]
        return b_ref[...] + c_ref[0:1, :] * u2 + c_ref[1:2, :] * u1 + c_ref[2:3, :] * u0

    gate = conv(0, cg_ref, bg_ref)
    val = conv(tn, cv_ref, bv_ref)
    o_ref[...] = (gate * jax.nn.sigmoid(gate) * val).astype(BF16)


def _ffn_up(l, h, w_up, w_conv, b_conv):
    tm, tn = 1024, 512
    nj = D_FF // tn
    kern = functools.partial(_ffn_up_kernel, tiles_per_seq=SEQ // tm)
    return pl.pallas_call(
        kern,
        grid=(nj, TOKENS // tm),
        in_specs=[
            pl.BlockSpec((tm, D_MODEL), lambda j, i: (i, 0)),
            pl.BlockSpec((None, D_MODEL, tn), lambda j, i: (l, 0, j)),
            pl.BlockSpec((None, D_MODEL, tn), lambda j, i: (l, 0, nj + j)),
            pl.BlockSpec((None, CONV_W, tn), lambda j, i: (l, 0, j)),
            pl.BlockSpec((None, CONV_W, tn), lambda j, i: (l, 0, nj + j)),
            pl.BlockSpec((None, 1, tn), lambda j, i: (l, 0, j)),
            pl.BlockSpec((None, 1, tn), lambda j, i: (l, 0, nj + j)),
        ],
        out_specs=pl.BlockSpec((tm, tn), lambda j, i: (i, j)),
        out_shape=jax.ShapeDtypeStruct((TOKENS, D_FF), BF16),
        scratch_shapes=[
            pltpu.VMEM((D_MODEL, tn), BF16),
            pltpu.VMEM((D_MODEL, tn), BF16),
            pltpu.VMEM((HALO + tm, 2 * tn), F32),
        ],
        compiler_params=_params("arbitrary", "arbitrary"),
        name="ffn_up",
    )(h, w_up, w_up, w_conv, w_conv, b_conv, b_conv)


def _ffn_down_kernel(a_ref, w_ref, x_ref, gt_ref, o_ref):
    o_ref[...] = x_ref[...] + gt_ref[...] * _dot(a_ref[...], w_ref[...].astype(BF16))


def _ffn_down(l, act, w_down, x, mod):
    tm, tn = 1024, 256
    per_b = SEQ // tm
    return pl.pallas_call(
        _ffn_down_kernel,
        grid=(TOKENS // tm, D_MODEL // tn),
        in_specs=[
            pl.BlockSpec((tm, D_FF), lambda i, j: (i, 0)),
            pl.BlockSpec((None, D_FF, tn), lambda i, j: (l, 0, j)),
            pl.BlockSpec((tm, tn), lambda i, j: (i, j)),
            _mod_spec(l, 5, per_b, tn),
        ],
        out_specs=pl.BlockSpec((tm, tn), lambda i, j: (i, j)),
        out_shape=jax.ShapeDtypeStruct((TOKENS, D_MODEL), F32),
        compiler_params=_params("arbitrary", "arbitrary"),
        name="ffn_down",
    )(act, w_down, x, mod)


def _pack_w_in(w_in):
    o = np.cumsum((0,) + (1024, 256, 256, 1024, 64, 16, 512, 256, 64, 2048, 2048))
    w_t = jnp.swapaxes(w_in, 1, 2)
    qa, ka, va, qi, ki, wi, mql, mkvl, mkr, ga, gb = (
        w_t[:, int(o[k]):int(o[k + 1])].astype(BF16) for k in range(11))
    z = lambda n: jnp.zeros((DEPTH, n, D_MODEL), BF16)
    packed = jnp.concatenate(
        [ga, gb, qa, qi, mql, ka, va, mkvl, ki, wi, z(48), mkr, z(64)], axis=1)
    assert packed.shape[1] == IN_PACKED
    return packed


def _pack_w_mq(w_mq_up):
    w = w_mq_up.reshape(DEPTH, M_Q_LORA, M_HEADS, M_QK)
    nope = w[..., :M_NOPE].reshape(DEPTH, M_Q_LORA, M_HEADS * M_NOPE)
    rope = w[..., M_NOPE:].reshape(DEPTH, M_Q_LORA, M_HEADS * M_ROPE)
    return jnp.concatenate([nope, rope], axis=-1).astype(BF16)


def _rope_consts():
    def inv(rot):
        return ROPE_THETA ** (-jnp.arange(0, rot, 2, dtype=F32) / rot)

    def lanes(v, group):
        rep = jnp.concatenate([v, v])
        pad = jnp.zeros((group - rep.shape[0],), F32)
        return jnp.tile(jnp.concatenate([rep, pad]), LANES // group)

    def sgn(half, group):
        v = jnp.concatenate([-jnp.ones((half,), F32), jnp.ones((half,), F32)])
        pad = jnp.zeros((group - 2 * half,), F32)
        return jnp.tile(jnp.concatenate([v, pad]), LANES // group)

    rot_a = A_HEAD_DIM // ROT_FRACTION
    rot_i = IDX_DIM // ROT_FRACTION
    inv3 = jnp.stack([lanes(inv(rot_a), LANES), lanes(inv(rot_i), IDX_DIM), lanes(inv(M_ROPE), M_ROPE)])
    sgn3 = jnp.stack([sgn(rot_a // 2, LANES), sgn(rot_i // 2, IDX_DIM), sgn(M_ROPE // 2, M_ROPE)])
    return inv3.reshape(3, 1, LANES), sgn3.reshape(3, 1, LANES)


def kernel(x, c, positions, g_attn, g_ffn, w_ada, b_ada, w_in, g_qa, g_ka, g_mq_lat, w_mq_up,
           g_mkv_lat, w_mkv_up, g_qm, g_km, w_pa, w_pb, w_o, w_up, w_conv, b_conv, w_down):
    pos_col = positions.astype(F32).reshape(TOKENS, 1)
    inv3, sgn3 = _rope_consts()
    tabs_cos, tabs_sin = _rope_tables(pos_col, inv3, sgn3)
    rot_a, rot_i = A_HEAD_DIM // ROT_FRACTION, IDX_DIM // ROT_FRACTION
    swaps = jnp.asarray(np.stack([_swap_matrix(rot_a // 2, LANES), _swap_matrix(rot_i // 2, IDX_DIM),
                                  _swap_matrix(M_ROPE // 2, M_ROPE)]), dtype=BF16)

    c8 = jnp.pad(c, ((0, 8 - BATCH), (0, 0)))
    mod = _ada(c8, w_ada, b_ada)[:, :BATCH].reshape(DEPTH, BATCH, N_ADA, 1, D_MODEL)

    w_in_p = _pack_w_in(w_in)
    w_mq_p = _pack_w_mq(w_mq_up)
    w_mkv_b = w_mkv_up.astype(BF16)
    w_pa_b, w_pb_b, w_o_b = w_pa.astype(BF16), w_pb.astype(BF16), w_o.astype(BF16)

    rows = lambda v: v.reshape(DEPTH, 1, -1)
    rope_gain = lambda v: rows(jnp.tile(v[:, M_NOPE:], (1, 2)))
    g_attn_r, g_ffn_r, b_conv_r = rows(g_attn), rows(g_ffn), rows(b_conv)
    g_qa_r, g_ka_r, g_mq_r, g_mkv_r = rows(g_qa), rows(g_ka), rows(g_mq_lat), rows(g_mkv_lat)
    gqn, gqr = rows(g_qm[:, :M_NOPE]), rope_gain(g_qm)
    gkn, gkr = rows(g_km[:, :M_NOPE]), rope_gain(g_km)

    xf = x.reshape(TOKENS, D_MODEL)
    for l in range(DEPTH):
        proj = _in_proj(l, xf, g_attn_r, mod, w_in_p)
        qa, ka, va, qi, ki2, wi, mqn, mkvn = _prep(
            l, proj, tabs_cos, tabs_sin, swaps, g_qa_r, g_ka_r, g_mq_r, g_mkv_r)
        mq, mk, mv = _mla_up(l, mqn, mkvn, proj, w_mq_p, w_mkv_b, tabs_cos, tabs_sin, swaps,
                             gqn, gqr, gkn, gkr)
        o_a = _dsa(qa, ka, va, qi, ki2, wi)
        o_b = _mla(mq, mk, mv)
        xf, h_ffn = _merge(l, o_a, o_b, proj, w_pa_b, w_pb_b, w_o_b, xf, mod, g_ffn_r)
        act = _ffn_up(l, h_ffn, w_up, w_conv, b_conv_r)
        xf = _ffn_down(l, act, w_down, xf, mod)
    return xf.reshape(BATCH, SEQ, D_MODEL)
```

```python
import functools

import jax
import jax.numpy as jnp
import numpy as np
from jax import lax
from jax.experimental import pallas as pl
from jax.experimental.pallas import tpu as pltpu

D_MODEL = 2048
BATCH = 4
SEQ = 2048
DEPTH = 4
A_HEADS = 8
A_KV_HEADS = 2
A_HEAD_DIM = 128
IDX_HEADS = 16
IDX_DIM = 64
TOPK_MAX = 256
M_HEADS = 8
M_Q_LORA = 512
M_KV_LORA = 256
M_NOPE = 128
M_ROPE = 64
M_V = 128
D_FF = 5632
CONV_W = 3
ROPE_THETA = 500000.0
ROT_FRACTION = 4
EPS = 1e-6
N_ADA = 6
IDX_W_SCALE = (IDX_HEADS * IDX_DIM) ** -0.5
N_SEL = min(TOPK_MAX, SEQ // 4)
TOKENS = BATCH * SEQ
LANES = 128
M_QK = M_NOPE + M_ROPE
M_QK_PAD = 256

BF16 = jnp.bfloat16
F32 = jnp.float32

COL_GATE_A = 0
COL_GATE_B = 2048
COL_QA = 4096
COL_QI = 5120
COL_MQL = 6144
COL_KV = 6656
COL_MKVL = 7168
COL_KIWI = 7424
COL_MKR = 7552
IN_PACKED = 7680

VMEM_LIMIT = 56 * 1024 * 1024


def _params(*sem):
    return pltpu.CompilerParams(dimension_semantics=sem, vmem_limit_bytes=VMEM_LIMIT)


def _dot(a, b):
    return jnp.dot(a, b, preferred_element_type=F32)


def _dot_nt(a, b):
    return lax.dot_general(a, b, (((1,), (1,)), ((), ())), preferred_element_type=F32)


def _ada_kernel(c_ref, w_ref, b_ref, o_ref):
    c = c_ref[...]
    c_act = (c * jax.nn.sigmoid(c)).astype(BF16)
    o_ref[0] = _dot(c_act, w_ref[0].astype(BF16)) + b_ref[0]


def _ada(c8, w_ada, b_ada):
    tn = 1024
    n = N_ADA * D_MODEL
    return pl.pallas_call(
        _ada_kernel,
        grid=(DEPTH, n // tn),
        in_specs=[
            pl.BlockSpec((8, D_MODEL), lambda l, j: (0, 0)),
            pl.BlockSpec((1, D_MODEL, tn), lambda l, j: (l, 0, j)),
            pl.BlockSpec((1, 1, tn), lambda l, j: (l, 0, j)),
        ],
        out_specs=pl.BlockSpec((1, 8, tn), lambda l, j: (l, 0, j)),
        out_shape=jax.ShapeDtypeStruct((DEPTH, 8, n), F32),
        compiler_params=_params("arbitrary", "arbitrary"),
        name="ada",
    )(c8, w_ada, b_ada.reshape(DEPTH, 1, n))


N_ROPE_TABLES = 3


def _split_dot(t, m2):
    hi = t.astype(BF16)
    lo = (t - hi.astype(F32)).astype(BF16)
    return _dot(jnp.concatenate([hi, lo], axis=1), m2)


def _rope_table_kernel(pos_ref, inv_ref, sgn_ref, exp_ref, cos_ref, sin_ref):
    ang = pos_ref[...] * inv_ref[...]
    c, s = jnp.cos(ang), jnp.sin(ang)
    for t in range(N_ROPE_TABLES):
        cos_ref[t] = _split_dot(c, exp_ref[t])
        sin_ref[t] = _split_dot(s, exp_ref[t]) * sgn_ref[t]


def _rope_tables(pos_col, inv_c, sgn3, expand):
    tm = 512
    n = N_ROPE_TABLES
    return pl.pallas_call(
        _rope_table_kernel,
        grid=(TOKENS // tm,),
        in_specs=[
            pl.BlockSpec((tm, 1), lambda i: (i, 0)),
            pl.BlockSpec((1, LANES), lambda i: (0, 0)),
            pl.BlockSpec((n, 1, LANES), lambda i: (0, 0, 0)),
            pl.BlockSpec((n, 2 * LANES, LANES), lambda i: (0, 0, 0)),
        ],
        out_specs=[pl.BlockSpec((n, tm, LANES), lambda i: (0, i, 0))] * 2,
        out_shape=[jax.ShapeDtypeStruct((n, TOKENS, LANES), F32)] * 2,
        compiler_params=_params("arbitrary"),
        name="rope_tables",
    )(pos_col, inv_c, sgn3, expand)


def _rope(t, cos, sin_signed, swap):
    return t * cos + _split_dot(t, swap) * sin_signed


def _swap_matrix(half, group):
    p = np.zeros((LANES, LANES), np.float32)
    for j in range(LANES):
        g = j % group
        if g < half:
            p[j + half, j] = 1.0
        elif g < 2 * half:
            p[j - half, j] = 1.0
    return np.concatenate([p, p], axis=0)


def _rms(t, g, width):
    ms = jnp.sum(t * t, axis=-1, keepdims=True) * (1.0 / width)
    return t * lax.rsqrt(ms + EPS) * g


NORM_ROWS = 16


def _modnorm_into(h_ref, x_ref, g_ref, sc_ref, sh_ref):
    g, sc1, sh = g_ref[...], 1.0 + sc_ref[...], sh_ref[...]

    def chunk(c, carry):
        rows = pl.ds(pl.multiple_of(c * NORM_ROWS, NORM_ROWS), NORM_ROWS)
        x = x_ref[rows, :]
        ms = jnp.mean(x * x, axis=-1, keepdims=True)
        y = x * lax.rsqrt(ms + EPS) * g
        h_ref[rows, :] = (y * sc1 + sh).astype(BF16)
        return carry

    lax.fori_loop(0, x_ref.shape[0] // NORM_ROWS, chunk, 0, unroll=8)


def _in_proj_kernel(x_ref, g_ref, sc_ref, sh_ref, w_ref, o_ref, h_ref):
    @pl.when(pl.program_id(1) == 0)
    def _():
        _modnorm_into(h_ref, x_ref, g_ref, sc_ref, sh_ref)

    o_ref[...] = _dot_nt(h_ref[...], w_ref[...])


def _mod_spec(l, k, per_b, width=D_MODEL):
    if width == D_MODEL:
        return pl.BlockSpec((None, None, None, 1, D_MODEL), lambda i, *_: (l, i // per_b, k, 0, 0))
    return pl.BlockSpec((None, None, None, 1, width), lambda i, j: (l, i // per_b, k, 0, j))


def _layer_row_spec(l, width):
    return pl.BlockSpec((None, 1, width), lambda *_: (l, 0, 0))


def _in_proj(l, x, g, mod, w):
    tm, tn = 1024, 768
    per_b = SEQ // tm
    n = w.shape[1]
    return pl.pallas_call(
        _in_proj_kernel,
        grid=(TOKENS // tm, n // tn),
        in_specs=[
            pl.BlockSpec((tm, D_MODEL), lambda i, j: (i, 0)),
            _layer_row_spec(l, D_MODEL),
            _mod_spec(l, 1, per_b),
            _mod_spec(l, 0, per_b),
            pl.BlockSpec((None, tn, D_MODEL), lambda i, j: (l, j, 0)),
        ],
        out_specs=pl.BlockSpec((tm, tn), lambda i, j: (i, j)),
        out_shape=jax.ShapeDtypeStruct((TOKENS, n), F32),
        scratch_shapes=[pltpu.VMEM((tm, D_MODEL), BF16)],
        compiler_params=_params("arbitrary", "arbitrary"),
        name="in_proj",
    )(x, g, mod, mod, w)


def _prep_kernel(qa_ref, qi_ref, mql_ref, kv_ref, mkvl_ref, kiwi_ref,
                 ca_ref, sa_ref, ci_ref, si_ref, pa_ref, pi_ref,
                 gqa_ref, gka_ref, gmq_ref, gmkv_ref,
                 qa_o, ka_o, va_o, qi_o, ki2_o, wi_o, mqn_o, mkvn_o):
    tm = qa_ref.shape[0]
    lane = lax.broadcasted_iota(jnp.int32, (tm, LANES), 1)
    ca, sa, pa = ca_ref[0], sa_ref[0], pa_ref[0]
    ci, si, pi = ci_ref[0], si_ref[0], pi_ref[0]
    scale_a = A_HEAD_DIM ** -0.5

    for h in range(A_HEADS):
        t = qa_ref[:, h * LANES:(h + 1) * LANES]
        r = _rope(_rms(t, gqa_ref[...], A_HEAD_DIM), ca, sa, pa)
        qa_o[h] = (r * scale_a).astype(BF16)
    for g in range(A_KV_HEADS):
        t = kv_ref[:, g * LANES:(g + 1) * LANES]
        r = _rope(_rms(t, gka_ref[...], A_HEAD_DIM), ca, sa, pa)
        ka_o[g] = r.astype(BF16)
        va_o[g] = kv_ref[:, (A_KV_HEADS + g) * LANES:(A_KV_HEADS + g + 1) * LANES].astype(BF16)
    for p in range(IDX_HEADS // 2):
        t = qi_ref[:, p * LANES:(p + 1) * LANES]
        r = _rope(t, ci, si, pi)
        qi_o[2 * p] = jnp.where(lane < IDX_DIM, r, 0.0).astype(BF16)
        qi_o[2 * p + 1] = jnp.where(lane < IDX_DIM, 0.0, r).astype(BF16)

    kiwi = kiwi_ref[...]
    ki = _rope(kiwi, ci, si, pi)
    ki2_o[...] = jnp.where(lane < IDX_DIM, ki, pltpu.roll(ki, IDX_DIM, 1)).astype(BF16)
    wi_o[...] = kiwi * IDX_W_SCALE

    mqn_o[...] = _rms(mql_ref[...], gmq_ref[...], M_Q_LORA).astype(BF16)
    mkvn_o[...] = _rms(mkvl_ref[...], gmkv_ref[...], M_KV_LORA).astype(BF16)


def _prep(l, proj, tabs_cos, tabs_sin, swaps, g_qa, g_ka, g_mq_lat, g_mkv_lat):
    tm = 256
    per_b = SEQ // tm

    def swap(t):
        return pl.BlockSpec((1, 2 * LANES, LANES), lambda i: (t, 0, 0))

    def col(width, start):
        idx = start // width
        return pl.BlockSpec((tm, width), lambda i: (i, idx))

    def tab(t):
        return pl.BlockSpec((1, tm, LANES), lambda i: (t, i, 0))

    def vec(width):
        return _layer_row_spec(l, width)

    def heads(n):
        return pl.BlockSpec((None, n, tm, LANES), lambda i: (i // per_b, 0, i % per_b, 0))

    def seq(width):
        return pl.BlockSpec((None, tm, width), lambda i: (i // per_b, i % per_b, 0))

    def tok(width):
        return pl.BlockSpec((tm, width), lambda i: (i, 0))

    return pl.pallas_call(
        _prep_kernel,
        grid=(TOKENS // tm,),
        in_specs=[
            col(1024, COL_QA), col(1024, COL_QI), col(512, COL_MQL), col(512, COL_KV),
            col(256, COL_MKVL), col(128, COL_KIWI),
            tab(0), tab(0), tab(1), tab(1), swap(0), swap(1),
            vec(A_HEAD_DIM), vec(A_HEAD_DIM), vec(M_Q_LORA), vec(M_KV_LORA),
        ],
        out_specs=[
            heads(A_HEADS), heads(A_KV_HEADS), heads(A_KV_HEADS),
            heads(IDX_HEADS), seq(LANES), seq(LANES),
            tok(M_Q_LORA), tok(M_KV_LORA),
        ],
        out_shape=[
            jax.ShapeDtypeStruct((BATCH, A_HEADS, SEQ, LANES), BF16),
            jax.ShapeDtypeStruct((BATCH, A_KV_HEADS, SEQ, LANES), BF16),
            jax.ShapeDtypeStruct((BATCH, A_KV_HEADS, SEQ, LANES), BF16),
            jax.ShapeDtypeStruct((BATCH, IDX_HEADS, SEQ, LANES), BF16),
            jax.ShapeDtypeStruct((BATCH, SEQ, LANES), BF16),
            jax.ShapeDtypeStruct((BATCH, SEQ, LANES), F32),
            jax.ShapeDtypeStruct((TOKENS, M_Q_LORA), BF16),
            jax.ShapeDtypeStruct((TOKENS, M_KV_LORA), BF16),
        ],
        compiler_params=_params("arbitrary"),
        name="prep",
    )(proj, proj, proj, proj, proj, proj, tabs_cos, tabs_sin, tabs_cos, tabs_sin, swaps, swaps,
      g_qa, g_ka, g_mq_lat, g_mkv_lat)


def _mla_up_kernel(mqn_ref, mkvn_ref, mkr_ref, wq_ref, wkv_ref, cm_ref, sm_ref, pm_ref,
                   gqn_ref, gqr_ref, gkn_ref, gkr_ref, q_o, k_o, v_o):
    tm = mqn_ref.shape[0]
    lane = lax.broadcasted_iota(jnp.int32, (tm, LANES), 1)
    low = lane < M_ROPE
    cm, sm, pm = cm_ref[0], sm_ref[0], pm_ref[0]
    scale_m = M_QK ** -0.5
    nope_w = M_HEADS * M_NOPE

    q = _dot(mqn_ref[...], wq_ref[...])
    kv = _dot(mkvn_ref[...], wkv_ref[...])

    for p in range(M_HEADS // 2):
        rp = q[:, nope_w + p * LANES: nope_w + (p + 1) * LANES]
        sq = rp * rp
        s_all = jnp.sum(sq, axis=-1, keepdims=True)
        s_lo = jnp.sum(jnp.where(low, sq, 0.0), axis=-1, keepdims=True)
        for e in range(2):
            h = 2 * p + e
            nope = q[:, h * M_NOPE:(h + 1) * M_NOPE]
            ss = jnp.sum(nope * nope, axis=-1, keepdims=True) + (s_lo if e == 0 else s_all - s_lo)
            rs = lax.rsqrt(ss * (1.0 / M_QK) + EPS)
            roped = _rope(rp * rs * gqr_ref[...], cm, sm, pm)
            if e == 1:
                roped = pltpu.roll(roped, M_ROPE, 1)
            q_o[h, :, 0:LANES] = (nope * rs * gqn_ref[...] * scale_m).astype(BF16)
            q_o[h, :, LANES:2 * LANES] = (jnp.where(low, roped, 0.0) * scale_m).astype(BF16)

    kr = mkr_ref[...]
    kr_ss = jnp.sum(kr * kr, axis=-1, keepdims=True)
    kr_roped = _rope(kr * gkr_ref[...], cm, sm, pm)
    kr_roped = jnp.where(low, kr_roped, 0.0)
    for h in range(M_HEADS):
        nope = kv[:, h * 2 * LANES: h * 2 * LANES + M_NOPE]
        ss = jnp.sum(nope * nope, axis=-1, keepdims=True) + kr_ss
        rs = lax.rsqrt(ss * (1.0 / M_QK) + EPS)
        k_o[h, :, 0:LANES] = (nope * rs * gkn_ref[...]).astype(BF16)
        k_o[h, :, LANES:2 * LANES] = (kr_roped * rs).astype(BF16)
        v_o[h] = kv[:, h * 2 * LANES + M_NOPE:(h + 1) * 2 * LANES].astype(BF16)


def _mla_up(l, mqn, mkvn, proj, wq, wkv, tabs_cos, tabs_sin, swaps, gqn, gqr, gkn, gkr):
    tm = 256
    per_b = SEQ // tm

    def vec():
        return _layer_row_spec(l, LANES)

    def heads(width):
        return pl.BlockSpec((None, M_HEADS, tm, width), lambda i: (i // per_b, 0, i % per_b, 0))

    return pl.pallas_call(
        _mla_up_kernel,
        grid=(TOKENS // tm,),
        in_specs=[
            pl.BlockSpec((tm, M_Q_LORA), lambda i: (i, 0)),
            pl.BlockSpec((tm, M_KV_LORA), lambda i: (i, 0)),
            pl.BlockSpec((tm, LANES), lambda i: (i, COL_MKR // LANES)),
            pl.BlockSpec((None,) + wq.shape[1:], lambda i: (l, 0, 0)),
            pl.BlockSpec((None,) + wkv.shape[1:], lambda i: (l, 0, 0)),
            pl.BlockSpec((1, tm, LANES), lambda i: (2, i, 0)),
            pl.BlockSpec((1, tm, LANES), lambda i: (2, i, 0)),
            pl.BlockSpec((1, 2 * LANES, LANES), lambda i: (2, 0, 0)),
            vec(), vec(), vec(), vec(),
        ],
        out_specs=[heads(M_QK_PAD), heads(M_QK_PAD), heads(M_V)],
        out_shape=[
            jax.ShapeDtypeStruct((BATCH, M_HEADS, SEQ, M_QK_PAD), BF16),
            jax.ShapeDtypeStruct((BATCH, M_HEADS, SEQ, M_QK_PAD), BF16),
            jax.ShapeDtypeStruct((BATCH, M_HEADS, SEQ, M_V), BF16),
        ],
        compiler_params=_params("arbitrary"),
        name="mla_up",
    )(mqn, mkvn, proj, wq, wkv, tabs_cos, tabs_sin, swaps, gqn, gqr, gkn, gkr)


INT_MIN = -(2 ** 31)
KEY_MASKED = INT_MIN

DSA_TQ = 256
KEY_CHUNK = 512
SEARCH_ROWS = 128
SEARCH_UNROLL = 5
IDX_CHUNK = 256
IDX_ROWS = 64
ATT_SPLIT = 4


def _dsa_index(i, n_chunks, qi_ref, ki2_ref, wi_ref, key_ref, d_refs, wb_ref):
    tq = qi_ref.shape[1]
    w = wi_ref[...]
    for h in range(IDX_HEADS):
        wb_ref[h] = jnp.broadcast_to(w[:, IDX_DIM + h:IDX_DIM + h + 1], (tq, LANES))

    def matmul(c, d_ref):
        start = pl.multiple_of(c * IDX_CHUNK, IDX_CHUNK)
        q_all = qi_ref[...].reshape(IDX_HEADS * tq, LANES)
        d_ref[...] = _dot_nt(q_all, ki2_ref[pl.ds(start, IDX_CHUNK), :])

    def head_sum(c, d_ref):
        for r0 in range(0, tq, IDX_ROWS):
            row_t = i * tq + r0 + lax.broadcasted_iota(jnp.int32, (IDX_ROWS, LANES), 0)
            for j0 in range(0, IDX_CHUNK, LANES):
                acc = jnp.zeros((IDX_ROWS, LANES), F32)
                for h in range(IDX_HEADS):
                    d = d_ref[h * tq + r0:h * tq + r0 + IDX_ROWS, j0:j0 + LANES]
                    acc = acc + jnp.maximum(d, 0.0) * wb_ref[h, r0:r0 + IDX_ROWS, :]
                bits = pltpu.bitcast(acc, jnp.int32)
                img = bits ^ ((bits >> 31) & jnp.int32(0x7FFFFFFF))
                col_t = c * IDX_CHUNK + j0 + lax.broadcasted_iota(jnp.int32, (IDX_ROWS, LANES), 1)
                key_ref[c, r0:r0 + IDX_ROWS, j0:j0 + LANES] = jnp.where(col_t <= row_t, img, KEY_MASKED)

    d0_ref, d1_ref = d_refs
    matmul(0, d0_ref)

    def pair(k, carry):
        c = 2 * k
        matmul(c + 1, d1_ref)
        head_sum(c, d0_ref)
        matmul(jnp.minimum(c + 2, n_chunks - 1), d0_ref)
        head_sum(c + 1, d1_ref)
        return carry

    lax.fori_loop(0, n_chunks // 2, pair, 0)


def _dsa_select(nk, nk_bias, key_ref, bias_ref):
    chunks = range(nk // IDX_CHUNK)
    n_sel = float(N_SEL)
    groups = list(range(0, key_ref.shape[1], SEARCH_ROWS))

    def count_ge(r0, cand):
        hits = None
        for c in chunks:
            hit = jnp.where(key_ref[c, r0:r0 + SEARCH_ROWS, :] >= cand, 1.0, 0.0)
            hits = hit if hits is None else hits + hit
        return jnp.sum(hits, axis=-1, keepdims=True)

    zero = jnp.zeros((SEARCH_ROWS, 1), jnp.int32)
    thr0 = tuple(jnp.where(count_ge(r0, zero) >= n_sel, jnp.int32(0), jnp.int32(INT_MIN))
                 for r0 in groups)

    def step(b, thrs):
        bit = lax.shift_left(jnp.int32(1), jnp.int32(30) - b)
        out = []
        for r0, thr in zip(groups, thrs):
            cand = thr | bit
            out.append(jnp.where(count_ge(r0, cand) >= n_sel, cand, thr))
        return tuple(out)

    thrs = lax.fori_loop(0, 30, step, thr0, unroll=SEARCH_UNROLL)
    thrs = step(jnp.int32(30), thrs)
    for r0, thr in zip(groups, thrs):
        floor = jnp.maximum(thr, jnp.int32(KEY_MASKED + 1))
        for c in range(nk_bias // IDX_CHUNK):
            sel = key_ref[c, r0:r0 + SEARCH_ROWS, :] >= floor
            bias_ref[r0:r0 + SEARCH_ROWS, c * IDX_CHUNK:(c + 1) * IDX_CHUNK] = jnp.where(sel, 0.0, -jnp.inf)


def _dsa_attend(nk, qa_ref, ka_ref, va_ref, bias_ref, o_ref, oh_ref, lg_refs, p_refs):
    tq = qa_ref.shape[1]
    rep = A_HEADS // A_KV_HEADS

    sub = rep // ATT_SPLIT

    def group(g, carry):
        k = ka_ref[g, 0:nk, :]
        v = va_ref[g, 0:nk, :]
        inv_l = [None] * ATT_SPLIT

        def scores(s):
            q = qa_ref[pl.ds(g * rep + s * sub, sub)].reshape(sub * tq, LANES)
            lg_refs[s][:, 0:nk] = _dot_nt(q, k)

        def softmax(s):
            logits = lg_refs[s][:, 0:nk].reshape(sub, tq, nk) + bias_ref[:, 0:nk][None]
            m = jnp.max(logits, axis=-1, keepdims=True)
            p = jnp.exp(logits - m)
            inv_l[s] = 1.0 / jnp.sum(p, axis=-1, keepdims=True)
            p_refs[s][:, 0:nk] = p.astype(BF16).reshape(sub * tq, nk)

        def values(s):
            o = _dot(p_refs[s][:, 0:nk], v).reshape(sub, tq, LANES) * inv_l[s]
            oh_ref[pl.ds(g * rep + s * sub, sub)] = o.astype(BF16)

        for t in range(ATT_SPLIT + 2):
            if t < ATT_SPLIT:
                scores(t)
            if 0 <= t - 1 < ATT_SPLIT:
                softmax(t - 1)
            if 0 <= t - 2 < ATT_SPLIT:
                values(t - 2)
        return carry

    lax.fori_loop(0, A_KV_HEADS, group, 0)
    for h in range(A_HEADS):
        o_ref[:, h * LANES:(h + 1) * LANES] = oh_ref[h]


def _dsa_kernel(qa_ref, ka_ref, va_ref, qi_ref, ki2_ref, wi_ref, o_ref,
                key_ref, bias_ref, d0_ref, d1_ref, wb_ref, oh_ref, *stage_refs):
    lg_refs, p_refs = stage_refs[:ATT_SPLIT], stage_refs[ATT_SPLIT:]
    i = pl.program_id(1)
    tq = DSA_TQ
    n_free = N_SEL // tq
    variant = i // (KEY_CHUNK // tq)

    for v in range(n_free):
        @pl.when(i == v)
        def _(v=v):
            nk = (v + 1) * tq
            row = i * tq + lax.broadcasted_iota(jnp.int32, (tq, nk), 0)
            col = lax.broadcasted_iota(jnp.int32, (tq, nk), 1)
            bias_ref[:, 0:nk] = jnp.where(col <= row, 0.0, -jnp.inf)
            _dsa_attend(nk, qa_ref, ka_ref, va_ref, bias_ref, o_ref, oh_ref, lg_refs, p_refs)

    @pl.when(i >= n_free)
    def _():
        n_chunks = (variant + 1) * (KEY_CHUNK // IDX_CHUNK)
        _dsa_index(i, n_chunks, qi_ref, ki2_ref, wi_ref, key_ref, (d0_ref, d1_ref), wb_ref)

    for v in range(n_free, SEQ // tq):
        @pl.when(i == v)
        def _(v=v):
            nk_att = (v // (KEY_CHUNK // tq) + 1) * KEY_CHUNK
            _dsa_select((v + 1) * tq, nk_att, key_ref, bias_ref)

    for v in range(SEQ // KEY_CHUNK):
        @pl.when((i >= n_free) & (variant == v))
        def _(v=v):
            nk = (v + 1) * KEY_CHUNK
            _dsa_attend(nk, qa_ref, ka_ref, va_ref, bias_ref, o_ref, oh_ref, lg_refs, p_refs)


def _dsa(qa, ka, va, qi, ki2, wi):
    tq = DSA_TQ
    nq = SEQ // tq
    att_rows = A_HEADS // A_KV_HEADS // ATT_SPLIT * tq
    return pl.pallas_call(
        _dsa_kernel,
        grid=(BATCH, nq),
        in_specs=[
            pl.BlockSpec((None, A_HEADS, tq, LANES), lambda b, i: (b, 0, i, 0)),
            pl.BlockSpec((None, A_KV_HEADS, SEQ, LANES), lambda b, i: (b, 0, 0, 0)),
            pl.BlockSpec((None, A_KV_HEADS, SEQ, LANES), lambda b, i: (b, 0, 0, 0)),
            pl.BlockSpec((None, IDX_HEADS, tq, LANES), lambda b, i: (b, 0, i, 0)),
            pl.BlockSpec((None, SEQ, LANES), lambda b, i: (b, 0, 0)),
            pl.BlockSpec((None, tq, LANES), lambda b, i: (b, i, 0)),
        ],
        out_specs=pl.BlockSpec((tq, A_HEADS * A_HEAD_DIM), lambda b, i: (b * nq + i, 0)),
        out_shape=jax.ShapeDtypeStruct((TOKENS, A_HEADS * A_HEAD_DIM), BF16),
        scratch_shapes=[
            pltpu.VMEM((SEQ // IDX_CHUNK, tq, IDX_CHUNK), jnp.int32),
            pltpu.VMEM((tq, SEQ), F32),
            pltpu.VMEM((IDX_HEADS * tq, IDX_CHUNK), F32),
            pltpu.VMEM((IDX_HEADS * tq, IDX_CHUNK), F32),
            pltpu.VMEM((IDX_HEADS, tq, LANES), F32),
            pltpu.VMEM((A_HEADS, tq, LANES), BF16),
        ] + [pltpu.VMEM((att_rows, SEQ), F32)] * ATT_SPLIT + [pltpu.VMEM((att_rows, SEQ), BF16)] * ATT_SPLIT,
        compiler_params=_params("arbitrary", "arbitrary"),
        name="dsa",
    )(qa, ka, va, qi, ki2, wi)


MLA_TQ = 256
MLA_HEADS_PER_STEP = 4
MLA_KEY_CHUNK = 256


def _mla_body(nk, i, q_ref, k_ref, v_ref, o_ref, lg_refs, p_refs):
    n_heads, tq = q_ref.shape[0], q_ref.shape[1]
    n0 = nk - MLA_KEY_CHUNK
    row = i * tq + lax.broadcasted_iota(jnp.int32, (tq, MLA_KEY_CHUNK), 0)
    col = n0 + lax.broadcasted_iota(jnp.int32, (tq, MLA_KEY_CHUNK), 1)
    causal = col <= row
    inv_l = [None] * n_heads

    def scores(h):
        lg_refs[h][:, 0:nk] = _dot_nt(q_ref[h], k_ref[h, 0:nk, :])

    def softmax(h):
        s1 = jnp.where(causal, lg_refs[h][:, n0:nk], -jnp.inf)
        m = jnp.max(s1, axis=-1, keepdims=True)
        if n0:
            s0 = lg_refs[h][:, 0:n0]
            m = jnp.maximum(m, jnp.max(s0, axis=-1, keepdims=True))
        p1 = jnp.exp(s1 - m)
        l = jnp.sum(p1, axis=-1, keepdims=True)
        p_refs[h][:, n0:nk] = p1.astype(BF16)
        if n0:
            p0 = jnp.exp(s0 - m)
            l = l + jnp.sum(p0, axis=-1, keepdims=True)
            p_refs[h][:, 0:n0] = p0.astype(BF16)
        inv_l[h] = 1.0 / l

    def values(h):
        o = _dot(p_refs[h][:, 0:nk], v_ref[h, 0:nk, :]) * inv_l[h]
        o_ref[:, h * M_V:(h + 1) * M_V] = o.astype(BF16)

    for t in range(n_heads + 2):
        if t < n_heads:
            scores(t)
        if 0 <= t - 1 < n_heads:
            softmax(t - 1)
        if 0 <= t - 2 < n_heads:
            values(t - 2)


def _mla_kernel(q_ref, k_ref, v_ref, o_ref, *stage_refs):
    n_heads = q_ref.shape[0]
    lg_refs, p_refs = stage_refs[:n_heads], stage_refs[n_heads:]
    i = pl.program_id(2)
    per_chunk = MLA_KEY_CHUNK // MLA_TQ
    for v in range(SEQ // MLA_KEY_CHUNK):
        @pl.when(i // per_chunk == v)
        def _(v=v):
            _mla_body((v + 1) * MLA_KEY_CHUNK, i, q_ref, k_ref, v_ref, o_ref, lg_refs, p_refs)


def _mla(q, k, v):
    tq, hg = MLA_TQ, MLA_HEADS_PER_STEP
    nq = SEQ // tq
    return pl.pallas_call(
        _mla_kernel,
        grid=(BATCH, M_HEADS // hg, nq),
        in_specs=[
            pl.BlockSpec((None, hg, tq, M_QK_PAD), lambda b, g, i: (b, g, i, 0)),
            pl.BlockSpec((None, hg, SEQ, M_QK_PAD), lambda b, g, i: (b, g, 0, 0)),
            pl.BlockSpec((None, hg, SEQ, M_V), lambda b, g, i: (b, g, 0, 0)),
        ],
        out_specs=pl.BlockSpec((tq, hg * M_V), lambda b, g, i: (b * nq + i, g)),
        out_shape=jax.ShapeDtypeStruct((TOKENS, M_HEADS * M_V), BF16),
        scratch_shapes=[pltpu.VMEM((tq, SEQ), F32)] * hg + [pltpu.VMEM((tq, SEQ), BF16)] * hg,
        compiler_params=_params("arbitrary", "arbitrary", "arbitrary"),
        name="mla",
    )(q, k, v)


def _merge_kernel(oa_ref, ob_ref, ga_ref, gb_ref, wpa_ref, wpb_ref, wo_ref, x_ref, gt_ref,
                  gf_ref, scf_ref, shf_ref, o_ref, hf_ref):
    a = _dot(oa_ref[...], wpa_ref[...])
    b = _dot(ob_ref[...], wpb_ref[...])
    merged = jax.nn.sigmoid(ga_ref[...]) * a + jax.nn.sigmoid(gb_ref[...]) * b
    y = _dot(merged.astype(BF16), wo_ref[...])
    xn = x_ref[...] + gt_ref[...] * y
    o_ref[...] = xn
    ms = jnp.mean(xn * xn, axis=-1, keepdims=True)
    hn = xn * lax.rsqrt(ms + EPS) * gf_ref[...]
    hf_ref[...] = (hn * (1.0 + scf_ref[...]) + shf_ref[...]).astype(BF16)


def _merge(l, oa, ob, proj, wpa, wpb, wo, x, mod, g_ffn):
    tm = 256
    per_b = SEQ // tm

    def resident(shape):
        return pl.BlockSpec((None,) + shape[1:], lambda i: (l, 0, 0), pipeline_mode=pl.Buffered(1))

    return pl.pallas_call(
        _merge_kernel,
        grid=(TOKENS // tm,),
        in_specs=[
            pl.BlockSpec((tm, A_HEADS * A_HEAD_DIM), lambda i: (i, 0)),
            pl.BlockSpec((tm, M_HEADS * M_V), lambda i: (i, 0)),
            pl.BlockSpec((tm, D_MODEL), lambda i: (i, COL_GATE_A // D_MODEL)),
            pl.BlockSpec((tm, D_MODEL), lambda i: (i, COL_GATE_B // D_MODEL)),
            resident(wpa.shape), resident(wpb.shape), resident(wo.shape),
            pl.BlockSpec((tm, D_MODEL), lambda i: (i, 0)),
            _mod_spec(l, 2, per_b),
            _layer_row_spec(l, D_MODEL),
            _mod_spec(l, 4, per_b),
            _mod_spec(l, 3, per_b),
        ],
        out_specs=[pl.BlockSpec((tm, D_MODEL), lambda i: (i, 0))] * 2,
        out_shape=[jax.ShapeDtypeStruct((TOKENS, D_MODEL), F32),
                   jax.ShapeDtypeStruct((TOKENS, D_MODEL), BF16)],
        compiler_params=_params("arbitrary"),
        name="merge",
    )(oa, ob, proj, proj, wpa, wpb, wo, x, mod, g_ffn, mod, mod)


HALO = 8


def _ffn_up_kernel(h_ref, wg_ref, wv_ref, cg_ref, cv_ref, bg_ref, bv_ref,
                   o_ref, wgb_ref, wvb_ref, buf_ref, *, tiles_per_seq):
    i = pl.program_id(1)
    tm, tn = o_ref.shape

    @pl.when(i == 0)
    def _():
        wgb_ref[...] = wg_ref[...].astype(BF16)
        wvb_ref[...] = wv_ref[...].astype(BF16)

    seq_start = (i % tiles_per_seq) == 0

    @pl.when(seq_start)
    def _():
        buf_ref[0:HALO, :] = jnp.zeros((HALO, 2 * tn), F32)

    @pl.when(jnp.logical_not(seq_start))
    def _():
        buf_ref[0:HALO, :] = buf_ref[tm:tm + HALO, :]

    h = h_ref[...]
    buf_ref[HALO:HALO + tm, 0:tn] = _dot(h, wgb_ref[...])
    buf_ref[HALO:HALO + tm, tn:2 * tn] = _dot(h, wvb_ref[...])

    def conv(lo, c_ref, b_ref):
        u0 = buf_ref[HALO:HALO + tm, lo:lo + tn]
        u1 = buf_ref[HALO - 1:HALO - 1 + tm, lo:lo + tn]
        u2 = buf_ref[HALO - 2:HALO - 2 + tm, lo:lo + tn]
        return b_ref[...] + c_ref[0:1, :] * u2 + c_ref[1:2, :] * u1 + c_ref[2:3, :] * u0

    gate = conv(0, cg_ref, bg_ref)
    val = conv(tn, cv_ref, bv_ref)
    o_ref[...] = (gate * jax.nn.sigmoid(gate) * val).astype(BF16)


def _ffn_up(l, h, w_up, w_conv, b_conv):
    tm, tn = 1024, 512
    nj = D_FF // tn
    kern = functools.partial(_ffn_up_kernel, tiles_per_seq=SEQ // tm)
    return pl.pallas_call(
        kern,
        grid=(nj, TOKENS // tm),
        in_specs=[
            pl.BlockSpec((tm, D_MODEL), lambda j, i: (i, 0)),
            pl.BlockSpec((None, D_MODEL, tn), lambda j, i: (l, 0, j)),
            pl.BlockSpec((None, D_MODEL, tn), lambda j, i: (l, 0, nj + j)),
            pl.BlockSpec((None, CONV_W, tn), lambda j, i: (l, 0, j)),
            pl.BlockSpec((None, CONV_W, tn), lambda j, i: (l, 0, nj + j)),
            pl.BlockSpec((None, 1, tn), lambda j, i: (l, 0, j)),
            pl.BlockSpec((None, 1, tn), lambda j, i: (l, 0, nj + j)),
        ],
        out_specs=pl.BlockSpec((tm, tn), lambda j, i: (i, j)),
        out_shape=jax.ShapeDtypeStruct((TOKENS, D_FF), BF16),
        scratch_shapes=[
            pltpu.VMEM((D_MODEL, tn), BF16),
            pltpu.VMEM((D_MODEL, tn), BF16),
            pltpu.VMEM((HALO + tm, 2 * tn), F32),
        ],
        compiler_params=_params("arbitrary", "arbitrary"),
        name="ffn_up",
    )(h, w_up, w_up, w_conv, w_conv, b_conv, b_conv)


def _ffn_down_kernel(a_ref, w_ref, x_ref, gt_ref, o_ref):
    o_ref[...] = x_ref[...] + gt_ref[...] * _dot(a_ref[...], w_ref[...].astype(BF16))


def _ffn_down(l, act, w_down, x, mod):
    tm, tn = 1024, 256
    per_b = SEQ // tm
    return pl.pallas_call(
        _ffn_down_kernel,
        grid=(TOKENS // tm, D_MODEL // tn),
        in_specs=[
            pl.BlockSpec((tm, D_FF), lambda i, j: (i, 0)),
            pl.BlockSpec((None, D_FF, tn), lambda i, j: (l, 0, j)),
            pl.BlockSpec((tm, tn), lambda i, j: (i, j)),
            _mod_spec(l, 5, per_b, tn),
        ],
        out_specs=pl.BlockSpec((tm, tn), lambda i, j: (i, j)),
        out_shape=jax.ShapeDtypeStruct((TOKENS, D_MODEL), F32),
        compiler_params=_params("arbitrary", "arbitrary"),
        name="ffn_down",
    )(act, w_down, x, mod)


def _pack_w_in(w_in):
    o = np.cumsum((0,) + (1024, 256, 256, 1024, 64, 16, 512, 256, 64, 2048, 2048))
    w_t = jnp.swapaxes(w_in, 1, 2)
    qa, ka, va, qi, ki, wi, mql, mkvl, mkr, ga, gb = (
        w_t[:, int(o[k]):int(o[k + 1])].astype(BF16) for k in range(11))
    z = lambda n: jnp.zeros((DEPTH, n, D_MODEL), BF16)
    packed = jnp.concatenate(
        [ga, gb, qa, qi, mql, ka, va, mkvl, ki, wi, z(48), mkr, z(64)], axis=1)
    assert packed.shape[1] == IN_PACKED
    return packed


def _pack_w_mq(w_mq_up):
    w = w_mq_up.reshape(DEPTH, M_Q_LORA, M_HEADS, M_QK)
    nope = w[..., :M_NOPE].reshape(DEPTH, M_Q_LORA, M_HEADS * M_NOPE)
    rope = w[..., M_NOPE:].reshape(DEPTH, M_Q_LORA, M_HEADS * M_ROPE)
    return jnp.concatenate([nope, rope], axis=-1).astype(BF16)


def _rope_consts():
    def inv(rot):
        return ROPE_THETA ** (-jnp.arange(0, rot, 2, dtype=F32) / rot)

    layouts = ((A_HEAD_DIM // ROT_FRACTION, LANES), (IDX_DIM // ROT_FRACTION, IDX_DIM), (M_ROPE, M_ROPE))
    freqs = [inv(rot) for rot, _ in layouts]
    used = sum(f.shape[0] for f in freqs)
    inv_c = jnp.concatenate(freqs + [jnp.zeros((LANES - used,), F32)]).reshape(1, LANES)
    zero_lane = LANES - 1

    sgn3 = np.zeros((N_ROPE_TABLES, 1, LANES), np.float32)
    expand = np.zeros((N_ROPE_TABLES, LANES, LANES), np.float32)
    off = 0
    for t, (rot, group) in enumerate(layouts):
        half = rot // 2
        for j in range(LANES):
            g = j % group
            if g < rot:
                expand[t, off + g % half, j] = 1.0
                sgn3[t, 0, j] = -1.0 if g < half else 1.0
            else:
                expand[t, zero_lane, j] = 1.0
        off += half
    expand2 = np.concatenate([expand, expand], axis=1)
    return inv_c, jnp.asarray(sgn3), jnp.asarray(expand2, dtype=BF16)


def kernel(x, c, positions, g_attn, g_ffn, w_ada, b_ada, w_in, g_qa, g_ka, g_mq_lat, w_mq_up,
           g_mkv_lat, w_mkv_up, g_qm, g_km, w_pa, w_pb, w_o, w_up, w_conv, b_conv, w_down):
    pos_col = positions.astype(F32).reshape(TOKENS, 1)
    inv_c, sgn3, expand = _rope_consts()
    tabs_cos, tabs_sin = _rope_tables(pos_col, inv_c, sgn3, expand)
    rot_a, rot_i = A_HEAD_DIM // ROT_FRACTION, IDX_DIM // ROT_FRACTION
    swaps = jnp.asarray(np.stack([_swap_matrix(rot_a // 2, LANES), _swap_matrix(rot_i // 2, IDX_DIM),
                                  _swap_matrix(M_ROPE // 2, M_ROPE)]), dtype=BF16)

    c8 = jnp.pad(c, ((0, 8 - BATCH), (0, 0)))
    mod = _ada(c8, w_ada, b_ada)[:, :BATCH].reshape(DEPTH, BATCH, N_ADA, 1, D_MODEL)

    w_in_p = _pack_w_in(w_in)
    w_mq_p = _pack_w_mq(w_mq_up)
    w_mkv_b = w_mkv_up.astype(BF16)
    w_pa_b, w_pb_b, w_o_b = w_pa.astype(BF16), w_pb.astype(BF16), w_o.astype(BF16)

    rows = lambda v: v.reshape(DEPTH, 1, -1)
    rope_gain = lambda v: rows(jnp.tile(v[:, M_NOPE:], (1, 2)))
    g_attn_r, g_ffn_r, b_conv_r = rows(g_attn), rows(g_ffn), rows(b_conv)
    g_qa_r, g_ka_r, g_mq_r, g_mkv_r = rows(g_qa), rows(g_ka), rows(g_mq_lat), rows(g_mkv_lat)
    gqn, gqr = rows(g_qm[:, :M_NOPE]), rope_gain(g_qm)
    gkn, gkr = rows(g_km[:, :M_NOPE]), rope_gain(g_km)

    xf = x.reshape(TOKENS, D_MODEL)
    for l in range(DEPTH):
        proj = _in_proj(l, xf, g_attn_r, mod, w_in_p)
        qa, ka, va, qi, ki2, wi, mqn, mkvn = _prep(
            l, proj, tabs_cos, tabs_sin, swaps, g_qa_r, g_ka_r, g_mq_r, g_mkv_r)
        mq, mk, mv = _mla_up(l, mqn, mkvn, proj, w_mq_p, w_mkv_b, tabs_cos, tabs_sin, swaps,
                             gqn, gqr, gkn, gkr)
        o_a = _dsa(qa, ka, va, qi, ki2, wi)
        o_b = _mla(mq, mk, mv)
        xf, h_ffn = _merge(l, o_a, o_b, proj, w_pa_b, w_pb_b, w_o_b, xf, mod, g_ffn_r)
        act = _ffn_up(l, h_ffn, w_up, w_conv, b_conv_r)
        xf = _ffn_down(l, act, w_down, xf, mod)
    return xf.reshape(BATCH, SEQ, D_MODEL)
```

```python
import functools

import jax
import jax.numpy as jnp
import numpy as np
from jax import lax
from jax.experimental import pallas as pl
from jax.experimental.pallas import tpu as pltpu

D_MODEL = 2048
BATCH = 4
SEQ = 2048
DEPTH = 4
A_HEADS = 8
A_KV_HEADS = 2
A_HEAD_DIM = 128
IDX_HEADS = 16
IDX_DIM = 64
TOPK_MAX = 256
M_HEADS = 8
M_Q_LORA = 512
M_KV_LORA = 256
M_NOPE = 128
M_ROPE = 64
M_V = 128
D_FF = 5632
CONV_W = 3
ROPE_THETA = 500000.0
ROT_FRACTION = 4
EPS = 1e-6
N_ADA = 6
IDX_W_SCALE = (IDX_HEADS * IDX_DIM) ** -0.5
N_SEL = min(TOPK_MAX, SEQ // 4)
TOKENS = BATCH * SEQ
LANES = 128
M_QK = M_NOPE + M_ROPE
M_QK_PAD = 256

BF16 = jnp.bfloat16
F32 = jnp.float32

COL_GATE_A = 0
COL_GATE_B = 2048
COL_QA = 4096
COL_QI = 5120
COL_MQL = 6144
COL_KV = 6656
COL_MKVL = 7168
COL_KIWI = 7424
COL_MKR = 7552
IN_PACKED = 7680

VMEM_LIMIT = 56 * 1024 * 1024


def _params(*sem):
    return pltpu.CompilerParams(dimension_semantics=sem, vmem_limit_bytes=VMEM_LIMIT)


def _dot(a, b):
    return jnp.dot(a, b, preferred_element_type=F32)


def _dot_nt(a, b):
    return lax.dot_general(a, b, (((1,), (1,)), ((), ())), preferred_element_type=F32)


def _ada_kernel(c_ref, w_ref, b_ref, o_ref):
    c = c_ref[...]
    c_act = (c * jax.nn.sigmoid(c)).astype(BF16)
    o_ref[0] = _dot(c_act, w_ref[0].astype(BF16)) + b_ref[0]


def _ada(c8, w_ada, b_ada):
    tn = 1024
    n = N_ADA * D_MODEL
    return pl.pallas_call(
        _ada_kernel,
        grid=(DEPTH, n // tn),
        in_specs=[
            pl.BlockSpec((8, D_MODEL), lambda l, j: (0, 0)),
            pl.BlockSpec((1, D_MODEL, tn), lambda l, j: (l, 0, j)),
            pl.BlockSpec((1, 1, tn), lambda l, j: (l, 0, j)),
        ],
        out_specs=pl.BlockSpec((1, 8, tn), lambda l, j: (l, 0, j)),
        out_shape=jax.ShapeDtypeStruct((DEPTH, 8, n), F32),
        compiler_params=_params("arbitrary", "arbitrary"),
        name="ada",
    )(c8, w_ada, b_ada.reshape(DEPTH, 1, n))


N_ROPE_TABLES = 3


def _split_dot(t, m2):
    hi = t.astype(BF16)
    lo = (t - hi.astype(F32)).astype(BF16)
    return _dot(jnp.concatenate([hi, lo], axis=1), m2)


def _rope_table_kernel(pos_ref, inv_ref, sgn_ref, exp_ref, cos_ref, sin_ref):
    ang = pos_ref[...] * inv_ref[...]
    c, s = jnp.cos(ang), jnp.sin(ang)
    for t in range(N_ROPE_TABLES):
        cos_ref[t] = _split_dot(c, exp_ref[t])
        sin_ref[t] = _split_dot(s, exp_ref[t]) * sgn_ref[t]


def _rope_tables(pos_col, inv_c, sgn3, expand):
    tm = 512
    n = N_ROPE_TABLES
    return pl.pallas_call(
        _rope_table_kernel,
        grid=(TOKENS // tm,),
        in_specs=[
            pl.BlockSpec((tm, 1), lambda i: (i, 0)),
            pl.BlockSpec((1, LANES), lambda i: (0, 0)),
            pl.BlockSpec((n, 1, LANES), lambda i: (0, 0, 0)),
            pl.BlockSpec((n, 2 * LANES, LANES), lambda i: (0, 0, 0)),
        ],
        out_specs=[pl.BlockSpec((n, tm, LANES), lambda i: (0, i, 0))] * 2,
        out_shape=[jax.ShapeDtypeStruct((n, TOKENS, LANES), F32)] * 2,
        compiler_params=_params("arbitrary"),
        name="rope_tables",
    )(pos_col, inv_c, sgn3, expand)


def _rope(t, cos, sin_signed, swap):
    return t * cos + _split_dot(t, swap) * sin_signed


def _swap_matrix(half, group):
    p = np.zeros((LANES, LANES), np.float32)
    for j in range(LANES):
        g = j % group
        if g < half:
            p[j + half, j] = 1.0
        elif g < 2 * half:
            p[j - half, j] = 1.0
    return np.concatenate([p, p], axis=0)


def _rms(t, g, width):
    ms = jnp.sum(t * t, axis=-1, keepdims=True) * (1.0 / width)
    return t * lax.rsqrt(ms + EPS) * g


NORM_ROWS = 16


def _modnorm_into(h_ref, x_ref, g_ref, sc_ref, sh_ref):
    g, sc1, sh = g_ref[...], 1.0 + sc_ref[...], sh_ref[...]

    def chunk(c, carry):
        rows = pl.ds(pl.multiple_of(c * NORM_ROWS, NORM_ROWS), NORM_ROWS)
        x = x_ref[rows, :]
        ms = jnp.mean(x * x, axis=-1, keepdims=True)
        y = x * lax.rsqrt(ms + EPS) * g
        h_ref[rows, :] = (y * sc1 + sh).astype(BF16)
        return carry

    lax.fori_loop(0, x_ref.shape[0] // NORM_ROWS, chunk, 0, unroll=8)


def _in_proj_kernel(x_ref, g_ref, sc_ref, sh_ref, w_ref, o_ref, h_ref):
    @pl.when(pl.program_id(1) == 0)
    def _():
        _modnorm_into(h_ref, x_ref, g_ref, sc_ref, sh_ref)

    o_ref[...] = _dot_nt(h_ref[...], w_ref[...])


def _mod_spec(l, k, per_b, width=D_MODEL):
    if width == D_MODEL:
        return pl.BlockSpec((None, None, None, 1, D_MODEL), lambda i, *_: (l, i // per_b, k, 0, 0))
    return pl.BlockSpec((None, None, None, 1, width), lambda i, j: (l, i // per_b, k, 0, j))


def _layer_row_spec(l, width):
    return pl.BlockSpec((None, 1, width), lambda *_: (l, 0, 0))


def _in_proj(l, x, g, mod, w):
    tm, tn = 1024, 1280
    per_b = SEQ // tm
    n = w.shape[1]
    return pl.pallas_call(
        _in_proj_kernel,
        grid=(TOKENS // tm, n // tn),
        in_specs=[
            pl.BlockSpec((tm, D_MODEL), lambda i, j: (i, 0)),
            _layer_row_spec(l, D_MODEL),
            _mod_spec(l, 1, per_b),
            _mod_spec(l, 0, per_b),
            pl.BlockSpec((None, tn, D_MODEL), lambda i, j: (l, j, 0)),
        ],
        out_specs=pl.BlockSpec((tm, tn), lambda i, j: (i, j)),
        out_shape=jax.ShapeDtypeStruct((TOKENS, n), F32),
        scratch_shapes=[pltpu.VMEM((tm, D_MODEL), BF16)],
        compiler_params=_params("arbitrary", "arbitrary"),
        name="in_proj",
    )(x, g, mod, mod, w)


def _prep_kernel(qa_ref, qi_ref, mql_ref, kv_ref, mkvl_ref, kiwi_ref,
                 ca_ref, sa_ref, ci_ref, si_ref, pa_ref, pi_ref,
                 gqa_ref, gka_ref, gmq_ref, gmkv_ref,
                 qa_o, ka_o, va_o, qi_o, ki2_o, wi_o, mqn_o, mkvn_o):
    tm = qa_ref.shape[0]
    lane = lax.broadcasted_iota(jnp.int32, (tm, LANES), 1)
    ca, sa, pa = ca_ref[0], sa_ref[0], pa_ref[0]
    ci, si, pi = ci_ref[0], si_ref[0], pi_ref[0]
    scale_a = A_HEAD_DIM ** -0.5

    for h in range(A_HEADS):
        t = qa_ref[:, h * LANES:(h + 1) * LANES]
        r = _rope(_rms(t, gqa_ref[...], A_HEAD_DIM), ca, sa, pa)
        qa_o[h] = (r * scale_a).astype(BF16)
    for g in range(A_KV_HEADS):
        t = kv_ref[:, g * LANES:(g + 1) * LANES]
        r = _rope(_rms(t, gka_ref[...], A_HEAD_DIM), ca, sa, pa)
        ka_o[g] = r.astype(BF16)
        va_o[g] = kv_ref[:, (A_KV_HEADS + g) * LANES:(A_KV_HEADS + g + 1) * LANES].astype(BF16)
    for p in range(IDX_HEADS // 2):
        t = qi_ref[:, p * LANES:(p + 1) * LANES]
        r = _rope(t, ci, si, pi)
        qi_o[2 * p] = jnp.where(lane < IDX_DIM, r, 0.0).astype(BF16)
        qi_o[2 * p + 1] = jnp.where(lane < IDX_DIM, 0.0, r).astype(BF16)

    kiwi = kiwi_ref[...]
    ki = _rope(kiwi, ci, si, pi)
    ki2_o[...] = jnp.where(lane < IDX_DIM, ki, pltpu.roll(ki, IDX_DIM, 1)).astype(BF16)
    wi_o[...] = kiwi * IDX_W_SCALE

    mqn_o[...] = _rms(mql_ref[...], gmq_ref[...], M_Q_LORA).astype(BF16)
    mkvn_o[...] = _rms(mkvl_ref[...], gmkv_ref[...], M_KV_LORA).astype(BF16)


def _prep(l, proj, tabs_cos, tabs_sin, swaps, g_qa, g_ka, g_mq_lat, g_mkv_lat):
    tm = 256
    per_b = SEQ // tm

    def swap(t):
        return pl.BlockSpec((1, 2 * LANES, LANES), lambda i: (t, 0, 0))

    def col(width, start):
        idx = start // width
        return pl.BlockSpec((tm, width), lambda i: (i, idx))

    def tab(t):
        return pl.BlockSpec((1, tm, LANES), lambda i: (t, i, 0))

    def vec(width):
        return _layer_row_spec(l, width)

    def heads(n):
        return pl.BlockSpec((None, n, tm, LANES), lambda i: (i // per_b, 0, i % per_b, 0))

    def seq(width):
        return pl.BlockSpec((None, tm, width), lambda i: (i // per_b, i % per_b, 0))

    def tok(width):
        return pl.BlockSpec((tm, width), lambda i: (i, 0))

    return pl.pallas_call(
        _prep_kernel,
        grid=(TOKENS // tm,),
        in_specs=[
            col(1024, COL_QA), col(1024, COL_QI), col(512, COL_MQL), col(512, COL_KV),
            col(256, COL_MKVL), col(128, COL_KIWI),
            tab(0), tab(0), tab(1), tab(1), swap(0), swap(1),
            vec(A_HEAD_DIM), vec(A_HEAD_DIM), vec(M_Q_LORA), vec(M_KV_LORA),
        ],
        out_specs=[
            heads(A_HEADS), heads(A_KV_HEADS), heads(A_KV_HEADS),
            heads(IDX_HEADS), seq(LANES), seq(LANES),
            tok(M_Q_LORA), tok(M_KV_LORA),
        ],
        out_shape=[
            jax.ShapeDtypeStruct((BATCH, A_HEADS, SEQ, LANES), BF16),
            jax.ShapeDtypeStruct((BATCH, A_KV_HEADS, SEQ, LANES), BF16),
            jax.ShapeDtypeStruct((BATCH, A_KV_HEADS, SEQ, LANES), BF16),
            jax.ShapeDtypeStruct((BATCH, IDX_HEADS, SEQ, LANES), BF16),
            jax.ShapeDtypeStruct((BATCH, SEQ, LANES), BF16),
            jax.ShapeDtypeStruct((BATCH, SEQ, LANES), F32),
            jax.ShapeDtypeStruct((TOKENS, M_Q_LORA), BF16),
            jax.ShapeDtypeStruct((TOKENS, M_KV_LORA), BF16),
        ],
        compiler_params=_params("arbitrary"),
        name="prep",
    )(proj, proj, proj, proj, proj, proj, tabs_cos, tabs_sin, tabs_cos, tabs_sin, swaps, swaps,
      g_qa, g_ka, g_mq_lat, g_mkv_lat)


def _mla_up_kernel(mqn_ref, mkvn_ref, mkr_ref, wq_ref, wkv_ref, cm_ref, sm_ref, pm_ref,
                   gqn_ref, gqr_ref, gkn_ref, gkr_ref, q_o, k_o, v_o):
    tm = mqn_ref.shape[0]
    lane = lax.broadcasted_iota(jnp.int32, (tm, LANES), 1)
    low = lane < M_ROPE
    cm, sm, pm = cm_ref[0], sm_ref[0], pm_ref[0]
    scale_m = M_QK ** -0.5
    nope_w = M_HEADS * M_NOPE

    q = _dot(mqn_ref[...], wq_ref[...])
    kv = _dot(mkvn_ref[...], wkv_ref[...])

    for p in range(M_HEADS // 2):
        rp = q[:, nope_w + p * LANES: nope_w + (p + 1) * LANES]
        sq = rp * rp
        s_all = jnp.sum(sq, axis=-1, keepdims=True)
        s_lo = jnp.sum(jnp.where(low, sq, 0.0), axis=-1, keepdims=True)
        for e in range(2):
            h = 2 * p + e
            nope = q[:, h * M_NOPE:(h + 1) * M_NOPE]
            ss = jnp.sum(nope * nope, axis=-1, keepdims=True) + (s_lo if e == 0 else s_all - s_lo)
            rs = lax.rsqrt(ss * (1.0 / M_QK) + EPS)
            roped = _rope(rp * rs * gqr_ref[...], cm, sm, pm)
            if e == 1:
                roped = pltpu.roll(roped, M_ROPE, 1)
            q_o[h, :, 0:LANES] = (nope * rs * gqn_ref[...] * scale_m).astype(BF16)
            q_o[h, :, LANES:2 * LANES] = (jnp.where(low, roped, 0.0) * scale_m).astype(BF16)

    kr = mkr_ref[...]
    kr_ss = jnp.sum(kr * kr, axis=-1, keepdims=True)
    kr_roped = _rope(kr * gkr_ref[...], cm, sm, pm)
    kr_roped = jnp.where(low, kr_roped, 0.0)
    for h in range(M_HEADS):
        nope = kv[:, h * 2 * LANES: h * 2 * LANES + M_NOPE]
        ss = jnp.sum(nope * nope, axis=-1, keepdims=True) + kr_ss
        rs = lax.rsqrt(ss * (1.0 / M_QK) + EPS)
        k_o[h, :, 0:LANES] = (nope * rs * gkn_ref[...]).astype(BF16)
        k_o[h, :, LANES:2 * LANES] = (kr_roped * rs).astype(BF16)
        v_o[h] = kv[:, h * 2 * LANES + M_NOPE:(h + 1) * 2 * LANES].astype(BF16)


def _mla_up(l, mqn, mkvn, proj, wq, wkv, tabs_cos, tabs_sin, swaps, gqn, gqr, gkn, gkr):
    tm = 256
    per_b = SEQ // tm

    def vec():
        return _layer_row_spec(l, LANES)

    def heads(width):
        return pl.BlockSpec((None, M_HEADS, tm, width), lambda i: (i // per_b, 0, i % per_b, 0))

    return pl.pallas_call(
        _mla_up_kernel,
        grid=(TOKENS // tm,),
        in_specs=[
            pl.BlockSpec((tm, M_Q_LORA), lambda i: (i, 0)),
            pl.BlockSpec((tm, M_KV_LORA), lambda i: (i, 0)),
            pl.BlockSpec((tm, LANES), lambda i: (i, COL_MKR // LANES)),
            pl.BlockSpec((None,) + wq.shape[1:], lambda i: (l, 0, 0)),
            pl.BlockSpec((None,) + wkv.shape[1:], lambda i: (l, 0, 0)),
            pl.BlockSpec((1, tm, LANES), lambda i: (2, i, 0)),
            pl.BlockSpec((1, tm, LANES), lambda i: (2, i, 0)),
            pl.BlockSpec((1, 2 * LANES, LANES), lambda i: (2, 0, 0)),
            vec(), vec(), vec(), vec(),
        ],
        out_specs=[heads(M_QK_PAD), heads(M_QK_PAD), heads(M_V)],
        out_shape=[
            jax.ShapeDtypeStruct((BATCH, M_HEADS, SEQ, M_QK_PAD), BF16),
            jax.ShapeDtypeStruct((BATCH, M_HEADS, SEQ, M_QK_PAD), BF16),
            jax.ShapeDtypeStruct((BATCH, M_HEADS, SEQ, M_V), BF16),
        ],
        compiler_params=_params("arbitrary"),
        name="mla_up",
    )(mqn, mkvn, proj, wq, wkv, tabs_cos, tabs_sin, swaps, gqn, gqr, gkn, gkr)


INT_MIN = -(2 ** 31)
KEY_MASKED = INT_MIN

DSA_TQ = 256
KEY_CHUNK = 512
SEARCH_ROWS = 128
SEARCH_UNROLL = 5
IDX_CHUNK = 256
IDX_ROWS = 64
ATT_SPLIT = 4


def _dsa_index(i, n_chunks, qi_ref, ki2_ref, wi_ref, key_ref, d_refs, wb_ref):
    tq = qi_ref.shape[1]
    w = wi_ref[...]
    for h in range(IDX_HEADS):
        wb_ref[h] = jnp.broadcast_to(w[:, IDX_DIM + h:IDX_DIM + h + 1], (tq, LANES))

    def matmul(c, d_ref):
        start = pl.multiple_of(c * IDX_CHUNK, IDX_CHUNK)
        q_all = qi_ref[...].reshape(IDX_HEADS * tq, LANES)
        d_ref[...] = _dot_nt(q_all, ki2_ref[pl.ds(start, IDX_CHUNK), :])

    def head_sum(c, d_ref):
        for r0 in range(0, tq, IDX_ROWS):
            row_t = i * tq + r0 + lax.broadcasted_iota(jnp.int32, (IDX_ROWS, LANES), 0)
            for j0 in range(0, IDX_CHUNK, LANES):
                acc = jnp.zeros((IDX_ROWS, LANES), F32)
                for h in range(IDX_HEADS):
                    d = d_ref[h * tq + r0:h * tq + r0 + IDX_ROWS, j0:j0 + LANES]
                    acc = acc + jnp.maximum(d, 0.0) * wb_ref[h, r0:r0 + IDX_ROWS, :]
                bits = pltpu.bitcast(acc, jnp.int32)
                img = bits ^ ((bits >> 31) & jnp.int32(0x7FFFFFFF))
                col_t = c * IDX_CHUNK + j0 + lax.broadcasted_iota(jnp.int32, (IDX_ROWS, LANES), 1)
                key_ref[c, r0:r0 + IDX_ROWS, j0:j0 + LANES] = jnp.where(col_t <= row_t, img, KEY_MASKED)

    d0_ref, d1_ref = d_refs
    matmul(0, d0_ref)

    def pair(k, carry):
        c = 2 * k
        matmul(c + 1, d1_ref)
        head_sum(c, d0_ref)
        matmul(jnp.minimum(c + 2, n_chunks - 1), d0_ref)
        head_sum(c + 1, d1_ref)
        return carry

    lax.fori_loop(0, n_chunks // 2, pair, 0)


def _dsa_select(nk, nk_bias, key_ref, bias_ref):
    chunks = range(nk // IDX_CHUNK)
    n_sel = float(N_SEL)
    groups = list(range(0, key_ref.shape[1], SEARCH_ROWS))

    def count_ge(r0, cand):
        hits = None
        for c in chunks:
            hit = jnp.where(key_ref[c, r0:r0 + SEARCH_ROWS, :] >= cand, 1.0, 0.0)
            hits = hit if hits is None else hits + hit
        return jnp.sum(hits, axis=-1, keepdims=True)

    zero = jnp.zeros((SEARCH_ROWS, 1), jnp.int32)
    thr0 = tuple(jnp.where(count_ge(r0, zero) >= n_sel, jnp.int32(0), jnp.int32(INT_MIN))
                 for r0 in groups)

    def step(b, thrs):
        bit = lax.shift_left(jnp.int32(1), jnp.int32(30) - b)
        out = []
        for r0, thr in zip(groups, thrs):
            cand = thr | bit
            out.append(jnp.where(count_ge(r0, cand) >= n_sel, cand, thr))
        return tuple(out)

    thrs = lax.fori_loop(0, 30, step, thr0, unroll=SEARCH_UNROLL)
    thrs = step(jnp.int32(30), thrs)
    for r0, thr in zip(groups, thrs):
        floor = jnp.maximum(thr, jnp.int32(KEY_MASKED + 1))
        for c in range(nk_bias // IDX_CHUNK):
            sel = key_ref[c, r0:r0 + SEARCH_ROWS, :] >= floor
            bias_ref[r0:r0 + SEARCH_ROWS, c * IDX_CHUNK:(c + 1) * IDX_CHUNK] = jnp.where(sel, 0.0, -jnp.inf)


def _dsa_attend(nk, qa_ref, ka_ref, va_ref, bias_ref, o_ref, oh_ref, lg_refs, p_refs):
    tq = qa_ref.shape[1]
    rep = A_HEADS // A_KV_HEADS

    sub = rep // ATT_SPLIT

    def group(g, carry):
        k = ka_ref[g, 0:nk, :]
        v = va_ref[g, 0:nk, :]
        inv_l = [None] * ATT_SPLIT

        def scores(s):
            q = qa_ref[pl.ds(g * rep + s * sub, sub)].reshape(sub * tq, LANES)
            lg_refs[s][:, 0:nk] = _dot_nt(q, k)

        def softmax(s):
            logits = lg_refs[s][:, 0:nk].reshape(sub, tq, nk) + bias_ref[:, 0:nk][None]
            m = jnp.max(logits, axis=-1, keepdims=True)
            p = jnp.exp(logits - m)
            inv_l[s] = 1.0 / jnp.sum(p, axis=-1, keepdims=True)
            p_refs[s][:, 0:nk] = p.astype(BF16).reshape(sub * tq, nk)

        def values(s):
            o = _dot(p_refs[s][:, 0:nk], v).reshape(sub, tq, LANES) * inv_l[s]
            oh_ref[pl.ds(g * rep + s * sub, sub)] = o.astype(BF16)

        for t in range(ATT_SPLIT + 2):
            if t < ATT_SPLIT:
                scores(t)
            if 0 <= t - 1 < ATT_SPLIT:
                softmax(t - 1)
            if 0 <= t - 2 < ATT_SPLIT:
                values(t - 2)
        return carry

    lax.fori_loop(0, A_KV_HEADS, group, 0)
    for h in range(A_HEADS):
        o_ref[:, h * LANES:(h + 1) * LANES] = oh_ref[h]


def _dsa_kernel(qa_ref, ka_ref, va_ref, qi_ref, ki2_ref, wi_ref, o_ref,
                key_ref, bias_ref, d0_ref, d1_ref, wb_ref, oh_ref, *stage_refs):
    lg_refs, p_refs = stage_refs[:ATT_SPLIT], stage_refs[ATT_SPLIT:]
    i = pl.program_id(1)
    tq = DSA_TQ
    n_free = N_SEL // tq
    variant = i // (KEY_CHUNK // tq)

    for v in range(n_free):
        @pl.when(i == v)
        def _(v=v):
            nk = (v + 1) * tq
            row = i * tq + lax.broadcasted_iota(jnp.int32, (tq, nk), 0)
            col = lax.broadcasted_iota(jnp.int32, (tq, nk), 1)
            bias_ref[:, 0:nk] = jnp.where(col <= row, 0.0, -jnp.inf)
            _dsa_attend(nk, qa_ref, ka_ref, va_ref, bias_ref, o_ref, oh_ref, lg_refs, p_refs)

    @pl.when(i >= n_free)
    def _():
        n_chunks = (variant + 1) * (KEY_CHUNK // IDX_CHUNK)
        _dsa_index(i, n_chunks, qi_ref, ki2_ref, wi_ref, key_ref, (d0_ref, d1_ref), wb_ref)

    for v in range(SEQ // KEY_CHUNK):
        @pl.when((i >= n_free) & (variant == v))
        def _(v=v):
            nk = (v + 1) * KEY_CHUNK
            _dsa_select(nk, nk, key_ref, bias_ref)
            _dsa_attend(nk, qa_ref, ka_ref, va_ref, bias_ref, o_ref, oh_ref, lg_refs, p_refs)


def _dsa(qa, ka, va, qi, ki2, wi):
    tq = DSA_TQ
    nq = SEQ // tq
    att_rows = A_HEADS // A_KV_HEADS // ATT_SPLIT * tq
    return pl.pallas_call(
        _dsa_kernel,
        grid=(BATCH, nq),
        in_specs=[
            pl.BlockSpec((None, A_HEADS, tq, LANES), lambda b, i: (b, 0, i, 0)),
            pl.BlockSpec((None, A_KV_HEADS, SEQ, LANES), lambda b, i: (b, 0, 0, 0)),
            pl.BlockSpec((None, A_KV_HEADS, SEQ, LANES), lambda b, i: (b, 0, 0, 0)),
            pl.BlockSpec((None, IDX_HEADS, tq, LANES), lambda b, i: (b, 0, i, 0)),
            pl.BlockSpec((None, SEQ, LANES), lambda b, i: (b, 0, 0)),
            pl.BlockSpec((None, tq, LANES), lambda b, i: (b, i, 0)),
        ],
        out_specs=pl.BlockSpec((tq, A_HEADS * A_HEAD_DIM), lambda b, i: (b * nq + i, 0)),
        out_shape=jax.ShapeDtypeStruct((TOKENS, A_HEADS * A_HEAD_DIM), BF16),
        scratch_shapes=[
            pltpu.VMEM((SEQ // IDX_CHUNK, tq, IDX_CHUNK), jnp.int32),
            pltpu.VMEM((tq, SEQ), F32),
            pltpu.VMEM((IDX_HEADS * tq, IDX_CHUNK), F32),
            pltpu.VMEM((IDX_HEADS * tq, IDX_CHUNK), F32),
            pltpu.VMEM((IDX_HEADS, tq, LANES), F32),
            pltpu.VMEM((A_HEADS, tq, LANES), BF16),
        ] + [pltpu.VMEM((att_rows, SEQ), F32)] * ATT_SPLIT + [pltpu.VMEM((att_rows, SEQ), BF16)] * ATT_SPLIT,
        compiler_params=_params("arbitrary", "arbitrary"),
        name="dsa",
    )(qa, ka, va, qi, ki2, wi)


MLA_TQ = 256
MLA_HEADS_PER_STEP = 4
MLA_KEY_CHUNK = 256


def _mla_body(nk, i, q_ref, k_ref, v_ref, o_ref, lg_refs, p_refs):
    n_heads, tq = q_ref.shape[0], q_ref.shape[1]
    n0 = nk - MLA_KEY_CHUNK
    row = i * tq + lax.broadcasted_iota(jnp.int32, (tq, MLA_KEY_CHUNK), 0)
    col = n0 + lax.broadcasted_iota(jnp.int32, (tq, MLA_KEY_CHUNK), 1)
    causal = col <= row
    inv_l = [None] * n_heads

    def scores(h):
        lg_refs[h][:, 0:nk] = _dot_nt(q_ref[h], k_ref[h, 0:nk, :])

    def softmax(h):
        s1 = jnp.where(causal, lg_refs[h][:, n0:nk], -jnp.inf)
        m = jnp.max(s1, axis=-1, keepdims=True)
        if n0:
            s0 = lg_refs[h][:, 0:n0]
            m = jnp.maximum(m, jnp.max(s0, axis=-1, keepdims=True))
        p1 = jnp.exp(s1 - m)
        l = jnp.sum(p1, axis=-1, keepdims=True)
        p_refs[h][:, n0:nk] = p1.astype(BF16)
        if n0:
            p0 = jnp.exp(s0 - m)
            l = l + jnp.sum(p0, axis=-1, keepdims=True)
            p_refs[h][:, 0:n0] = p0.astype(BF16)
        inv_l[h] = 1.0 / l

    def values(h):
        o = _dot(p_refs[h][:, 0:nk], v_ref[h, 0:nk, :]) * inv_l[h]
        o_ref[:, h * M_V:(h + 1) * M_V] = o.astype(BF16)

    for t in range(n_heads + 2):
        if t < n_heads:
            scores(t)
        if 0 <= t - 1 < n_heads:
            softmax(t - 1)
        if 0 <= t - 2 < n_heads:
            values(t - 2)


def _mla_kernel(q_ref, k_ref, v_ref, o_ref, *stage_refs):
    n_heads = q_ref.shape[0]
    lg_refs, p_refs = stage_refs[:n_heads], stage_refs[n_heads:]
    i = pl.program_id(2)
    per_chunk = MLA_KEY_CHUNK // MLA_TQ
    for v in range(SEQ // MLA_KEY_CHUNK):
        @pl.when(i // per_chunk == v)
        def _(v=v):
            _mla_body((v + 1) * MLA_KEY_CHUNK, i, q_ref, k_ref, v_ref, o_ref, lg_refs, p_refs)


def _mla(q, k, v):
    tq, hg = MLA_TQ, MLA_HEADS_PER_STEP
    nq = SEQ // tq
    return pl.pallas_call(
        _mla_kernel,
        grid=(BATCH, M_HEADS // hg, nq),
        in_specs=[
            pl.BlockSpec((None, hg, tq, M_QK_PAD), lambda b, g, i: (b, g, i, 0)),
            pl.BlockSpec((None, hg, SEQ, M_QK_PAD), lambda b, g, i: (b, g, 0, 0)),
            pl.BlockSpec((None, hg, SEQ, M_V), lambda b, g, i: (b, g, 0, 0)),
        ],
        out_specs=pl.BlockSpec((tq, hg * M_V), lambda b, g, i: (b * nq + i, g)),
        out_shape=jax.ShapeDtypeStruct((TOKENS, M_HEADS * M_V), BF16),
        scratch_shapes=[pltpu.VMEM((tq, SEQ), F32)] * hg + [pltpu.VMEM((tq, SEQ), BF16)] * hg,
        compiler_params=_params("arbitrary", "arbitrary", "arbitrary"),
        name="mla",
    )(q, k, v)


def _merge_kernel(oa_ref, ob_ref, ga_ref, gb_ref, wpa_ref, wpb_ref, wo_ref, x_ref, gt_ref,
                  gf_ref, scf_ref, shf_ref, o_ref, hf_ref):
    a = _dot(oa_ref[...], wpa_ref[...])
    b = _dot(ob_ref[...], wpb_ref[...])
    merged = jax.nn.sigmoid(ga_ref[...]) * a + jax.nn.sigmoid(gb_ref[...]) * b
    y = _dot(merged.astype(BF16), wo_ref[...])
    xn = x_ref[...] + gt_ref[...] * y
    o_ref[...] = xn
    ms = jnp.mean(xn * xn, axis=-1, keepdims=True)
    hn = xn * lax.rsqrt(ms + EPS) * gf_ref[...]
    hf_ref[...] = (hn * (1.0 + scf_ref[...]) + shf_ref[...]).astype(BF16)


def _merge(l, oa, ob, proj, wpa, wpb, wo, x, mod, g_ffn):
    tm = 256
    per_b = SEQ // tm

    def resident(shape):
        return pl.BlockSpec((None,) + shape[1:], lambda i: (l, 0, 0), pipeline_mode=pl.Buffered(1))

    return pl.pallas_call(
        _merge_kernel,
        grid=(TOKENS // tm,),
        in_specs=[
            pl.BlockSpec((tm, A_HEADS * A_HEAD_DIM), lambda i: (i, 0)),
            pl.BlockSpec((tm, M_HEADS * M_V), lambda i: (i, 0)),
            pl.BlockSpec((tm, D_MODEL), lambda i: (i, COL_GATE_A // D_MODEL)),
            pl.BlockSpec((tm, D_MODEL), lambda i: (i, COL_GATE_B // D_MODEL)),
            resident(wpa.shape), resident(wpb.shape), resident(wo.shape),
            pl.BlockSpec((tm, D_MODEL), lambda i: (i, 0)),
            _mod_spec(l, 2, per_b),
            _layer_row_spec(l, D_MODEL),
            _mod_spec(l, 4, per_b),
            _mod_spec(l, 3, per_b),
        ],
        out_specs=[pl.BlockSpec((tm, D_MODEL), lambda i: (i, 0))] * 2,
        out_shape=[jax.ShapeDtypeStruct((TOKENS, D_MODEL), F32),
                   jax.ShapeDtypeStruct((TOKENS, D_MODEL), BF16)],
        compiler_params=_params("arbitrary"),
        name="merge",
    )(oa, ob, proj, proj, wpa, wpb, wo, x, mod, g_ffn, mod, mod)


HALO = 8


def _ffn_up_kernel(h_ref, wg_ref, wv_ref, cg_ref, cv_ref, bg_ref, bv_ref,
                   o_ref, wgb_ref, wvb_ref, buf_ref, *, tiles_per_seq):
    i = pl.program_id(1)
    tm, tn = o_ref.shape

    @pl.when(i == 0)
    def _():
        wgb_ref[...] = wg_ref[...].astype(BF16)
        wvb_ref[...] = wv_ref[...].astype(BF16)

    seq_start = (i % tiles_per_seq) == 0

    @pl.when(seq_start)
    def _():
        buf_ref[0:HALO, :] = jnp.zeros((HALO, 2 * tn), F32)

    @pl.when(jnp.logical_not(seq_start))
    def _():
        buf_ref[0:HALO, :] = buf_ref[tm:tm + HALO, :]

    h = h_ref[...]
    buf_ref[HALO:HALO + tm, 0:tn] = _dot(h, wgb_ref[...])
    buf_ref[HALO:HALO + tm, tn:2 * tn] = _dot(h, wvb_ref[...])

    def conv(lo, c_ref, b_ref):
        u0 = buf_ref[HALO:HALO + tm, lo:lo + tn]
        u1 = buf_ref[HALO - 1:HALO - 1 + tm, lo:lo + tn]
        u2 = buf_ref[HALO - 2:HALO - 2 + tm, lo:lo + tn]
        return b_ref[...] + c_ref[0:1, :] * u2 + c_ref[1:2, :] * u1 + c_ref[2:3, :] * u0

    gate = conv(0, cg_ref, bg_ref)
    val = conv(tn, cv_ref, bv_ref)
    o_ref[...] = (gate * jax.nn.sigmoid(gate) * val).astype(BF16)


def _ffn_up(l, h, w_up, w_conv, b_conv):
    tm, tn = 1024, 512
    nj = D_FF // tn
    kern = functools.partial(_ffn_up_kernel, tiles_per_seq=SEQ // tm)
    return pl.pallas_call(
        kern,
        grid=(nj, TOKENS // tm),
        in_specs=[
            pl.BlockSpec((tm, D_MODEL), lambda j, i: (i, 0)),
            pl.BlockSpec((None, D_MODEL, tn), lambda j, i: (l, 0, j)),
            pl.BlockSpec((None, D_MODEL, tn), lambda j, i: (l, 0, nj + j)),
            pl.BlockSpec((None, CONV_W, tn), lambda j, i: (l, 0, j)),
            pl.BlockSpec((None, CONV_W, tn), lambda j, i: (l, 0, nj + j)),
            pl.BlockSpec((None, 1, tn), lambda j, i: (l, 0, j)),
            pl.BlockSpec((None, 1, tn), lambda j, i: (l, 0, nj + j)),
        ],
        out_specs=pl.BlockSpec((tm, tn), lambda j, i: (i, j)),
        out_shape=jax.ShapeDtypeStruct((TOKENS, D_FF), BF16),
        scratch_shapes=[
            pltpu.VMEM((D_MODEL, tn), BF16),
            pltpu.VMEM((D_MODEL, tn), BF16),
            pltpu.VMEM((HALO + tm, 2 * tn), F32),
        ],
        compiler_params=_params("arbitrary", "arbitrary"),
        name="ffn_up",
    )(h, w_up, w_up, w_conv, w_conv, b_conv, b_conv)


def _ffn_down_kernel(a_ref, w_ref, x_ref, gt_ref, o_ref):
    o_ref[...] = x_ref[...] + gt_ref[...] * _dot(a_ref[...], w_ref[...])


def _ffn_down(l, act, w_down, x, mod):
    tm, tn = 1024, 512
    per_b = SEQ // tm
    return pl.pallas_call(
        _ffn_down_kernel,
        grid=(TOKENS // tm, D_MODEL // tn),
        in_specs=[
            pl.BlockSpec((tm, D_FF), lambda i, j: (i, 0)),
            pl.BlockSpec((None, D_FF, tn), lambda i, j: (l, 0, j)),
            pl.BlockSpec((tm, tn), lambda i, j: (i, j)),
            _mod_spec(l, 5, per_b, tn),
        ],
        out_specs=pl.BlockSpec((tm, tn), lambda i, j: (i, j)),
        out_shape=jax.ShapeDtypeStruct((TOKENS, D_MODEL), F32),
        compiler_params=_params("arbitrary", "arbitrary"),
        name="ffn_down",
    )(act, w_down, x, mod)


def _pack_w_in(w_in):
    o = np.cumsum((0,) + (1024, 256, 256, 1024, 64, 16, 512, 256, 64, 2048, 2048))
    w_t = jnp.swapaxes(w_in, 1, 2)
    qa, ka, va, qi, ki, wi, mql, mkvl, mkr, ga, gb = (
        w_t[:, int(o[k]):int(o[k + 1])].astype(BF16) for k in range(11))
    z = lambda n: jnp.zeros((DEPTH, n, D_MODEL), BF16)
    packed = jnp.concatenate(
        [ga, gb, qa, qi, mql, ka, va, mkvl, ki, wi, z(48), mkr, z(64)], axis=1)
    assert packed.shape[1] == IN_PACKED
    return packed


def _pack_w_mq(w_mq_up):
    w = w_mq_up.reshape(DEPTH, M_Q_LORA, M_HEADS, M_QK)
    nope = w[..., :M_NOPE].reshape(DEPTH, M_Q_LORA, M_HEADS * M_NOPE)
    rope = w[..., M_NOPE:].reshape(DEPTH, M_Q_LORA, M_HEADS * M_ROPE)
    return jnp.concatenate([nope, rope], axis=-1).astype(BF16)


def _rope_consts():
    def inv(rot):
        return ROPE_THETA ** (-jnp.arange(0, rot, 2, dtype=F32) / rot)

    layouts = ((A_HEAD_DIM // ROT_FRACTION, LANES), (IDX_DIM // ROT_FRACTION, IDX_DIM), (M_ROPE, M_ROPE))
    freqs = [inv(rot) for rot, _ in layouts]
    used = sum(f.shape[0] for f in freqs)
    inv_c = jnp.concatenate(freqs + [jnp.zeros((LANES - used,), F32)]).reshape(1, LANES)
    zero_lane = LANES - 1

    sgn3 = np.zeros((N_ROPE_TABLES, 1, LANES), np.float32)
    expand = np.zeros((N_ROPE_TABLES, LANES, LANES), np.float32)
    off = 0
    for t, (rot, group) in enumerate(layouts):
        half = rot // 2
        for j in range(LANES):
            g = j % group
            if g < rot:
                expand[t, off + g % half, j] = 1.0
                sgn3[t, 0, j] = -1.0 if g < half else 1.0
            else:
                expand[t, zero_lane, j] = 1.0
        off += half
    expand2 = np.concatenate([expand, expand], axis=1)
    return inv_c, jnp.asarray(sgn3), jnp.asarray(expand2, dtype=BF16)


def kernel(x, c, positions, g_attn, g_ffn, w_ada, b_ada, w_in, g_qa, g_ka, g_mq_lat, w_mq_up,
           g_mkv_lat, w_mkv_up, g_qm, g_km, w_pa, w_pb, w_o, w_up, w_conv, b_conv, w_down):
    pos_col = positions.astype(F32).reshape(TOKENS, 1)
    inv_c, sgn3, expand = _rope_consts()
    tabs_cos, tabs_sin = _rope_tables(pos_col, inv_c, sgn3, expand)
    rot_a, rot_i = A_HEAD_DIM // ROT_FRACTION, IDX_DIM // ROT_FRACTION
    swaps = jnp.asarray(np.stack([_swap_matrix(rot_a // 2, LANES), _swap_matrix(rot_i // 2, IDX_DIM),
                                  _swap_matrix(M_ROPE // 2, M_ROPE)]), dtype=BF16)

    c8 = jnp.pad(c, ((0, 8 - BATCH), (0, 0)))
    mod = _ada(c8, w_ada, b_ada)[:, :BATCH].reshape(DEPTH, BATCH, N_ADA, 1, D_MODEL)

    w_in_p = _pack_w_in(w_in)
    w_mq_p = _pack_w_mq(w_mq_up)
    w_mkv_b = w_mkv_up.astype(BF16)
    w_pa_b, w_pb_b, w_o_b = w_pa.astype(BF16), w_pb.astype(BF16), w_o.astype(BF16)
    w_down_b = w_down.astype(BF16)

    rows = lambda v: v.reshape(DEPTH, 1, -1)
    rope_gain = lambda v: rows(jnp.tile(v[:, M_NOPE:], (1, 2)))
    g_attn_r, g_ffn_r, b_conv_r = rows(g_attn), rows(g_ffn), rows(b_conv)
    g_qa_r, g_ka_r, g_mq_r, g_mkv_r = rows(g_qa), rows(g_ka), rows(g_mq_lat), rows(g_mkv_lat)
    gqn, gqr = rows(g_qm[:, :M_NOPE]), rope_gain(g_qm)
    gkn, gkr = rows(g_km[:, :M_NOPE]), rope_gain(g_km)

    xf = x.reshape(TOKENS, D_MODEL)
    for l in range(DEPTH):
        proj = _in_proj(l, xf, g_attn_r, mod, w_in_p)
        qa, ka, va, qi, ki2, wi, mqn, mkvn = _prep(
            l, proj, tabs_cos, tabs_sin, swaps, g_qa_r, g_ka_r, g_mq_r, g_mkv_r)
        mq, mk, mv = _mla_up(l, mqn, mkvn, proj, w_mq_p, w_mkv_b, tabs_cos, tabs_sin, swaps,
                             gqn, gqr, gkn, gkr)
        o_a = _dsa(qa, ka, va, qi, ki2, wi)
        o_b = _mla(mq, mk, mv)
        xf, h_ffn = _merge(l, o_a, o_b, proj, w_pa_b, w_pb_b, w_o_b, xf, mod, g_ffn_r)
        act = _ffn_up(l, h_ffn, w_up, w_conv, b_conv_r)
        xf = _ffn_down(l, act, w_down_b, xf, mod)
    return xf.reshape(BATCH, SEQ, D_MODEL)
```

```python
import functools

import jax
import jax.numpy as jnp
import numpy as np
from jax import lax
from jax.experimental import pallas as pl
from jax.experimental.pallas import tpu as pltpu

D_MODEL = 2048
BATCH = 4
SEQ = 2048
DEPTH = 4
A_HEADS = 8
A_KV_HEADS = 2
A_HEAD_DIM = 128
IDX_HEADS = 16
IDX_DIM = 64
TOPK_MAX = 256
M_HEADS = 8
M_Q_LORA = 512
M_KV_LORA = 256
M_NOPE = 128
M_ROPE = 64
M_V = 128
D_FF = 5632
CONV_W = 3
ROPE_THETA = 500000.0
ROT_FRACTION = 4
EPS = 1e-6
N_ADA = 6
IDX_W_SCALE = (IDX_HEADS * IDX_DIM) ** -0.5
N_SEL = min(TOPK_MAX, SEQ // 4)
TOKENS = BATCH * SEQ
LANES = 128
M_QK = M_NOPE + M_ROPE
M_QK_PAD = 256

BF16 = jnp.bfloat16
F32 = jnp.float32

COL_GATE_A = 0
COL_GATE_B = 2048
COL_QA = 4096
COL_QI = 5120
COL_MQL = 6144
COL_KV = 6656
COL_MKVL = 7168
COL_KIWI = 7424
COL_MKR = 7552
IN_PACKED = 7680

VMEM_LIMIT = 56 * 1024 * 1024


def _params(*sem):
    return pltpu.CompilerParams(dimension_semantics=sem, vmem_limit_bytes=VMEM_LIMIT)


def _dot(a, b):
    return jnp.dot(a, b, preferred_element_type=F32)


def _dot_nt(a, b):
    return lax.dot_general(a, b, (((1,), (1,)), ((), ())), preferred_element_type=F32)


def _ada_kernel(c_ref, w_ref, b_ref, o_ref):
    c = c_ref[...]
    c_act = (c * jax.nn.sigmoid(c)).astype(BF16)
    o_ref[0] = _dot(c_act, w_ref[0].astype(BF16)) + b_ref[0]


def _ada(c8, w_ada, b_ada):
    tn = 1024
    n = N_ADA * D_MODEL
    return pl.pallas_call(
        _ada_kernel,
        grid=(DEPTH, n // tn),
        in_specs=[
            pl.BlockSpec((8, D_MODEL), lambda l, j: (0, 0)),
            pl.BlockSpec((1, D_MODEL, tn), lambda l, j: (l, 0, j)),
            pl.BlockSpec((1, 1, tn), lambda l, j: (l, 0, j)),
        ],
        out_specs=pl.BlockSpec((1, 8, tn), lambda l, j: (l, 0, j)),
        out_shape=jax.ShapeDtypeStruct((DEPTH, 8, n), F32),
        compiler_params=_params("arbitrary", "arbitrary"),
        name="ada",
    )(c8, w_ada, b_ada.reshape(DEPTH, 1, n))


N_ROPE_TABLES = 3


def _split_dot(t, m2):
    hi = t.astype(BF16)
    lo = (t - hi.astype(F32)).astype(BF16)
    return _dot(jnp.concatenate([hi, lo], axis=1), m2)


def _rope_table_kernel(pos_ref, inv_ref, sgn_ref, exp_ref, cos_ref, sin_ref):
    ang = pos_ref[...] * inv_ref[...]
    c, s = jnp.cos(ang), jnp.sin(ang)
    for t in range(N_ROPE_TABLES):
        cos_ref[t] = _split_dot(c, exp_ref[t])
        sin_ref[t] = _split_dot(s, exp_ref[t]) * sgn_ref[t]


def _rope_tables(pos_col, inv_c, sgn3, expand):
    tm = 512
    n = N_ROPE_TABLES
    return pl.pallas_call(
        _rope_table_kernel,
        grid=(TOKENS // tm,),
        in_specs=[
            pl.BlockSpec((tm, 1), lambda i: (i, 0)),
            pl.BlockSpec((1, LANES), lambda i: (0, 0)),
            pl.BlockSpec((n, 1, LANES), lambda i: (0, 0, 0)),
            pl.BlockSpec((n, 2 * LANES, LANES), lambda i: (0, 0, 0)),
        ],
        out_specs=[pl.BlockSpec((n, tm, LANES), lambda i: (0, i, 0))] * 2,
        out_shape=[jax.ShapeDtypeStruct((n, TOKENS, LANES), F32)] * 2,
        compiler_params=_params("arbitrary"),
        name="rope_tables",
    )(pos_col, inv_c, sgn3, expand)


def _rope(t, cos, sin_signed, swap):
    return t * cos + _split_dot(t, swap) * sin_signed


def _swap_matrix(half, group):
    p = np.zeros((LANES, LANES), np.float32)
    for j in range(LANES):
        g = j % group
        if g < half:
            p[j + half, j] = 1.0
        elif g < 2 * half:
            p[j - half, j] = 1.0
    return np.concatenate([p, p], axis=0)


def _rms(t, g, width):
    ms = jnp.sum(t * t, axis=-1, keepdims=True) * (1.0 / width)
    return t * lax.rsqrt(ms + EPS) * g


NORM_ROWS = 16


def _modnorm_into(h_ref, x_ref, g_ref, sc_ref, sh_ref):
    g, sc1, sh = g_ref[...], 1.0 + sc_ref[...], sh_ref[...]

    def chunk(c, carry):
        rows = pl.ds(pl.multiple_of(c * NORM_ROWS, NORM_ROWS), NORM_ROWS)
        x = x_ref[rows, :]
        ms = jnp.mean(x * x, axis=-1, keepdims=True)
        y = x * lax.rsqrt(ms + EPS) * g
        h_ref[rows, :] = (y * sc1 + sh).astype(BF16)
        return carry

    lax.fori_loop(0, x_ref.shape[0] // NORM_ROWS, chunk, 0, unroll=8)


def _in_proj_kernel(x_ref, g_ref, sc_ref, sh_ref, w_ref, o_ref, h_ref):
    @pl.when(pl.program_id(1) == 0)
    def _():
        _modnorm_into(h_ref, x_ref, g_ref, sc_ref, sh_ref)

    o_ref[...] = _dot_nt(h_ref[...], w_ref[...]).astype(o_ref.dtype)


def _mod_spec(l, k, per_b, width=D_MODEL):
    if width == D_MODEL:
        return pl.BlockSpec((None, None, None, 1, D_MODEL), lambda i, *_: (l, i // per_b, k, 0, 0))
    return pl.BlockSpec((None, None, None, 1, width), lambda i, j: (l, i // per_b, k, 0, j))


def _layer_row_spec(l, width):
    return pl.BlockSpec((None, 1, width), lambda *_: (l, 0, 0))


def _in_proj(l, x, g, mod, w):
    tm, tn = 1024, 1536
    per_b = SEQ // tm
    n = w.shape[1]
    return pl.pallas_call(
        _in_proj_kernel,
        grid=(TOKENS // tm, n // tn),
        in_specs=[
            pl.BlockSpec((tm, D_MODEL), lambda i, j: (i, 0)),
            _layer_row_spec(l, D_MODEL),
            _mod_spec(l, 1, per_b),
            _mod_spec(l, 0, per_b),
            pl.BlockSpec((None, tn, D_MODEL), lambda i, j: (l, j, 0)),
        ],
        out_specs=pl.BlockSpec((tm, tn), lambda i, j: (i, j)),
        out_shape=jax.ShapeDtypeStruct((TOKENS, n), BF16),
        scratch_shapes=[pltpu.VMEM((tm, D_MODEL), BF16)],
        compiler_params=_params("arbitrary", "arbitrary"),
        name="in_proj",
    )(x, g, mod, mod, w)


def _prep_kernel(qa_ref, qi_ref, mql_ref, kv_ref, mkvl_ref, kiwi_ref,
                 ca_ref, sa_ref, ci_ref, si_ref, pa_ref, pi_ref,
                 gqa_ref, gka_ref, gmq_ref, gmkv_ref,
                 qa_o, ka_o, va_o, qi_o, ki2_o, wi_o, mqn_o, mkvn_o):
    tm = qa_ref.shape[0]
    lane = lax.broadcasted_iota(jnp.int32, (tm, LANES), 1)
    ca, sa, pa = ca_ref[0], sa_ref[0], pa_ref[0]
    ci, si, pi = ci_ref[0], si_ref[0], pi_ref[0]
    scale_a = A_HEAD_DIM ** -0.5

    for h in range(A_HEADS):
        t = qa_ref[:, h * LANES:(h + 1) * LANES].astype(F32)
        r = _rope(_rms(t, gqa_ref[...], A_HEAD_DIM), ca, sa, pa)
        qa_o[h] = (r * scale_a).astype(BF16)
    for g in range(A_KV_HEADS):
        t = kv_ref[:, g * LANES:(g + 1) * LANES].astype(F32)
        r = _rope(_rms(t, gka_ref[...], A_HEAD_DIM), ca, sa, pa)
        ka_o[g] = r.astype(BF16)
        va_o[g] = kv_ref[:, (A_KV_HEADS + g) * LANES:(A_KV_HEADS + g + 1) * LANES]
    for p in range(IDX_HEADS // 2):
        t = qi_ref[:, p * LANES:(p + 1) * LANES].astype(F32)
        r = _rope(t, ci, si, pi)
        qi_o[2 * p] = jnp.where(lane < IDX_DIM, r, 0.0).astype(BF16)
        qi_o[2 * p + 1] = jnp.where(lane < IDX_DIM, 0.0, r).astype(BF16)

    kiwi = kiwi_ref[...].astype(F32)
    ki = _rope(kiwi, ci, si, pi)
    ki2_o[...] = jnp.where(lane < IDX_DIM, ki, pltpu.roll(ki, IDX_DIM, 1)).astype(BF16)
    wi_o[...] = kiwi * IDX_W_SCALE

    mqn_o[...] = _rms(mql_ref[...].astype(F32), gmq_ref[...], M_Q_LORA).astype(BF16)
    mkvn_o[...] = _rms(mkvl_ref[...].astype(F32), gmkv_ref[...], M_KV_LORA).astype(BF16)


def _prep(l, proj, tabs_cos, tabs_sin, swaps, g_qa, g_ka, g_mq_lat, g_mkv_lat):
    tm = 256
    per_b = SEQ // tm

    def swap(t):
        return pl.BlockSpec((1, 2 * LANES, LANES), lambda i: (t, 0, 0))

    def col(width, start):
        idx = start // width
        return pl.BlockSpec((tm, width), lambda i: (i, idx))

    def tab(t):
        return pl.BlockSpec((1, tm, LANES), lambda i: (t, i, 0))

    def vec(width):
        return _layer_row_spec(l, width)

    def heads(n):
        return pl.BlockSpec((None, n, tm, LANES), lambda i: (i // per_b, 0, i % per_b, 0))

    def seq(width):
        return pl.BlockSpec((None, tm, width), lambda i: (i // per_b, i % per_b, 0))

    def tok(width):
        return pl.BlockSpec((tm, width), lambda i: (i, 0))

    return pl.pallas_call(
        _prep_kernel,
        grid=(TOKENS // tm,),
        in_specs=[
            col(1024, COL_QA), col(1024, COL_QI), col(512, COL_MQL), col(512, COL_KV),
            col(256, COL_MKVL), col(128, COL_KIWI),
            tab(0), tab(0), tab(1), tab(1), swap(0), swap(1),
            vec(A_HEAD_DIM), vec(A_HEAD_DIM), vec(M_Q_LORA), vec(M_KV_LORA),
        ],
        out_specs=[
            heads(A_HEADS), heads(A_KV_HEADS), heads(A_KV_HEADS),
            heads(IDX_HEADS), seq(LANES), seq(LANES),
            tok(M_Q_LORA), tok(M_KV_LORA),
        ],
        out_shape=[
            jax.ShapeDtypeStruct((BATCH, A_HEADS, SEQ, LANES), BF16),
            jax.ShapeDtypeStruct((BATCH, A_KV_HEADS, SEQ, LANES), BF16),
            jax.ShapeDtypeStruct((BATCH, A_KV_HEADS, SEQ, LANES), BF16),
            jax.ShapeDtypeStruct((BATCH, IDX_HEADS, SEQ, LANES), BF16),
            jax.ShapeDtypeStruct((BATCH, SEQ, LANES), BF16),
            jax.ShapeDtypeStruct((BATCH, SEQ, LANES), F32),
            jax.ShapeDtypeStruct((TOKENS, M_Q_LORA), BF16),
            jax.ShapeDtypeStruct((TOKENS, M_KV_LORA), BF16),
        ],
        compiler_params=_params("arbitrary"),
        name="prep",
    )(proj, proj, proj, proj, proj, proj, tabs_cos, tabs_sin, tabs_cos, tabs_sin, swaps, swaps,
      g_qa, g_ka, g_mq_lat, g_mkv_lat)


def _mla_up_kernel(mqn_ref, mkvn_ref, mkr_ref, wq_ref, wkv_ref, cm_ref, sm_ref, pm_ref,
                   gqn_ref, gqr_ref, gkn_ref, gkr_ref, q_o, k_o, v_o):
    tm = mqn_ref.shape[0]
    lane = lax.broadcasted_iota(jnp.int32, (tm, LANES), 1)
    low = lane < M_ROPE
    cm, sm, pm = cm_ref[0], sm_ref[0], pm_ref[0]
    scale_m = M_QK ** -0.5
    nope_w = M_HEADS * M_NOPE

    q = _dot(mqn_ref[...], wq_ref[...])
    kv = _dot(mkvn_ref[...], wkv_ref[...])

    for p in range(M_HEADS // 2):
        rp = q[:, nope_w + p * LANES: nope_w + (p + 1) * LANES]
        sq = rp * rp
        s_all = jnp.sum(sq, axis=-1, keepdims=True)
        s_lo = jnp.sum(jnp.where(low, sq, 0.0), axis=-1, keepdims=True)
        for e in range(2):
            h = 2 * p + e
            nope = q[:, h * M_NOPE:(h + 1) * M_NOPE]
            ss = jnp.sum(nope * nope, axis=-1, keepdims=True) + (s_lo if e == 0 else s_all - s_lo)
            rs = lax.rsqrt(ss * (1.0 / M_QK) + EPS)
            roped = _rope(rp * rs * gqr_ref[...], cm, sm, pm)
            if e == 1:
                roped = pltpu.roll(roped, M_ROPE, 1)
            q_o[h, :, 0:LANES] = (nope * rs * gqn_ref[...] * scale_m).astype(BF16)
            q_o[h, :, LANES:2 * LANES] = (jnp.where(low, roped, 0.0) * scale_m).astype(BF16)

    kr = mkr_ref[...].astype(F32)
    kr_ss = jnp.sum(kr * kr, axis=-1, keepdims=True)
    kr_roped = _rope(kr * gkr_ref[...], cm, sm, pm)
    kr_roped = jnp.where(low, kr_roped, 0.0)
    for h in range(M_HEADS):
        nope = kv[:, h * 2 * LANES: h * 2 * LANES + M_NOPE]
        ss = jnp.sum(nope * nope, axis=-1, keepdims=True) + kr_ss
        rs = lax.rsqrt(ss * (1.0 / M_QK) + EPS)
        k_o[h, :, 0:LANES] = (nope * rs * gkn_ref[...]).astype(BF16)
        k_o[h, :, LANES:2 * LANES] = (kr_roped * rs).astype(BF16)
        v_o[h] = kv[:, h * 2 * LANES + M_NOPE:(h + 1) * 2 * LANES].astype(BF16)


def _mla_up(l, mqn, mkvn, proj, wq, wkv, tabs_cos, tabs_sin, swaps, gqn, gqr, gkn, gkr):
    tm = 256
    per_b = SEQ // tm

    def vec():
        return _layer_row_spec(l, LANES)

    def heads(width):
        return pl.BlockSpec((None, M_HEADS, tm, width), lambda i: (i // per_b, 0, i % per_b, 0))

    return pl.pallas_call(
        _mla_up_kernel,
        grid=(TOKENS // tm,),
        in_specs=[
            pl.BlockSpec((tm, M_Q_LORA), lambda i: (i, 0)),
            pl.BlockSpec((tm, M_KV_LORA), lambda i: (i, 0)),
            pl.BlockSpec((tm, LANES), lambda i: (i, COL_MKR // LANES)),
            pl.BlockSpec((None,) + wq.shape[1:], lambda i: (l, 0, 0)),
            pl.BlockSpec((None,) + wkv.shape[1:], lambda i: (l, 0, 0)),
            pl.BlockSpec((1, tm, LANES), lambda i: (2, i, 0)),
            pl.BlockSpec((1, tm, LANES), lambda i: (2, i, 0)),
            pl.BlockSpec((1, 2 * LANES, LANES), lambda i: (2, 0, 0)),
            vec(), vec(), vec(), vec(),
        ],
        out_specs=[heads(M_QK_PAD), heads(M_QK_PAD), heads(M_V)],
        out_shape=[
            jax.ShapeDtypeStruct((BATCH, M_HEADS, SEQ, M_QK_PAD), BF16),
            jax.ShapeDtypeStruct((BATCH, M_HEADS, SEQ, M_QK_PAD), BF16),
            jax.ShapeDtypeStruct((BATCH, M_HEADS, SEQ, M_V), BF16),
        ],
        compiler_params=_params("arbitrary"),
        name="mla_up",
    )(mqn, mkvn, proj, wq, wkv, tabs_cos, tabs_sin, swaps, gqn, gqr, gkn, gkr)


INT_MIN = -(2 ** 31)
KEY_MASKED = INT_MIN

DSA_TQ = 256
KEY_CHUNK = 512
SEARCH_ROWS = 128
SEARCH_UNROLL = 5
IDX_CHUNK = 256
IDX_ROWS = 64
ATT_SPLIT = 4


def _dsa_index(i, n_chunks, qi_ref, ki2_ref, wi_ref, key_ref, d_refs, wb_ref):
    tq = qi_ref.shape[1]
    w = wi_ref[...]
    for h in range(IDX_HEADS):
        wb_ref[h] = jnp.broadcast_to(w[:, IDX_DIM + h:IDX_DIM + h + 1], (tq, LANES))

    def matmul(c, d_ref):
        start = pl.multiple_of(c * IDX_CHUNK, IDX_CHUNK)
        q_all = qi_ref[...].reshape(IDX_HEADS * tq, LANES)
        d_ref[...] = _dot_nt(q_all, ki2_ref[pl.ds(start, IDX_CHUNK), :])

    def head_sum(c, d_ref):
        for r0 in range(0, tq, IDX_ROWS):
            row_t = i * tq + r0 + lax.broadcasted_iota(jnp.int32, (IDX_ROWS, LANES), 0)
            for j0 in range(0, IDX_CHUNK, LANES):
                acc = jnp.zeros((IDX_ROWS, LANES), F32)
                for h in range(IDX_HEADS):
                    d = d_ref[h * tq + r0:h * tq + r0 + IDX_ROWS, j0:j0 + LANES]
                    acc = acc + jnp.maximum(d, 0.0) * wb_ref[h, r0:r0 + IDX_ROWS, :]
                bits = pltpu.bitcast(acc, jnp.int32)
                img = bits ^ ((bits >> 31) & jnp.int32(0x7FFFFFFF))
                col_t = c * IDX_CHUNK + j0 + lax.broadcasted_iota(jnp.int32, (IDX_ROWS, LANES), 1)
                key_ref[c, r0:r0 + IDX_ROWS, j0:j0 + LANES] = jnp.where(col_t <= row_t, img, KEY_MASKED)

    d0_ref, d1_ref = d_refs
    matmul(0, d0_ref)

    def pair(k, carry):
        c = 2 * k
        matmul(c + 1, d1_ref)
        head_sum(c, d0_ref)
        matmul(jnp.minimum(c + 2, n_chunks - 1), d0_ref)
        head_sum(c + 1, d1_ref)
        return carry

    lax.fori_loop(0, n_chunks // 2, pair, 0)


def _dsa_select(nk, nk_bias, key_ref, bias_ref):
    chunks = range(nk // IDX_CHUNK)
    n_sel = float(N_SEL)
    groups = list(range(0, key_ref.shape[1], SEARCH_ROWS))

    def count_ge(r0, cand):
        hits = None
        for c in chunks:
            hit = jnp.where(key_ref[c, r0:r0 + SEARCH_ROWS, :] >= cand, 1.0, 0.0)
            hits = hit if hits is None else hits + hit
        return jnp.sum(hits, axis=-1, keepdims=True)

    zero = jnp.zeros((SEARCH_ROWS, 1), jnp.int32)
    thr0 = tuple(jnp.where(count_ge(r0, zero) >= n_sel, jnp.int32(0), jnp.int32(INT_MIN))
                 for r0 in groups)

    def step(b, thrs):
        bit = lax.shift_left(jnp.int32(1), jnp.int32(30) - b)
        out = []
        for r0, thr in zip(groups, thrs):
            cand = thr | bit
            out.append(jnp.where(count_ge(r0, cand) >= n_sel, cand, thr))
        return tuple(out)

    thrs = lax.fori_loop(0, 30, step, thr0, unroll=SEARCH_UNROLL)
    thrs = step(jnp.int32(30), thrs)
    for r0, thr in zip(groups, thrs):
        floor = jnp.maximum(thr, jnp.int32(KEY_MASKED + 1))
        for c in range(nk_bias // IDX_CHUNK):
            sel = key_ref[c, r0:r0 + SEARCH_ROWS, :] >= floor
            bias_ref[r0:r0 + SEARCH_ROWS, c * IDX_CHUNK:(c + 1) * IDX_CHUNK] = jnp.where(sel, 0.0, -jnp.inf)


def _dsa_attend(nk, qa_ref, ka_ref, va_ref, bias_ref, o_ref, oh_ref, lg_refs, p_refs):
    tq = qa_ref.shape[1]
    rep = A_HEADS // A_KV_HEADS

    sub = rep // ATT_SPLIT

    def group(g, carry):
        k = ka_ref[g, 0:nk, :]
        v = va_ref[g, 0:nk, :]
        inv_l = [None] * ATT_SPLIT

        def scores(s):
            q = qa_ref[pl.ds(g * rep + s * sub, sub)].reshape(sub * tq, LANES)
            lg_refs[s][:, 0:nk] = _dot_nt(q, k)

        def softmax(s):
            logits = lg_refs[s][:, 0:nk].reshape(sub, tq, nk) + bias_ref[:, 0:nk][None]
            m = jnp.max(logits, axis=-1, keepdims=True)
            p = jnp.exp(logits - m)
            inv_l[s] = 1.0 / jnp.sum(p, axis=-1, keepdims=True)
            p_refs[s][:, 0:nk] = p.astype(BF16).reshape(sub * tq, nk)

        def values(s):
            o = _dot(p_refs[s][:, 0:nk], v).reshape(sub, tq, LANES) * inv_l[s]
            oh_ref[pl.ds(g * rep + s * sub, sub)] = o.astype(BF16)

        for t in range(ATT_SPLIT + 2):
            if t < ATT_SPLIT:
                scores(t)
            if 0 <= t - 1 < ATT_SPLIT:
                softmax(t - 1)
            if 0 <= t - 2 < ATT_SPLIT:
                values(t - 2)
        return carry

    lax.fori_loop(0, A_KV_HEADS, group, 0)
    for h in range(A_HEADS):
        o_ref[:, h * LANES:(h + 1) * LANES] = oh_ref[h]


def _dsa_kernel(qa_ref, ka_ref, va_ref, qi_ref, ki2_ref, wi_ref, o_ref,
                key_ref, bias_ref, d0_ref, d1_ref, wb_ref, oh_ref, *stage_refs):
    lg_refs, p_refs = stage_refs[:ATT_SPLIT], stage_refs[ATT_SPLIT:]
    i = pl.program_id(1)
    tq = DSA_TQ
    n_free = N_SEL // tq
    variant = i // (KEY_CHUNK // tq)

    for v in range(n_free):
        @pl.when(i == v)
        def _(v=v):
            nk = (v + 1) * tq
            row = i * tq + lax.broadcasted_iota(jnp.int32, (tq, nk), 0)
            col = lax.broadcasted_iota(jnp.int32, (tq, nk), 1)
            bias_ref[:, 0:nk] = jnp.where(col <= row, 0.0, -jnp.inf)
            _dsa_attend(nk, qa_ref, ka_ref, va_ref, bias_ref, o_ref, oh_ref, lg_refs, p_refs)

    @pl.when(i >= n_free)
    def _():
        n_chunks = (variant + 1) * (KEY_CHUNK // IDX_CHUNK)
        _dsa_index(i, n_chunks, qi_ref, ki2_ref, wi_ref, key_ref, (d0_ref, d1_ref), wb_ref)

    for v in range(SEQ // KEY_CHUNK):
        @pl.when((i >= n_free) & (variant == v))
        def _(v=v):
            nk = (v + 1) * KEY_CHUNK
            _dsa_select(nk, nk, key_ref, bias_ref)
            _dsa_attend(nk, qa_ref, ka_ref, va_ref, bias_ref, o_ref, oh_ref, lg_refs, p_refs)


def _dsa(qa, ka, va, qi, ki2, wi):
    tq = DSA_TQ
    nq = SEQ // tq
    att_rows = A_HEADS // A_KV_HEADS // ATT_SPLIT * tq
    return pl.pallas_call(
        _dsa_kernel,
        grid=(BATCH, nq),
        in_specs=[
            pl.BlockSpec((None, A_HEADS, tq, LANES), lambda b, i: (b, 0, i, 0)),
            pl.BlockSpec((None, A_KV_HEADS, SEQ, LANES), lambda b, i: (b, 0, 0, 0)),
            pl.BlockSpec((None, A_KV_HEADS, SEQ, LANES), lambda b, i: (b, 0, 0, 0)),
            pl.BlockSpec((None, IDX_HEADS, tq, LANES), lambda b, i: (b, 0, i, 0)),
            pl.BlockSpec((None, SEQ, LANES), lambda b, i: (b, 0, 0)),
            pl.BlockSpec((None, tq, LANES), lambda b, i: (b, i, 0)),
        ],
        out_specs=pl.BlockSpec((tq, A_HEADS * A_HEAD_DIM), lambda b, i: (b * nq + i, 0)),
        out_shape=jax.ShapeDtypeStruct((TOKENS, A_HEADS * A_HEAD_DIM), BF16),
        scratch_shapes=[
            pltpu.VMEM((SEQ // IDX_CHUNK, tq, IDX_CHUNK), jnp.int32),
            pltpu.VMEM((tq, SEQ), F32),
            pltpu.VMEM((IDX_HEADS * tq, IDX_CHUNK), F32),
            pltpu.VMEM((IDX_HEADS * tq, IDX_CHUNK), F32),
            pltpu.VMEM((IDX_HEADS, tq, LANES), F32),
            pltpu.VMEM((A_HEADS, tq, LANES), BF16),
        ] + [pltpu.VMEM((att_rows, SEQ), F32)] * ATT_SPLIT + [pltpu.VMEM((att_rows, SEQ), BF16)] * ATT_SPLIT,
        compiler_params=_params("arbitrary", "arbitrary"),
        name="dsa",
    )(qa, ka, va, qi, ki2, wi)


MLA_TQ = 256
MLA_HEADS_PER_STEP = 4
MLA_KEY_CHUNK = 256


def _mla_body(nk, i, q_ref, k_ref, v_ref, o_ref, lg_refs, p_refs):
    n_heads, tq = q_ref.shape[0], q_ref.shape[1]
    n0 = nk - MLA_KEY_CHUNK
    row = i * tq + lax.broadcasted_iota(jnp.int32, (tq, MLA_KEY_CHUNK), 0)
    col = n0 + lax.broadcasted_iota(jnp.int32, (tq, MLA_KEY_CHUNK), 1)
    causal = col <= row
    inv_l = [None] * n_heads

    def scores(h):
        lg_refs[h][:, 0:nk] = _dot_nt(q_ref[h], k_ref[h, 0:nk, :])

    def softmax(h):
        s1 = jnp.where(causal, lg_refs[h][:, n0:nk], -jnp.inf)
        m = jnp.max(s1, axis=-1, keepdims=True)
        if n0:
            s0 = lg_refs[h][:, 0:n0]
            m = jnp.maximum(m, jnp.max(s0, axis=-1, keepdims=True))
        p1 = jnp.exp(s1 - m)
        l = jnp.sum(p1, axis=-1, keepdims=True)
        p_refs[h][:, n0:nk] = p1.astype(BF16)
        if n0:
            p0 = jnp.exp(s0 - m)
            l = l + jnp.sum(p0, axis=-1, keepdims=True)
            p_refs[h][:, 0:n0] = p0.astype(BF16)
        inv_l[h] = 1.0 / l

    def values(h):
        o = _dot(p_refs[h][:, 0:nk], v_ref[h, 0:nk, :]) * inv_l[h]
        o_ref[:, h * M_V:(h + 1) * M_V] = o.astype(BF16)

    for t in range(n_heads + 2):
        if t < n_heads:
            scores(t)
        if 0 <= t - 1 < n_heads:
            softmax(t - 1)
        if 0 <= t - 2 < n_heads:
            values(t - 2)


def _mla_kernel(q_ref, k_ref, v_ref, o_ref, *stage_refs):
    n_heads = q_ref.shape[0]
    lg_refs, p_refs = stage_refs[:n_heads], stage_refs[n_heads:]
    i = pl.program_id(2)
    per_chunk = MLA_KEY_CHUNK // MLA_TQ
    for v in range(SEQ // MLA_KEY_CHUNK):
        @pl.when(i // per_chunk == v)
        def _(v=v):
            _mla_body((v + 1) * MLA_KEY_CHUNK, i, q_ref, k_ref, v_ref, o_ref, lg_refs, p_refs)


def _mla(q, k, v):
    tq, hg = MLA_TQ, MLA_HEADS_PER_STEP
    nq = SEQ // tq
    return pl.pallas_call(
        _mla_kernel,
        grid=(BATCH, M_HEADS // hg, nq),
        in_specs=[
            pl.BlockSpec((None, hg, tq, M_QK_PAD), lambda b, g, i: (b, g, i, 0)),
            pl.BlockSpec((None, hg, SEQ, M_QK_PAD), lambda b, g, i: (b, g, 0, 0)),
            pl.BlockSpec((None, hg, SEQ, M_V), lambda b, g, i: (b, g, 0, 0)),
        ],
        out_specs=pl.BlockSpec((tq, hg * M_V), lambda b, g, i: (b * nq + i, g)),
        out_shape=jax.ShapeDtypeStruct((TOKENS, M_HEADS * M_V), BF16),
        scratch_shapes=[pltpu.VMEM((tq, SEQ), F32)] * hg + [pltpu.VMEM((tq, SEQ), BF16)] * hg,
        compiler_params=_params("arbitrary", "arbitrary", "arbitrary"),
        name="mla",
    )(q, k, v)


def _merge_kernel(oa_ref, ob_ref, ga_ref, gb_ref, wpa_ref, wpb_ref, wo_ref, x_ref, gt_ref,
                  gf_ref, scf_ref, shf_ref, o_ref, hf_ref):
    a = _dot(oa_ref[...], wpa_ref[...])
    b = _dot(ob_ref[...], wpb_ref[...])
    merged = jax.nn.sigmoid(ga_ref[...].astype(F32)) * a + jax.nn.sigmoid(gb_ref[...].astype(F32)) * b
    y = _dot(merged.astype(BF16), wo_ref[...])
    xn = x_ref[...] + gt_ref[...] * y
    o_ref[...] = xn
    ms = jnp.mean(xn * xn, axis=-1, keepdims=True)
    hn = xn * lax.rsqrt(ms + EPS) * gf_ref[...]
    hf_ref[...] = (hn * (1.0 + scf_ref[...]) + shf_ref[...]).astype(BF16)


def _merge(l, oa, ob, proj, wpa, wpb, wo, x, mod, g_ffn):
    tm = 256
    per_b = SEQ // tm

    def resident(shape):
        return pl.BlockSpec((None,) + shape[1:], lambda i: (l, 0, 0), pipeline_mode=pl.Buffered(1))

    return pl.pallas_call(
        _merge_kernel,
        grid=(TOKENS // tm,),
        in_specs=[
            pl.BlockSpec((tm, A_HEADS * A_HEAD_DIM), lambda i: (i, 0)),
            pl.BlockSpec((tm, M_HEADS * M_V), lambda i: (i, 0)),
            pl.BlockSpec((tm, D_MODEL), lambda i: (i, COL_GATE_A // D_MODEL)),
            pl.BlockSpec((tm, D_MODEL), lambda i: (i, COL_GATE_B // D_MODEL)),
            resident(wpa.shape), resident(wpb.shape), resident(wo.shape),
            pl.BlockSpec((tm, D_MODEL), lambda i: (i, 0)),
            _mod_spec(l, 2, per_b),
            _layer_row_spec(l, D_MODEL),
            _mod_spec(l, 4, per_b),
            _mod_spec(l, 3, per_b),
        ],
        out_specs=[pl.BlockSpec((tm, D_MODEL), lambda i: (i, 0))] * 2,
        out_shape=[jax.ShapeDtypeStruct((TOKENS, D_MODEL), F32),
                   jax.ShapeDtypeStruct((TOKENS, D_MODEL), BF16)],
        compiler_params=_params("arbitrary"),
        name="merge",
    )(oa, ob, proj, proj, wpa, wpb, wo, x, mod, g_ffn, mod, mod)


HALO = 8


def _ffn_up_kernel(h_ref, wg_ref, wv_ref, cg_ref, cv_ref, bg_ref, bv_ref,
                   o_ref, wgb_ref, wvb_ref, buf_ref, *, tiles_per_seq):
    i = pl.program_id(1)
    tm, tn = o_ref.shape

    @pl.when(i == 0)
    def _():
        wgb_ref[...] = wg_ref[...].astype(BF16)
        wvb_ref[...] = wv_ref[...].astype(BF16)

    seq_start = (i % tiles_per_seq) == 0

    @pl.when(seq_start)
    def _():
        buf_ref[0:HALO, :] = jnp.zeros((HALO, 2 * tn), F32)

    @pl.when(jnp.logical_not(seq_start))
    def _():
        buf_ref[0:HALO, :] = buf_ref[tm:tm + HALO, :]

    h = h_ref[...]
    buf_ref[HALO:HALO + tm, 0:tn] = _dot(h, wgb_ref[...])
    buf_ref[HALO:HALO + tm, tn:2 * tn] = _dot(h, wvb_ref[...])

    def conv(lo, c_ref, b_ref):
        u0 = buf_ref[HALO:HALO + tm, lo:lo + tn]
        u1 = buf_ref[HALO - 1:HALO - 1 + tm, lo:lo + tn]
        u2 = buf_ref[HALO - 2:HALO - 2 + tm, lo:lo + tn]
        return b_ref[...] + c_ref[0:1, :] * u2 + c_ref[1:2, :] * u1 + c_ref[2:3, :] * u0

    gate = conv(0, cg_ref, bg_ref)
    val = conv(tn, cv_ref, bv_ref)
    o_ref[...] = (gate * jax.nn.sigmoid(gate) * val).astype(BF16)


def _ffn_up(l, h, w_up, w_conv, b_conv):
    tm, tn = 1024, 512
    nj = D_FF // tn
    kern = functools.partial(_ffn_up_kernel, tiles_per_seq=SEQ // tm)
    return pl.pallas_call(
        kern,
        grid=(nj, TOKENS // tm),
        in_specs=[
            pl.BlockSpec((tm, D_MODEL), lambda j, i: (i, 0)),
            pl.BlockSpec((None, D_MODEL, tn), lambda j, i: (l, 0, j)),
            pl.BlockSpec((None, D_MODEL, tn), lambda j, i: (l, 0, nj + j)),
            pl.BlockSpec((None, CONV_W, tn), lambda j, i: (l, 0, j)),
            pl.BlockSpec((None, CONV_W, tn), lambda j, i: (l, 0, nj + j)),
            pl.BlockSpec((None, 1, tn), lambda j, i: (l, 0, j)),
            pl.BlockSpec((None, 1, tn), lambda j, i: (l, 0, nj + j)),
        ],
        out_specs=pl.BlockSpec((tm, tn), lambda j, i: (i, j)),
        out_shape=jax.ShapeDtypeStruct((TOKENS, D_FF), BF16),
        scratch_shapes=[
            pltpu.VMEM((D_MODEL, tn), BF16),
            pltpu.VMEM((D_MODEL, tn), BF16),
            pltpu.VMEM((HALO + tm, 2 * tn), F32),
        ],
        compiler_params=_params("arbitrary", "arbitrary"),
        name="ffn_up",
    )(h, w_up, w_up, w_conv, w_conv, b_conv, b_conv)


def _ffn_down_kernel(a_ref, w_ref, x_ref, gt_ref, o_ref):
    o_ref[...] = x_ref[...] + gt_ref[...] * _dot(a_ref[...], w_ref[...])


def _ffn_down(l, act, w_down, x, mod):
    tm, tn = 1024, 512
    per_b = SEQ // tm
    return pl.pallas_call(
        _ffn_down_kernel,
        grid=(TOKENS // tm, D_MODEL // tn),
        in_specs=[
            pl.BlockSpec((tm, D_FF), lambda i, j: (i, 0)),
            pl.BlockSpec((None, D_FF, tn), lambda i, j: (l, 0, j)),
            pl.BlockSpec((tm, tn), lambda i, j: (i, j)),
            _mod_spec(l, 5, per_b, tn),
        ],
        out_specs=pl.BlockSpec((tm, tn), lambda i, j: (i, j)),
        out_shape=jax.ShapeDtypeStruct((TOKENS, D_MODEL), F32),
        compiler_params=_params("arbitrary", "arbitrary"),
        name="ffn_down",
    )(act, w_down, x, mod)


def _pack_w_in(w_in):
    o = np.cumsum((0,) + (1024, 256, 256, 1024, 64, 16, 512, 256, 64, 2048, 2048))
    w_t = jnp.swapaxes(w_in, 1, 2)
    qa, ka, va, qi, ki, wi, mql, mkvl, mkr, ga, gb = (
        w_t[:, int(o[k]):int(o[k + 1])].astype(BF16) for k in range(11))
    z = lambda n: jnp.zeros((DEPTH, n, D_MODEL), BF16)
    packed = jnp.concatenate(
        [ga, gb, qa, qi, mql, ka, va, mkvl, ki, wi, z(48), mkr, z(64)], axis=1)
    assert packed.shape[1] == IN_PACKED
    return packed


def _pack_w_mq(w_mq_up):
    w = w_mq_up.reshape(DEPTH, M_Q_LORA, M_HEADS, M_QK)
    nope = w[..., :M_NOPE].reshape(DEPTH, M_Q_LORA, M_HEADS * M_NOPE)
    rope = w[..., M_NOPE:].reshape(DEPTH, M_Q_LORA, M_HEADS * M_ROPE)
    return jnp.concatenate([nope, rope], axis=-1).astype(BF16)


def _rope_consts():
    def inv(rot):
        return ROPE_THETA ** (-jnp.arange(0, rot, 2, dtype=F32) / rot)

    layouts = ((A_HEAD_DIM // ROT_FRACTION, LANES), (IDX_DIM // ROT_FRACTION, IDX_DIM), (M_ROPE, M_ROPE))
    freqs = [inv(rot) for rot, _ in layouts]
    used = sum(f.shape[0] for f in freqs)
    inv_c = jnp.concatenate(freqs + [jnp.zeros((LANES - used,), F32)]).reshape(1, LANES)
    zero_lane = LANES - 1

    sgn3 = np.zeros((N_ROPE_TABLES, 1, LANES), np.float32)
    expand = np.zeros((N_ROPE_TABLES, LANES, LANES), np.float32)
    off = 0
    for t, (rot, group) in enumerate(layouts):
        half = rot // 2
        for j in range(LANES):
            g = j % group
            if g < rot:
                expand[t, off + g % half, j] = 1.0
                sgn3[t, 0, j] = -1.0 if g < half else 1.0
            else:
                expand[t, zero_lane, j] = 1.0
        off += half
    expand2 = np.concatenate([expand, expand], axis=1)
    return inv_c, jnp.asarray(sgn3), jnp.asarray(expand2, dtype=BF16)


def kernel(x, c, positions, g_attn, g_ffn, w_ada, b_ada, w_in, g_qa, g_ka, g_mq_lat, w_mq_up,
           g_mkv_lat, w_mkv_up, g_qm, g_km, w_pa, w_pb, w_o, w_up, w_conv, b_conv, w_down):
    pos_col = positions.astype(F32).reshape(TOKENS, 1)
    inv_c, sgn3, expand = _rope_consts()
    tabs_cos, tabs_sin = _rope_tables(pos_col, inv_c, sgn3, expand)
    rot_a, rot_i = A_HEAD_DIM // ROT_FRACTION, IDX_DIM // ROT_FRACTION
    swaps = jnp.asarray(np.stack([_swap_matrix(rot_a // 2, LANES), _swap_matrix(rot_i // 2, IDX_DIM),
                                  _swap_matrix(M_ROPE // 2, M_ROPE)]), dtype=BF16)

    c8 = jnp.pad(c, ((0, 8 - BATCH), (0, 0)))
    mod = _ada(c8, w_ada, b_ada)[:, :BATCH].reshape(DEPTH, BATCH, N_ADA, 1, D_MODEL)

    w_in_p = _pack_w_in(w_in)
    w_mq_p = _pack_w_mq(w_mq_up)
    w_mkv_b = w_mkv_up.astype(BF16)
    w_pa_b, w_pb_b, w_o_b = w_pa.astype(BF16), w_pb.astype(BF16), w_o.astype(BF16)
    w_down_b = w_down.astype(BF16)

    rows = lambda v: v.reshape(DEPTH, 1, -1)
    rope_gain = lambda v: rows(jnp.tile(v[:, M_NOPE:], (1, 2)))
    g_attn_r, g_ffn_r, b_conv_r = rows(g_attn), rows(g_ffn), rows(b_conv)
    g_qa_r, g_ka_r, g_mq_r, g_mkv_r = rows(g_qa), rows(g_ka), rows(g_mq_lat), rows(g_mkv_lat)
    gqn, gqr = rows(g_qm[:, :M_NOPE]), rope_gain(g_qm)
    gkn, gkr = rows(g_km[:, :M_NOPE]), rope_gain(g_km)

    xf = x.reshape(TOKENS, D_MODEL)
    for l in range(DEPTH):
        proj = _in_proj(l, xf, g_attn_r, mod, w_in_p)
        qa, ka, va, qi, ki2, wi, mqn, mkvn = _prep(
            l, proj, tabs_cos, tabs_sin, swaps, g_qa_r, g_ka_r, g_mq_r, g_mkv_r)
        mq, mk, mv = _mla_up(l, mqn, mkvn, proj, w_mq_p, w_mkv_b, tabs_cos, tabs_sin, swaps,
                             gqn, gqr, gkn, gkr)
        o_a = _dsa(qa, ka, va, qi, ki2, wi)
        o_b = _mla(mq, mk, mv)
        xf, h_ffn = _merge(l, o_a, o_b, proj, w_pa_b, w_pb_b, w_o_b, xf, mod, g_ffn_r)
        act = _ffn_up(l, h_ffn, w_up, w_conv, b_conv_r)
        xf = _ffn_down(l, act, w_down_b, xf, mod)
    return xf.reshape(BATCH, SEQ, D_MODEL)
```

```python
import functools

import jax
import jax.numpy as jnp
import numpy as np
from jax import lax
from jax.experimental import pallas as pl
from jax.experimental.pallas import tpu as pltpu

D_MODEL = 2048
BATCH = 4
SEQ = 2048
DEPTH = 4
A_HEADS = 8
A_KV_HEADS = 2
A_HEAD_DIM = 128
IDX_HEADS = 16
IDX_DIM = 64
TOPK_MAX = 256
M_HEADS = 8
M_Q_LORA = 512
M_KV_LORA = 256
M_NOPE = 128
M_ROPE = 64
M_V = 128
D_FF = 5632
CONV_W = 3
ROPE_THETA = 500000.0
ROT_FRACTION = 4
EPS = 1e-6
N_ADA = 6
IDX_W_SCALE = (IDX_HEADS * IDX_DIM) ** -0.5
N_SEL = min(TOPK_MAX, SEQ // 4)
TOKENS = BATCH * SEQ
LANES = 128
M_QK = M_NOPE + M_ROPE
M_QK_PAD = 256

BF16 = jnp.bfloat16
F32 = jnp.float32

COL_GATE_A = 0
COL_GATE_B = 2048
COL_QA = 4096
COL_QI = 5120
COL_MQL = 6144
COL_KV = 6656
COL_MKVL = 7168
COL_KIWI = 7424
COL_MKR = 7552
IN_PACKED = 7680

VMEM_LIMIT = 56 * 1024 * 1024


def _params(*sem):
    return pltpu.CompilerParams(dimension_semantics=sem, vmem_limit_bytes=VMEM_LIMIT)


def _dot(a, b):
    return jnp.dot(a, b, preferred_element_type=F32)


def _dot_nt(a, b):
    return lax.dot_general(a, b, (((1,), (1,)), ((), ())), preferred_element_type=F32)


def _ada_kernel(c_ref, w_ref, b_ref, o_ref):
    c = c_ref[...]
    c_act = (c * jax.nn.sigmoid(c)).astype(BF16)
    o_ref[0] = _dot(c_act, w_ref[0].astype(BF16)) + b_ref[0]


def _ada(c8, w_ada, b_ada):
    tn = 1024
    n = N_ADA * D_MODEL
    return pl.pallas_call(
        _ada_kernel,
        grid=(DEPTH, n // tn),
        in_specs=[
            pl.BlockSpec((8, D_MODEL), lambda l, j: (0, 0)),
            pl.BlockSpec((1, D_MODEL, tn), lambda l, j: (l, 0, j)),
            pl.BlockSpec((1, 1, tn), lambda l, j: (l, 0, j)),
        ],
        out_specs=pl.BlockSpec((1, 8, tn), lambda l, j: (l, 0, j)),
        out_shape=jax.ShapeDtypeStruct((DEPTH, 8, n), F32),
        compiler_params=_params("arbitrary", "arbitrary"),
        name="ada",
    )(c8, w_ada, b_ada.reshape(DEPTH, 1, n))


N_ROPE_TABLES = 3


def _split_dot(t, m2):
    hi = t.astype(BF16)
    lo = (t - hi.astype(F32)).astype(BF16)
    return _dot(jnp.concatenate([hi, lo], axis=1), m2)


def _rope_table_kernel(pos_ref, inv_ref, sgn_ref, exp_ref, cos_ref, sin_ref):
    ang = pos_ref[...] * inv_ref[...]
    c, s = jnp.cos(ang), jnp.sin(ang)
    for t in range(N_ROPE_TABLES):
        cos_ref[t] = _split_dot(c, exp_ref[t])
        sin_ref[t] = _split_dot(s, exp_ref[t]) * sgn_ref[t]


def _rope_tables(pos_col, inv_c, sgn3, expand):
    tm = 512
    n = N_ROPE_TABLES
    return pl.pallas_call(
        _rope_table_kernel,
        grid=(TOKENS // tm,),
        in_specs=[
            pl.BlockSpec((tm, 1), lambda i: (i, 0)),
            pl.BlockSpec((1, LANES), lambda i: (0, 0)),
            pl.BlockSpec((n, 1, LANES), lambda i: (0, 0, 0)),
            pl.BlockSpec((n, 2 * LANES, LANES), lambda i: (0, 0, 0)),
        ],
        out_specs=[pl.BlockSpec((n, tm, LANES), lambda i: (0, i, 0))] * 2,
        out_shape=[jax.ShapeDtypeStruct((n, TOKENS, LANES), F32)] * 2,
        compiler_params=_params("arbitrary"),
        name="rope_tables",
    )(pos_col, inv_c, sgn3, expand)


def _rope(t, cos, sin_signed, swap):
    return t * cos + _split_dot(t, swap) * sin_signed


def _swap_matrix(half, group):
    p = np.zeros((LANES, LANES), np.float32)
    for j in range(LANES):
        g = j % group
        if g < half:
            p[j + half, j] = 1.0
        elif g < 2 * half:
            p[j - half, j] = 1.0
    return np.concatenate([p, p], axis=0)


def _rms(t, g, width):
    ms = jnp.sum(t * t, axis=-1, keepdims=True) * (1.0 / width)
    return t * lax.rsqrt(ms + EPS) * g


NORM_ROWS = 16


def _modnorm_into(h_ref, x_ref, g_ref, sc_ref, sh_ref):
    g, sc1, sh = g_ref[...], 1.0 + sc_ref[...], sh_ref[...]

    def chunk(c, carry):
        rows = pl.ds(pl.multiple_of(c * NORM_ROWS, NORM_ROWS), NORM_ROWS)
        x = x_ref[rows, :]
        ms = jnp.mean(x * x, axis=-1, keepdims=True)
        y = x * lax.rsqrt(ms + EPS) * g
        h_ref[rows, :] = (y * sc1 + sh).astype(BF16)
        return carry

    lax.fori_loop(0, x_ref.shape[0] // NORM_ROWS, chunk, 0, unroll=8)


def _in_proj_kernel(x_ref, g_ref, sc_ref, sh_ref, w_ref, o_ref, h_ref):
    @pl.when(pl.program_id(1) == 0)
    def _():
        _modnorm_into(h_ref, x_ref, g_ref, sc_ref, sh_ref)

    o_ref[...] = _dot_nt(h_ref[...], w_ref[...]).astype(o_ref.dtype)


def _mod_spec(l, k, per_b, width=D_MODEL):
    if width == D_MODEL:
        return pl.BlockSpec((None, None, None, 1, D_MODEL), lambda i, *_: (l, i // per_b, k, 0, 0))
    return pl.BlockSpec((None, None, None, 1, width), lambda i, j: (l, i // per_b, k, 0, j))


def _layer_row_spec(l, width):
    return pl.BlockSpec((None, 1, width), lambda *_: (l, 0, 0))


def _in_proj(l, x, g, mod, w):
    tm, tn = 1024, 1920
    per_b = SEQ // tm
    n = w.shape[1]
    return pl.pallas_call(
        _in_proj_kernel,
        grid=(TOKENS // tm, n // tn),
        in_specs=[
            pl.BlockSpec((tm, D_MODEL), lambda i, j: (i, 0)),
            _layer_row_spec(l, D_MODEL),
            _mod_spec(l, 1, per_b),
            _mod_spec(l, 0, per_b),
            pl.BlockSpec((None, tn, D_MODEL), lambda i, j: (l, j, 0)),
        ],
        out_specs=pl.BlockSpec((tm, tn), lambda i, j: (i, j)),
        out_shape=jax.ShapeDtypeStruct((TOKENS, n), BF16),
        scratch_shapes=[pltpu.VMEM((tm, D_MODEL), BF16)],
        compiler_params=_params("arbitrary", "arbitrary"),
        name="in_proj",
    )(x, g, mod, mod, w)


def _prep_kernel(qa_ref, qi_ref, mql_ref, kv_ref, mkvl_ref, kiwi_ref,
                 ca_ref, sa_ref, ci_ref, si_ref, pa_ref, pi_ref,
                 gqa_ref, gka_ref, gmq_ref, gmkv_ref,
                 qa_o, ka_o, va_o, qi_o, ki2_o, wi_o, mqn_o, mkvn_o):
    tm = qa_ref.shape[0]
    lane = lax.broadcasted_iota(jnp.int32, (tm, LANES), 1)
    ca, sa, pa = ca_ref[0], sa_ref[0], pa_ref[0]
    ci, si, pi = ci_ref[0], si_ref[0], pi_ref[0]
    scale_a = A_HEAD_DIM ** -0.5

    for h in range(A_HEADS):
        t = qa_ref[:, h * LANES:(h + 1) * LANES].astype(F32)
        r = _rope(_rms(t, gqa_ref[...], A_HEAD_DIM), ca, sa, pa)
        qa_o[h] = (r * scale_a).astype(BF16)
    for g in range(A_KV_HEADS):
        t = kv_ref[:, g * LANES:(g + 1) * LANES].astype(F32)
        r = _rope(_rms(t, gka_ref[...], A_HEAD_DIM), ca, sa, pa)
        ka_o[g] = r.astype(BF16)
        va_o[g] = kv_ref[:, (A_KV_HEADS + g) * LANES:(A_KV_HEADS + g + 1) * LANES]
    for p in range(IDX_HEADS // 2):
        t = qi_ref[:, p * LANES:(p + 1) * LANES].astype(F32)
        r = _rope(t, ci, si, pi)
        qi_o[2 * p] = jnp.where(lane < IDX_DIM, r, 0.0).astype(BF16)
        qi_o[2 * p + 1] = jnp.where(lane < IDX_DIM, 0.0, r).astype(BF16)

    kiwi = kiwi_ref[...].astype(F32)
    ki = _rope(kiwi, ci, si, pi)
    ki2_o[...] = jnp.where(lane < IDX_DIM, ki, pltpu.roll(ki, IDX_DIM, 1)).astype(BF16)
    wi_o[...] = kiwi * IDX_W_SCALE

    mqn_o[...] = _rms(mql_ref[...].astype(F32), gmq_ref[...], M_Q_LORA).astype(BF16)
    mkvn_o[...] = _rms(mkvl_ref[...].astype(F32), gmkv_ref[...], M_KV_LORA).astype(BF16)


def _prep(l, proj, tabs_cos, tabs_sin, swaps, g_qa, g_ka, g_mq_lat, g_mkv_lat):
    tm = 512
    per_b = SEQ // tm

    def swap(t):
        return pl.BlockSpec((1, 2 * LANES, LANES), lambda i: (t, 0, 0))

    def col(width, start):
        idx = start // width
        return pl.BlockSpec((tm, width), lambda i: (i, idx))

    def tab(t):
        return pl.BlockSpec((1, tm, LANES), lambda i: (t, i, 0))

    def vec(width):
        return _layer_row_spec(l, width)

    def heads(n):
        return pl.BlockSpec((None, n, tm, LANES), lambda i: (i // per_b, 0, i % per_b, 0))

    def seq(width):
        return pl.BlockSpec((None, tm, width), lambda i: (i // per_b, i % per_b, 0))

    def tok(width):
        return pl.BlockSpec((tm, width), lambda i: (i, 0))

    return pl.pallas_call(
        _prep_kernel,
        grid=(TOKENS // tm,),
        in_specs=[
            col(1024, COL_QA), col(1024, COL_QI), col(512, COL_MQL), col(512, COL_KV),
            col(256, COL_MKVL), col(128, COL_KIWI),
            tab(0), tab(0), tab(1), tab(1), swap(0), swap(1),
            vec(A_HEAD_DIM), vec(A_HEAD_DIM), vec(M_Q_LORA), vec(M_KV_LORA),
        ],
        out_specs=[
            heads(A_HEADS), heads(A_KV_HEADS), heads(A_KV_HEADS),
            heads(IDX_HEADS), seq(LANES), seq(LANES),
            tok(M_Q_LORA), tok(M_KV_LORA),
        ],
        out_shape=[
            jax.ShapeDtypeStruct((BATCH, A_HEADS, SEQ, LANES), BF16),
            jax.ShapeDtypeStruct((BATCH, A_KV_HEADS, SEQ, LANES), BF16),
            jax.ShapeDtypeStruct((BATCH, A_KV_HEADS, SEQ, LANES), BF16),
            jax.ShapeDtypeStruct((BATCH, IDX_HEADS, SEQ, LANES), BF16),
            jax.ShapeDtypeStruct((BATCH, SEQ, LANES), BF16),
            jax.ShapeDtypeStruct((BATCH, SEQ, LANES), F32),
            jax.ShapeDtypeStruct((TOKENS, M_Q_LORA), BF16),
            jax.ShapeDtypeStruct((TOKENS, M_KV_LORA), BF16),
        ],
        compiler_params=_params("arbitrary"),
        name="prep",
    )(proj, proj, proj, proj, proj, proj, tabs_cos, tabs_sin, tabs_cos, tabs_sin, swaps, swaps,
      g_qa, g_ka, g_mq_lat, g_mkv_lat)


def _mla_up_kernel(mqn_ref, mkvn_ref, mkr_ref, wq_ref, wkv_ref, cm_ref, sm_ref, pm_ref,
                   gqn_ref, gqr_ref, gkn_ref, gkr_ref, q_o, k_o, v_o):
    tm = mqn_ref.shape[0]
    lane = lax.broadcasted_iota(jnp.int32, (tm, LANES), 1)
    low = lane < M_ROPE
    cm, sm, pm = cm_ref[0], sm_ref[0], pm_ref[0]
    scale_m = M_QK ** -0.5
    nope_w = M_HEADS * M_NOPE

    q = _dot(mqn_ref[...], wq_ref[...])
    kv = _dot(mkvn_ref[...], wkv_ref[...])

    for p in range(M_HEADS // 2):
        rp = q[:, nope_w + p * LANES: nope_w + (p + 1) * LANES]
        sq = rp * rp
        s_all = jnp.sum(sq, axis=-1, keepdims=True)
        s_lo = jnp.sum(jnp.where(low, sq, 0.0), axis=-1, keepdims=True)
        for e in range(2):
            h = 2 * p + e
            nope = q[:, h * M_NOPE:(h + 1) * M_NOPE]
            ss = jnp.sum(nope * nope, axis=-1, keepdims=True) + (s_lo if e == 0 else s_all - s_lo)
            rs = lax.rsqrt(ss * (1.0 / M_QK) + EPS)
            roped = _rope(rp * rs * gqr_ref[...], cm, sm, pm)
            if e == 1:
                roped = pltpu.roll(roped, M_ROPE, 1)
            q_o[h, :, 0:LANES] = (nope * rs * gqn_ref[...] * scale_m).astype(BF16)
            q_o[h, :, LANES:2 * LANES] = (jnp.where(low, roped, 0.0) * scale_m).astype(BF16)

    kr = mkr_ref[...].astype(F32)
    kr_ss = jnp.sum(kr * kr, axis=-1, keepdims=True)
    kr_roped = _rope(kr * gkr_ref[...], cm, sm, pm)
    kr_roped = jnp.where(low, kr_roped, 0.0)
    for h in range(M_HEADS):
        nope = kv[:, h * 2 * LANES: h * 2 * LANES + M_NOPE]
        ss = jnp.sum(nope * nope, axis=-1, keepdims=True) + kr_ss
        rs = lax.rsqrt(ss * (1.0 / M_QK) + EPS)
        k_o[h, :, 0:LANES] = (nope * rs * gkn_ref[...]).astype(BF16)
        k_o[h, :, LANES:2 * LANES] = (kr_roped * rs).astype(BF16)
        v_o[h] = kv[:, h * 2 * LANES + M_NOPE:(h + 1) * 2 * LANES].astype(BF16)


def _mla_up(l, mqn, mkvn, proj, wq, wkv, tabs_cos, tabs_sin, swaps, gqn, gqr, gkn, gkr):
    tm = 256
    per_b = SEQ // tm

    def vec():
        return _layer_row_spec(l, LANES)

    def heads(width):
        return pl.BlockSpec((None, M_HEADS, tm, width), lambda i: (i // per_b, 0, i % per_b, 0))

    return pl.pallas_call(
        _mla_up_kernel,
        grid=(TOKENS // tm,),
        in_specs=[
            pl.BlockSpec((tm, M_Q_LORA), lambda i: (i, 0)),
            pl.BlockSpec((tm, M_KV_LORA), lambda i: (i, 0)),
            pl.BlockSpec((tm, LANES), lambda i: (i, COL_MKR // LANES)),
            pl.BlockSpec((None,) + wq.shape[1:], lambda i: (l, 0, 0)),
            pl.BlockSpec((None,) + wkv.shape[1:], lambda i: (l, 0, 0)),
            pl.BlockSpec((1, tm, LANES), lambda i: (2, i, 0)),
            pl.BlockSpec((1, tm, LANES), lambda i: (2, i, 0)),
            pl.BlockSpec((1, 2 * LANES, LANES), lambda i: (2, 0, 0)),
            vec(), vec(), vec(), vec(),
        ],
        out_specs=[heads(M_QK_PAD), heads(M_QK_PAD), heads(M_V)],
        out_shape=[
            jax.ShapeDtypeStruct((BATCH, M_HEADS, SEQ, M_QK_PAD), BF16),
            jax.ShapeDtypeStruct((BATCH, M_HEADS, SEQ, M_QK_PAD), BF16),
            jax.ShapeDtypeStruct((BATCH, M_HEADS, SEQ, M_V), BF16),
        ],
        compiler_params=_params("arbitrary"),
        name="mla_up",
    )(mqn, mkvn, proj, wq, wkv, tabs_cos, tabs_sin, swaps, gqn, gqr, gkn, gkr)


INT_MIN = -(2 ** 31)
KEY_MASKED = INT_MIN

DSA_TQ = 256
KEY_CHUNK = 512
SEARCH_ROWS = 128
SEARCH_UNROLL = 5
IDX_CHUNK = 256
IDX_ROWS = 64
ATT_SPLIT = 4
assert DSA_TQ == IDX_CHUNK and KEY_CHUNK == 2 * IDX_CHUNK


def _dsa_index(i, n_chunks, qi_ref, ki2_ref, wi_ref, key_ref, d_refs, wb_ref):
    tq = qi_ref.shape[1]
    w = wi_ref[...]
    for h in range(IDX_HEADS):
        wb_ref[h] = jnp.broadcast_to(w[:, IDX_DIM + h:IDX_DIM + h + 1], (tq, LANES))

    def matmul(c, d_ref):
        start = pl.multiple_of(c * IDX_CHUNK, IDX_CHUNK)
        q_all = qi_ref[...].reshape(IDX_HEADS * tq, LANES)
        d_ref[...] = _dot_nt(q_all, ki2_ref[pl.ds(start, IDX_CHUNK), :])

    def head_sum(c, d_ref):
        for r0 in range(0, tq, IDX_ROWS):
            row_t = i * tq + r0 + lax.broadcasted_iota(jnp.int32, (IDX_ROWS, LANES), 0)
            for j0 in range(0, IDX_CHUNK, LANES):
                acc = jnp.zeros((IDX_ROWS, LANES), F32)
                for h in range(IDX_HEADS):
                    d = d_ref[h * tq + r0:h * tq + r0 + IDX_ROWS, j0:j0 + LANES]
                    acc = acc + jnp.maximum(d, 0.0) * wb_ref[h, r0:r0 + IDX_ROWS, :]
                bits = pltpu.bitcast(acc, jnp.int32)
                img = bits ^ ((bits >> 31) & jnp.int32(0x7FFFFFFF))
                col_t = c * IDX_CHUNK + j0 + lax.broadcasted_iota(jnp.int32, (IDX_ROWS, LANES), 1)
                key_ref[c, r0:r0 + IDX_ROWS, j0:j0 + LANES] = jnp.where(col_t <= row_t, img, KEY_MASKED)

    d0_ref, d1_ref = d_refs
    matmul(0, d0_ref)

    def pair(k, carry):
        c = 2 * k
        matmul(c + 1, d1_ref)
        head_sum(c, d0_ref)
        matmul(jnp.minimum(c + 2, n_chunks - 1), d0_ref)
        head_sum(c + 1, d1_ref)
        return carry

    lax.fori_loop(0, n_chunks // 2, pair, 0)

    @pl.when(n_chunks % 2 == 1)
    def _():
        head_sum(n_chunks - 1, d0_ref)
        key_ref[n_chunks] = jnp.full(key_ref.shape[1:], KEY_MASKED, jnp.int32)


def _dsa_select(nk, nk_bias, key_ref, bias_ref):
    chunks = range(nk // IDX_CHUNK)
    n_sel = float(N_SEL)
    groups = list(range(0, key_ref.shape[1], SEARCH_ROWS))

    def count_ge(r0, cand):
        hits = None
        for c in chunks:
            hit = jnp.where(key_ref[c, r0:r0 + SEARCH_ROWS, :] >= cand, 1.0, 0.0)
            hits = hit if hits is None else hits + hit
        return jnp.sum(hits, axis=-1, keepdims=True)

    zero = jnp.zeros((SEARCH_ROWS, 1), jnp.int32)
    thr0 = tuple(jnp.where(count_ge(r0, zero) >= n_sel, jnp.int32(0), jnp.int32(INT_MIN))
                 for r0 in groups)

    def step(b, thrs):
        bit = lax.shift_left(jnp.int32(1), jnp.int32(30) - b)
        out = []
        for r0, thr in zip(groups, thrs):
            cand = thr | bit
            out.append(jnp.where(count_ge(r0, cand) >= n_sel, cand, thr))
        return tuple(out)

    thrs = lax.fori_loop(0, 30, step, thr0, unroll=SEARCH_UNROLL)
    thrs = step(jnp.int32(30), thrs)
    for r0, thr in zip(groups, thrs):
        floor = jnp.maximum(thr, jnp.int32(KEY_MASKED + 1))
        for c in range(nk_bias // IDX_CHUNK):
            sel = key_ref[c, r0:r0 + SEARCH_ROWS, :] >= floor
            bias_ref[r0:r0 + SEARCH_ROWS, c * IDX_CHUNK:(c + 1) * IDX_CHUNK] = jnp.where(sel, 0.0, -jnp.inf)


def _dsa_attend(nk, qa_ref, ka_ref, va_ref, bias_ref, o_ref, oh_ref, lg_refs, p_refs):
    tq = qa_ref.shape[1]
    rep = A_HEADS // A_KV_HEADS

    sub = rep // ATT_SPLIT

    def group(g, carry):
        k = ka_ref[g, 0:nk, :]
        v = va_ref[g, 0:nk, :]
        inv_l = [None] * ATT_SPLIT

        def scores(s):
            q = qa_ref[pl.ds(g * rep + s * sub, sub)].reshape(sub * tq, LANES)
            lg_refs[s][:, 0:nk] = _dot_nt(q, k)

        def softmax(s):
            logits = lg_refs[s][:, 0:nk].reshape(sub, tq, nk) + bias_ref[:, 0:nk][None]
            m = jnp.max(logits, axis=-1, keepdims=True)
            p = jnp.exp(logits - m)
            inv_l[s] = 1.0 / jnp.sum(p, axis=-1, keepdims=True)
            p_refs[s][:, 0:nk] = p.astype(BF16).reshape(sub * tq, nk)

        def values(s):
            o = _dot(p_refs[s][:, 0:nk], v).reshape(sub, tq, LANES) * inv_l[s]
            oh_ref[pl.ds(g * rep + s * sub, sub)] = o.astype(BF16)

        for t in range(ATT_SPLIT + 2):
            if t < ATT_SPLIT:
                scores(t)
            if 0 <= t - 1 < ATT_SPLIT:
                softmax(t - 1)
            if 0 <= t - 2 < ATT_SPLIT:
                values(t - 2)
        return carry

    lax.fori_loop(0, A_KV_HEADS, group, 0)
    for h in range(A_HEADS):
        o_ref[:, h * LANES:(h + 1) * LANES] = oh_ref[h]


def _dsa_kernel(qa_ref, ka_ref, va_ref, qi_ref, ki2_ref, wi_ref, o_ref,
                key_ref, bias_ref, d0_ref, d1_ref, wb_ref, oh_ref, *stage_refs):
    lg_refs, p_refs = stage_refs[:ATT_SPLIT], stage_refs[ATT_SPLIT:]
    i = pl.program_id(1)
    tq = DSA_TQ
    n_free = N_SEL // tq
    variant = i // (KEY_CHUNK // tq)

    for v in range(n_free):
        @pl.when(i == v)
        def _(v=v):
            nk = (v + 1) * tq
            row = i * tq + lax.broadcasted_iota(jnp.int32, (tq, nk), 0)
            col = lax.broadcasted_iota(jnp.int32, (tq, nk), 1)
            bias_ref[:, 0:nk] = jnp.where(col <= row, 0.0, -jnp.inf)
            _dsa_attend(nk, qa_ref, ka_ref, va_ref, bias_ref, o_ref, oh_ref, lg_refs, p_refs)

    @pl.when(i >= n_free)
    def _():
        n_chunks = (i + 1) * (tq // IDX_CHUNK)
        _dsa_index(i, n_chunks, qi_ref, ki2_ref, wi_ref, key_ref, (d0_ref, d1_ref), wb_ref)

    for v in range(SEQ // KEY_CHUNK):
        @pl.when((i >= n_free) & (variant == v))
        def _(v=v):
            nk = (v + 1) * KEY_CHUNK
            _dsa_select(nk, nk, key_ref, bias_ref)
            _dsa_attend(nk, qa_ref, ka_ref, va_ref, bias_ref, o_ref, oh_ref, lg_refs, p_refs)


def _dsa(qa, ka, va, qi, ki2, wi):
    tq = DSA_TQ
    nq = SEQ // tq
    att_rows = A_HEADS // A_KV_HEADS // ATT_SPLIT * tq
    return pl.pallas_call(
        _dsa_kernel,
        grid=(BATCH, nq),
        in_specs=[
            pl.BlockSpec((None, A_HEADS, tq, LANES), lambda b, i: (b, 0, i, 0)),
            pl.BlockSpec((None, A_KV_HEADS, SEQ, LANES), lambda b, i: (b, 0, 0, 0)),
            pl.BlockSpec((None, A_KV_HEADS, SEQ, LANES), lambda b, i: (b, 0, 0, 0)),
            pl.BlockSpec((None, IDX_HEADS, tq, LANES), lambda b, i: (b, 0, i, 0)),
            pl.BlockSpec((None, SEQ, LANES), lambda b, i: (b, 0, 0)),
            pl.BlockSpec((None, tq, LANES), lambda b, i: (b, i, 0)),
        ],
        out_specs=pl.BlockSpec((tq, A_HEADS * A_HEAD_DIM), lambda b, i: (b * nq + i, 0)),
        out_shape=jax.ShapeDtypeStruct((TOKENS, A_HEADS * A_HEAD_DIM), BF16),
        scratch_shapes=[
            pltpu.VMEM((SEQ // IDX_CHUNK, tq, IDX_CHUNK), jnp.int32),
            pltpu.VMEM((tq, SEQ), F32),
            pltpu.VMEM((IDX_HEADS * tq, IDX_CHUNK), F32),
            pltpu.VMEM((IDX_HEADS * tq, IDX_CHUNK), F32),
            pltpu.VMEM((IDX_HEADS, tq, LANES), F32),
            pltpu.VMEM((A_HEADS, tq, LANES), BF16),
        ] + [pltpu.VMEM((att_rows, SEQ), F32)] * ATT_SPLIT + [pltpu.VMEM((att_rows, SEQ), BF16)] * ATT_SPLIT,
        compiler_params=_params("arbitrary", "arbitrary"),
        name="dsa",
    )(qa, ka, va, qi, ki2, wi)


MLA_TQ = 256
MLA_HEADS_PER_STEP = 4
MLA_KEY_CHUNK = 256


def _mla_body(nk, i, q_ref, k_ref, v_ref, o_ref, lg_refs, p_refs):
    n_heads, tq = q_ref.shape[0], q_ref.shape[1]
    n0 = nk - MLA_KEY_CHUNK
    row = i * tq + lax.broadcasted_iota(jnp.int32, (tq, MLA_KEY_CHUNK), 0)
    col = n0 + lax.broadcasted_iota(jnp.int32, (tq, MLA_KEY_CHUNK), 1)
    causal = col <= row
    inv_l = [None] * n_heads

    def scores(h):
        lg_refs[h][:, 0:nk] = _dot_nt(q_ref[h], k_ref[h, 0:nk, :])

    def softmax(h):
        s1 = jnp.where(causal, lg_refs[h][:, n0:nk], -jnp.inf)
        m = jnp.max(s1, axis=-1, keepdims=True)
        if n0:
            s0 = lg_refs[h][:, 0:n0]
            m = jnp.maximum(m, jnp.max(s0, axis=-1, keepdims=True))
        p1 = jnp.exp(s1 - m)
        l = jnp.sum(p1, axis=-1, keepdims=True)
        p_refs[h][:, n0:nk] = p1.astype(BF16)
        if n0:
            p0 = jnp.exp(s0 - m)
            l = l + jnp.sum(p0, axis=-1, keepdims=True)
            p_refs[h][:, 0:n0] = p0.astype(BF16)
        inv_l[h] = 1.0 / l

    def values(h):
        o = _dot(p_refs[h][:, 0:nk], v_ref[h, 0:nk, :]) * inv_l[h]
        o_ref[:, h * M_V:(h + 1) * M_V] = o.astype(BF16)

    for t in range(n_heads + 2):
        if t < n_heads:
            scores(t)
        if 0 <= t - 1 < n_heads:
            softmax(t - 1)
        if 0 <= t - 2 < n_heads:
            values(t - 2)


def _mla_kernel(q_ref, k_ref, v_ref, o_ref, *stage_refs):
    n_heads = q_ref.shape[0]
    lg_refs, p_refs = stage_refs[:n_heads], stage_refs[n_heads:]
    i = pl.program_id(2)
    per_chunk = MLA_KEY_CHUNK // MLA_TQ
    for v in range(SEQ // MLA_KEY_CHUNK):
        @pl.when(i // per_chunk == v)
        def _(v=v):
            _mla_body((v + 1) * MLA_KEY_CHUNK, i, q_ref, k_ref, v_ref, o_ref, lg_refs, p_refs)


def _mla(q, k, v):
    tq, hg = MLA_TQ, MLA_HEADS_PER_STEP
    nq = SEQ // tq
    return pl.pallas_call(
        _mla_kernel,
        grid=(BATCH, M_HEADS // hg, nq),
        in_specs=[
            pl.BlockSpec((None, hg, tq, M_QK_PAD), lambda b, g, i: (b, g, i, 0)),
            pl.BlockSpec((None, hg, SEQ, M_QK_PAD), lambda b, g, i: (b, g, 0, 0)),
            pl.BlockSpec((None, hg, SEQ, M_V), lambda b, g, i: (b, g, 0, 0)),
        ],
        out_specs=pl.BlockSpec((tq, hg * M_V), lambda b, g, i: (b * nq + i, g)),
        out_shape=jax.ShapeDtypeStruct((TOKENS, M_HEADS * M_V), BF16),
        scratch_shapes=[pltpu.VMEM((tq, SEQ), F32)] * hg + [pltpu.VMEM((tq, SEQ), BF16)] * hg,
        compiler_params=_params("arbitrary", "arbitrary", "arbitrary"),
        name="mla",
    )(q, k, v)


def _merge_kernel(oa_ref, ob_ref, ga_ref, gb_ref, wpa_ref, wpb_ref, wo_ref, x_ref, gt_ref,
                  gf_ref, scf_ref, shf_ref, o_ref, hf_ref):
    a = _dot(oa_ref[...], wpa_ref[...])
    b = _dot(ob_ref[...], wpb_ref[...])
    merged = jax.nn.sigmoid(ga_ref[...].astype(F32)) * a + jax.nn.sigmoid(gb_ref[...].astype(F32)) * b
    y = _dot(merged.astype(BF16), wo_ref[...])
    xn = x_ref[...] + gt_ref[...] * y
    o_ref[...] = xn
    ms = jnp.mean(xn * xn, axis=-1, keepdims=True)
    hn = xn * lax.rsqrt(ms + EPS) * gf_ref[...]
    hf_ref[...] = (hn * (1.0 + scf_ref[...]) + shf_ref[...]).astype(BF16)


def _merge(l, oa, ob, proj, wpa, wpb, wo, x, mod, g_ffn):
    tm = 256
    per_b = SEQ // tm

    def resident(shape):
        return pl.BlockSpec((None,) + shape[1:], lambda i: (l, 0, 0), pipeline_mode=pl.Buffered(1))

    return pl.pallas_call(
        _merge_kernel,
        grid=(TOKENS // tm,),
        in_specs=[
            pl.BlockSpec((tm, A_HEADS * A_HEAD_DIM), lambda i: (i, 0)),
            pl.BlockSpec((tm, M_HEADS * M_V), lambda i: (i, 0)),
            pl.BlockSpec((tm, D_MODEL), lambda i: (i, COL_GATE_A // D_MODEL)),
            pl.BlockSpec((tm, D_MODEL), lambda i: (i, COL_GATE_B // D_MODEL)),
            resident(wpa.shape), resident(wpb.shape), resident(wo.shape),
            pl.BlockSpec((tm, D_MODEL), lambda i: (i, 0)),
            _mod_spec(l, 2, per_b),
            _layer_row_spec(l, D_MODEL),
            _mod_spec(l, 4, per_b),
            _mod_spec(l, 3, per_b),
        ],
        out_specs=[pl.BlockSpec((tm, D_MODEL), lambda i: (i, 0))] * 2,
        out_shape=[jax.ShapeDtypeStruct((TOKENS, D_MODEL), F32),
                   jax.ShapeDtypeStruct((TOKENS, D_MODEL), BF16)],
        compiler_params=_params("arbitrary"),
        name="merge",
    )(oa, ob, proj, proj, wpa, wpb, wo, x, mod, g_ffn, mod, mod)


HALO = 8


def _ffn_up_kernel(h_ref, wg_ref, wv_ref, cg_ref, cv_ref, bg_ref, bv_ref,
                   o_ref, wgb_ref, wvb_ref, buf_ref, *, tiles_per_seq):
    i = pl.program_id(1)
    tm, tn = o_ref.shape

    @pl.when(i == 0)
    def _():
        wgb_ref[...] = wg_ref[...].astype(BF16)
        wvb_ref[...] = wv_ref[...].astype(BF16)

    seq_start = (i % tiles_per_seq) == 0

    @pl.when(seq_start)
    def _():
        buf_ref[0:HALO, :] = jnp.zeros((HALO, 2 * tn), F32)

    @pl.when(jnp.logical_not(seq_start))
    def _():
        buf_ref[0:HALO, :] = buf_ref[tm:tm + HALO, :]

    h = h_ref[...]
    buf_ref[HALO:HALO + tm, 0:tn] = _dot(h, wgb_ref[...])
    buf_ref[HALO:HALO + tm, tn:2 * tn] = _dot(h, wvb_ref[...])

    def conv(lo, c_ref, b_ref):
        u0 = buf_ref[HALO:HALO + tm, lo:lo + tn]
        u1 = buf_ref[HALO - 1:HALO - 1 + tm, lo:lo + tn]
        u2 = buf_ref[HALO - 2:HALO - 2 + tm, lo:lo + tn]
        return b_ref[...] + c_ref[0:1, :] * u2 + c_ref[1:2, :] * u1 + c_ref[2:3, :] * u0

    gate = conv(0, cg_ref, bg_ref)
    val = conv(tn, cv_ref, bv_ref)
    o_ref[...] = (gate * jax.nn.sigmoid(gate) * val).astype(BF16)


def _ffn_up(l, h, w_up, w_conv, b_conv):
    tm, tn = 1024, 512
    nj = D_FF // tn
    kern = functools.partial(_ffn_up_kernel, tiles_per_seq=SEQ // tm)
    return pl.pallas_call(
        kern,
        grid=(nj, TOKENS // tm),
        in_specs=[
            pl.BlockSpec((tm, D_MODEL), lambda j, i: (i, 0)),
            pl.BlockSpec((None, D_MODEL, tn), lambda j, i: (l, 0, j)),
            pl.BlockSpec((None, D_MODEL, tn), lambda j, i: (l, 0, nj + j)),
            pl.BlockSpec((None, CONV_W, tn), lambda j, i: (l, 0, j)),
            pl.BlockSpec((None, CONV_W, tn), lambda j, i: (l, 0, nj + j)),
            pl.BlockSpec((None, 1, tn), lambda j, i: (l, 0, j)),
            pl.BlockSpec((None, 1, tn), lambda j, i: (l, 0, nj + j)),
        ],
        out_specs=pl.BlockSpec((tm, tn), lambda j, i: (i, j)),
        out_shape=jax.ShapeDtypeStruct((TOKENS, D_FF), BF16),
        scratch_shapes=[
            pltpu.VMEM((D_MODEL, tn), BF16),
            pltpu.VMEM((D_MODEL, tn), BF16),
            pltpu.VMEM((HALO + tm, 2 * tn), F32),
        ],
        compiler_params=_params("arbitrary", "arbitrary"),
        name="ffn_up",
    )(h, w_up, w_up, w_conv, w_conv, b_conv, b_conv)


def _ffn_down_kernel(a_ref, w_ref, x_ref, gt_ref, o_ref):
    o_ref[...] = x_ref[...] + gt_ref[...] * _dot(a_ref[...], w_ref[...])


def _ffn_down(l, act, w_down, x, mod):
    tm, tn = 1024, 512
    per_b = SEQ // tm
    return pl.pallas_call(
        _ffn_down_kernel,
        grid=(TOKENS // tm, D_MODEL // tn),
        in_specs=[
            pl.BlockSpec((tm, D_FF), lambda i, j: (i, 0)),
            pl.BlockSpec((None, D_FF, tn), lambda i, j: (l, 0, j)),
            pl.BlockSpec((tm, tn), lambda i, j: (i, j)),
            _mod_spec(l, 5, per_b, tn),
        ],
        out_specs=pl.BlockSpec((tm, tn), lambda i, j: (i, j)),
        out_shape=jax.ShapeDtypeStruct((TOKENS, D_MODEL), F32),
        compiler_params=_params("arbitrary", "arbitrary"),
        name="ffn_down",
    )(act, w_down, x, mod)


def _pack_w_in(w_in):
    o = np.cumsum((0,) + (1024, 256, 256, 1024, 64, 16, 512, 256, 64, 2048, 2048))
    w_t = jnp.swapaxes(w_in, 1, 2)
    qa, ka, va, qi, ki, wi, mql, mkvl, mkr, ga, gb = (
        w_t[:, int(o[k]):int(o[k + 1])].astype(BF16) for k in range(11))
    z = lambda n: jnp.zeros((DEPTH, n, D_MODEL), BF16)
    packed = jnp.concatenate(
        [ga, gb, qa, qi, mql, ka, va, mkvl, ki, wi, z(48), mkr, z(64)], axis=1)
    assert packed.shape[1] == IN_PACKED
    return packed


def _pack_w_mq(w_mq_up):
    w = w_mq_up.reshape(DEPTH, M_Q_LORA, M_HEADS, M_QK)
    nope = w[..., :M_NOPE].reshape(DEPTH, M_Q_LORA, M_HEADS * M_NOPE)
    rope = w[..., M_NOPE:].reshape(DEPTH, M_Q_LORA, M_HEADS * M_ROPE)
    return jnp.concatenate([nope, rope], axis=-1).astype(BF16)


def _rope_consts():
    def inv(rot):
        return ROPE_THETA ** (-jnp.arange(0, rot, 2, dtype=F32) / rot)

    layouts = ((A_HEAD_DIM // ROT_FRACTION, LANES), (IDX_DIM // ROT_FRACTION, IDX_DIM), (M_ROPE, M_ROPE))
    freqs = [inv(rot) for rot, _ in layouts]
    used = sum(f.shape[0] for f in freqs)
    inv_c = jnp.concatenate(freqs + [jnp.zeros((LANES - used,), F32)]).reshape(1, LANES)
    zero_lane = LANES - 1

    sgn3 = np.zeros((N_ROPE_TABLES, 1, LANES), np.float32)
    expand = np.zeros((N_ROPE_TABLES, LANES, LANES), np.float32)
    off = 0
    for t, (rot, group) in enumerate(layouts):
        half = rot // 2
        for j in range(LANES):
            g = j % group
            if g < rot:
                expand[t, off + g % half, j] = 1.0
                sgn3[t, 0, j] = -1.0 if g < half else 1.0
            else:
                expand[t, zero_lane, j] = 1.0
        off += half
    expand2 = np.concatenate([expand, expand], axis=1)
    return inv_c, jnp.asarray(sgn3), jnp.asarray(expand2, dtype=BF16)


def kernel(x, c, positions, g_attn, g_ffn, w_ada, b_ada, w_in, g_qa, g_ka, g_mq_lat, w_mq_up,
           g_mkv_lat, w_mkv_up, g_qm, g_km, w_pa, w_pb, w_o, w_up, w_conv, b_conv, w_down):
    pos_col = positions.astype(F32).reshape(TOKENS, 1)
    inv_c, sgn3, expand = _rope_consts()
    tabs_cos, tabs_sin = _rope_tables(pos_col, inv_c, sgn3, expand)
    rot_a, rot_i = A_HEAD_DIM // ROT_FRACTION, IDX_DIM // ROT_FRACTION
    swaps = jnp.asarray(np.stack([_swap_matrix(rot_a // 2, LANES), _swap_matrix(rot_i // 2, IDX_DIM),
                                  _swap_matrix(M_ROPE // 2, M_ROPE)]), dtype=BF16)

    c8 = jnp.pad(c, ((0, 8 - BATCH), (0, 0)))
    mod = _ada(c8, w_ada, b_ada)[:, :BATCH].reshape(DEPTH, BATCH, N_ADA, 1, D_MODEL)

    w_in_p = _pack_w_in(w_in)
    w_mq_p = _pack_w_mq(w_mq_up)
    w_mkv_b = w_mkv_up.astype(BF16)
    w_pa_b, w_pb_b, w_o_b = w_pa.astype(BF16), w_pb.astype(BF16), w_o.astype(BF16)
    w_down_b = w_down.astype(BF16)

    rows = lambda v: v.reshape(DEPTH, 1, -1)
    rope_gain = lambda v: rows(jnp.tile(v[:, M_NOPE:], (1, 2)))
    g_attn_r, g_ffn_r, b_conv_r = rows(g_attn), rows(g_ffn), rows(b_conv)
    g_qa_r, g_ka_r, g_mq_r, g_mkv_r = rows(g_qa), rows(g_ka), rows(g_mq_lat), rows(g_mkv_lat)
    gqn, gqr = rows(g_qm[:, :M_NOPE]), rope_gain(g_qm)
    gkn, gkr = rows(g_km[:, :M_NOPE]), rope_gain(g_km)

    xf = x.reshape(TOKENS, D_MODEL)
    for l in range(DEPTH):
        proj = _in_proj(l, xf, g_attn_r, mod, w_in_p)
        qa, ka, va, qi, ki2, wi, mqn, mkvn = _prep(
            l, proj, tabs_cos, tabs_sin, swaps, g_qa_r, g_ka_r, g_mq_r, g_mkv_r)
        mq, mk, mv = _mla_up(l, mqn, mkvn, proj, w_mq_p, w_mkv_b, tabs_cos, tabs_sin, swaps,
                             gqn, gqr, gkn, gkr)
        o_a = _dsa(qa, ka, va, qi, ki2, wi)
        o_b = _mla(mq, mk, mv)
        xf, h_ffn = _merge(l, o_a, o_b, proj, w_pa_b, w_pb_b, w_o_b, xf, mod, g_ffn_r)
        act = _ffn_up(l, h_ffn, w_up, w_conv, b_conv_r)
        xf = _ffn_down(l, act, w_down_b, xf, mod)
    return xf.reshape(BATCH, SEQ, D_MODEL)
```

```python
import functools

import jax
import jax.numpy as jnp
import numpy as np
from jax import lax
from jax.experimental import pallas as pl
from jax.experimental.pallas import tpu as pltpu

D_MODEL = 2048
BATCH = 4
SEQ = 2048
DEPTH = 4
A_HEADS = 8
A_KV_HEADS = 2
A_HEAD_DIM = 128
IDX_HEADS = 16
IDX_DIM = 64
TOPK_MAX = 256
M_HEADS = 8
M_Q_LORA = 512
M_KV_LORA = 256
M_NOPE = 128
M_ROPE = 64
M_V = 128
D_FF = 5632
CONV_W = 3
ROPE_THETA = 500000.0
ROT_FRACTION = 4
EPS = 1e-6
N_ADA = 6
IDX_W_SCALE = (IDX_HEADS * IDX_DIM) ** -0.5
N_SEL = min(TOPK_MAX, SEQ // 4)
TOKENS = BATCH * SEQ
LANES = 128
M_QK = M_NOPE + M_ROPE
M_QK_PAD = 256

BF16 = jnp.bfloat16
F32 = jnp.float32

COL_GATE_A = 0
COL_GATE_B = 2048
COL_QA = 4096
COL_QI = 5120
COL_MQL = 6144
COL_KV = 6656
COL_MKVL = 7168
COL_KIWI = 7424
COL_MKR = 7552
IN_PACKED = 7680

VMEM_LIMIT = 56 * 1024 * 1024


def _params(*sem):
    return pltpu.CompilerParams(dimension_semantics=sem, vmem_limit_bytes=VMEM_LIMIT)


def _dot(a, b):
    return jnp.dot(a, b, preferred_element_type=F32)


def _dot_nt(a, b):
    return lax.dot_general(a, b, (((1,), (1,)), ((), ())), preferred_element_type=F32)


def _ada_kernel(c_ref, w_ref, b_ref, o_ref):
    c = c_ref[...]
    c_act = (c * jax.nn.sigmoid(c)).astype(BF16)
    o_ref[0] = _dot(c_act, w_ref[0].astype(BF16)) + b_ref[0]


def _ada(c8, w_ada, b_ada):
    tn = 1024
    n = N_ADA * D_MODEL
    return pl.pallas_call(
        _ada_kernel,
        grid=(DEPTH, n // tn),
        in_specs=[
            pl.BlockSpec((8, D_MODEL), lambda l, j: (0, 0)),
            pl.BlockSpec((1, D_MODEL, tn), lambda l, j: (l, 0, j)),
            pl.BlockSpec((1, 1, tn), lambda l, j: (l, 0, j)),
        ],
        out_specs=pl.BlockSpec((1, 8, tn), lambda l, j: (l, 0, j)),
        out_shape=jax.ShapeDtypeStruct((DEPTH, 8, n), F32),
        compiler_params=_params("arbitrary", "arbitrary"),
        name="ada",
    )(c8, w_ada, b_ada.reshape(DEPTH, 1, n))


N_ROPE_TABLES = 3


def _split_dot(t, m2):
    hi = t.astype(BF16)
    lo = (t - hi.astype(F32)).astype(BF16)
    return _dot(jnp.concatenate([hi, lo], axis=1), m2)


def _rope_table_kernel(pos_ref, inv_ref, sgn_ref, exp_ref, cos_ref, sin_ref):
    ang = pos_ref[...] * inv_ref[...]
    c, s = jnp.cos(ang), jnp.sin(ang)
    for t in range(N_ROPE_TABLES):
        cos_ref[t] = _split_dot(c, exp_ref[t])
        sin_ref[t] = _split_dot(s, exp_ref[t]) * sgn_ref[t]


def _rope_tables(pos_col, inv_c, sgn3, expand):
    tm = 512
    n = N_ROPE_TABLES
    return pl.pallas_call(
        _rope_table_kernel,
        grid=(TOKENS // tm,),
        in_specs=[
            pl.BlockSpec((tm, 1), lambda i: (i, 0)),
            pl.BlockSpec((1, LANES), lambda i: (0, 0)),
            pl.BlockSpec((n, 1, LANES), lambda i: (0, 0, 0)),
            pl.BlockSpec((n, 2 * LANES, LANES), lambda i: (0, 0, 0)),
        ],
        out_specs=[pl.BlockSpec((n, tm, LANES), lambda i: (0, i, 0))] * 2,
        out_shape=[jax.ShapeDtypeStruct((n, TOKENS, LANES), F32)] * 2,
        compiler_params=_params("arbitrary"),
        name="rope_tables",
    )(pos_col, inv_c, sgn3, expand)


def _rope(t, cos, sin_signed, swap):
    return t * cos + _split_dot(t, swap) * sin_signed


def _swap_matrix(half, group):
    p = np.zeros((LANES, LANES), np.float32)
    for j in range(LANES):
        g = j % group
        if g < half:
            p[j + half, j] = 1.0
        elif g < 2 * half:
            p[j - half, j] = 1.0
    return np.concatenate([p, p], axis=0)


def _rms(t, g, width):
    ms = jnp.sum(t * t, axis=-1, keepdims=True) * (1.0 / width)
    return t * lax.rsqrt(ms + EPS) * g


NORM_ROWS = 16


def _modnorm_into(h_ref, x_ref, g_ref, sc_ref, sh_ref):
    g, sc1, sh = g_ref[...], 1.0 + sc_ref[...], sh_ref[...]

    def chunk(c, carry):
        rows = pl.ds(pl.multiple_of(c * NORM_ROWS, NORM_ROWS), NORM_ROWS)
        x = x_ref[rows, :]
        ms = jnp.mean(x * x, axis=-1, keepdims=True)
        y = x * lax.rsqrt(ms + EPS) * g
        h_ref[rows, :] = (y * sc1 + sh).astype(BF16)
        return carry

    lax.fori_loop(0, x_ref.shape[0] // NORM_ROWS, chunk, 0, unroll=8)


def _in_proj_kernel(x_ref, g_ref, sc_ref, sh_ref, w_ref, o_ref, h_ref):
    @pl.when(pl.program_id(1) == 0)
    def _():
        _modnorm_into(h_ref, x_ref, g_ref, sc_ref, sh_ref)

    o_ref[...] = _dot_nt(h_ref[...], w_ref[...]).astype(o_ref.dtype)


def _mod_spec(l, k, per_b, width=D_MODEL):
    if width == D_MODEL:
        return pl.BlockSpec((None, None, None, 1, D_MODEL), lambda i, *_: (l, i // per_b, k, 0, 0))
    return pl.BlockSpec((None, None, None, 1, width), lambda i, j: (l, i // per_b, k, 0, j))


def _layer_row_spec(l, width):
    return pl.BlockSpec((None, 1, width), lambda *_: (l, 0, 0))


def _in_proj(l, x, g, mod, w):
    tm, tn = 1024, 1536
    per_b = SEQ // tm
    n = w.shape[1]
    return pl.pallas_call(
        _in_proj_kernel,
        grid=(TOKENS // tm, n // tn),
        in_specs=[
            pl.BlockSpec((tm, D_MODEL), lambda i, j: (i, 0)),
            _layer_row_spec(l, D_MODEL),
            _mod_spec(l, 1, per_b),
            _mod_spec(l, 0, per_b),
            pl.BlockSpec((None, tn, D_MODEL), lambda i, j: (l, j, 0)),
        ],
        out_specs=pl.BlockSpec((tm, tn), lambda i, j: (i, j)),
        out_shape=jax.ShapeDtypeStruct((TOKENS, n), BF16),
        scratch_shapes=[pltpu.VMEM((tm, D_MODEL), BF16)],
        compiler_params=_params("arbitrary", "arbitrary"),
        name="in_proj",
    )(x, g, mod, mod, w)


def _prep_kernel(qa_ref, qi_ref, mql_ref, kv_ref, mkvl_ref, kiwi_ref,
                 ca_ref, sa_ref, ci_ref, si_ref, pa_ref, pi_ref,
                 gqa_ref, gka_ref, gmq_ref, gmkv_ref,
                 qa_o, ka_o, va_o, qi_o, ki2_o, wi_o, mqn_o, mkvn_o):
    tm = qa_ref.shape[0]
    lane = lax.broadcasted_iota(jnp.int32, (tm, LANES), 1)
    ca, sa, pa = ca_ref[0], sa_ref[0], pa_ref[0]
    ci, si, pi = ci_ref[0], si_ref[0], pi_ref[0]
    scale_a = A_HEAD_DIM ** -0.5

    for h in range(A_HEADS):
        t = qa_ref[:, h * LANES:(h + 1) * LANES].astype(F32)
        r = _rope(_rms(t, gqa_ref[...], A_HEAD_DIM), ca, sa, pa)
        qa_o[h] = (r * scale_a).astype(BF16)
    for g in range(A_KV_HEADS):
        t = kv_ref[:, g * LANES:(g + 1) * LANES].astype(F32)
        r = _rope(_rms(t, gka_ref[...], A_HEAD_DIM), ca, sa, pa)
        ka_o[g] = r.astype(BF16)
        va_o[g] = kv_ref[:, (A_KV_HEADS + g) * LANES:(A_KV_HEADS + g + 1) * LANES]
    for p in range(IDX_HEADS // 2):
        t = qi_ref[:, p * LANES:(p + 1) * LANES].astype(F32)
        r = _rope(t, ci, si, pi)
        qi_o[2 * p] = jnp.where(lane < IDX_DIM, r, 0.0).astype(BF16)
        qi_o[2 * p + 1] = jnp.where(lane < IDX_DIM, 0.0, r).astype(BF16)

    kiwi = kiwi_ref[...].astype(F32)
    ki = _rope(kiwi, ci, si, pi)
    ki2_o[...] = jnp.where(lane < IDX_DIM, ki, pltpu.roll(ki, IDX_DIM, 1)).astype(BF16)
    wi_o[...] = kiwi * IDX_W_SCALE

    mqn_o[...] = _rms(mql_ref[...].astype(F32), gmq_ref[...], M_Q_LORA).astype(BF16)
    mkvn_o[...] = _rms(mkvl_ref[...].astype(F32), gmkv_ref[...], M_KV_LORA).astype(BF16)


def _prep(l, proj, tabs_cos, tabs_sin, swaps, g_qa, g_ka, g_mq_lat, g_mkv_lat):
    tm = 1024
    per_b = SEQ // tm

    def swap(t):
        return pl.BlockSpec((1, 2 * LANES, LANES), lambda i: (t, 0, 0))

    def col(width, start):
        idx = start // width
        return pl.BlockSpec((tm, width), lambda i: (i, idx))

    def tab(t):
        return pl.BlockSpec((1, tm, LANES), lambda i: (t, i, 0))

    def vec(width):
        return _layer_row_spec(l, width)

    def heads(n):
        return pl.BlockSpec((None, n, tm, LANES), lambda i: (i // per_b, 0, i % per_b, 0))

    def seq(width):
        return pl.BlockSpec((None, tm, width), lambda i: (i // per_b, i % per_b, 0))

    def tok(width):
        return pl.BlockSpec((tm, width), lambda i: (i, 0))

    return pl.pallas_call(
        _prep_kernel,
        grid=(TOKENS // tm,),
        in_specs=[
            col(1024, COL_QA), col(1024, COL_QI), col(512, COL_MQL), col(512, COL_KV),
            col(256, COL_MKVL), col(128, COL_KIWI),
            tab(0), tab(0), tab(1), tab(1), swap(0), swap(1),
            vec(A_HEAD_DIM), vec(A_HEAD_DIM), vec(M_Q_LORA), vec(M_KV_LORA),
        ],
        out_specs=[
            heads(A_HEADS), heads(A_KV_HEADS), heads(A_KV_HEADS),
            heads(IDX_HEADS), seq(LANES), seq(LANES),
            tok(M_Q_LORA), tok(M_KV_LORA),
        ],
        out_shape=[
            jax.ShapeDtypeStruct((BATCH, A_HEADS, SEQ, LANES), BF16),
            jax.ShapeDtypeStruct((BATCH, A_KV_HEADS, SEQ, LANES), BF16),
            jax.ShapeDtypeStruct((BATCH, A_KV_HEADS, SEQ, LANES), BF16),
            jax.ShapeDtypeStruct((BATCH, IDX_HEADS, SEQ, LANES), BF16),
            jax.ShapeDtypeStruct((BATCH, SEQ, LANES), BF16),
            jax.ShapeDtypeStruct((BATCH, SEQ, LANES), F32),
            jax.ShapeDtypeStruct((TOKENS, M_Q_LORA), BF16),
            jax.ShapeDtypeStruct((TOKENS, M_KV_LORA), BF16),
        ],
        compiler_params=_params("arbitrary"),
        name="prep",
    )(proj, proj, proj, proj, proj, proj, tabs_cos, tabs_sin, tabs_cos, tabs_sin, swaps, swaps,
      g_qa, g_ka, g_mq_lat, g_mkv_lat)


def _mla_up_kernel(mqn_ref, mkvn_ref, mkr_ref, wq_ref, wkv_ref, cm_ref, sm_ref, pm_ref,
                   gqn_ref, gqr_ref, gkn_ref, gkr_ref, q_o, k_o, v_o):
    tm = mqn_ref.shape[0]
    lane = lax.broadcasted_iota(jnp.int32, (tm, LANES), 1)
    low = lane < M_ROPE
    cm, sm, pm = cm_ref[0], sm_ref[0], pm_ref[0]
    scale_m = M_QK ** -0.5
    nope_w = M_HEADS * M_NOPE

    q = _dot(mqn_ref[...], wq_ref[...])
    kv = _dot(mkvn_ref[...], wkv_ref[...])

    for p in range(M_HEADS // 2):
        rp = q[:, nope_w + p * LANES: nope_w + (p + 1) * LANES]
        sq = rp * rp
        s_all = jnp.sum(sq, axis=-1, keepdims=True)
        s_lo = jnp.sum(jnp.where(low, sq, 0.0), axis=-1, keepdims=True)
        for e in range(2):
            h = 2 * p + e
            nope = q[:, h * M_NOPE:(h + 1) * M_NOPE]
            ss = jnp.sum(nope * nope, axis=-1, keepdims=True) + (s_lo if e == 0 else s_all - s_lo)
            rs = lax.rsqrt(ss * (1.0 / M_QK) + EPS)
            roped = _rope(rp * rs * gqr_ref[...], cm, sm, pm)
            if e == 1:
                roped = pltpu.roll(roped, M_ROPE, 1)
            q_o[h, :, 0:LANES] = (nope * rs * gqn_ref[...] * scale_m).astype(BF16)
            q_o[h, :, LANES:2 * LANES] = (jnp.where(low, roped, 0.0) * scale_m).astype(BF16)

    kr = mkr_ref[...].astype(F32)
    kr_ss = jnp.sum(kr * kr, axis=-1, keepdims=True)
    kr_roped = _rope(kr * gkr_ref[...], cm, sm, pm)
    kr_roped = jnp.where(low, kr_roped, 0.0)
    for h in range(M_HEADS):
        nope = kv[:, h * 2 * LANES: h * 2 * LANES + M_NOPE]
        ss = jnp.sum(nope * nope, axis=-1, keepdims=True) + kr_ss
        rs = lax.rsqrt(ss * (1.0 / M_QK) + EPS)
        k_o[h, :, 0:LANES] = (nope * rs * gkn_ref[...]).astype(BF16)
        k_o[h, :, LANES:2 * LANES] = (kr_roped * rs).astype(BF16)
        v_o[h] = kv[:, h * 2 * LANES + M_NOPE:(h + 1) * 2 * LANES].astype(BF16)


def _mla_up(l, mqn, mkvn, proj, wq, wkv, tabs_cos, tabs_sin, swaps, gqn, gqr, gkn, gkr):
    tm = 256
    per_b = SEQ // tm

    def vec():
        return _layer_row_spec(l, LANES)

    def heads(width):
        return pl.BlockSpec((None, M_HEADS, tm, width), lambda i: (i // per_b, 0, i % per_b, 0))

    return pl.pallas_call(
        _mla_up_kernel,
        grid=(TOKENS // tm,),
        in_specs=[
            pl.BlockSpec((tm, M_Q_LORA), lambda i: (i, 0)),
            pl.BlockSpec((tm, M_KV_LORA), lambda i: (i, 0)),
            pl.BlockSpec((tm, LANES), lambda i: (i, COL_MKR // LANES)),
            pl.BlockSpec((None,) + wq.shape[1:], lambda i: (l, 0, 0)),
            pl.BlockSpec((None,) + wkv.shape[1:], lambda i: (l, 0, 0)),
            pl.BlockSpec((1, tm, LANES), lambda i: (2, i, 0)),
            pl.BlockSpec((1, tm, LANES), lambda i: (2, i, 0)),
            pl.BlockSpec((1, 2 * LANES, LANES), lambda i: (2, 0, 0)),
            vec(), vec(), vec(), vec(),
        ],
        out_specs=[heads(M_QK_PAD), heads(M_QK_PAD), heads(M_V)],
        out_shape=[
            jax.ShapeDtypeStruct((BATCH, M_HEADS, SEQ, M_QK_PAD), BF16),
            jax.ShapeDtypeStruct((BATCH, M_HEADS, SEQ, M_QK_PAD), BF16),
            jax.ShapeDtypeStruct((BATCH, M_HEADS, SEQ, M_V), BF16),
        ],
        compiler_params=_params("arbitrary"),
        name="mla_up",
    )(mqn, mkvn, proj, wq, wkv, tabs_cos, tabs_sin, swaps, gqn, gqr, gkn, gkr)


INT_MIN = -(2 ** 31)
KEY_MASKED = INT_MIN

DSA_TQ = 256
KEY_CHUNK = 512
SEARCH_ROWS = 128
SEARCH_UNROLL = 5
IDX_CHUNK = 256
IDX_ROWS = 64
ATT_SPLIT = 4
assert DSA_TQ == IDX_CHUNK and KEY_CHUNK == 2 * IDX_CHUNK


def _dsa_index(i, n_chunks, qi_ref, ki2_ref, wi_ref, key_ref, d_refs, wb_ref):
    tq = qi_ref.shape[1]
    w = wi_ref[...]
    for h in range(IDX_HEADS):
        wb_ref[h] = jnp.broadcast_to(w[:, IDX_DIM + h:IDX_DIM + h + 1], (tq, LANES))

    def matmul(c, d_ref):
        start = pl.multiple_of(c * IDX_CHUNK, IDX_CHUNK)
        q_all = qi_ref[...].reshape(IDX_HEADS * tq, LANES)
        d_ref[...] = _dot_nt(q_all, ki2_ref[pl.ds(start, IDX_CHUNK), :])

    def head_sum(c, d_ref):
        for r0 in range(0, tq, IDX_ROWS):
            row_t = i * tq + r0 + lax.broadcasted_iota(jnp.int32, (IDX_ROWS, LANES), 0)
            for j0 in range(0, IDX_CHUNK, LANES):
                acc = jnp.zeros((IDX_ROWS, LANES), F32)
                for h in range(IDX_HEADS):
                    d = d_ref[h * tq + r0:h * tq + r0 + IDX_ROWS, j0:j0 + LANES]
                    acc = acc + jnp.maximum(d, 0.0) * wb_ref[h, r0:r0 + IDX_ROWS, :]
                bits = pltpu.bitcast(acc, jnp.int32)
                img = bits ^ ((bits >> 31) & jnp.int32(0x7FFFFFFF))
                col_t = c * IDX_CHUNK + j0 + lax.broadcasted_iota(jnp.int32, (IDX_ROWS, LANES), 1)
                key_ref[c, r0:r0 + IDX_ROWS, j0:j0 + LANES] = jnp.where(col_t <= row_t, img, KEY_MASKED)

    d0_ref, d1_ref = d_refs
    matmul(0, d0_ref)

    def pair(k, carry):
        c = 2 * k
        matmul(c + 1, d1_ref)
        head_sum(c, d0_ref)
        matmul(jnp.minimum(c + 2, n_chunks - 1), d0_ref)
        head_sum(c + 1, d1_ref)
        return carry

    lax.fori_loop(0, n_chunks // 2, pair, 0)

    @pl.when(n_chunks % 2 == 1)
    def _():
        head_sum(n_chunks - 1, d0_ref)
        key_ref[n_chunks] = jnp.full(key_ref.shape[1:], KEY_MASKED, jnp.int32)


def _dsa_select(nk, nk_bias, key_ref, bias_ref):
    chunks = range(nk // IDX_CHUNK)
    n_sel = float(N_SEL)
    groups = list(range(0, key_ref.shape[1], SEARCH_ROWS))

    def count_ge(r0, cand):
        hits = None
        for c in chunks:
            hit = jnp.where(key_ref[c, r0:r0 + SEARCH_ROWS, :] >= cand, 1.0, 0.0)
            hits = hit if hits is None else hits + hit
        return jnp.sum(hits, axis=-1, keepdims=True)

    zero = jnp.zeros((SEARCH_ROWS, 1), jnp.int32)
    thr0 = tuple(jnp.where(count_ge(r0, zero) >= n_sel, jnp.int32(0), jnp.int32(INT_MIN))
                 for r0 in groups)

    def step(b, thrs):
        bit = lax.shift_left(jnp.int32(1), jnp.int32(30) - b)
        out = []
        for r0, thr in zip(groups, thrs):
            cand = thr | bit
            out.append(jnp.where(count_ge(r0, cand) >= n_sel, cand, thr))
        return tuple(out)

    thrs = lax.fori_loop(0, 30, step, thr0, unroll=SEARCH_UNROLL)
    thrs = step(jnp.int32(30), thrs)
    for r0, thr in zip(groups, thrs):
        floor = jnp.maximum(thr, jnp.int32(KEY_MASKED + 1))
        for c in range(nk_bias // IDX_CHUNK):
            sel = key_ref[c, r0:r0 + SEARCH_ROWS, :] >= floor
            bias_ref[r0:r0 + SEARCH_ROWS, c * IDX_CHUNK:(c + 1) * IDX_CHUNK] = jnp.where(sel, 0.0, -jnp.inf)


def _dsa_attend(nk, qa_ref, ka_ref, va_ref, bias_ref, o_ref, oh_ref, lg_refs, p_refs):
    tq = qa_ref.shape[1]
    rep = A_HEADS // A_KV_HEADS

    sub = rep // ATT_SPLIT

    def group(g, carry):
        k = ka_ref[g, 0:nk, :]
        v = va_ref[g, 0:nk, :]
        inv_l = [None] * ATT_SPLIT

        def scores(s):
            q = qa_ref[pl.ds(g * rep + s * sub, sub)].reshape(sub * tq, LANES)
            lg_refs[s][:, 0:nk] = _dot_nt(q, k)

        def softmax(s):
            logits = lg_refs[s][:, 0:nk].reshape(sub, tq, nk) + bias_ref[:, 0:nk][None]
            m = jnp.max(logits, axis=-1, keepdims=True)
            p = jnp.exp(logits - m)
            inv_l[s] = 1.0 / jnp.sum(p, axis=-1, keepdims=True)
            p_refs[s][:, 0:nk] = p.astype(BF16).reshape(sub * tq, nk)

        def values(s):
            o = _dot(p_refs[s][:, 0:nk], v).reshape(sub, tq, LANES) * inv_l[s]
            oh_ref[pl.ds(g * rep + s * sub, sub)] = o.astype(BF16)

        for t in range(ATT_SPLIT + 2):
            if t < ATT_SPLIT:
                scores(t)
            if 0 <= t - 1 < ATT_SPLIT:
                softmax(t - 1)
            if 0 <= t - 2 < ATT_SPLIT:
                values(t - 2)
        return carry

    lax.fori_loop(0, A_KV_HEADS, group, 0)
    for h in range(A_HEADS):
        o_ref[:, h * LANES:(h + 1) * LANES] = oh_ref[h]


def _dsa_kernel(qa_ref, ka_ref, va_ref, qi_ref, ki2_ref, wi_ref, o_ref,
                key_ref, bias_ref, d0_ref, d1_ref, wb_ref, oh_ref, *stage_refs):
    lg_refs, p_refs = stage_refs[:ATT_SPLIT], stage_refs[ATT_SPLIT:]
    i = pl.program_id(1)
    tq = DSA_TQ
    n_free = N_SEL // tq
    variant = i // (KEY_CHUNK // tq)

    for v in range(n_free):
        @pl.when(i == v)
        def _(v=v):
            nk = (v + 1) * tq
            row = i * tq + lax.broadcasted_iota(jnp.int32, (tq, nk), 0)
            col = lax.broadcasted_iota(jnp.int32, (tq, nk), 1)
            bias_ref[:, 0:nk] = jnp.where(col <= row, 0.0, -jnp.inf)
            _dsa_attend(nk, qa_ref, ka_ref, va_ref, bias_ref, o_ref, oh_ref, lg_refs, p_refs)

    @pl.when(i >= n_free)
    def _():
        n_chunks = (i + 1) * (tq // IDX_CHUNK)
        _dsa_index(i, n_chunks, qi_ref, ki2_ref, wi_ref, key_ref, (d0_ref, d1_ref), wb_ref)

    for v in range(SEQ // KEY_CHUNK):
        @pl.when((i >= n_free) & (variant == v))
        def _(v=v):
            nk = (v + 1) * KEY_CHUNK
            _dsa_select(nk, nk, key_ref, bias_ref)
            _dsa_attend(nk, qa_ref, ka_ref, va_ref, bias_ref, o_ref, oh_ref, lg_refs, p_refs)


def _dsa(qa, ka, va, qi, ki2, wi):
    tq = DSA_TQ
    nq = SEQ // tq
    att_rows = A_HEADS // A_KV_HEADS // ATT_SPLIT * tq
    return pl.pallas_call(
        _dsa_kernel,
        grid=(BATCH, nq),
        in_specs=[
            pl.BlockSpec((None, A_HEADS, tq, LANES), lambda b, i: (b, 0, i, 0)),
            pl.BlockSpec((None, A_KV_HEADS, SEQ, LANES), lambda b, i: (b, 0, 0, 0)),
            pl.BlockSpec((None, A_KV_HEADS, SEQ, LANES), lambda b, i: (b, 0, 0, 0)),
            pl.BlockSpec((None, IDX_HEADS, tq, LANES), lambda b, i: (b, 0, i, 0)),
            pl.BlockSpec((None, SEQ, LANES), lambda b, i: (b, 0, 0)),
            pl.BlockSpec((None, tq, LANES), lambda b, i: (b, i, 0)),
        ],
        out_specs=pl.BlockSpec((tq, A_HEADS * A_HEAD_DIM), lambda b, i: (b * nq + i, 0)),
        out_shape=jax.ShapeDtypeStruct((TOKENS, A_HEADS * A_HEAD_DIM), BF16),
        scratch_shapes=[
            pltpu.VMEM((SEQ // IDX_CHUNK, tq, IDX_CHUNK), jnp.int32),
            pltpu.VMEM((tq, SEQ), F32),
            pltpu.VMEM((IDX_HEADS * tq, IDX_CHUNK), F32),
            pltpu.VMEM((IDX_HEADS * tq, IDX_CHUNK), F32),
            pltpu.VMEM((IDX_HEADS, tq, LANES), F32),
            pltpu.VMEM((A_HEADS, tq, LANES), BF16),
        ] + [pltpu.VMEM((att_rows, SEQ), F32)] * ATT_SPLIT + [pltpu.VMEM((att_rows, SEQ), BF16)] * ATT_SPLIT,
        compiler_params=_params("arbitrary", "arbitrary"),
        name="dsa",
    )(qa, ka, va, qi, ki2, wi)


MLA_TQ = 256
MLA_HEADS_PER_STEP = 4
MLA_KEY_CHUNK = 256


def _mla_body(nk, i, q_ref, k_ref, v_ref, o_ref, lg_refs, p_refs):
    n_heads, tq = q_ref.shape[0], q_ref.shape[1]
    n0 = nk - MLA_KEY_CHUNK
    row = i * tq + lax.broadcasted_iota(jnp.int32, (tq, MLA_KEY_CHUNK), 0)
    col = n0 + lax.broadcasted_iota(jnp.int32, (tq, MLA_KEY_CHUNK), 1)
    causal = col <= row
    inv_l = [None] * n_heads

    def scores(h):
        lg_refs[h][:, 0:nk] = _dot_nt(q_ref[h], k_ref[h, 0:nk, :])

    def softmax(h):
        s1 = jnp.where(causal, lg_refs[h][:, n0:nk], -jnp.inf)
        m = jnp.max(s1, axis=-1, keepdims=True)
        if n0:
            s0 = lg_refs[h][:, 0:n0]
            m = jnp.maximum(m, jnp.max(s0, axis=-1, keepdims=True))
        p1 = jnp.exp(s1 - m)
        l = jnp.sum(p1, axis=-1, keepdims=True)
        p_refs[h][:, n0:nk] = p1.astype(BF16)
        if n0:
            p0 = jnp.exp(s0 - m)
            l = l + jnp.sum(p0, axis=-1, keepdims=True)
            p_refs[h][:, 0:n0] = p0.astype(BF16)
        inv_l[h] = 1.0 / l

    def values(h):
        o = _dot(p_refs[h][:, 0:nk], v_ref[h, 0:nk, :]) * inv_l[h]
        o_ref[:, h * M_V:(h + 1) * M_V] = o.astype(BF16)

    for t in range(n_heads + 2):
        if t < n_heads:
            scores(t)
        if 0 <= t - 1 < n_heads:
            softmax(t - 1)
        if 0 <= t - 2 < n_heads:
            values(t - 2)


def _mla_kernel(q_ref, k_ref, v_ref, o_ref, *stage_refs):
    n_heads = q_ref.shape[0]
    lg_refs, p_refs = stage_refs[:n_heads], stage_refs[n_heads:]
    i = pl.program_id(2)
    per_chunk = MLA_KEY_CHUNK // MLA_TQ
    for v in range(SEQ // MLA_KEY_CHUNK):
        @pl.when(i // per_chunk == v)
        def _(v=v):
            _mla_body((v + 1) * MLA_KEY_CHUNK, i, q_ref, k_ref, v_ref, o_ref, lg_refs, p_refs)


def _mla(q, k, v):
    tq, hg = MLA_TQ, MLA_HEADS_PER_STEP
    nq = SEQ // tq
    return pl.pallas_call(
        _mla_kernel,
        grid=(BATCH, M_HEADS // hg, nq),
        in_specs=[
            pl.BlockSpec((None, hg, tq, M_QK_PAD), lambda b, g, i: (b, g, i, 0)),
            pl.BlockSpec((None, hg, SEQ, M_QK_PAD), lambda b, g, i: (b, g, 0, 0)),
            pl.BlockSpec((None, hg, SEQ, M_V), lambda b, g, i: (b, g, 0, 0)),
        ],
        out_specs=pl.BlockSpec((tq, hg * M_V), lambda b, g, i: (b * nq + i, g)),
        out_shape=jax.ShapeDtypeStruct((TOKENS, M_HEADS * M_V), BF16),
        scratch_shapes=[pltpu.VMEM((tq, SEQ), F32)] * hg + [pltpu.VMEM((tq, SEQ), BF16)] * hg,
        compiler_params=_params("arbitrary", "arbitrary", "arbitrary"),
        name="mla",
    )(q, k, v)


def _merge_kernel(oa_ref, ob_ref, ga_ref, gb_ref, wpa_ref, wpb_ref, wo_ref, x_ref, gt_ref,
                  gf_ref, scf_ref, shf_ref, o_ref, hf_ref):
    a = _dot(oa_ref[...], wpa_ref[...])
    b = _dot(ob_ref[...], wpb_ref[...])
    merged = jax.nn.sigmoid(ga_ref[...].astype(F32)) * a + jax.nn.sigmoid(gb_ref[...].astype(F32)) * b
    y = _dot(merged.astype(BF16), wo_ref[...])
    xn = x_ref[...] + gt_ref[...] * y
    o_ref[...] = xn
    ms = jnp.mean(xn * xn, axis=-1, keepdims=True)
    hn = xn * lax.rsqrt(ms + EPS) * gf_ref[...]
    hf_ref[...] = (hn * (1.0 + scf_ref[...]) + shf_ref[...]).astype(BF16)


def _merge(l, oa, ob, proj, wpa, wpb, wo, x, mod, g_ffn):
    tm = 256
    per_b = SEQ // tm

    def resident(shape):
        return pl.BlockSpec((None,) + shape[1:], lambda i: (l, 0, 0), pipeline_mode=pl.Buffered(1))

    return pl.pallas_call(
        _merge_kernel,
        grid=(TOKENS // tm,),
        in_specs=[
            pl.BlockSpec((tm, A_HEADS * A_HEAD_DIM), lambda i: (i, 0)),
            pl.BlockSpec((tm, M_HEADS * M_V), lambda i: (i, 0)),
            pl.BlockSpec((tm, D_MODEL), lambda i: (i, COL_GATE_A // D_MODEL)),
            pl.BlockSpec((tm, D_MODEL), lambda i: (i, COL_GATE_B // D_MODEL)),
            resident(wpa.shape), resident(wpb.shape), resident(wo.shape),
            pl.BlockSpec((tm, D_MODEL), lambda i: (i, 0)),
            _mod_spec(l, 2, per_b),
            _layer_row_spec(l, D_MODEL),
            _mod_spec(l, 4, per_b),
            _mod_spec(l, 3, per_b),
        ],
        out_specs=[pl.BlockSpec((tm, D_MODEL), lambda i: (i, 0))] * 2,
        out_shape=[jax.ShapeDtypeStruct((TOKENS, D_MODEL), F32),
                   jax.ShapeDtypeStruct((TOKENS, D_MODEL), BF16)],
        compiler_params=_params("arbitrary"),
        name="merge",
    )(oa, ob, proj, proj, wpa, wpb, wo, x, mod, g_ffn, mod, mod)


HALO = 8


def _ffn_up_kernel(h_ref, wg_ref, wv_ref, cg_ref, cv_ref, bg_ref, bv_ref,
                   o_ref, wgb_ref, wvb_ref, buf_ref, *, tiles_per_seq):
    i = pl.program_id(1)
    tm, tn = o_ref.shape

    @pl.when(i == 0)
    def _():
        wgb_ref[...] = wg_ref[...].astype(BF16)
        wvb_ref[...] = wv_ref[...].astype(BF16)

    seq_start = (i % tiles_per_seq) == 0

    @pl.when(seq_start)
    def _():
        buf_ref[0:HALO, :] = jnp.zeros((HALO, 2 * tn), F32)

    @pl.when(jnp.logical_not(seq_start))
    def _():
        buf_ref[0:HALO, :] = buf_ref[tm:tm + HALO, :]

    h = h_ref[...]
    buf_ref[HALO:HALO + tm, 0:tn] = _dot(h, wgb_ref[...])
    buf_ref[HALO:HALO + tm, tn:2 * tn] = _dot(h, wvb_ref[...])

    def conv(lo, c_ref, b_ref):
        u0 = buf_ref[HALO:HALO + tm, lo:lo + tn]
        u1 = buf_ref[HALO - 1:HALO - 1 + tm, lo:lo + tn]
        u2 = buf_ref[HALO - 2:HALO - 2 + tm, lo:lo + tn]
        return b_ref[...] + c_ref[0:1, :] * u2 + c_ref[1:2, :] * u1 + c_ref[2:3, :] * u0

    gate = conv(0, cg_ref, bg_ref)
    val = conv(tn, cv_ref, bv_ref)
    o_ref[...] = (gate * jax.nn.sigmoid(gate) * val).astype(BF16)


def _ffn_up(l, h, w_up, w_conv, b_conv):
    tm, tn = 1024, 512
    nj = D_FF // tn
    kern = functools.partial(_ffn_up_kernel, tiles_per_seq=SEQ // tm)
    return pl.pallas_call(
        kern,
        grid=(nj, TOKENS // tm),
        in_specs=[
            pl.BlockSpec((tm, D_MODEL), lambda j, i: (i, 0)),
            pl.BlockSpec((None, D_MODEL, tn), lambda j, i: (l, 0, j)),
            pl.BlockSpec((None, D_MODEL, tn), lambda j, i: (l, 0, nj + j)),
            pl.BlockSpec((None, CONV_W, tn), lambda j, i: (l, 0, j)),
            pl.BlockSpec((None, CONV_W, tn), lambda j, i: (l, 0, nj + j)),
            pl.BlockSpec((None, 1, tn), lambda j, i: (l, 0, j)),
            pl.BlockSpec((None, 1, tn), lambda j, i: (l, 0, nj + j)),
        ],
        out_specs=pl.BlockSpec((tm, tn), lambda j, i: (i, j)),
        out_shape=jax.ShapeDtypeStruct((TOKENS, D_FF), BF16),
        scratch_shapes=[
            pltpu.VMEM((D_MODEL, tn), BF16),
            pltpu.VMEM((D_MODEL, tn), BF16),
            pltpu.VMEM((HALO + tm, 2 * tn), F32),
        ],
        compiler_params=_params("arbitrary", "arbitrary"),
        name="ffn_up",
    )(h, w_up, w_up, w_conv, w_conv, b_conv, b_conv)


def _ffn_down_kernel(a_ref, w_ref, x_ref, gt_ref, o_ref):
    o_ref[...] = x_ref[...] + gt_ref[...] * _dot(a_ref[...], w_ref[...])


def _ffn_down(l, act, w_down, x, mod):
    tm, tn = 1024, 512
    per_b = SEQ // tm
    return pl.pallas_call(
        _ffn_down_kernel,
        grid=(TOKENS // tm, D_MODEL // tn),
        in_specs=[
            pl.BlockSpec((tm, D_FF), lambda i, j: (i, 0)),
            pl.BlockSpec((None, D_FF, tn), lambda i, j: (l, 0, j)),
            pl.BlockSpec((tm, tn), lambda i, j: (i, j)),
            _mod_spec(l, 5, per_b, tn),
        ],
        out_specs=pl.BlockSpec((tm, tn), lambda i, j: (i, j)),
        out_shape=jax.ShapeDtypeStruct((TOKENS, D_MODEL), F32),
        compiler_params=_params("arbitrary", "arbitrary"),
        name="ffn_down",
    )(act, w_down, x, mod)


def _pack_w_in(w_in):
    o = np.cumsum((0,) + (1024, 256, 256, 1024, 64, 16, 512, 256, 64, 2048, 2048))
    w_t = jnp.swapaxes(w_in, 1, 2)
    qa, ka, va, qi, ki, wi, mql, mkvl, mkr, ga, gb = (
        w_t[:, int(o[k]):int(o[k + 1])].astype(BF16) for k in range(11))
    z = lambda n: jnp.zeros((DEPTH, n, D_MODEL), BF16)
    packed = jnp.concatenate(
        [ga, gb, qa, qi, mql, ka, va, mkvl, ki, wi, z(48), mkr, z(64)], axis=1)
    assert packed.shape[1] == IN_PACKED
    return packed


def _pack_w_mq(w_mq_up):
    w = w_mq_up.reshape(DEPTH, M_Q_LORA, M_HEADS, M_QK)
    nope = w[..., :M_NOPE].reshape(DEPTH, M_Q_LORA, M_HEADS * M_NOPE)
    rope = w[..., M_NOPE:].reshape(DEPTH, M_Q_LORA, M_HEADS * M_ROPE)
    return jnp.concatenate([nope, rope], axis=-1).astype(BF16)


def _rope_consts():
    def inv(rot):
        return ROPE_THETA ** (-jnp.arange(0, rot, 2, dtype=F32) / rot)

    layouts = ((A_HEAD_DIM // ROT_FRACTION, LANES), (IDX_DIM // ROT_FRACTION, IDX_DIM), (M_ROPE, M_ROPE))
    freqs = [inv(rot) for rot, _ in layouts]
    used = sum(f.shape[0] for f in freqs)
    inv_c = jnp.concatenate(freqs + [jnp.zeros((LANES - used,), F32)]).reshape(1, LANES)
    zero_lane = LANES - 1

    sgn3 = np.zeros((N_ROPE_TABLES, 1, LANES), np.float32)
    expand = np.zeros((N_ROPE_TABLES, LANES, LANES), np.float32)
    off = 0
    for t, (rot, group) in enumerate(layouts):
        half = rot // 2
        for j in range(LANES):
            g = j % group
            if g < rot:
                expand[t, off + g % half, j] = 1.0
                sgn3[t, 0, j] = -1.0 if g < half else 1.0
            else:
                expand[t, zero_lane, j] = 1.0
        off += half
    expand2 = np.concatenate([expand, expand], axis=1)
    return inv_c, jnp.asarray(sgn3), jnp.asarray(expand2, dtype=BF16)


def kernel(x, c, positions, g_attn, g_ffn, w_ada, b_ada, w_in, g_qa, g_ka, g_mq_lat, w_mq_up,
           g_mkv_lat, w_mkv_up, g_qm, g_km, w_pa, w_pb, w_o, w_up, w_conv, b_conv, w_down):
    pos_col = positions.astype(F32).reshape(TOKENS, 1)
    inv_c, sgn3, expand = _rope_consts()
    tabs_cos, tabs_sin = _rope_tables(pos_col, inv_c, sgn3, expand)
    rot_a, rot_i = A_HEAD_DIM // ROT_FRACTION, IDX_DIM // ROT_FRACTION
    swaps = jnp.asarray(np.stack([_swap_matrix(rot_a // 2, LANES), _swap_matrix(rot_i // 2, IDX_DIM),
                                  _swap_matrix(M_ROPE // 2, M_ROPE)]), dtype=BF16)

    c8 = jnp.pad(c, ((0, 8 - BATCH), (0, 0)))
    mod = _ada(c8, w_ada, b_ada)[:, :BATCH].reshape(DEPTH, BATCH, N_ADA, 1, D_MODEL)

    w_in_p = _pack_w_in(w_in)
    w_mq_p = _pack_w_mq(w_mq_up)
    w_mkv_b = w_mkv_up.astype(BF16)
    w_pa_b, w_pb_b, w_o_b = w_pa.astype(BF16), w_pb.astype(BF16), w_o.astype(BF16)
    w_down_b = w_down.astype(BF16)

    rows = lambda v: v.reshape(DEPTH, 1, -1)
    rope_gain = lambda v: rows(jnp.tile(v[:, M_NOPE:], (1, 2)))
    g_attn_r, g_ffn_r, b_conv_r = rows(g_attn), rows(g_ffn), rows(b_conv)
    g_qa_r, g_ka_r, g_mq_r, g_mkv_r = rows(g_qa), rows(g_ka), rows(g_mq_lat), rows(g_mkv_lat)
    gqn, gqr = rows(g_qm[:, :M_NOPE]), rope_gain(g_qm)
    gkn, gkr = rows(g_km[:, :M_NOPE]), rope_gain(g_km)

    xf = x.reshape(TOKENS, D_MODEL)
    for l in range(DEPTH):
        proj = _in_proj(l, xf, g_attn_r, mod, w_in_p)
        qa, ka, va, qi, ki2, wi, mqn, mkvn = _prep(
            l, proj, tabs_cos, tabs_sin, swaps, g_qa_r, g_ka_r, g_mq_r, g_mkv_r)
        mq, mk, mv = _mla_up(l, mqn, mkvn, proj, w_mq_p, w_mkv_b, tabs_cos, tabs_sin, swaps,
                             gqn, gqr, gkn, gkr)
        o_a = _dsa(qa, ka, va, qi, ki2, wi)
        o_b = _mla(mq, mk, mv)
        xf, h_ffn = _merge(l, o_a, o_b, proj, w_pa_b, w_pb_b, w_o_b, xf, mod, g_ffn_r)
        act = _ffn_up(l, h_ffn, w_up, w_conv, b_conv_r)
        xf = _ffn_down(l, act, w_down_b, xf, mod)
    return xf.reshape(BATCH, SEQ, D_MODEL)
```

```python
import functools

import jax
import jax.numpy as jnp
import numpy as np
from jax import lax
from jax.experimental import pallas as pl
from jax.experimental.pallas import tpu as pltpu

D_MODEL = 2048
BATCH = 4
SEQ = 2048
DEPTH = 4
A_HEADS = 8
A_KV_HEADS = 2
A_HEAD_DIM = 128
IDX_HEADS = 16
IDX_DIM = 64
TOPK_MAX = 256
M_HEADS = 8
M_Q_LORA = 512
M_KV_LORA = 256
M_NOPE = 128
M_ROPE = 64
M_V = 128
D_FF = 5632
CONV_W = 3
ROPE_THETA = 500000.0
ROT_FRACTION = 4
EPS = 1e-6
N_ADA = 6
IDX_W_SCALE = (IDX_HEADS * IDX_DIM) ** -0.5
N_SEL = min(TOPK_MAX, SEQ // 4)
TOKENS = BATCH * SEQ
LANES = 128
M_QK = M_NOPE + M_ROPE
M_QK_PAD = 256

BF16 = jnp.bfloat16
F32 = jnp.float32

COL_GATE_A = 0
COL_GATE_B = 2048
COL_QA = 4096
COL_QI = 5120
COL_MQL = 6144
COL_KV = 6656
COL_MKVL = 7168
COL_KIWI = 7424
COL_MKR = 7552
IN_PACKED = 7680

VMEM_LIMIT = 56 * 1024 * 1024


def _params(*sem):
    return pltpu.CompilerParams(dimension_semantics=sem, vmem_limit_bytes=VMEM_LIMIT)


def _dot(a, b):
    return jnp.dot(a, b, preferred_element_type=F32)


def _dot_nt(a, b):
    return lax.dot_general(a, b, (((1,), (1,)), ((), ())), preferred_element_type=F32)


def _ada_kernel(c_ref, w_ref, b_ref, o_ref):
    c = c_ref[...]
    c_act = (c * jax.nn.sigmoid(c)).astype(BF16)
    o_ref[0] = _dot(c_act, w_ref[0].astype(BF16)) + b_ref[0]


def _ada(c8, w_ada, b_ada):
    tn = 2048
    n = N_ADA * D_MODEL
    return pl.pallas_call(
        _ada_kernel,
        grid=(DEPTH, n // tn),
        in_specs=[
            pl.BlockSpec((8, D_MODEL), lambda l, j: (0, 0)),
            pl.BlockSpec((1, D_MODEL, tn), lambda l, j: (l, 0, j)),
            pl.BlockSpec((1, 1, tn), lambda l, j: (l, 0, j)),
        ],
        out_specs=pl.BlockSpec((1, 8, tn), lambda l, j: (l, 0, j)),
        out_shape=jax.ShapeDtypeStruct((DEPTH, 8, n), F32),
        compiler_params=_params("arbitrary", "arbitrary"),
        name="ada",
    )(c8, w_ada, b_ada.reshape(DEPTH, 1, n))


N_ROPE_TABLES = 3


def _split_dot(t, m2):
    hi = t.astype(BF16)
    lo = (t - hi.astype(F32)).astype(BF16)
    return _dot(jnp.concatenate([hi, lo], axis=1), m2)


def _rope_table_kernel(pos_ref, inv_ref, sgn_ref, exp_ref, cos_ref, sin_ref):
    ang = pos_ref[...] * inv_ref[...]
    c, s = jnp.cos(ang), jnp.sin(ang)
    for t in range(N_ROPE_TABLES):
        cos_ref[t] = _split_dot(c, exp_ref[t])
        sin_ref[t] = _split_dot(s, exp_ref[t]) * sgn_ref[t]


def _rope_tables(pos_col, inv_c, sgn3, expand):
    tm = 512
    n = N_ROPE_TABLES
    return pl.pallas_call(
        _rope_table_kernel,
        grid=(TOKENS // tm,),
        in_specs=[
            pl.BlockSpec((tm, 1), lambda i: (i, 0)),
            pl.BlockSpec((1, LANES), lambda i: (0, 0)),
            pl.BlockSpec((n, 1, LANES), lambda i: (0, 0, 0)),
            pl.BlockSpec((n, 2 * LANES, LANES), lambda i: (0, 0, 0)),
        ],
        out_specs=[pl.BlockSpec((n, tm, LANES), lambda i: (0, i, 0))] * 2,
        out_shape=[jax.ShapeDtypeStruct((n, TOKENS, LANES), F32)] * 2,
        compiler_params=_params("arbitrary"),
        name="rope_tables",
    )(pos_col, inv_c, sgn3, expand)


def _rope(t, cos, sin_signed, swap):
    return t * cos + _split_dot(t, swap) * sin_signed


def _swap_matrix(half, group):
    p = np.zeros((LANES, LANES), np.float32)
    for j in range(LANES):
        g = j % group
        if g < half:
            p[j + half, j] = 1.0
        elif g < 2 * half:
            p[j - half, j] = 1.0
    return np.concatenate([p, p], axis=0)


def _rms(t, g, width):
    ms = jnp.sum(t * t, axis=-1, keepdims=True) * (1.0 / width)
    return t * lax.rsqrt(ms + EPS) * g


NORM_ROWS = 16


def _modnorm_into(h_ref, x_ref, g_ref, sc_ref, sh_ref):
    g, sc1, sh = g_ref[...], 1.0 + sc_ref[...], sh_ref[...]

    def chunk(c, carry):
        rows = pl.ds(pl.multiple_of(c * NORM_ROWS, NORM_ROWS), NORM_ROWS)
        x = x_ref[rows, :]
        ms = jnp.mean(x * x, axis=-1, keepdims=True)
        y = x * lax.rsqrt(ms + EPS) * g
        h_ref[rows, :] = (y * sc1 + sh).astype(BF16)
        return carry

    lax.fori_loop(0, x_ref.shape[0] // NORM_ROWS, chunk, 0, unroll=8)


def _in_proj_kernel(x_ref, g_ref, sc_ref, sh_ref, w_ref, o_ref, h_ref):
    @pl.when(pl.program_id(1) == 0)
    def _():
        _modnorm_into(h_ref, x_ref, g_ref, sc_ref, sh_ref)

    o_ref[...] = _dot_nt(h_ref[...], w_ref[...]).astype(o_ref.dtype)


def _mod_spec(l, k, per_b, width=D_MODEL):
    if width == D_MODEL:
        return pl.BlockSpec((None, None, None, 1, D_MODEL), lambda i, *_: (l, i // per_b, k, 0, 0))
    return pl.BlockSpec((None, None, None, 1, width), lambda i, j: (l, i // per_b, k, 0, j))


def _layer_row_spec(l, width):
    return pl.BlockSpec((None, 1, width), lambda *_: (l, 0, 0))


def _in_proj(l, x, g, mod, w):
    tm, tn = 1024, 1536
    per_b = SEQ // tm
    n = w.shape[1]
    return pl.pallas_call(
        _in_proj_kernel,
        grid=(TOKENS // tm, n // tn),
        in_specs=[
            pl.BlockSpec((tm, D_MODEL), lambda i, j: (i, 0)),
            _layer_row_spec(l, D_MODEL),
            _mod_spec(l, 1, per_b),
            _mod_spec(l, 0, per_b),
            pl.BlockSpec((None, tn, D_MODEL), lambda i, j: (l, j, 0)),
        ],
        out_specs=pl.BlockSpec((tm, tn), lambda i, j: (i, j)),
        out_shape=jax.ShapeDtypeStruct((TOKENS, n), BF16),
        scratch_shapes=[pltpu.VMEM((tm, D_MODEL), BF16)],
        compiler_params=_params("arbitrary", "arbitrary"),
        name="in_proj",
    )(x, g, mod, mod, w)


def _prep_kernel(qa_ref, qi_ref, mql_ref, kv_ref, mkvl_ref, kiwi_ref,
                 ca_ref, sa_ref, ci_ref, si_ref, pa_ref, pi_ref,
                 gqa_ref, gka_ref, gmq_ref, gmkv_ref,
                 qa_o, ka_o, va_o, qi_o, ki2_o, wi_o, mqn_o, mkvn_o):
    tm = qa_ref.shape[0]
    lane = lax.broadcasted_iota(jnp.int32, (tm, LANES), 1)
    ca, sa, pa = ca_ref[0], sa_ref[0], pa_ref[0]
    ci, si, pi = ci_ref[0], si_ref[0], pi_ref[0]
    scale_a = A_HEAD_DIM ** -0.5

    for h in range(A_HEADS):
        t = qa_ref[:, h * LANES:(h + 1) * LANES].astype(F32)
        r = _rope(_rms(t, gqa_ref[...], A_HEAD_DIM), ca, sa, pa)
        qa_o[h] = (r * scale_a).astype(BF16)
    for g in range(A_KV_HEADS):
        t = kv_ref[:, g * LANES:(g + 1) * LANES].astype(F32)
        r = _rope(_rms(t, gka_ref[...], A_HEAD_DIM), ca, sa, pa)
        ka_o[g] = r.astype(BF16)
        va_o[g] = kv_ref[:, (A_KV_HEADS + g) * LANES:(A_KV_HEADS + g + 1) * LANES]
    for p in range(IDX_HEADS // 2):
        t = qi_ref[:, p * LANES:(p + 1) * LANES].astype(F32)
        r = _rope(t, ci, si, pi)
        qi_o[2 * p] = jnp.where(lane < IDX_DIM, r, 0.0).astype(BF16)
        qi_o[2 * p + 1] = jnp.where(lane < IDX_DIM, 0.0, r).astype(BF16)

    kiwi = kiwi_ref[...].astype(F32)
    ki = _rope(kiwi, ci, si, pi)
    ki2_o[...] = jnp.where(lane < IDX_DIM, ki, pltpu.roll(ki, IDX_DIM, 1)).astype(BF16)
    wi_o[...] = kiwi * IDX_W_SCALE

    mqn_o[...] = _rms(mql_ref[...].astype(F32), gmq_ref[...], M_Q_LORA).astype(BF16)
    mkvn_o[...] = _rms(mkvl_ref[...].astype(F32), gmkv_ref[...], M_KV_LORA).astype(BF16)


def _prep(l, proj, tabs_cos, tabs_sin, swaps, g_qa, g_ka, g_mq_lat, g_mkv_lat):
    tm = 1024
    per_b = SEQ // tm

    def swap(t):
        return pl.BlockSpec((1, 2 * LANES, LANES), lambda i: (t, 0, 0))

    def col(width, start):
        idx = start // width
        return pl.BlockSpec((tm, width), lambda i: (i, idx))

    def tab(t):
        return pl.BlockSpec((1, tm, LANES), lambda i: (t, i, 0))

    def vec(width):
        return _layer_row_spec(l, width)

    def heads(n):
        return pl.BlockSpec((None, n, tm, LANES), lambda i: (i // per_b, 0, i % per_b, 0))

    def seq(width):
        return pl.BlockSpec((None, tm, width), lambda i: (i // per_b, i % per_b, 0))

    def tok(width):
        return pl.BlockSpec((tm, width), lambda i: (i, 0))

    return pl.pallas_call(
        _prep_kernel,
        grid=(TOKENS // tm,),
        in_specs=[
            col(1024, COL_QA), col(1024, COL_QI), col(512, COL_MQL), col(512, COL_KV),
            col(256, COL_MKVL), col(128, COL_KIWI),
            tab(0), tab(0), tab(1), tab(1), swap(0), swap(1),
            vec(A_HEAD_DIM), vec(A_HEAD_DIM), vec(M_Q_LORA), vec(M_KV_LORA),
        ],
        out_specs=[
            heads(A_HEADS), heads(A_KV_HEADS), heads(A_KV_HEADS),
            heads(IDX_HEADS), seq(LANES), seq(LANES),
            tok(M_Q_LORA), tok(M_KV_LORA),
        ],
        out_shape=[
            jax.ShapeDtypeStruct((BATCH, A_HEADS, SEQ, LANES), BF16),
            jax.ShapeDtypeStruct((BATCH, A_KV_HEADS, SEQ, LANES), BF16),
            jax.ShapeDtypeStruct((BATCH, A_KV_HEADS, SEQ, LANES), BF16),
            jax.ShapeDtypeStruct((BATCH, IDX_HEADS, SEQ, LANES), BF16),
            jax.ShapeDtypeStruct((BATCH, SEQ, LANES), BF16),
            jax.ShapeDtypeStruct((BATCH, SEQ, LANES), F32),
            jax.ShapeDtypeStruct((TOKENS, M_Q_LORA), BF16),
            jax.ShapeDtypeStruct((TOKENS, M_KV_LORA), BF16),
        ],
        compiler_params=_params("arbitrary"),
        name="prep",
    )(proj, proj, proj, proj, proj, proj, tabs_cos, tabs_sin, tabs_cos, tabs_sin, swaps, swaps,
      g_qa, g_ka, g_mq_lat, g_mkv_lat)


def _mla_up_kernel(mqn_ref, mkvn_ref, mkr_ref, wq_ref, wkv_ref, cm_ref, sm_ref, pm_ref,
                   gqn_ref, gqr_ref, gkn_ref, gkr_ref, q_o, k_o, v_o):
    tm = mqn_ref.shape[0]
    lane = lax.broadcasted_iota(jnp.int32, (tm, LANES), 1)
    low = lane < M_ROPE
    cm, sm, pm = cm_ref[0], sm_ref[0], pm_ref[0]
    nope_w = M_HEADS * M_NOPE

    q = _dot(mqn_ref[...], wq_ref[...])
    kv = _dot(mkvn_ref[...], wkv_ref[...])

    for p in range(M_HEADS // 2):
        rp = q[:, nope_w + p * LANES: nope_w + (p + 1) * LANES]
        sq = rp * rp
        s_all = jnp.sum(sq, axis=-1, keepdims=True)
        s_lo = jnp.sum(jnp.where(low, sq, 0.0), axis=-1, keepdims=True)
        for e in range(2):
            h = 2 * p + e
            nope = q[:, h * M_NOPE:(h + 1) * M_NOPE]
            ss = jnp.sum(nope * nope, axis=-1, keepdims=True) + (s_lo if e == 0 else s_all - s_lo)
            rs = lax.rsqrt(ss * (1.0 / M_QK) + EPS)
            roped = _rope(rp * rs * gqr_ref[...], cm, sm, pm)
            if e == 1:
                roped = pltpu.roll(roped, M_ROPE, 1)
            q_o[h, :, 0:LANES] = (nope * rs * gqn_ref[...]).astype(BF16)
            q_o[h, :, LANES:2 * LANES] = jnp.where(low, roped, 0.0).astype(BF16)

    kr = mkr_ref[...].astype(F32)
    kr_ss = jnp.sum(kr * kr, axis=-1, keepdims=True)
    kr_roped = _rope(kr * gkr_ref[...], cm, sm, pm)
    kr_roped = jnp.where(low, kr_roped, 0.0)
    for h in range(M_HEADS):
        nope = kv[:, h * 2 * LANES: h * 2 * LANES + M_NOPE]
        ss = jnp.sum(nope * nope, axis=-1, keepdims=True) + kr_ss
        rs = lax.rsqrt(ss * (1.0 / M_QK) + EPS)
        k_o[h, :, 0:LANES] = (nope * rs * gkn_ref[...]).astype(BF16)
        k_o[h, :, LANES:2 * LANES] = (kr_roped * rs).astype(BF16)
        v_o[h] = kv[:, h * 2 * LANES + M_NOPE:(h + 1) * 2 * LANES].astype(BF16)


def _mla_up(l, mqn, mkvn, proj, wq, wkv, tabs_cos, tabs_sin, swaps, gqn, gqr, gkn, gkr):
    tm = 256
    per_b = SEQ // tm

    def vec():
        return _layer_row_spec(l, LANES)

    def heads(width):
        return pl.BlockSpec((None, M_HEADS, tm, width), lambda i: (i // per_b, 0, i % per_b, 0))

    return pl.pallas_call(
        _mla_up_kernel,
        grid=(TOKENS // tm,),
        in_specs=[
            pl.BlockSpec((tm, M_Q_LORA), lambda i: (i, 0)),
            pl.BlockSpec((tm, M_KV_LORA), lambda i: (i, 0)),
            pl.BlockSpec((tm, LANES), lambda i: (i, COL_MKR // LANES)),
            pl.BlockSpec((None,) + wq.shape[1:], lambda i: (l, 0, 0)),
            pl.BlockSpec((None,) + wkv.shape[1:], lambda i: (l, 0, 0)),
            pl.BlockSpec((1, tm, LANES), lambda i: (2, i, 0)),
            pl.BlockSpec((1, tm, LANES), lambda i: (2, i, 0)),
            pl.BlockSpec((1, 2 * LANES, LANES), lambda i: (2, 0, 0)),
            vec(), vec(), vec(), vec(),
        ],
        out_specs=[heads(M_QK_PAD), heads(M_QK_PAD), heads(M_V)],
        out_shape=[
            jax.ShapeDtypeStruct((BATCH, M_HEADS, SEQ, M_QK_PAD), BF16),
            jax.ShapeDtypeStruct((BATCH, M_HEADS, SEQ, M_QK_PAD), BF16),
            jax.ShapeDtypeStruct((BATCH, M_HEADS, SEQ, M_V), BF16),
        ],
        compiler_params=_params("arbitrary"),
        name="mla_up",
    )(mqn, mkvn, proj, wq, wkv, tabs_cos, tabs_sin, swaps, gqn, gqr, gkn, gkr)


INT_MIN = -(2 ** 31)
KEY_MASKED = INT_MIN

DSA_TQ = 256
KEY_CHUNK = 512
SEARCH_ROWS = 128
SEARCH_UNROLL = 5
IDX_CHUNK = 256
IDX_ROWS = 64
ATT_SPLIT = 4
assert DSA_TQ == IDX_CHUNK and KEY_CHUNK == 2 * IDX_CHUNK


def _dsa_index(i, n_chunks, qi_ref, ki2_ref, wi_ref, key_ref, d_refs, wb_ref):
    tq = qi_ref.shape[1]
    w = wi_ref[...]
    for h in range(IDX_HEADS):
        wb_ref[h] = jnp.broadcast_to(w[:, IDX_DIM + h:IDX_DIM + h + 1], (tq, LANES))

    def matmul(c, d_ref):
        start = pl.multiple_of(c * IDX_CHUNK, IDX_CHUNK)
        q_all = qi_ref[...].reshape(IDX_HEADS * tq, LANES)
        d_ref[...] = _dot_nt(q_all, ki2_ref[pl.ds(start, IDX_CHUNK), :])

    def head_sum(c, d_ref):
        for r0 in range(0, tq, IDX_ROWS):
            row_t = i * tq + r0 + lax.broadcasted_iota(jnp.int32, (IDX_ROWS, LANES), 0)
            for j0 in range(0, IDX_CHUNK, LANES):
                acc = jnp.zeros((IDX_ROWS, LANES), F32)
                for h in range(IDX_HEADS):
                    d = d_ref[h * tq + r0:h * tq + r0 + IDX_ROWS, j0:j0 + LANES]
                    acc = acc + jnp.maximum(d, 0.0) * wb_ref[h, r0:r0 + IDX_ROWS, :]
                bits = pltpu.bitcast(acc, jnp.int32)
                img = bits ^ ((bits >> 31) & jnp.int32(0x7FFFFFFF))
                col_t = c * IDX_CHUNK + j0 + lax.broadcasted_iota(jnp.int32, (IDX_ROWS, LANES), 1)
                key_ref[c, r0:r0 + IDX_ROWS, j0:j0 + LANES] = jnp.where(col_t <= row_t, img, KEY_MASKED)

    d0_ref, d1_ref = d_refs
    matmul(0, d0_ref)

    def pair(k, carry):
        c = 2 * k
        matmul(c + 1, d1_ref)
        head_sum(c, d0_ref)
        matmul(jnp.minimum(c + 2, n_chunks - 1), d0_ref)
        head_sum(c + 1, d1_ref)
        return carry

    lax.fori_loop(0, n_chunks // 2, pair, 0)

    @pl.when(n_chunks % 2 == 1)
    def _():
        head_sum(n_chunks - 1, d0_ref)
        key_ref[n_chunks] = jnp.full(key_ref.shape[1:], KEY_MASKED, jnp.int32)


def _dsa_select(nk, nk_bias, key_ref, bias_ref):
    chunks = range(nk // IDX_CHUNK)
    n_sel = float(N_SEL)
    groups = list(range(0, key_ref.shape[1], SEARCH_ROWS))

    def count_ge(r0, cand):
        hits = None
        for c in chunks:
            hit = jnp.where(key_ref[c, r0:r0 + SEARCH_ROWS, :] >= cand, 1.0, 0.0)
            hits = hit if hits is None else hits + hit
        return jnp.sum(hits, axis=-1, keepdims=True)

    zero = jnp.zeros((SEARCH_ROWS, 1), jnp.int32)
    thr0 = tuple(jnp.where(count_ge(r0, zero) >= n_sel, jnp.int32(0), jnp.int32(INT_MIN))
                 for r0 in groups)

    def step(b, thrs):
        bit = lax.shift_left(jnp.int32(1), jnp.int32(30) - b)
        out = []
        for r0, thr in zip(groups, thrs):
            cand = thr | bit
            out.append(jnp.where(count_ge(r0, cand) >= n_sel, cand, thr))
        return tuple(out)

    thrs = lax.fori_loop(0, 30, step, thr0, unroll=SEARCH_UNROLL)
    thrs = step(jnp.int32(30), thrs)
    for r0, thr in zip(groups, thrs):
        floor = jnp.maximum(thr, jnp.int32(KEY_MASKED + 1))
        for c in range(nk_bias // IDX_CHUNK):
            sel = key_ref[c, r0:r0 + SEARCH_ROWS, :] >= floor
            bias_ref[r0:r0 + SEARCH_ROWS, c * IDX_CHUNK:(c + 1) * IDX_CHUNK] = jnp.where(sel, 0.0, -jnp.inf)


def _dsa_attend(nk, qa_ref, ka_ref, va_ref, bias_ref, o_ref, oh_ref, lg_refs, p_refs):
    tq = qa_ref.shape[1]
    rep = A_HEADS // A_KV_HEADS

    sub = rep // ATT_SPLIT

    def group(g, carry):
        k = ka_ref[g, 0:nk, :]
        v = va_ref[g, 0:nk, :]
        inv_l = [None] * ATT_SPLIT

        def scores(s):
            q = qa_ref[pl.ds(g * rep + s * sub, sub)].reshape(sub * tq, LANES)
            lg_refs[s][:, 0:nk] = _dot_nt(q, k)

        def softmax(s):
            logits = lg_refs[s][:, 0:nk].reshape(sub, tq, nk) + bias_ref[:, 0:nk][None]
            m = jnp.max(logits, axis=-1, keepdims=True)
            p = jnp.exp(logits - m)
            inv_l[s] = 1.0 / jnp.sum(p, axis=-1, keepdims=True)
            p_refs[s][:, 0:nk] = p.astype(BF16).reshape(sub * tq, nk)

        def values(s):
            o = _dot(p_refs[s][:, 0:nk], v).reshape(sub, tq, LANES) * inv_l[s]
            oh_ref[pl.ds(g * rep + s * sub, sub)] = o.astype(BF16)

        for t in range(ATT_SPLIT + 2):
            if t < ATT_SPLIT:
                scores(t)
            if 0 <= t - 1 < ATT_SPLIT:
                softmax(t - 1)
            if 0 <= t - 2 < ATT_SPLIT:
                values(t - 2)
        return carry

    lax.fori_loop(0, A_KV_HEADS, group, 0)
    for h in range(A_HEADS):
        o_ref[:, h * LANES:(h + 1) * LANES] = oh_ref[h]


def _dsa_kernel(qa_ref, ka_ref, va_ref, qi_ref, ki2_ref, wi_ref, o_ref,
                key_ref, bias_ref, d0_ref, d1_ref, wb_ref, oh_ref, *stage_refs):
    lg_refs, p_refs = stage_refs[:ATT_SPLIT], stage_refs[ATT_SPLIT:]
    i = pl.program_id(1)
    tq = DSA_TQ
    n_free = N_SEL // tq
    variant = i // (KEY_CHUNK // tq)

    for v in range(n_free):
        @pl.when(i == v)
        def _(v=v):
            nk = (v + 1) * tq
            row = i * tq + lax.broadcasted_iota(jnp.int32, (tq, nk), 0)
            col = lax.broadcasted_iota(jnp.int32, (tq, nk), 1)
            bias_ref[:, 0:nk] = jnp.where(col <= row, 0.0, -jnp.inf)
            _dsa_attend(nk, qa_ref, ka_ref, va_ref, bias_ref, o_ref, oh_ref, lg_refs, p_refs)

    @pl.when(i >= n_free)
    def _():
        n_chunks = (i + 1) * (tq // IDX_CHUNK)
        _dsa_index(i, n_chunks, qi_ref, ki2_ref, wi_ref, key_ref, (d0_ref, d1_ref), wb_ref)

    for v in range(SEQ // KEY_CHUNK):
        @pl.when((i >= n_free) & (variant == v))
        def _(v=v):
            nk = (v + 1) * KEY_CHUNK
            _dsa_select(nk, nk, key_ref, bias_ref)
            _dsa_attend(nk, qa_ref, ka_ref, va_ref, bias_ref, o_ref, oh_ref, lg_refs, p_refs)


def _dsa(qa, ka, va, qi, ki2, wi):
    tq = DSA_TQ
    nq = SEQ // tq
    att_rows = A_HEADS // A_KV_HEADS // ATT_SPLIT * tq
    return pl.pallas_call(
        _dsa_kernel,
        grid=(BATCH, nq),
        in_specs=[
            pl.BlockSpec((None, A_HEADS, tq, LANES), lambda b, i: (b, 0, i, 0)),
            pl.BlockSpec((None, A_KV_HEADS, SEQ, LANES), lambda b, i: (b, 0, 0, 0)),
            pl.BlockSpec((None, A_KV_HEADS, SEQ, LANES), lambda b, i: (b, 0, 0, 0)),
            pl.BlockSpec((None, IDX_HEADS, tq, LANES), lambda b, i: (b, 0, i, 0)),
            pl.BlockSpec((None, SEQ, LANES), lambda b, i: (b, 0, 0)),
            pl.BlockSpec((None, tq, LANES), lambda b, i: (b, i, 0)),
        ],
        out_specs=pl.BlockSpec((tq, A_HEADS * A_HEAD_DIM), lambda b, i: (b * nq + i, 0)),
        out_shape=jax.ShapeDtypeStruct((TOKENS, A_HEADS * A_HEAD_DIM), BF16),
        scratch_shapes=[
            pltpu.VMEM((SEQ // IDX_CHUNK, tq, IDX_CHUNK), jnp.int32),
            pltpu.VMEM((tq, SEQ), F32),
            pltpu.VMEM((IDX_HEADS * tq, IDX_CHUNK), F32),
            pltpu.VMEM((IDX_HEADS * tq, IDX_CHUNK), F32),
            pltpu.VMEM((IDX_HEADS, tq, LANES), F32),
            pltpu.VMEM((A_HEADS, tq, LANES), BF16),
        ] + [pltpu.VMEM((att_rows, SEQ), F32)] * ATT_SPLIT + [pltpu.VMEM((att_rows, SEQ), BF16)] * ATT_SPLIT,
        compiler_params=_params("arbitrary", "arbitrary"),
        name="dsa",
    )(qa, ka, va, qi, ki2, wi)


MLA_TQ = 256
MLA_HEADS_PER_STEP = 4
MLA_KEY_CHUNK = 256


def _mla_body(nk, i, q_ref, k_ref, v_ref, o_ref, lg_refs, p_refs):
    n_heads, tq = q_ref.shape[0], q_ref.shape[1]
    n0 = nk - MLA_KEY_CHUNK
    row = i * tq + lax.broadcasted_iota(jnp.int32, (tq, MLA_KEY_CHUNK), 0)
    col = n0 + lax.broadcasted_iota(jnp.int32, (tq, MLA_KEY_CHUNK), 1)
    causal = col <= row
    inv_l = [None] * n_heads

    def scores(h):
        lg_refs[h][:, 0:nk] = _dot_nt(q_ref[h], k_ref[h, 0:nk, :])

    def softmax(h):
        s1 = jnp.where(causal, lg_refs[h][:, n0:nk], -jnp.inf)
        m = jnp.max(s1, axis=-1, keepdims=True)
        if n0:
            s0 = lg_refs[h][:, 0:n0]
            m = jnp.maximum(m, jnp.max(s0, axis=-1, keepdims=True))
        p1 = jnp.exp(s1 - m)
        l = jnp.sum(p1, axis=-1, keepdims=True)
        p_refs[h][:, n0:nk] = p1.astype(BF16)
        if n0:
            p0 = jnp.exp(s0 - m)
            l = l + jnp.sum(p0, axis=-1, keepdims=True)
            p_refs[h][:, 0:n0] = p0.astype(BF16)
        inv_l[h] = 1.0 / l

    def values(h):
        o = _dot(p_refs[h][:, 0:nk], v_ref[h, 0:nk, :]) * inv_l[h]
        o_ref[:, h * M_V:(h + 1) * M_V] = o.astype(BF16)

    for t in range(n_heads + 2):
        if t < n_heads:
            scores(t)
        if 0 <= t - 1 < n_heads:
            softmax(t - 1)
        if 0 <= t - 2 < n_heads:
            values(t - 2)


def _mla_kernel(q_ref, k_ref, v_ref, o_ref, *stage_refs):
    n_heads = q_ref.shape[0]
    lg_refs, p_refs = stage_refs[:n_heads], stage_refs[n_heads:]
    i = pl.program_id(2)
    per_chunk = MLA_KEY_CHUNK // MLA_TQ
    for v in range(SEQ // MLA_KEY_CHUNK):
        @pl.when(i // per_chunk == v)
        def _(v=v):
            _mla_body((v + 1) * MLA_KEY_CHUNK, i, q_ref, k_ref, v_ref, o_ref, lg_refs, p_refs)


def _mla(q, k, v):
    tq, hg = MLA_TQ, MLA_HEADS_PER_STEP
    nq = SEQ // tq
    return pl.pallas_call(
        _mla_kernel,
        grid=(BATCH, M_HEADS // hg, nq),
        in_specs=[
            pl.BlockSpec((None, hg, tq, M_QK_PAD), lambda b, g, i: (b, g, i, 0)),
            pl.BlockSpec((None, hg, SEQ, M_QK_PAD), lambda b, g, i: (b, g, 0, 0)),
            pl.BlockSpec((None, hg, SEQ, M_V), lambda b, g, i: (b, g, 0, 0)),
        ],
        out_specs=pl.BlockSpec((tq, hg * M_V), lambda b, g, i: (b * nq + i, g)),
        out_shape=jax.ShapeDtypeStruct((TOKENS, M_HEADS * M_V), BF16),
        scratch_shapes=[pltpu.VMEM((tq, SEQ), F32)] * hg + [pltpu.VMEM((tq, SEQ), BF16)] * hg,
        compiler_params=_params("arbitrary", "arbitrary", "arbitrary"),
        name="mla",
    )(q, k, v)


def _merge_kernel(oa_ref, ob_ref, ga_ref, gb_ref, wpa_ref, wpb_ref, wo_ref, x_ref, gt_ref,
                  gf_ref, scf_ref, shf_ref, o_ref, hf_ref):
    a = _dot(oa_ref[...], wpa_ref[...])
    b = _dot(ob_ref[...], wpb_ref[...])
    merged = jax.nn.sigmoid(ga_ref[...].astype(F32)) * a + jax.nn.sigmoid(gb_ref[...].astype(F32)) * b
    y = _dot(merged.astype(BF16), wo_ref[...])
    xn = x_ref[...] + gt_ref[...] * y
    o_ref[...] = xn
    ms = jnp.mean(xn * xn, axis=-1, keepdims=True)
    hn = xn * lax.rsqrt(ms + EPS) * gf_ref[...]
    hf_ref[...] = (hn * (1.0 + scf_ref[...]) + shf_ref[...]).astype(BF16)


def _merge(l, oa, ob, proj, wpa, wpb, wo, x, mod, g_ffn):
    tm = 256
    per_b = SEQ // tm

    def resident(shape):
        return pl.BlockSpec((None,) + shape[1:], lambda i: (l, 0, 0), pipeline_mode=pl.Buffered(1))

    return pl.pallas_call(
        _merge_kernel,
        grid=(TOKENS // tm,),
        in_specs=[
            pl.BlockSpec((tm, A_HEADS * A_HEAD_DIM), lambda i: (i, 0)),
            pl.BlockSpec((tm, M_HEADS * M_V), lambda i: (i, 0)),
            pl.BlockSpec((tm, D_MODEL), lambda i: (i, COL_GATE_A // D_MODEL)),
            pl.BlockSpec((tm, D_MODEL), lambda i: (i, COL_GATE_B // D_MODEL)),
            resident(wpa.shape), resident(wpb.shape), resident(wo.shape),
            pl.BlockSpec((tm, D_MODEL), lambda i: (i, 0)),
            _mod_spec(l, 2, per_b),
            _layer_row_spec(l, D_MODEL),
            _mod_spec(l, 4, per_b),
            _mod_spec(l, 3, per_b),
        ],
        out_specs=[pl.BlockSpec((tm, D_MODEL), lambda i: (i, 0))] * 2,
        out_shape=[jax.ShapeDtypeStruct((TOKENS, D_MODEL), F32),
                   jax.ShapeDtypeStruct((TOKENS, D_MODEL), BF16)],
        compiler_params=_params("arbitrary"),
        name="merge",
    )(oa, ob, proj, proj, wpa, wpb, wo, x, mod, g_ffn, mod, mod)


HALO = 8


def _ffn_up_kernel(h_ref, wg_ref, wv_ref, cg_ref, cv_ref, bg_ref, bv_ref,
                   o_ref, wgb_ref, wvb_ref, buf_ref, *, tiles_per_seq):
    i = pl.program_id(1)
    tm, tn = o_ref.shape

    @pl.when(i == 0)
    def _():
        wgb_ref[...] = wg_ref[...].astype(BF16)
        wvb_ref[...] = wv_ref[...].astype(BF16)

    seq_start = (i % tiles_per_seq) == 0

    @pl.when(seq_start)
    def _():
        buf_ref[0:HALO, :] = jnp.zeros((HALO, 2 * tn), F32)

    @pl.when(jnp.logical_not(seq_start))
    def _():
        buf_ref[0:HALO, :] = buf_ref[tm:tm + HALO, :]

    h = h_ref[...]
    buf_ref[HALO:HALO + tm, 0:tn] = _dot(h, wgb_ref[...])
    buf_ref[HALO:HALO + tm, tn:2 * tn] = _dot(h, wvb_ref[...])

    def conv(lo, c_ref, b_ref):
        u0 = buf_ref[HALO:HALO + tm, lo:lo + tn]
        u1 = buf_ref[HALO - 1:HALO - 1 + tm, lo:lo + tn]
        u2 = buf_ref[HALO - 2:HALO - 2 + tm, lo:lo + tn]
        return b_ref[...] + c_ref[0:1, :] * u2 + c_ref[1:2, :] * u1 + c_ref[2:3, :] * u0

    gate = conv(0, cg_ref, bg_ref)
    val = conv(tn, cv_ref, bv_ref)
    o_ref[...] = (gate * jax.nn.sigmoid(gate) * val).astype(BF16)


def _ffn_up(l, h, w_up, w_conv, b_conv):
    tm, tn = 1024, 512
    nj = D_FF // tn
    kern = functools.partial(_ffn_up_kernel, tiles_per_seq=SEQ // tm)
    return pl.pallas_call(
        kern,
        grid=(nj, TOKENS // tm),
        in_specs=[
            pl.BlockSpec((tm, D_MODEL), lambda j, i: (i, 0)),
            pl.BlockSpec((None, D_MODEL, tn), lambda j, i: (l, 0, j)),
            pl.BlockSpec((None, D_MODEL, tn), lambda j, i: (l, 0, nj + j)),
            pl.BlockSpec((None, CONV_W, tn), lambda j, i: (l, 0, j)),
            pl.BlockSpec((None, CONV_W, tn), lambda j, i: (l, 0, nj + j)),
            pl.BlockSpec((None, 1, tn), lambda j, i: (l, 0, j)),
            pl.BlockSpec((None, 1, tn), lambda j, i: (l, 0, nj + j)),
        ],
        out_specs=pl.BlockSpec((tm, tn), lambda j, i: (i, j)),
        out_shape=jax.ShapeDtypeStruct((TOKENS, D_FF), BF16),
        scratch_shapes=[
            pltpu.VMEM((D_MODEL, tn), BF16),
            pltpu.VMEM((D_MODEL, tn), BF16),
            pltpu.VMEM((HALO + tm, 2 * tn), F32),
        ],
        compiler_params=_params("arbitrary", "arbitrary"),
        name="ffn_up",
    )(h, w_up, w_up, w_conv, w_conv, b_conv, b_conv)


def _ffn_down_kernel(a_ref, w_ref, x_ref, gt_ref, o_ref):
    o_ref[...] = x_ref[...] + gt_ref[...] * _dot(a_ref[...], w_ref[...])


def _ffn_down(l, act, w_down, x, mod):
    tm, tn = 1024, 512
    per_b = SEQ // tm
    return pl.pallas_call(
        _ffn_down_kernel,
        grid=(TOKENS // tm, D_MODEL // tn),
        in_specs=[
            pl.BlockSpec((tm, D_FF), lambda i, j: (i, 0)),
            pl.BlockSpec((None, D_FF, tn), lambda i, j: (l, 0, j)),
            pl.BlockSpec((tm, tn), lambda i, j: (i, j)),
            _mod_spec(l, 5, per_b, tn),
        ],
        out_specs=pl.BlockSpec((tm, tn), lambda i, j: (i, j)),
        out_shape=jax.ShapeDtypeStruct((TOKENS, D_MODEL), F32),
        compiler_params=_params("arbitrary", "arbitrary"),
        name="ffn_down",
    )(act, w_down, x, mod)


def _pack_w_in(w_in):
    o = np.cumsum((0,) + (1024, 256, 256, 1024, 64, 16, 512, 256, 64, 2048, 2048))
    w_t = jnp.swapaxes(w_in, 1, 2)
    qa, ka, va, qi, ki, wi, mql, mkvl, mkr, ga, gb = (
        w_t[:, int(o[k]):int(o[k + 1])].astype(BF16) for k in range(11))
    z = lambda n: jnp.zeros((DEPTH, n, D_MODEL), BF16)
    packed = jnp.concatenate(
        [ga, gb, qa, qi, mql, ka, va, mkvl, ki, wi, z(48), mkr, z(64)], axis=1)
    assert packed.shape[1] == IN_PACKED
    return packed


def _pack_w_mq(w_mq_up):
    w = w_mq_up.reshape(DEPTH, M_Q_LORA, M_HEADS, M_QK)
    nope = w[..., :M_NOPE].reshape(DEPTH, M_Q_LORA, M_HEADS * M_NOPE)
    rope = w[..., M_NOPE:].reshape(DEPTH, M_Q_LORA, M_HEADS * M_ROPE)
    return jnp.concatenate([nope, rope], axis=-1).astype(BF16)


def _rope_consts():
    def inv(rot):
        return ROPE_THETA ** (-jnp.arange(0, rot, 2, dtype=F32) / rot)

    layouts = ((A_HEAD_DIM // ROT_FRACTION, LANES), (IDX_DIM // ROT_FRACTION, IDX_DIM), (M_ROPE, M_ROPE))
    freqs = [inv(rot) for rot, _ in layouts]
    used = sum(f.shape[0] for f in freqs)
    inv_c = jnp.concatenate(freqs + [jnp.zeros((LANES - used,), F32)]).reshape(1, LANES)
    zero_lane = LANES - 1

    sgn3 = np.zeros((N_ROPE_TABLES, 1, LANES), np.float32)
    expand = np.zeros((N_ROPE_TABLES, LANES, LANES), np.float32)
    off = 0
    for t, (rot, group) in enumerate(layouts):
        half = rot // 2
        for j in range(LANES):
            g = j % group
            if g < rot:
                expand[t, off + g % half, j] = 1.0
                sgn3[t, 0, j] = -1.0 if g < half else 1.0
            else:
                expand[t, zero_lane, j] = 1.0
        off += half
    expand2 = np.concatenate([expand, expand], axis=1)
    return inv_c, jnp.asarray(sgn3), jnp.asarray(expand2, dtype=BF16)


def kernel(x, c, positions, g_attn, g_ffn, w_ada, b_ada, w_in, g_qa, g_ka, g_mq_lat, w_mq_up,
           g_mkv_lat, w_mkv_up, g_qm, g_km, w_pa, w_pb, w_o, w_up, w_conv, b_conv, w_down):
    pos_col = positions.astype(F32).reshape(TOKENS, 1)
    inv_c, sgn3, expand = _rope_consts()
    tabs_cos, tabs_sin = _rope_tables(pos_col, inv_c, sgn3, expand)
    rot_a, rot_i = A_HEAD_DIM // ROT_FRACTION, IDX_DIM // ROT_FRACTION
    swaps = jnp.asarray(np.stack([_swap_matrix(rot_a // 2, LANES), _swap_matrix(rot_i // 2, IDX_DIM),
                                  _swap_matrix(M_ROPE // 2, M_ROPE)]), dtype=BF16)

    c8 = jnp.pad(c, ((0, 8 - BATCH), (0, 0)))
    mod = _ada(c8, w_ada, b_ada)[:, :BATCH].reshape(DEPTH, BATCH, N_ADA, 1, D_MODEL)

    w_in_p = _pack_w_in(w_in)
    w_mq_p = _pack_w_mq(w_mq_up)
    w_mkv_b = w_mkv_up.astype(BF16)
    w_pa_b, w_pb_b, w_o_b = w_pa.astype(BF16), w_pb.astype(BF16), w_o.astype(BF16)
    w_down_b = w_down.astype(BF16)

    rows = lambda v: v.reshape(DEPTH, 1, -1)
    rope_gain = lambda v: rows(jnp.tile(v[:, M_NOPE:], (1, 2)))
    g_attn_r, g_ffn_r, b_conv_r = rows(g_attn), rows(g_ffn), rows(b_conv)
    g_qa_r, g_ka_r, g_mq_r, g_mkv_r = rows(g_qa), rows(g_ka), rows(g_mq_lat), rows(g_mkv_lat)
    g_qm_s = g_qm * (M_QK ** -0.5)
    gqn, gqr = rows(g_qm_s[:, :M_NOPE]), rope_gain(g_qm_s)
    gkn, gkr = rows(g_km[:, :M_NOPE]), rope_gain(g_km)

    xf = x.reshape(TOKENS, D_MODEL)
    for l in range(DEPTH):
        proj = _in_proj(l, xf, g_attn_r, mod, w_in_p)
        qa, ka, va, qi, ki2, wi, mqn, mkvn = _prep(
            l, proj, tabs_cos, tabs_sin, swaps, g_qa_r, g_ka_r, g_mq_r, g_mkv_r)
        mq, mk, mv = _mla_up(l, mqn, mkvn, proj, w_mq_p, w_mkv_b, tabs_cos, tabs_sin, swaps,
                             gqn, gqr, gkn, gkr)
        o_a = _dsa(qa, ka, va, qi, ki2, wi)
        o_b = _mla(mq, mk, mv)
        xf, h_ffn = _merge(l, o_a, o_b, proj, w_pa_b, w_pb_b, w_o_b, xf, mod, g_ffn_r)
        act = _ffn_up(l, h_ffn, w_up, w_conv, b_conv_r)
        xf = _ffn_down(l, act, w_down_b, xf, mod)
    return xf.reshape(BATCH, SEQ, D_MODEL)
```

```python
import functools

import jax
import jax.numpy as jnp
import numpy as np
from jax import lax
from jax.experimental import pallas as pl
from jax.experimental.pallas import tpu as pltpu

D_MODEL = 2048
BATCH = 4
SEQ = 2048
DEPTH = 4
A_HEADS = 8
A_KV_HEADS = 2
A_HEAD_DIM = 128
IDX_HEADS = 16
IDX_DIM = 64
TOPK_MAX = 256
M_HEADS = 8
M_Q_LORA = 512
M_KV_LORA = 256
M_NOPE = 128
M_ROPE = 64
M_V = 128
D_FF = 5632
CONV_W = 3
ROPE_THETA = 500000.0
ROT_FRACTION = 4
EPS = 1e-6
N_ADA = 6
IDX_W_SCALE = (IDX_HEADS * IDX_DIM) ** -0.5
N_SEL = min(TOPK_MAX, SEQ // 4)
TOKENS = BATCH * SEQ
LANES = 128
M_QK = M_NOPE + M_ROPE
M_QK_PAD = 256

BF16 = jnp.bfloat16
F32 = jnp.float32

COL_GATE_A = 0
COL_GATE_B = 2048
COL_QA = 4096
COL_QI = 5120
COL_MQL = 6144
COL_KV = 6656
COL_MKVL = 7168
COL_KIWI = 7424
COL_MKR = 7552
IN_PACKED = 7680

VMEM_LIMIT = 56 * 1024 * 1024


def _params(*sem):
    return pltpu.CompilerParams(dimension_semantics=sem, vmem_limit_bytes=VMEM_LIMIT)


def _dot(a, b):
    return jnp.dot(a, b, preferred_element_type=F32)


def _dot_nt(a, b):
    return lax.dot_general(a, b, (((1,), (1,)), ((), ())), preferred_element_type=F32)


def _ada_kernel(c_ref, w_ref, b_ref, o_ref):
    c = c_ref[...]
    c_act = (c * jax.nn.sigmoid(c)).astype(BF16)
    o_ref[0] = _dot(c_act, w_ref[0].astype(BF16)) + b_ref[0]


def _ada(c8, w_ada, b_ada):
    tn = 2048
    n = N_ADA * D_MODEL
    return pl.pallas_call(
        _ada_kernel,
        grid=(DEPTH, n // tn),
        in_specs=[
            pl.BlockSpec((8, D_MODEL), lambda l, j: (0, 0)),
            pl.BlockSpec((1, D_MODEL, tn), lambda l, j: (l, 0, j)),
            pl.BlockSpec((1, 1, tn), lambda l, j: (l, 0, j)),
        ],
        out_specs=pl.BlockSpec((1, 8, tn), lambda l, j: (l, 0, j)),
        out_shape=jax.ShapeDtypeStruct((DEPTH, 8, n), F32),
        compiler_params=_params("arbitrary", "arbitrary"),
        name="ada",
    )(c8, w_ada, b_ada.reshape(DEPTH, 1, n))


N_ROPE_TABLES = 3


def _split_dot(t, m2):
    hi = t.astype(BF16)
    lo = (t - hi.astype(F32)).astype(BF16)
    return _dot(jnp.concatenate([hi, lo], axis=1), m2)


def _rope_table_kernel(pos_ref, inv_ref, sgn_ref, exp_ref, cos_ref, sin_ref):
    ang = pos_ref[...] * inv_ref[...]
    c, s = jnp.cos(ang), jnp.sin(ang)
    for t in range(N_ROPE_TABLES):
        cos_ref[t] = _split_dot(c, exp_ref[t])
        sin_ref[t] = _split_dot(s, exp_ref[t]) * sgn_ref[t]


def _rope_tables(pos_col, inv_c, sgn3, expand):
    tm = 512
    n = N_ROPE_TABLES
    return pl.pallas_call(
        _rope_table_kernel,
        grid=(TOKENS // tm,),
        in_specs=[
            pl.BlockSpec((tm, 1), lambda i: (i, 0)),
            pl.BlockSpec((1, LANES), lambda i: (0, 0)),
            pl.BlockSpec((n, 1, LANES), lambda i: (0, 0, 0)),
            pl.BlockSpec((n, 2 * LANES, LANES), lambda i: (0, 0, 0)),
        ],
        out_specs=[pl.BlockSpec((n, tm, LANES), lambda i: (0, i, 0))] * 2,
        out_shape=[jax.ShapeDtypeStruct((n, TOKENS, LANES), F32)] * 2,
        compiler_params=_params("arbitrary"),
        name="rope_tables",
    )(pos_col, inv_c, sgn3, expand)


def _rope(t, cos, sin_signed, swap):
    return t * cos + _split_dot(t, swap) * sin_signed


def _swap_matrix(half, group):
    p = np.zeros((LANES, LANES), np.float32)
    for j in range(LANES):
        g = j % group
        if g < half:
            p[j + half, j] = 1.0
        elif g < 2 * half:
            p[j - half, j] = 1.0
    return np.concatenate([p, p], axis=0)


def _rms(t, g, width):
    ms = jnp.sum(t * t, axis=-1, keepdims=True) * (1.0 / width)
    return t * lax.rsqrt(ms + EPS) * g


NORM_ROWS = 16


def _modnorm_into(h_ref, x_ref, g_ref, sc_ref, sh_ref):
    g, sc1, sh = g_ref[...], 1.0 + sc_ref[...], sh_ref[...]

    def chunk(c, carry):
        rows = pl.ds(pl.multiple_of(c * NORM_ROWS, NORM_ROWS), NORM_ROWS)
        x = x_ref[rows, :]
        ms = jnp.mean(x * x, axis=-1, keepdims=True)
        y = x * lax.rsqrt(ms + EPS) * g
        h_ref[rows, :] = (y * sc1 + sh).astype(BF16)
        return carry

    lax.fori_loop(0, x_ref.shape[0] // NORM_ROWS, chunk, 0, unroll=8)


def _in_proj_kernel(x_ref, g_ref, sc_ref, sh_ref, w_ref, o_ref, h_ref):
    @pl.when(pl.program_id(1) == 0)
    def _():
        _modnorm_into(h_ref, x_ref, g_ref, sc_ref, sh_ref)

    o_ref[...] = _dot_nt(h_ref[...], w_ref[...]).astype(o_ref.dtype)


def _mod_spec(l, k, per_b, width=D_MODEL):
    if width == D_MODEL:
        return pl.BlockSpec((None, None, None, 1, D_MODEL), lambda i, *_: (l, i // per_b, k, 0, 0))
    return pl.BlockSpec((None, None, None, 1, width), lambda i, j: (l, i // per_b, k, 0, j))


def _layer_row_spec(l, width):
    return pl.BlockSpec((None, 1, width), lambda *_: (l, 0, 0))


def _in_proj(l, x, g, mod, w):
    tm, tn = 1024, 1536
    per_b = SEQ // tm
    n = w.shape[1]
    return pl.pallas_call(
        _in_proj_kernel,
        grid=(TOKENS // tm, n // tn),
        in_specs=[
            pl.BlockSpec((tm, D_MODEL), lambda i, j: (i, 0)),
            _layer_row_spec(l, D_MODEL),
            _mod_spec(l, 1, per_b),
            _mod_spec(l, 0, per_b),
            pl.BlockSpec((None, tn, D_MODEL), lambda i, j: (l, j, 0)),
        ],
        out_specs=pl.BlockSpec((tm, tn), lambda i, j: (i, j)),
        out_shape=jax.ShapeDtypeStruct((TOKENS, n), BF16),
        scratch_shapes=[pltpu.VMEM((tm, D_MODEL), BF16)],
        compiler_params=_params("arbitrary", "arbitrary"),
        name="in_proj",
    )(x, g, mod, mod, w)


def _prep_kernel(qa_ref, qi_ref, mql_ref, kv_ref, mkvl_ref, kiwi_ref,
                 ca_ref, sa_ref, ci_ref, si_ref, pa_ref, pi_ref,
                 gqa_ref, gka_ref, gmq_ref, gmkv_ref,
                 qa_o, ka_o, va_o, qi_o, ki2_o, wi_o, mqn_o, mkvn_o):
    tm = qa_ref.shape[0]
    lane = lax.broadcasted_iota(jnp.int32, (tm, LANES), 1)
    ca, sa, pa = ca_ref[0], sa_ref[0], pa_ref[0]
    ci, si, pi = ci_ref[0], si_ref[0], pi_ref[0]
    scale_a = A_HEAD_DIM ** -0.5

    for h in range(A_HEADS):
        t = qa_ref[:, h * LANES:(h + 1) * LANES].astype(F32)
        r = _rope(_rms(t, gqa_ref[...], A_HEAD_DIM), ca, sa, pa)
        qa_o[h] = (r * scale_a).astype(BF16)
    for g in range(A_KV_HEADS):
        t = kv_ref[:, g * LANES:(g + 1) * LANES].astype(F32)
        r = _rope(_rms(t, gka_ref[...], A_HEAD_DIM), ca, sa, pa)
        ka_o[g] = r.astype(BF16)
        va_o[g] = kv_ref[:, (A_KV_HEADS + g) * LANES:(A_KV_HEADS + g + 1) * LANES]
    for p in range(IDX_HEADS // 2):
        t = qi_ref[:, p * LANES:(p + 1) * LANES].astype(F32)
        r = _rope(t, ci, si, pi)
        qi_o[2 * p] = jnp.where(lane < IDX_DIM, r, 0.0).astype(BF16)
        qi_o[2 * p + 1] = jnp.where(lane < IDX_DIM, 0.0, r).astype(BF16)

    kiwi = kiwi_ref[...].astype(F32)
    ki = _rope(kiwi, ci, si, pi)
    ki2_o[...] = jnp.where(lane < IDX_DIM, ki, pltpu.roll(ki, IDX_DIM, 1)).astype(BF16)
    wi_o[...] = kiwi * IDX_W_SCALE

    mqn_o[...] = _rms(mql_ref[...].astype(F32), gmq_ref[...], M_Q_LORA).astype(BF16)
    mkvn_o[...] = _rms(mkvl_ref[...].astype(F32), gmkv_ref[...], M_KV_LORA).astype(BF16)


def _prep(l, proj, tabs_cos, tabs_sin, swaps, g_qa, g_ka, g_mq_lat, g_mkv_lat):
    tm = 1024
    per_b = SEQ // tm

    def swap(t):
        return pl.BlockSpec((1, 2 * LANES, LANES), lambda i: (t, 0, 0))

    def col(width, start):
        idx = start // width
        return pl.BlockSpec((tm, width), lambda i: (i, idx))

    def tab(t):
        return pl.BlockSpec((1, tm, LANES), lambda i: (t, i, 0))

    def vec(width):
        return _layer_row_spec(l, width)

    def heads(n):
        return pl.BlockSpec((None, n, tm, LANES), lambda i: (i // per_b, 0, i % per_b, 0))

    def seq(width):
        return pl.BlockSpec((None, tm, width), lambda i: (i // per_b, i % per_b, 0))

    def tok(width):
        return pl.BlockSpec((tm, width), lambda i: (i, 0))

    return pl.pallas_call(
        _prep_kernel,
        grid=(TOKENS // tm,),
        in_specs=[
            col(1024, COL_QA), col(1024, COL_QI), col(512, COL_MQL), col(512, COL_KV),
            col(256, COL_MKVL), col(128, COL_KIWI),
            tab(0), tab(0), tab(1), tab(1), swap(0), swap(1),
            vec(A_HEAD_DIM), vec(A_HEAD_DIM), vec(M_Q_LORA), vec(M_KV_LORA),
        ],
        out_specs=[
            heads(A_HEADS), heads(A_KV_HEADS), heads(A_KV_HEADS),
            heads(IDX_HEADS), seq(LANES), seq(LANES),
            tok(M_Q_LORA), tok(M_KV_LORA),
        ],
        out_shape=[
            jax.ShapeDtypeStruct((BATCH, A_HEADS, SEQ, LANES), BF16),
            jax.ShapeDtypeStruct((BATCH, A_KV_HEADS, SEQ, LANES), BF16),
            jax.ShapeDtypeStruct((BATCH, A_KV_HEADS, SEQ, LANES), BF16),
            jax.ShapeDtypeStruct((BATCH, IDX_HEADS, SEQ, LANES), BF16),
            jax.ShapeDtypeStruct((BATCH, SEQ, LANES), BF16),
            jax.ShapeDtypeStruct((BATCH, SEQ, LANES), F32),
            jax.ShapeDtypeStruct((TOKENS, M_Q_LORA), BF16),
            jax.ShapeDtypeStruct((TOKENS, M_KV_LORA), BF16),
        ],
        compiler_params=_params("arbitrary"),
        name="prep",
    )(proj, proj, proj, proj, proj, proj, tabs_cos, tabs_sin, tabs_cos, tabs_sin, swaps, swaps,
      g_qa, g_ka, g_mq_lat, g_mkv_lat)


def _mla_up_kernel(mqn_ref, mkvn_ref, mkr_ref, wq_ref, wkv_ref, cm_ref, sm_ref, pm_ref,
                   gqn_ref, gqr_ref, gkn_ref, gkr_ref, q_o, k_o, v_o):
    tm = mqn_ref.shape[0]
    lane = lax.broadcasted_iota(jnp.int32, (tm, LANES), 1)
    low = lane < M_ROPE
    cm, sm, pm = cm_ref[0], sm_ref[0], pm_ref[0]
    nope_w = M_HEADS * M_NOPE

    q = _dot(mqn_ref[...], wq_ref[...])
    kv = _dot(mkvn_ref[...], wkv_ref[...])

    for p in range(M_HEADS // 2):
        rp = q[:, nope_w + p * LANES: nope_w + (p + 1) * LANES]
        sq = rp * rp
        s_all = jnp.sum(sq, axis=-1, keepdims=True)
        s_lo = jnp.sum(jnp.where(low, sq, 0.0), axis=-1, keepdims=True)
        for e in range(2):
            h = 2 * p + e
            nope = q[:, h * M_NOPE:(h + 1) * M_NOPE]
            ss = jnp.sum(nope * nope, axis=-1, keepdims=True) + (s_lo if e == 0 else s_all - s_lo)
            rs = lax.rsqrt(ss * (1.0 / M_QK) + EPS)
            roped = _rope(rp * rs * gqr_ref[...], cm, sm, pm)
            if e == 1:
                roped = pltpu.roll(roped, M_ROPE, 1)
            q_o[h, :, 0:LANES] = (nope * rs * gqn_ref[...]).astype(BF16)
            q_o[h, :, LANES:2 * LANES] = jnp.where(low, roped, 0.0).astype(BF16)

    kr = mkr_ref[...].astype(F32)
    kr_ss = jnp.sum(kr * kr, axis=-1, keepdims=True)
    kr_roped = _rope(kr * gkr_ref[...], cm, sm, pm)
    kr_roped = jnp.where(low, kr_roped, 0.0)
    for h in range(M_HEADS):
        nope = kv[:, h * 2 * LANES: h * 2 * LANES + M_NOPE]
        ss = jnp.sum(nope * nope, axis=-1, keepdims=True) + kr_ss
        rs = lax.rsqrt(ss * (1.0 / M_QK) + EPS)
        k_o[h, :, 0:LANES] = (nope * rs * gkn_ref[...]).astype(BF16)
        k_o[h, :, LANES:2 * LANES] = (kr_roped * rs).astype(BF16)
        v_o[h] = kv[:, h * 2 * LANES + M_NOPE:(h + 1) * 2 * LANES].astype(BF16)


def _mla_up(l, mqn, mkvn, proj, wq, wkv, tabs_cos, tabs_sin, swaps, gqn, gqr, gkn, gkr):
    tm = 256
    per_b = SEQ // tm

    def vec():
        return _layer_row_spec(l, LANES)

    def heads(width):
        return pl.BlockSpec((None, M_HEADS, tm, width), lambda i: (i // per_b, 0, i % per_b, 0))

    return pl.pallas_call(
        _mla_up_kernel,
        grid=(TOKENS // tm,),
        in_specs=[
            pl.BlockSpec((tm, M_Q_LORA), lambda i: (i, 0)),
            pl.BlockSpec((tm, M_KV_LORA), lambda i: (i, 0)),
            pl.BlockSpec((tm, LANES), lambda i: (i, COL_MKR // LANES)),
            pl.BlockSpec((None,) + wq.shape[1:], lambda i: (l, 0, 0)),
            pl.BlockSpec((None,) + wkv.shape[1:], lambda i: (l, 0, 0)),
            pl.BlockSpec((1, tm, LANES), lambda i: (2, i, 0)),
            pl.BlockSpec((1, tm, LANES), lambda i: (2, i, 0)),
            pl.BlockSpec((1, 2 * LANES, LANES), lambda i: (2, 0, 0)),
            vec(), vec(), vec(), vec(),
        ],
        out_specs=[heads(M_QK_PAD), heads(M_QK_PAD), heads(M_V)],
        out_shape=[
            jax.ShapeDtypeStruct((BATCH, M_HEADS, SEQ, M_QK_PAD), BF16),
            jax.ShapeDtypeStruct((BATCH, M_HEADS, SEQ, M_QK_PAD), BF16),
            jax.ShapeDtypeStruct((BATCH, M_HEADS, SEQ, M_V), BF16),
        ],
        compiler_params=_params("arbitrary"),
        name="mla_up",
    )(mqn, mkvn, proj, wq, wkv, tabs_cos, tabs_sin, swaps, gqn, gqr, gkn, gkr)


INT_MIN = -(2 ** 31)
KEY_MASKED = INT_MIN

DSA_TQ = 256
KEY_CHUNK = 512
SEARCH_ROWS = 128
SEARCH_UNROLL = 5
IDX_CHUNK = 256
IDX_ROWS = 64
ATT_SPLIT = 4
assert DSA_TQ == IDX_CHUNK and KEY_CHUNK == 2 * IDX_CHUNK


def _dsa_index(i, n_chunks, qi_ref, ki2_ref, wi_ref, key_ref, d_refs, wb_ref):
    tq = qi_ref.shape[1]
    w = wi_ref[...]
    for h in range(IDX_HEADS):
        wb_ref[h] = jnp.broadcast_to(w[:, IDX_DIM + h:IDX_DIM + h + 1], (tq, LANES))

    def matmul(c, d_ref):
        start = pl.multiple_of(c * IDX_CHUNK, IDX_CHUNK)
        q_all = qi_ref[...].reshape(IDX_HEADS * tq, LANES)
        d_ref[...] = _dot_nt(q_all, ki2_ref[pl.ds(start, IDX_CHUNK), :])

    def head_sum(c, d_ref):
        for r0 in range(0, tq, IDX_ROWS):
            row_t = i * tq + r0 + lax.broadcasted_iota(jnp.int32, (IDX_ROWS, LANES), 0)
            for j0 in range(0, IDX_CHUNK, LANES):
                acc = jnp.zeros((IDX_ROWS, LANES), F32)
                for h in range(IDX_HEADS):
                    d = d_ref[h * tq + r0:h * tq + r0 + IDX_ROWS, j0:j0 + LANES]
                    acc = acc + jnp.maximum(d, 0.0) * wb_ref[h, r0:r0 + IDX_ROWS, :]
                bits = pltpu.bitcast(acc, jnp.int32)
                img = bits ^ ((bits >> 31) & jnp.int32(0x7FFFFFFF))
                col_t = c * IDX_CHUNK + j0 + lax.broadcasted_iota(jnp.int32, (IDX_ROWS, LANES), 1)
                key_ref[c, r0:r0 + IDX_ROWS, j0:j0 + LANES] = jnp.where(col_t <= row_t, img, KEY_MASKED)

    d0_ref, d1_ref = d_refs
    matmul(0, d0_ref)

    def pair(k, carry):
        c = 2 * k
        matmul(c + 1, d1_ref)
        head_sum(c, d0_ref)
        matmul(jnp.minimum(c + 2, n_chunks - 1), d0_ref)
        head_sum(c + 1, d1_ref)
        return carry

    lax.fori_loop(0, n_chunks // 2, pair, 0)

    @pl.when(n_chunks % 2 == 1)
    def _():
        head_sum(n_chunks - 1, d0_ref)
        key_ref[n_chunks] = jnp.full(key_ref.shape[1:], KEY_MASKED, jnp.int32)


def _dsa_select(nk, nk_bias, key_ref, bias_ref):
    chunks = range(nk // IDX_CHUNK)
    n_sel = float(N_SEL)
    groups = list(range(0, key_ref.shape[1], SEARCH_ROWS))

    def count_ge(r0, cand):
        hits = None
        for c in chunks:
            hit = jnp.where(key_ref[c, r0:r0 + SEARCH_ROWS, :] >= cand, 1.0, 0.0)
            hits = hit if hits is None else hits + hit
        return jnp.sum(hits, axis=-1, keepdims=True)

    zero = jnp.zeros((SEARCH_ROWS, 1), jnp.int32)
    thr0 = tuple(jnp.where(count_ge(r0, zero) >= n_sel, jnp.int32(0), jnp.int32(INT_MIN))
                 for r0 in groups)

    def step(b, thrs):
        bit = lax.shift_left(jnp.int32(1), jnp.int32(30) - b)
        out = []
        for r0, thr in zip(groups, thrs):
            cand = thr | bit
            out.append(jnp.where(count_ge(r0, cand) >= n_sel, cand, thr))
        return tuple(out)

    thrs = lax.fori_loop(0, 30, step, thr0, unroll=SEARCH_UNROLL)
    thrs = step(jnp.int32(30), thrs)
    for r0, thr in zip(groups, thrs):
        floor = jnp.maximum(thr, jnp.int32(KEY_MASKED + 1))
        for c in range(nk_bias // IDX_CHUNK):
            sel = key_ref[c, r0:r0 + SEARCH_ROWS, :] >= floor
            bias_ref[r0:r0 + SEARCH_ROWS, c * IDX_CHUNK:(c + 1) * IDX_CHUNK] = jnp.where(sel, 0.0, -jnp.inf)


def _dsa_attend(nk, qa_ref, ka_ref, va_ref, bias_ref, o_ref, oh_ref, lg_refs, p_refs):
    tq = qa_ref.shape[1]
    rep = A_HEADS // A_KV_HEADS

    sub = rep // ATT_SPLIT

    def group(g, carry):
        k = ka_ref[g, 0:nk, :]
        v = va_ref[g, 0:nk, :]
        inv_l = [None] * ATT_SPLIT

        def scores(s):
            q = qa_ref[pl.ds(g * rep + s * sub, sub)].reshape(sub * tq, LANES)
            lg_refs[s][:, 0:nk] = _dot_nt(q, k)

        def softmax(s):
            logits = lg_refs[s][:, 0:nk].reshape(sub, tq, nk) + bias_ref[:, 0:nk][None]
            m = jnp.max(logits, axis=-1, keepdims=True)
            p = jnp.exp(logits - m)
            inv_l[s] = 1.0 / jnp.sum(p, axis=-1, keepdims=True)
            p_refs[s][:, 0:nk] = p.astype(BF16).reshape(sub * tq, nk)

        def values(s):
            o = _dot(p_refs[s][:, 0:nk], v).reshape(sub, tq, LANES) * inv_l[s]
            oh_ref[pl.ds(g * rep + s * sub, sub)] = o.astype(BF16)

        for t in range(ATT_SPLIT + 2):
            if t < ATT_SPLIT:
                scores(t)
            if 0 <= t - 1 < ATT_SPLIT:
                softmax(t - 1)
            if 0 <= t - 2 < ATT_SPLIT:
                values(t - 2)
        return carry

    lax.fori_loop(0, A_KV_HEADS, group, 0)
    for h in range(A_HEADS):
        o_ref[:, h * LANES:(h + 1) * LANES] = oh_ref[h]


def _dsa_kernel(qa_ref, ka_ref, va_ref, qi_ref, ki2_ref, wi_ref, o_ref,
                key_ref, bias_ref, d0_ref, d1_ref, wb_ref, oh_ref, *stage_refs):
    lg_refs, p_refs = stage_refs[:ATT_SPLIT], stage_refs[ATT_SPLIT:]
    i = pl.program_id(1)
    tq = DSA_TQ
    n_free = N_SEL // tq
    variant = i // (KEY_CHUNK // tq)

    for v in range(n_free):
        @pl.when(i == v)
        def _(v=v):
            nk = (v + 1) * tq
            row = i * tq + lax.broadcasted_iota(jnp.int32, (tq, nk), 0)
            col = lax.broadcasted_iota(jnp.int32, (tq, nk), 1)
            bias_ref[:, 0:nk] = jnp.where(col <= row, 0.0, -jnp.inf)
            _dsa_attend(nk, qa_ref, ka_ref, va_ref, bias_ref, o_ref, oh_ref, lg_refs, p_refs)

    @pl.when(i >= n_free)
    def _():
        n_chunks = (i + 1) * (tq // IDX_CHUNK)
        _dsa_index(i, n_chunks, qi_ref, ki2_ref, wi_ref, key_ref, (d0_ref, d1_ref), wb_ref)

    for v in range(SEQ // KEY_CHUNK):
        @pl.when((i >= n_free) & (variant == v))
        def _(v=v):
            nk = (v + 1) * KEY_CHUNK
            _dsa_select(nk, nk, key_ref, bias_ref)
            _dsa_attend(nk, qa_ref, ka_ref, va_ref, bias_ref, o_ref, oh_ref, lg_refs, p_refs)


def _dsa(qa, ka, va, qi, ki2, wi):
    tq = DSA_TQ
    nq = SEQ // tq
    att_rows = A_HEADS // A_KV_HEADS // ATT_SPLIT * tq
    return pl.pallas_call(
        _dsa_kernel,
        grid=(BATCH, nq),
        in_specs=[
            pl.BlockSpec((None, A_HEADS, tq, LANES), lambda b, i: (b, 0, i, 0)),
            pl.BlockSpec((None, A_KV_HEADS, SEQ, LANES), lambda b, i: (b, 0, 0, 0)),
            pl.BlockSpec((None, A_KV_HEADS, SEQ, LANES), lambda b, i: (b, 0, 0, 0)),
            pl.BlockSpec((None, IDX_HEADS, tq, LANES), lambda b, i: (b, 0, i, 0)),
            pl.BlockSpec((None, SEQ, LANES), lambda b, i: (b, 0, 0)),
            pl.BlockSpec((None, tq, LANES), lambda b, i: (b, i, 0)),
        ],
        out_specs=pl.BlockSpec((tq, A_HEADS * A_HEAD_DIM), lambda b, i: (b * nq + i, 0)),
        out_shape=jax.ShapeDtypeStruct((TOKENS, A_HEADS * A_HEAD_DIM), BF16),
        scratch_shapes=[
            pltpu.VMEM((SEQ // IDX_CHUNK, tq, IDX_CHUNK), jnp.int32),
            pltpu.VMEM((tq, SEQ), F32),
            pltpu.VMEM((IDX_HEADS * tq, IDX_CHUNK), F32),
            pltpu.VMEM((IDX_HEADS * tq, IDX_CHUNK), F32),
            pltpu.VMEM((IDX_HEADS, tq, LANES), F32),
            pltpu.VMEM((A_HEADS, tq, LANES), BF16),
        ] + [pltpu.VMEM((att_rows, SEQ), F32)] * ATT_SPLIT + [pltpu.VMEM((att_rows, SEQ), BF16)] * ATT_SPLIT,
        compiler_params=_params("arbitrary", "arbitrary"),
        name="dsa",
    )(qa, ka, va, qi, ki2, wi)


MLA_TQ = 256
MLA_HEADS_PER_STEP = 4
MLA_KEY_CHUNK = 256


def _mla_body(nk, i, q_ref, k_ref, v_ref, o_ref, lg_refs, p_refs):
    n_heads, tq = q_ref.shape[0], q_ref.shape[1]
    n0 = nk - MLA_KEY_CHUNK
    row = i * tq + lax.broadcasted_iota(jnp.int32, (tq, MLA_KEY_CHUNK), 0)
    col = n0 + lax.broadcasted_iota(jnp.int32, (tq, MLA_KEY_CHUNK), 1)
    causal = col <= row
    inv_l = [None] * n_heads

    def scores(h):
        lg_refs[h][:, 0:nk] = _dot_nt(q_ref[h], k_ref[h, 0:nk, :])

    def softmax(h):
        s1 = jnp.where(causal, lg_refs[h][:, n0:nk], -jnp.inf)
        m = jnp.max(s1, axis=-1, keepdims=True)
        if n0:
            s0 = lg_refs[h][:, 0:n0]
            m = jnp.maximum(m, jnp.max(s0, axis=-1, keepdims=True))
        p1 = jnp.exp(s1 - m)
        l = jnp.sum(p1, axis=-1, keepdims=True)
        p_refs[h][:, n0:nk] = p1.astype(BF16)
        if n0:
            p0 = jnp.exp(s0 - m)
            l = l + jnp.sum(p0, axis=-1, keepdims=True)
            p_refs[h][:, 0:n0] = p0.astype(BF16)
        inv_l[h] = 1.0 / l

    def values(h):
        o = _dot(p_refs[h][:, 0:nk], v_ref[h, 0:nk, :]) * inv_l[h]
        o_ref[:, h * M_V:(h + 1) * M_V] = o.astype(BF16)

    for t in range(n_heads + 2):
        if t < n_heads:
            scores(t)
        if 0 <= t - 1 < n_heads:
            softmax(t - 1)
        if 0 <= t - 2 < n_heads:
            values(t - 2)


def _mla_kernel(q_ref, k_ref, v_ref, o_ref, *stage_refs):
    n_heads = q_ref.shape[0]
    lg_refs, p_refs = stage_refs[:n_heads], stage_refs[n_heads:]
    i = pl.program_id(2)
    per_chunk = MLA_KEY_CHUNK // MLA_TQ
    for v in range(SEQ // MLA_KEY_CHUNK):
        @pl.when(i // per_chunk == v)
        def _(v=v):
            _mla_body((v + 1) * MLA_KEY_CHUNK, i, q_ref, k_ref, v_ref, o_ref, lg_refs, p_refs)


def _mla(q, k, v):
    tq, hg = MLA_TQ, MLA_HEADS_PER_STEP
    nq = SEQ // tq
    return pl.pallas_call(
        _mla_kernel,
        grid=(BATCH, M_HEADS // hg, nq),
        in_specs=[
            pl.BlockSpec((None, hg, tq, M_QK_PAD), lambda b, g, i: (b, g, i, 0)),
            pl.BlockSpec((None, hg, SEQ, M_QK_PAD), lambda b, g, i: (b, g, 0, 0)),
            pl.BlockSpec((None, hg, SEQ, M_V), lambda b, g, i: (b, g, 0, 0)),
        ],
        out_specs=pl.BlockSpec((tq, hg * M_V), lambda b, g, i: (b * nq + i, g)),
        out_shape=jax.ShapeDtypeStruct((TOKENS, M_HEADS * M_V), BF16),
        scratch_shapes=[pltpu.VMEM((tq, SEQ), F32)] * hg + [pltpu.VMEM((tq, SEQ), BF16)] * hg,
        compiler_params=_params("arbitrary", "arbitrary", "arbitrary"),
        name="mla",
    )(q, k, v)


def _merge_kernel(oa_ref, ob_ref, ga_ref, gb_ref, wpa_ref, wpb_ref, wo_ref, x_ref, gt_ref,
                  gf_ref, scf_ref, shf_ref, o_ref, hf_ref):
    a = _dot(oa_ref[...], wpa_ref[...])
    b = _dot(ob_ref[...], wpb_ref[...])
    merged = jax.nn.sigmoid(ga_ref[...].astype(F32)) * a + jax.nn.sigmoid(gb_ref[...].astype(F32)) * b
    y = _dot(merged.astype(BF16), wo_ref[...])
    xn = x_ref[...] + gt_ref[...] * y
    o_ref[...] = xn
    ms = jnp.mean(xn * xn, axis=-1, keepdims=True)
    hn = xn * lax.rsqrt(ms + EPS) * gf_ref[...]
    hf_ref[...] = (hn * (1.0 + scf_ref[...]) + shf_ref[...]).astype(BF16)


def _merge(l, oa, ob, proj, wpa, wpb, wo, x, mod, g_ffn):
    tm = 256
    per_b = SEQ // tm

    def resident(shape):
        return pl.BlockSpec((None,) + shape[1:], lambda i: (l, 0, 0), pipeline_mode=pl.Buffered(1))

    return pl.pallas_call(
        _merge_kernel,
        grid=(TOKENS // tm,),
        in_specs=[
            pl.BlockSpec((tm, A_HEADS * A_HEAD_DIM), lambda i: (i, 0)),
            pl.BlockSpec((tm, M_HEADS * M_V), lambda i: (i, 0)),
            pl.BlockSpec((tm, D_MODEL), lambda i: (i, COL_GATE_A // D_MODEL)),
            pl.BlockSpec((tm, D_MODEL), lambda i: (i, COL_GATE_B // D_MODEL)),
            resident(wpa.shape), resident(wpb.shape), resident(wo.shape),
            pl.BlockSpec((tm, D_MODEL), lambda i: (i, 0)),
            _mod_spec(l, 2, per_b),
            _layer_row_spec(l, D_MODEL),
            _mod_spec(l, 4, per_b),
            _mod_spec(l, 3, per_b),
        ],
        out_specs=[pl.BlockSpec((tm, D_MODEL), lambda i: (i, 0))] * 2,
        out_shape=[jax.ShapeDtypeStruct((TOKENS, D_MODEL), F32),
                   jax.ShapeDtypeStruct((TOKENS, D_MODEL), BF16)],
        compiler_params=_params("arbitrary"),
        name="merge",
    )(oa, ob, proj, proj, wpa, wpb, wo, x, mod, g_ffn, mod, mod)


HALO = 8


def _ffn_up_kernel(h_ref, wg_ref, wv_ref, cg_ref, cv_ref, bg_ref, bv_ref,
                   o_ref, wgb_ref, wvb_ref, buf_ref, *, tiles_per_seq):
    i = pl.program_id(1)
    tm, tn = o_ref.shape

    @pl.when(i == 0)
    def _():
        wgb_ref[...] = wg_ref[...].astype(BF16)
        wvb_ref[...] = wv_ref[...].astype(BF16)

    seq_start = (i % tiles_per_seq) == 0

    @pl.when(seq_start)
    def _():
        buf_ref[0:HALO, :] = jnp.zeros((HALO, 2 * tn), F32)

    @pl.when(jnp.logical_not(seq_start))
    def _():
        buf_ref[0:HALO, :] = buf_ref[tm:tm + HALO, :]

    h = h_ref[...]
    buf_ref[HALO:HALO + tm, 0:tn] = _dot(h, wgb_ref[...])
    buf_ref[HALO:HALO + tm, tn:2 * tn] = _dot(h, wvb_ref[...])

    def conv(lo, c_ref, b_ref):
        u0 = buf_ref[HALO:HALO + tm, lo:lo + tn]
        u1 = buf_ref[HALO - 1:HALO - 1 + tm, lo:lo + tn]
        u2 = buf_ref[HALO - 2:HALO - 2 + tm, lo:lo + tn]
        return b_ref[...] + c_ref[0:1, :] * u2 + c_ref[1:2, :] * u1 + c_ref[2:3, :] * u0

    gate = conv(0, cg_ref, bg_ref)
    val = conv(tn, cv_ref, bv_ref)
    o_ref[...] = (gate * jax.nn.sigmoid(gate) * val).astype(BF16)


def _ffn_up(l, h, w_up, w_conv, b_conv):
    tm, tn = 1024, 512
    nj = D_FF // tn
    kern = functools.partial(_ffn_up_kernel, tiles_per_seq=SEQ // tm)
    return pl.pallas_call(
        kern,
        grid=(nj, TOKENS // tm),
        in_specs=[
            pl.BlockSpec((tm, D_MODEL), lambda j, i: (i, 0)),
            pl.BlockSpec((None, D_MODEL, tn), lambda j, i: (l, 0, j)),
            pl.BlockSpec((None, D_MODEL, tn), lambda j, i: (l, 0, nj + j)),
            pl.BlockSpec((None, CONV_W, tn), lambda j, i: (l, 0, j)),
            pl.BlockSpec((None, CONV_W, tn), lambda j, i: (l, 0, nj + j)),
            pl.BlockSpec((None, 1, tn), lambda j, i: (l, 0, j)),
            pl.BlockSpec((None, 1, tn), lambda j, i: (l, 0, nj + j)),
        ],
        out_specs=pl.BlockSpec((tm, tn), lambda j, i: (i, j)),
        out_shape=jax.ShapeDtypeStruct((TOKENS, D_FF), BF16),
        scratch_shapes=[
            pltpu.VMEM((D_MODEL, tn), BF16),
            pltpu.VMEM((D_MODEL, tn), BF16),
            pltpu.VMEM((HALO + tm, 2 * tn), F32),
        ],
        compiler_params=_params("arbitrary", "arbitrary"),
        name="ffn_up",
    )(h, w_up, w_up, w_conv, w_conv, b_conv, b_conv)


W_RING = 3


def _ffn_down_kernel(a_ref, w_hbm, x_ref, gt_ref, o_ref, wbuf, sem, *, layer):
    n_j = pl.num_programs(1)
    n_steps = pl.num_programs(0) * n_j
    s = pl.program_id(0) * n_j + pl.program_id(1)
    tn = o_ref.shape[1]

    def tile_copy(step):
        col = pl.multiple_of((step % n_j) * tn, tn)
        slot = step % W_RING
        return pltpu.make_async_copy(w_hbm.at[layer, :, pl.ds(col, tn)], wbuf.at[slot], sem.at[slot])

    @pl.when(s == 0)
    def _():
        tile_copy(0).start()
        tile_copy(1).start()

    @pl.when(s + 2 < n_steps)
    def _():
        tile_copy(s + 2).start()

    tile_copy(s).wait()
    o_ref[...] = x_ref[...] + gt_ref[...] * _dot(a_ref[...], wbuf[s % W_RING])


def _ffn_down(l, act, w_down, x, mod):
    tm, tn = 1024, 512
    per_b = SEQ // tm
    return pl.pallas_call(
        functools.partial(_ffn_down_kernel, layer=l),
        grid=(TOKENS // tm, D_MODEL // tn),
        in_specs=[
            pl.BlockSpec((tm, D_FF), lambda i, j: (i, 0)),
            pl.BlockSpec(memory_space=pl.ANY),
            pl.BlockSpec((tm, tn), lambda i, j: (i, j)),
            _mod_spec(l, 5, per_b, tn),
        ],
        out_specs=pl.BlockSpec((tm, tn), lambda i, j: (i, j)),
        out_shape=jax.ShapeDtypeStruct((TOKENS, D_MODEL), F32),
        scratch_shapes=[pltpu.VMEM((W_RING, D_FF, tn), BF16), pltpu.SemaphoreType.DMA((W_RING,))],
        compiler_params=_params("arbitrary", "arbitrary"),
        name="ffn_down",
    )(act, w_down, x, mod)


def _pack_w_in(w_in):
    o = np.cumsum((0,) + (1024, 256, 256, 1024, 64, 16, 512, 256, 64, 2048, 2048))
    w_t = jnp.swapaxes(w_in, 1, 2)
    qa, ka, va, qi, ki, wi, mql, mkvl, mkr, ga, gb = (
        w_t[:, int(o[k]):int(o[k + 1])].astype(BF16) for k in range(11))
    z = lambda n: jnp.zeros((DEPTH, n, D_MODEL), BF16)
    packed = jnp.concatenate(
        [ga, gb, qa, qi, mql, ka, va, mkvl, ki, wi, z(48), mkr, z(64)], axis=1)
    assert packed.shape[1] == IN_PACKED
    return packed


def _pack_w_mq(w_mq_up):
    w = w_mq_up.reshape(DEPTH, M_Q_LORA, M_HEADS, M_QK)
    nope = w[..., :M_NOPE].reshape(DEPTH, M_Q_LORA, M_HEADS * M_NOPE)
    rope = w[..., M_NOPE:].reshape(DEPTH, M_Q_LORA, M_HEADS * M_ROPE)
    return jnp.concatenate([nope, rope], axis=-1).astype(BF16)


def _rope_consts():
    def inv(rot):
        return ROPE_THETA ** (-jnp.arange(0, rot, 2, dtype=F32) / rot)

    layouts = ((A_HEAD_DIM // ROT_FRACTION, LANES), (IDX_DIM // ROT_FRACTION, IDX_DIM), (M_ROPE, M_ROPE))
    freqs = [inv(rot) for rot, _ in layouts]
    used = sum(f.shape[0] for f in freqs)
    inv_c = jnp.concatenate(freqs + [jnp.zeros((LANES - used,), F32)]).reshape(1, LANES)
    zero_lane = LANES - 1

    sgn3 = np.zeros((N_ROPE_TABLES, 1, LANES), np.float32)
    expand = np.zeros((N_ROPE_TABLES, LANES, LANES), np.float32)
    off = 0
    for t, (rot, group) in enumerate(layouts):
        half = rot // 2
        for j in range(LANES):
            g = j % group
            if g < rot:
                expand[t, off + g % half, j] = 1.0
                sgn3[t, 0, j] = -1.0 if g < half else 1.0
            else:
                expand[t, zero_lane, j] = 1.0
        off += half
    expand2 = np.concatenate([expand, expand], axis=1)
    return inv_c, jnp.asarray(sgn3), jnp.asarray(expand2, dtype=BF16)


def kernel(x, c, positions, g_attn, g_ffn, w_ada, b_ada, w_in, g_qa, g_ka, g_mq_lat, w_mq_up,
           g_mkv_lat, w_mkv_up, g_qm, g_km, w_pa, w_pb, w_o, w_up, w_conv, b_conv, w_down):
    pos_col = positions.astype(F32).reshape(TOKENS, 1)
    inv_c, sgn3, expand = _rope_consts()
    tabs_cos, tabs_sin = _rope_tables(pos_col, inv_c, sgn3, expand)
    rot_a, rot_i = A_HEAD_DIM // ROT_FRACTION, IDX_DIM // ROT_FRACTION
    swaps = jnp.asarray(np.stack([_swap_matrix(rot_a // 2, LANES), _swap_matrix(rot_i // 2, IDX_DIM),
                                  _swap_matrix(M_ROPE // 2, M_ROPE)]), dtype=BF16)

    c8 = jnp.pad(c, ((0, 8 - BATCH), (0, 0)))
    mod = _ada(c8, w_ada, b_ada)[:, :BATCH].reshape(DEPTH, BATCH, N_ADA, 1, D_MODEL)

    w_in_p = _pack_w_in(w_in)
    w_mq_p = _pack_w_mq(w_mq_up)
    w_mkv_b = w_mkv_up.astype(BF16)
    w_pa_b, w_pb_b, w_o_b = w_pa.astype(BF16), w_pb.astype(BF16), w_o.astype(BF16)
    w_down_b = w_down.astype(BF16)

    rows = lambda v: v.reshape(DEPTH, 1, -1)
    rope_gain = lambda v: rows(jnp.tile(v[:, M_NOPE:], (1, 2)))
    g_attn_r, g_ffn_r, b_conv_r = rows(g_attn), rows(g_ffn), rows(b_conv)
    g_qa_r, g_ka_r, g_mq_r, g_mkv_r = rows(g_qa), rows(g_ka), rows(g_mq_lat), rows(g_mkv_lat)
    g_qm_s = g_qm * (M_QK ** -0.5)
    gqn, gqr = rows(g_qm_s[:, :M_NOPE]), rope_gain(g_qm_s)
    gkn, gkr = rows(g_km[:, :M_NOPE]), rope_gain(g_km)

    xf = x.reshape(TOKENS, D_MODEL)
    for l in range(DEPTH):
        proj = _in_proj(l, xf, g_attn_r, mod, w_in_p)
        qa, ka, va, qi, ki2, wi, mqn, mkvn = _prep(
            l, proj, tabs_cos, tabs_sin, swaps, g_qa_r, g_ka_r, g_mq_r, g_mkv_r)
        mq, mk, mv = _mla_up(l, mqn, mkvn, proj, w_mq_p, w_mkv_b, tabs_cos, tabs_sin, swaps,
                             gqn, gqr, gkn, gkr)
        o_a = _dsa(qa, ka, va, qi, ki2, wi)
        o_b = _mla(mq, mk, mv)
        xf, h_ffn = _merge(l, o_a, o_b, proj, w_pa_b, w_pb_b, w_o_b, xf, mod, g_ffn_r)
        act = _ffn_up(l, h_ffn, w_up, w_conv, b_conv_r)
        xf = _ffn_down(l, act, w_down_b, xf, mod)
    return xf.reshape(BATCH, SEQ, D_MODEL)
```

```python
import functools

import jax
import jax.numpy as jnp
import numpy as np
from jax import lax
from jax.experimental import pallas as pl
from jax.experimental.pallas import tpu as pltpu

D_MODEL = 2048
BATCH = 4
SEQ = 2048
DEPTH = 4
A_HEADS = 8
A_KV_HEADS = 2
A_HEAD_DIM = 128
IDX_HEADS = 16
IDX_DIM = 64
TOPK_MAX = 256
M_HEADS = 8
M_Q_LORA = 512
M_KV_LORA = 256
M_NOPE = 128
M_ROPE = 64
M_V = 128
D_FF = 5632
CONV_W = 3
ROPE_THETA = 500000.0
ROT_FRACTION = 4
EPS = 1e-6
N_ADA = 6
IDX_W_SCALE = (IDX_HEADS * IDX_DIM) ** -0.5
N_SEL = min(TOPK_MAX, SEQ // 4)
TOKENS = BATCH * SEQ
LANES = 128
M_QK = M_NOPE + M_ROPE
M_QK_PAD = 256

BF16 = jnp.bfloat16
F32 = jnp.float32

COL_GATE_A = 0
COL_GATE_B = 2048
COL_QA = 4096
COL_QI = 5120
COL_MQL = 6144
COL_KV = 6656
COL_MKVL = 7168
COL_KIWI = 7424
COL_MKR = 7552
IN_PACKED = 7680

VMEM_LIMIT = 56 * 1024 * 1024


def _params(*sem):
    return pltpu.CompilerParams(dimension_semantics=sem, vmem_limit_bytes=VMEM_LIMIT)


def _dot(a, b):
    return jnp.dot(a, b, preferred_element_type=F32)


def _dot_nt(a, b):
    return lax.dot_general(a, b, (((1,), (1,)), ((), ())), preferred_element_type=F32)


def _ada_kernel(c_ref, w_ref, b_ref, o_ref):
    c = c_ref[...]
    c_act = (c * jax.nn.sigmoid(c)).astype(BF16)
    o_ref[0] = _dot(c_act, w_ref[0].astype(BF16)) + b_ref[0]


def _ada(c8, w_ada, b_ada):
    tn = 2048
    n = N_ADA * D_MODEL
    return pl.pallas_call(
        _ada_kernel,
        grid=(DEPTH, n // tn),
        in_specs=[
            pl.BlockSpec((8, D_MODEL), lambda l, j: (0, 0)),
            pl.BlockSpec((1, D_MODEL, tn), lambda l, j: (l, 0, j)),
            pl.BlockSpec((1, 1, tn), lambda l, j: (l, 0, j)),
        ],
        out_specs=pl.BlockSpec((1, 8, tn), lambda l, j: (l, 0, j)),
        out_shape=jax.ShapeDtypeStruct((DEPTH, 8, n), F32),
        compiler_params=_params("arbitrary", "arbitrary"),
        name="ada",
    )(c8, w_ada, b_ada.reshape(DEPTH, 1, n))


N_ROPE_TABLES = 3


def _split_dot(t, m2):
    hi = t.astype(BF16)
    lo = (t - hi.astype(F32)).astype(BF16)
    return _dot(jnp.concatenate([hi, lo], axis=1), m2)


def _rope_table_kernel(pos_ref, inv_ref, sgn_ref, exp_ref, cos_ref, sin_ref):
    ang = pos_ref[...] * inv_ref[...]
    c, s = jnp.cos(ang), jnp.sin(ang)
    for t in range(N_ROPE_TABLES):
        cos_ref[t] = _split_dot(c, exp_ref[t])
        sin_ref[t] = _split_dot(s, exp_ref[t]) * sgn_ref[t]


def _rope_tables(pos_col, inv_c, sgn3, expand):
    tm = 512
    n = N_ROPE_TABLES
    return pl.pallas_call(
        _rope_table_kernel,
        grid=(TOKENS // tm,),
        in_specs=[
            pl.BlockSpec((tm, 1), lambda i: (i, 0)),
            pl.BlockSpec((1, LANES), lambda i: (0, 0)),
            pl.BlockSpec((n, 1, LANES), lambda i: (0, 0, 0)),
            pl.BlockSpec((n, 2 * LANES, LANES), lambda i: (0, 0, 0)),
        ],
        out_specs=[pl.BlockSpec((n, tm, LANES), lambda i: (0, i, 0))] * 2,
        out_shape=[jax.ShapeDtypeStruct((n, TOKENS, LANES), F32)] * 2,
        compiler_params=_params("arbitrary"),
        name="rope_tables",
    )(pos_col, inv_c, sgn3, expand)


def _rope(t, cos, sin_signed, swap):
    return t * cos + _split_dot(t, swap) * sin_signed


def _swap_matrix(half, group):
    p = np.zeros((LANES, LANES), np.float32)
    for j in range(LANES):
        g = j % group
        if g < half:
            p[j + half, j] = 1.0
        elif g < 2 * half:
            p[j - half, j] = 1.0
    return np.concatenate([p, p], axis=0)


def _rms(t, g, width):
    ms = jnp.sum(t * t, axis=-1, keepdims=True) * (1.0 / width)
    return t * lax.rsqrt(ms + EPS) * g


NORM_ROWS = 16


def _modnorm_into(h_ref, x_ref, g_ref, sc_ref, sh_ref):
    g, sc1, sh = g_ref[...], 1.0 + sc_ref[...], sh_ref[...]

    def chunk(c, carry):
        rows = pl.ds(pl.multiple_of(c * NORM_ROWS, NORM_ROWS), NORM_ROWS)
        x = x_ref[rows, :]
        ms = jnp.mean(x * x, axis=-1, keepdims=True)
        y = x * lax.rsqrt(ms + EPS) * g
        h_ref[rows, :] = (y * sc1 + sh).astype(BF16)
        return carry

    lax.fori_loop(0, x_ref.shape[0] // NORM_ROWS, chunk, 0, unroll=8)


def _in_proj_kernel(x_ref, g_ref, sc_ref, sh_ref, w_hbm, o_ref, h_ref, wbuf, sem, *, layer):
    n_j = pl.num_programs(1)
    n_steps = pl.num_programs(0) * n_j
    s = pl.program_id(0) * n_j + pl.program_id(1)
    tn = o_ref.shape[1]

    def tile_copy(step):
        row = pl.multiple_of((step % n_j) * tn, tn)
        slot = step % W_RING
        return pltpu.make_async_copy(w_hbm.at[layer, pl.ds(row, tn), :], wbuf.at[slot], sem.at[slot])

    @pl.when(s == 0)
    def _():
        tile_copy(0).start()
        tile_copy(1).start()

    @pl.when(s + 2 < n_steps)
    def _():
        tile_copy(s + 2).start()

    @pl.when(pl.program_id(1) == 0)
    def _():
        _modnorm_into(h_ref, x_ref, g_ref, sc_ref, sh_ref)

    tile_copy(s).wait()
    o_ref[...] = _dot_nt(h_ref[...], wbuf[s % W_RING]).astype(o_ref.dtype)


def _mod_spec(l, k, per_b, width=D_MODEL):
    if width == D_MODEL:
        return pl.BlockSpec((None, None, None, 1, D_MODEL), lambda i, *_: (l, i // per_b, k, 0, 0))
    return pl.BlockSpec((None, None, None, 1, width), lambda i, j: (l, i // per_b, k, 0, j))


def _layer_row_spec(l, width):
    return pl.BlockSpec((None, 1, width), lambda *_: (l, 0, 0))


def _in_proj(l, x, g, mod, w):
    tm, tn = 1024, 1536
    per_b = SEQ // tm
    n = w.shape[1]
    return pl.pallas_call(
        functools.partial(_in_proj_kernel, layer=l),
        grid=(TOKENS // tm, n // tn),
        in_specs=[
            pl.BlockSpec((tm, D_MODEL), lambda i, j: (i, 0)),
            _layer_row_spec(l, D_MODEL),
            _mod_spec(l, 1, per_b),
            _mod_spec(l, 0, per_b),
            pl.BlockSpec(memory_space=pl.ANY),
        ],
        out_specs=pl.BlockSpec((tm, tn), lambda i, j: (i, j)),
        out_shape=jax.ShapeDtypeStruct((TOKENS, n), BF16),
        scratch_shapes=[pltpu.VMEM((tm, D_MODEL), BF16), pltpu.VMEM((W_RING, tn, D_MODEL), BF16),
                        pltpu.SemaphoreType.DMA((W_RING,))],
        compiler_params=_params("arbitrary", "arbitrary"),
        name="in_proj",
    )(x, g, mod, mod, w)


def _prep_kernel(qa_ref, qi_ref, mql_ref, kv_ref, mkvl_ref, kiwi_ref,
                 ca_ref, sa_ref, ci_ref, si_ref, pa_ref, pi_ref,
                 gqa_ref, gka_ref, gmq_ref, gmkv_ref,
                 qa_o, ka_o, va_o, qi_o, ki2_o, wi_o, mqn_o, mkvn_o):
    tm = qa_ref.shape[0]
    lane = lax.broadcasted_iota(jnp.int32, (tm, LANES), 1)
    ca, sa, pa = ca_ref[0], sa_ref[0], pa_ref[0]
    ci, si, pi = ci_ref[0], si_ref[0], pi_ref[0]
    scale_a = A_HEAD_DIM ** -0.5

    for h in range(A_HEADS):
        t = qa_ref[:, h * LANES:(h + 1) * LANES].astype(F32)
        r = _rope(_rms(t, gqa_ref[...], A_HEAD_DIM), ca, sa, pa)
        qa_o[h] = (r * scale_a).astype(BF16)
    for g in range(A_KV_HEADS):
        t = kv_ref[:, g * LANES:(g + 1) * LANES].astype(F32)
        r = _rope(_rms(t, gka_ref[...], A_HEAD_DIM), ca, sa, pa)
        ka_o[g] = r.astype(BF16)
        va_o[g] = kv_ref[:, (A_KV_HEADS + g) * LANES:(A_KV_HEADS + g + 1) * LANES]
    for p in range(IDX_HEADS // 2):
        t = qi_ref[:, p * LANES:(p + 1) * LANES].astype(F32)
        r = _rope(t, ci, si, pi)
        qi_o[2 * p] = jnp.where(lane < IDX_DIM, r, 0.0).astype(BF16)
        qi_o[2 * p + 1] = jnp.where(lane < IDX_DIM, 0.0, r).astype(BF16)

    kiwi = kiwi_ref[...].astype(F32)
    ki = _rope(kiwi, ci, si, pi)
    ki2_o[...] = jnp.where(lane < IDX_DIM, ki, pltpu.roll(ki, IDX_DIM, 1)).astype(BF16)
    wi_o[...] = kiwi * IDX_W_SCALE

    mqn_o[...] = _rms(mql_ref[...].astype(F32), gmq_ref[...], M_Q_LORA).astype(BF16)
    mkvn_o[...] = _rms(mkvl_ref[...].astype(F32), gmkv_ref[...], M_KV_LORA).astype(BF16)


def _prep(l, proj, tabs_cos, tabs_sin, swaps, g_qa, g_ka, g_mq_lat, g_mkv_lat):
    tm = 1024
    per_b = SEQ // tm

    def swap(t):
        return pl.BlockSpec((1, 2 * LANES, LANES), lambda i: (t, 0, 0))

    def col(width, start):
        idx = start // width
        return pl.BlockSpec((tm, width), lambda i: (i, idx))

    def tab(t):
        return pl.BlockSpec((1, tm, LANES), lambda i: (t, i, 0))

    def vec(width):
        return _layer_row_spec(l, width)

    def heads(n):
        return pl.BlockSpec((None, n, tm, LANES), lambda i: (i // per_b, 0, i % per_b, 0))

    def seq(width):
        return pl.BlockSpec((None, tm, width), lambda i: (i // per_b, i % per_b, 0))

    def tok(width):
        return pl.BlockSpec((tm, width), lambda i: (i, 0))

    return pl.pallas_call(
        _prep_kernel,
        grid=(TOKENS // tm,),
        in_specs=[
            col(1024, COL_QA), col(1024, COL_QI), col(512, COL_MQL), col(512, COL_KV),
            col(256, COL_MKVL), col(128, COL_KIWI),
            tab(0), tab(0), tab(1), tab(1), swap(0), swap(1),
            vec(A_HEAD_DIM), vec(A_HEAD_DIM), vec(M_Q_LORA), vec(M_KV_LORA),
        ],
        out_specs=[
            heads(A_HEADS), heads(A_KV_HEADS), heads(A_KV_HEADS),
            heads(IDX_HEADS), seq(LANES), seq(LANES),
            tok(M_Q_LORA), tok(M_KV_LORA),
        ],
        out_shape=[
            jax.ShapeDtypeStruct((BATCH, A_HEADS, SEQ, LANES), BF16),
            jax.ShapeDtypeStruct((BATCH, A_KV_HEADS, SEQ, LANES), BF16),
            jax.ShapeDtypeStruct((BATCH, A_KV_HEADS, SEQ, LANES), BF16),
            jax.ShapeDtypeStruct((BATCH, IDX_HEADS, SEQ, LANES), BF16),
            jax.ShapeDtypeStruct((BATCH, SEQ, LANES), BF16),
            jax.ShapeDtypeStruct((BATCH, SEQ, LANES), F32),
            jax.ShapeDtypeStruct((TOKENS, M_Q_LORA), BF16),
            jax.ShapeDtypeStruct((TOKENS, M_KV_LORA), BF16),
        ],
        compiler_params=_params("arbitrary"),
        name="prep",
    )(proj, proj, proj, proj, proj, proj, tabs_cos, tabs_sin, tabs_cos, tabs_sin, swaps, swaps,
      g_qa, g_ka, g_mq_lat, g_mkv_lat)


def _mla_up_kernel(mqn_ref, mkvn_ref, mkr_ref, wq_ref, wkv_ref, cm_ref, sm_ref, pm_ref,
                   gqn_ref, gqr_ref, gkn_ref, gkr_ref, q_o, k_o, v_o):
    tm = mqn_ref.shape[0]
    lane = lax.broadcasted_iota(jnp.int32, (tm, LANES), 1)
    low = lane < M_ROPE
    cm, sm, pm = cm_ref[0], sm_ref[0], pm_ref[0]
    nope_w = M_HEADS * M_NOPE

    q = _dot(mqn_ref[...], wq_ref[...])
    kv = _dot(mkvn_ref[...], wkv_ref[...])

    for p in range(M_HEADS // 2):
        rp = q[:, nope_w + p * LANES: nope_w + (p + 1) * LANES]
        sq = rp * rp
        s_all = jnp.sum(sq, axis=-1, keepdims=True)
        s_lo = jnp.sum(jnp.where(low, sq, 0.0), axis=-1, keepdims=True)
        for e in range(2):
            h = 2 * p + e
            nope = q[:, h * M_NOPE:(h + 1) * M_NOPE]
            ss = jnp.sum(nope * nope, axis=-1, keepdims=True) + (s_lo if e == 0 else s_all - s_lo)
            rs = lax.rsqrt(ss * (1.0 / M_QK) + EPS)
            roped = _rope(rp * rs * gqr_ref[...], cm, sm, pm)
            if e == 1:
                roped = pltpu.roll(roped, M_ROPE, 1)
            q_o[h, :, 0:LANES] = (nope * rs * gqn_ref[...]).astype(BF16)
            q_o[h, :, LANES:2 * LANES] = jnp.where(low, roped, 0.0).astype(BF16)

    kr = mkr_ref[...].astype(F32)
    kr_ss = jnp.sum(kr * kr, axis=-1, keepdims=True)
    kr_roped = _rope(kr * gkr_ref[...], cm, sm, pm)
    kr_roped = jnp.where(low, kr_roped, 0.0)
    for h in range(M_HEADS):
        nope = kv[:, h * 2 * LANES: h * 2 * LANES + M_NOPE]
        ss = jnp.sum(nope * nope, axis=-1, keepdims=True) + kr_ss
        rs = lax.rsqrt(ss * (1.0 / M_QK) + EPS)
        k_o[h, :, 0:LANES] = (nope * rs * gkn_ref[...]).astype(BF16)
        k_o[h, :, LANES:2 * LANES] = (kr_roped * rs).astype(BF16)
        v_o[h] = kv[:, h * 2 * LANES + M_NOPE:(h + 1) * 2 * LANES].astype(BF16)


def _mla_up(l, mqn, mkvn, proj, wq, wkv, tabs_cos, tabs_sin, swaps, gqn, gqr, gkn, gkr):
    tm = 256
    per_b = SEQ // tm

    def vec():
        return _layer_row_spec(l, LANES)

    def heads(width):
        return pl.BlockSpec((None, M_HEADS, tm, width), lambda i: (i // per_b, 0, i % per_b, 0))

    return pl.pallas_call(
        _mla_up_kernel,
        grid=(TOKENS // tm,),
        in_specs=[
            pl.BlockSpec((tm, M_Q_LORA), lambda i: (i, 0)),
            pl.BlockSpec((tm, M_KV_LORA), lambda i: (i, 0)),
            pl.BlockSpec((tm, LANES), lambda i: (i, COL_MKR // LANES)),
            pl.BlockSpec((None,) + wq.shape[1:], lambda i: (l, 0, 0)),
            pl.BlockSpec((None,) + wkv.shape[1:], lambda i: (l, 0, 0)),
            pl.BlockSpec((1, tm, LANES), lambda i: (2, i, 0)),
            pl.BlockSpec((1, tm, LANES), lambda i: (2, i, 0)),
            pl.BlockSpec((1, 2 * LANES, LANES), lambda i: (2, 0, 0)),
            vec(), vec(), vec(), vec(),
        ],
        out_specs=[heads(M_QK_PAD), heads(M_QK_PAD), heads(M_V)],
        out_shape=[
            jax.ShapeDtypeStruct((BATCH, M_HEADS, SEQ, M_QK_PAD), BF16),
            jax.ShapeDtypeStruct((BATCH, M_HEADS, SEQ, M_QK_PAD), BF16),
            jax.ShapeDtypeStruct((BATCH, M_HEADS, SEQ, M_V), BF16),
        ],
        compiler_params=_params("arbitrary"),
        name="mla_up",
    )(mqn, mkvn, proj, wq, wkv, tabs_cos, tabs_sin, swaps, gqn, gqr, gkn, gkr)


INT_MIN = -(2 ** 31)
KEY_MASKED = INT_MIN

DSA_TQ = 256
KEY_CHUNK = 512
SEARCH_ROWS = 128
SEARCH_UNROLL = 5
IDX_CHUNK = 256
IDX_ROWS = 64
ATT_SPLIT = 4
assert DSA_TQ == IDX_CHUNK and KEY_CHUNK == 2 * IDX_CHUNK


def _dsa_index(i, n_chunks, qi_ref, ki2_ref, wi_ref, key_ref, d_refs, wb_ref):
    tq = qi_ref.shape[1]
    w = wi_ref[...]
    for h in range(IDX_HEADS):
        wb_ref[h] = jnp.broadcast_to(w[:, IDX_DIM + h:IDX_DIM + h + 1], (tq, LANES))

    def matmul(c, d_ref):
        start = pl.multiple_of(c * IDX_CHUNK, IDX_CHUNK)
        q_all = qi_ref[...].reshape(IDX_HEADS * tq, LANES)
        d_ref[...] = _dot_nt(q_all, ki2_ref[pl.ds(start, IDX_CHUNK), :])

    def head_sum(c, d_ref):
        for r0 in range(0, tq, IDX_ROWS):
            row_t = i * tq + r0 + lax.broadcasted_iota(jnp.int32, (IDX_ROWS, LANES), 0)
            for j0 in range(0, IDX_CHUNK, LANES):
                acc = jnp.zeros((IDX_ROWS, LANES), F32)
                for h in range(IDX_HEADS):
                    d = d_ref[h * tq + r0:h * tq + r0 + IDX_ROWS, j0:j0 + LANES]
                    acc = acc + jnp.maximum(d, 0.0) * wb_ref[h, r0:r0 + IDX_ROWS, :]
                bits = pltpu.bitcast(acc, jnp.int32)
                img = bits ^ ((bits >> 31) & jnp.int32(0x7FFFFFFF))
                col_t = c * IDX_CHUNK + j0 + lax.broadcasted_iota(jnp.int32, (IDX_ROWS, LANES), 1)
                key_ref[c, r0:r0 + IDX_ROWS, j0:j0 + LANES] = jnp.where(col_t <= row_t, img, KEY_MASKED)

    d0_ref, d1_ref = d_refs
    matmul(0, d0_ref)

    def pair(k, carry):
        c = 2 * k
        matmul(c + 1, d1_ref)
        head_sum(c, d0_ref)
        matmul(jnp.minimum(c + 2, n_chunks - 1), d0_ref)
        head_sum(c + 1, d1_ref)
        return carry

    lax.fori_loop(0, n_chunks // 2, pair, 0)

    @pl.when(n_chunks % 2 == 1)
    def _():
        head_sum(n_chunks - 1, d0_ref)
        key_ref[n_chunks] = jnp.full(key_ref.shape[1:], KEY_MASKED, jnp.int32)


def _dsa_select(nk, nk_bias, key_ref, bias_ref):
    chunks = range(nk // IDX_CHUNK)
    n_sel = float(N_SEL)
    groups = list(range(0, key_ref.shape[1], SEARCH_ROWS))

    def count_ge(r0, cand):
        hits = None
        for c in chunks:
            hit = jnp.where(key_ref[c, r0:r0 + SEARCH_ROWS, :] >= cand, 1.0, 0.0)
            hits = hit if hits is None else hits + hit
        return jnp.sum(hits, axis=-1, keepdims=True)

    zero = jnp.zeros((SEARCH_ROWS, 1), jnp.int32)
    thr0 = tuple(jnp.where(count_ge(r0, zero) >= n_sel, jnp.int32(0), jnp.int32(INT_MIN))
                 for r0 in groups)

    def step(b, thrs):
        bit = lax.shift_left(jnp.int32(1), jnp.int32(30) - b)
        out = []
        for r0, thr in zip(groups, thrs):
            cand = thr | bit
            out.append(jnp.where(count_ge(r0, cand) >= n_sel, cand, thr))
        return tuple(out)

    thrs = lax.fori_loop(0, 30, step, thr0, unroll=SEARCH_UNROLL)
    thrs = step(jnp.int32(30), thrs)
    for r0, thr in zip(groups, thrs):
        floor = jnp.maximum(thr, jnp.int32(KEY_MASKED + 1))
        for c in range(nk_bias // IDX_CHUNK):
            sel = key_ref[c, r0:r0 + SEARCH_ROWS, :] >= floor
            bias_ref[r0:r0 + SEARCH_ROWS, c * IDX_CHUNK:(c + 1) * IDX_CHUNK] = jnp.where(sel, 0.0, -jnp.inf)


def _dsa_attend(nk, qa_ref, ka_ref, va_ref, bias_ref, o_ref, oh_ref, lg_refs, p_refs):
    tq = qa_ref.shape[1]
    rep = A_HEADS // A_KV_HEADS

    sub = rep // ATT_SPLIT

    def group(g, carry):
        k = ka_ref[g, 0:nk, :]
        v = va_ref[g, 0:nk, :]
        inv_l = [None] * ATT_SPLIT

        def scores(s):
            q = qa_ref[pl.ds(g * rep + s * sub, sub)].reshape(sub * tq, LANES)
            lg_refs[s][:, 0:nk] = _dot_nt(q, k)

        def softmax(s):
            logits = lg_refs[s][:, 0:nk].reshape(sub, tq, nk) + bias_ref[:, 0:nk][None]
            m = jnp.max(logits, axis=-1, keepdims=True)
            p = jnp.exp(logits - m)
            inv_l[s] = 1.0 / jnp.sum(p, axis=-1, keepdims=True)
            p_refs[s][:, 0:nk] = p.astype(BF16).reshape(sub * tq, nk)

        def values(s):
            o = _dot(p_refs[s][:, 0:nk], v).reshape(sub, tq, LANES) * inv_l[s]
            oh_ref[pl.ds(g * rep + s * sub, sub)] = o.astype(BF16)

        for t in range(ATT_SPLIT + 2):
            if t < ATT_SPLIT:
                scores(t)
            if 0 <= t - 1 < ATT_SPLIT:
                softmax(t - 1)
            if 0 <= t - 2 < ATT_SPLIT:
                values(t - 2)
        return carry

    lax.fori_loop(0, A_KV_HEADS, group, 0)
    for h in range(A_HEADS):
        o_ref[:, h * LANES:(h + 1) * LANES] = oh_ref[h]


def _dsa_kernel(qa_ref, ka_ref, va_ref, qi_ref, ki2_ref, wi_ref, o_ref,
                key_ref, bias_ref, d0_ref, d1_ref, wb_ref, oh_ref, *stage_refs):
    lg_refs, p_refs = stage_refs[:ATT_SPLIT], stage_refs[ATT_SPLIT:]
    i = pl.program_id(1)
    tq = DSA_TQ
    n_free = N_SEL // tq
    variant = i // (KEY_CHUNK // tq)

    for v in range(n_free):
        @pl.when(i == v)
        def _(v=v):
            nk = (v + 1) * tq
            row = i * tq + lax.broadcasted_iota(jnp.int32, (tq, nk), 0)
            col = lax.broadcasted_iota(jnp.int32, (tq, nk), 1)
            bias_ref[:, 0:nk] = jnp.where(col <= row, 0.0, -jnp.inf)
            _dsa_attend(nk, qa_ref, ka_ref, va_ref, bias_ref, o_ref, oh_ref, lg_refs, p_refs)

    @pl.when(i >= n_free)
    def _():
        n_chunks = (i + 1) * (tq // IDX_CHUNK)
        _dsa_index(i, n_chunks, qi_ref, ki2_ref, wi_ref, key_ref, (d0_ref, d1_ref), wb_ref)

    for v in range(SEQ // KEY_CHUNK):
        @pl.when((i >= n_free) & (variant == v))
        def _(v=v):
            nk = (v + 1) * KEY_CHUNK
            _dsa_select(nk, nk, key_ref, bias_ref)
            _dsa_attend(nk, qa_ref, ka_ref, va_ref, bias_ref, o_ref, oh_ref, lg_refs, p_refs)


def _dsa(qa, ka, va, qi, ki2, wi):
    tq = DSA_TQ
    nq = SEQ // tq
    att_rows = A_HEADS // A_KV_HEADS // ATT_SPLIT * tq
    return pl.pallas_call(
        _dsa_kernel,
        grid=(BATCH, nq),
        in_specs=[
            pl.BlockSpec((None, A_HEADS, tq, LANES), lambda b, i: (b, 0, i, 0)),
            pl.BlockSpec((None, A_KV_HEADS, SEQ, LANES), lambda b, i: (b, 0, 0, 0)),
            pl.BlockSpec((None, A_KV_HEADS, SEQ, LANES), lambda b, i: (b, 0, 0, 0)),
            pl.BlockSpec((None, IDX_HEADS, tq, LANES), lambda b, i: (b, 0, i, 0)),
            pl.BlockSpec((None, SEQ, LANES), lambda b, i: (b, 0, 0)),
            pl.BlockSpec((None, tq, LANES), lambda b, i: (b, i, 0)),
        ],
        out_specs=pl.BlockSpec((tq, A_HEADS * A_HEAD_DIM), lambda b, i: (b * nq + i, 0)),
        out_shape=jax.ShapeDtypeStruct((TOKENS, A_HEADS * A_HEAD_DIM), BF16),
        scratch_shapes=[
            pltpu.VMEM((SEQ // IDX_CHUNK, tq, IDX_CHUNK), jnp.int32),
            pltpu.VMEM((tq, SEQ), F32),
            pltpu.VMEM((IDX_HEADS * tq, IDX_CHUNK), F32),
            pltpu.VMEM((IDX_HEADS * tq, IDX_CHUNK), F32),
            pltpu.VMEM((IDX_HEADS, tq, LANES), F32),
            pltpu.VMEM((A_HEADS, tq, LANES), BF16),
        ] + [pltpu.VMEM((att_rows, SEQ), F32)] * ATT_SPLIT + [pltpu.VMEM((att_rows, SEQ), BF16)] * ATT_SPLIT,
        compiler_params=_params("arbitrary", "arbitrary"),
        name="dsa",
    )(qa, ka, va, qi, ki2, wi)


MLA_TQ = 256
MLA_HEADS_PER_STEP = 4
MLA_KEY_CHUNK = 256


def _mla_body(nk, i, q_ref, k_ref, v_ref, o_ref, lg_refs, p_refs):
    n_heads, tq = q_ref.shape[0], q_ref.shape[1]
    n0 = nk - MLA_KEY_CHUNK
    row = i * tq + lax.broadcasted_iota(jnp.int32, (tq, MLA_KEY_CHUNK), 0)
    col = n0 + lax.broadcasted_iota(jnp.int32, (tq, MLA_KEY_CHUNK), 1)
    causal = col <= row
    inv_l = [None] * n_heads

    def scores(h):
        lg_refs[h][:, 0:nk] = _dot_nt(q_ref[h], k_ref[h, 0:nk, :])

    def softmax(h):
        s1 = jnp.where(causal, lg_refs[h][:, n0:nk], -jnp.inf)
        m = jnp.max(s1, axis=-1, keepdims=True)
        if n0:
            s0 = lg_refs[h][:, 0:n0]
            m = jnp.maximum(m, jnp.max(s0, axis=-1, keepdims=True))
        p1 = jnp.exp(s1 - m)
        l = jnp.sum(p1, axis=-1, keepdims=True)
        p_refs[h][:, n0:nk] = p1.astype(BF16)
        if n0:
            p0 = jnp.exp(s0 - m)
            l = l + jnp.sum(p0, axis=-1, keepdims=True)
            p_refs[h][:, 0:n0] = p0.astype(BF16)
        inv_l[h] = 1.0 / l

    def values(h):
        o = _dot(p_refs[h][:, 0:nk], v_ref[h, 0:nk, :]) * inv_l[h]
        o_ref[:, h * M_V:(h + 1) * M_V] = o.astype(BF16)

    for t in range(n_heads + 2):
        if t < n_heads:
            scores(t)
        if 0 <= t - 1 < n_heads:
            softmax(t - 1)
        if 0 <= t - 2 < n_heads:
            values(t - 2)


def _mla_kernel(q_ref, k_ref, v_ref, o_ref, *stage_refs):
    n_heads = q_ref.shape[0]
    lg_refs, p_refs = stage_refs[:n_heads], stage_refs[n_heads:]
    i = pl.program_id(2)
    per_chunk = MLA_KEY_CHUNK // MLA_TQ
    for v in range(SEQ // MLA_KEY_CHUNK):
        @pl.when(i // per_chunk == v)
        def _(v=v):
            _mla_body((v + 1) * MLA_KEY_CHUNK, i, q_ref, k_ref, v_ref, o_ref, lg_refs, p_refs)


def _mla(q, k, v):
    tq, hg = MLA_TQ, MLA_HEADS_PER_STEP
    nq = SEQ // tq
    return pl.pallas_call(
        _mla_kernel,
        grid=(BATCH, M_HEADS // hg, nq),
        in_specs=[
            pl.BlockSpec((None, hg, tq, M_QK_PAD), lambda b, g, i: (b, g, i, 0)),
            pl.BlockSpec((None, hg, SEQ, M_QK_PAD), lambda b, g, i: (b, g, 0, 0)),
            pl.BlockSpec((None, hg, SEQ, M_V), lambda b, g, i: (b, g, 0, 0)),
        ],
        out_specs=pl.BlockSpec((tq, hg * M_V), lambda b, g, i: (b * nq + i, g)),
        out_shape=jax.ShapeDtypeStruct((TOKENS, M_HEADS * M_V), BF16),
        scratch_shapes=[pltpu.VMEM((tq, SEQ), F32)] * hg + [pltpu.VMEM((tq, SEQ), BF16)] * hg,
        compiler_params=_params("arbitrary", "arbitrary", "arbitrary"),
        name="mla",
    )(q, k, v)


def _merge_kernel(oa_ref, ob_ref, ga_ref, gb_ref, wpa_ref, wpb_ref, wo_ref, x_ref, gt_ref,
                  gf_ref, scf_ref, shf_ref, o_ref, hf_ref):
    a = _dot(oa_ref[...], wpa_ref[...])
    b = _dot(ob_ref[...], wpb_ref[...])
    merged = jax.nn.sigmoid(ga_ref[...].astype(F32)) * a + jax.nn.sigmoid(gb_ref[...].astype(F32)) * b
    y = _dot(merged.astype(BF16), wo_ref[...])
    xn = x_ref[...] + gt_ref[...] * y
    o_ref[...] = xn
    ms = jnp.mean(xn * xn, axis=-1, keepdims=True)
    hn = xn * lax.rsqrt(ms + EPS) * gf_ref[...]
    hf_ref[...] = (hn * (1.0 + scf_ref[...]) + shf_ref[...]).astype(BF16)


def _merge(l, oa, ob, proj, wpa, wpb, wo, x, mod, g_ffn):
    tm = 256
    per_b = SEQ // tm

    def resident(shape):
        return pl.BlockSpec((None,) + shape[1:], lambda i: (l, 0, 0), pipeline_mode=pl.Buffered(1))

    return pl.pallas_call(
        _merge_kernel,
        grid=(TOKENS // tm,),
        in_specs=[
            pl.BlockSpec((tm, A_HEADS * A_HEAD_DIM), lambda i: (i, 0)),
            pl.BlockSpec((tm, M_HEADS * M_V), lambda i: (i, 0)),
            pl.BlockSpec((tm, D_MODEL), lambda i: (i, COL_GATE_A // D_MODEL)),
            pl.BlockSpec((tm, D_MODEL), lambda i: (i, COL_GATE_B // D_MODEL)),
            resident(wpa.shape), resident(wpb.shape), resident(wo.shape),
            pl.BlockSpec((tm, D_MODEL), lambda i: (i, 0)),
            _mod_spec(l, 2, per_b),
            _layer_row_spec(l, D_MODEL),
            _mod_spec(l, 4, per_b),
            _mod_spec(l, 3, per_b),
        ],
        out_specs=[pl.BlockSpec((tm, D_MODEL), lambda i: (i, 0))] * 2,
        out_shape=[jax.ShapeDtypeStruct((TOKENS, D_MODEL), F32),
                   jax.ShapeDtypeStruct((TOKENS, D_MODEL), BF16)],
        compiler_params=_params("arbitrary"),
        name="merge",
    )(oa, ob, proj, proj, wpa, wpb, wo, x, mod, g_ffn, mod, mod)


HALO = 8


def _ffn_up_kernel(h_ref, wg_ref, wv_ref, cg_ref, cv_ref, bg_ref, bv_ref,
                   o_ref, wgb_ref, wvb_ref, buf_ref, *, tiles_per_seq):
    i = pl.program_id(1)
    tm, tn = o_ref.shape

    @pl.when(i == 0)
    def _():
        wgb_ref[...] = wg_ref[...].astype(BF16)
        wvb_ref[...] = wv_ref[...].astype(BF16)

    seq_start = (i % tiles_per_seq) == 0

    @pl.when(seq_start)
    def _():
        buf_ref[0:HALO, :] = jnp.zeros((HALO, 2 * tn), F32)

    @pl.when(jnp.logical_not(seq_start))
    def _():
        buf_ref[0:HALO, :] = buf_ref[tm:tm + HALO, :]

    h = h_ref[...]
    buf_ref[HALO:HALO + tm, 0:tn] = _dot(h, wgb_ref[...])
    buf_ref[HALO:HALO + tm, tn:2 * tn] = _dot(h, wvb_ref[...])

    def conv(lo, c_ref, b_ref):
        u0 = buf_ref[HALO:HALO + tm, lo:lo + tn]
        u1 = buf_ref[HALO - 1:HALO - 1 + tm, lo:lo + tn]
        u2 = buf_ref[HALO - 2:HALO - 2 + tm, lo:lo + tn]
        return b_ref[...] + c_ref[0:1, :] * u2 + c_ref[1:2, :] * u1 + c_ref[2:3, :] * u0

    gate = conv(0, cg_ref, bg_ref)
    val = conv(tn, cv_ref, bv_ref)
    o_ref[...] = (gate * jax.nn.sigmoid(gate) * val).astype(BF16)


def _ffn_up(l, h, w_up, w_conv, b_conv):
    tm, tn = 1024, 512
    nj = D_FF // tn
    kern = functools.partial(_ffn_up_kernel, tiles_per_seq=SEQ // tm)
    return pl.pallas_call(
        kern,
        grid=(nj, TOKENS // tm),
        in_specs=[
            pl.BlockSpec((tm, D_MODEL), lambda j, i: (i, 0)),
            pl.BlockSpec((None, D_MODEL, tn), lambda j, i: (l, 0, j)),
            pl.BlockSpec((None, D_MODEL, tn), lambda j, i: (l, 0, nj + j)),
            pl.BlockSpec((None, CONV_W, tn), lambda j, i: (l, 0, j)),
            pl.BlockSpec((None, CONV_W, tn), lambda j, i: (l, 0, nj + j)),
            pl.BlockSpec((None, 1, tn), lambda j, i: (l, 0, j)),
            pl.BlockSpec((None, 1, tn), lambda j, i: (l, 0, nj + j)),
        ],
        out_specs=pl.BlockSpec((tm, tn), lambda j, i: (i, j)),
        out_shape=jax.ShapeDtypeStruct((TOKENS, D_FF), BF16),
        scratch_shapes=[
            pltpu.VMEM((D_MODEL, tn), BF16),
            pltpu.VMEM((D_MODEL, tn), BF16),
            pltpu.VMEM((HALO + tm, 2 * tn), F32),
        ],
        compiler_params=_params("arbitrary", "arbitrary"),
        name="ffn_up",
    )(h, w_up, w_up, w_conv, w_conv, b_conv, b_conv)


W_RING = 3


def _ffn_down_kernel(a_ref, w_hbm, x_ref, gt_ref, o_ref, wbuf, sem, *, layer):
    n_j = pl.num_programs(1)
    n_steps = pl.num_programs(0) * n_j
    s = pl.program_id(0) * n_j + pl.program_id(1)
    tn = o_ref.shape[1]

    def tile_copy(step):
        col = pl.multiple_of((step % n_j) * tn, tn)
        slot = step % W_RING
        return pltpu.make_async_copy(w_hbm.at[layer, :, pl.ds(col, tn)], wbuf.at[slot], sem.at[slot])

    @pl.when(s == 0)
    def _():
        tile_copy(0).start()
        tile_copy(1).start()

    @pl.when(s + 2 < n_steps)
    def _():
        tile_copy(s + 2).start()

    tile_copy(s).wait()
    o_ref[...] = x_ref[...] + gt_ref[...] * _dot(a_ref[...], wbuf[s % W_RING])


def _ffn_down(l, act, w_down, x, mod):
    tm, tn = 1024, 512
    per_b = SEQ // tm
    return pl.pallas_call(
        functools.partial(_ffn_down_kernel, layer=l),
        grid=(TOKENS // tm, D_MODEL // tn),
        in_specs=[
            pl.BlockSpec((tm, D_FF), lambda i, j: (i, 0)),
            pl.BlockSpec(memory_space=pl.ANY),
            pl.BlockSpec((tm, tn), lambda i, j: (i, j)),
            _mod_spec(l, 5, per_b, tn),
        ],
        out_specs=pl.BlockSpec((tm, tn), lambda i, j: (i, j)),
        out_shape=jax.ShapeDtypeStruct((TOKENS, D_MODEL), F32),
        scratch_shapes=[pltpu.VMEM((W_RING, D_FF, tn), BF16), pltpu.SemaphoreType.DMA((W_RING,))],
        compiler_params=_params("arbitrary", "arbitrary"),
        name="ffn_down",
    )(act, w_down, x, mod)


def _pack_w_in(w_in):
    o = np.cumsum((0,) + (1024, 256, 256, 1024, 64, 16, 512, 256, 64, 2048, 2048))
    w_t = jnp.swapaxes(w_in, 1, 2)
    qa, ka, va, qi, ki, wi, mql, mkvl, mkr, ga, gb = (
        w_t[:, int(o[k]):int(o[k + 1])].astype(BF16) for k in range(11))
    z = lambda n: jnp.zeros((DEPTH, n, D_MODEL), BF16)
    packed = jnp.concatenate(
        [ga, gb, qa, qi, mql, ka, va, mkvl, ki, wi, z(48), mkr, z(64)], axis=1)
    assert packed.shape[1] == IN_PACKED
    return packed


def _pack_w_mq(w_mq_up):
    w = w_mq_up.reshape(DEPTH, M_Q_LORA, M_HEADS, M_QK)
    nope = w[..., :M_NOPE].reshape(DEPTH, M_Q_LORA, M_HEADS * M_NOPE)
    rope = w[..., M_NOPE:].reshape(DEPTH, M_Q_LORA, M_HEADS * M_ROPE)
    return jnp.concatenate([nope, rope], axis=-1).astype(BF16)


def _rope_consts():
    def inv(rot):
        return ROPE_THETA ** (-jnp.arange(0, rot, 2, dtype=F32) / rot)

    layouts = ((A_HEAD_DIM // ROT_FRACTION, LANES), (IDX_DIM // ROT_FRACTION, IDX_DIM), (M_ROPE, M_ROPE))
    freqs = [inv(rot) for rot, _ in layouts]
    used = sum(f.shape[0] for f in freqs)
    inv_c = jnp.concatenate(freqs + [jnp.zeros((LANES - used,), F32)]).reshape(1, LANES)
    zero_lane = LANES - 1

    sgn3 = np.zeros((N_ROPE_TABLES, 1, LANES), np.float32)
    expand = np.zeros((N_ROPE_TABLES, LANES, LANES), np.float32)
    off = 0
    for t, (rot, group) in enumerate(layouts):
        half = rot // 2
        for j in range(LANES):
            g = j % group
            if g < rot:
                expand[t, off + g % half, j] = 1.0
                sgn3[t, 0, j] = -1.0 if g < half else 1.0
            else:
                expand[t, zero_lane, j] = 1.0
        off += half
    expand2 = np.concatenate([expand, expand], axis=1)
    return inv_c, jnp.asarray(sgn3), jnp.asarray(expand2, dtype=BF16)


def kernel(x, c, positions, g_attn, g_ffn, w_ada, b_ada, w_in, g_qa, g_ka, g_mq_lat, w_mq_up,
           g_mkv_lat, w_mkv_up, g_qm, g_km, w_pa, w_pb, w_o, w_up, w_conv, b_conv, w_down):
    pos_col = positions.astype(F32).reshape(TOKENS, 1)
    inv_c, sgn3, expand = _rope_consts()
    tabs_cos, tabs_sin = _rope_tables(pos_col, inv_c, sgn3, expand)
    rot_a, rot_i = A_HEAD_DIM // ROT_FRACTION, IDX_DIM // ROT_FRACTION
    swaps = jnp.asarray(np.stack([_swap_matrix(rot_a // 2, LANES), _swap_matrix(rot_i // 2, IDX_DIM),
                                  _swap_matrix(M_ROPE // 2, M_ROPE)]), dtype=BF16)

    c8 = jnp.pad(c, ((0, 8 - BATCH), (0, 0)))
    mod = _ada(c8, w_ada, b_ada)[:, :BATCH].reshape(DEPTH, BATCH, N_ADA, 1, D_MODEL)

    w_in_p = _pack_w_in(w_in)
    w_mq_p = _pack_w_mq(w_mq_up)
    w_mkv_b = w_mkv_up.astype(BF16)
    w_pa_b, w_pb_b, w_o_b = w_pa.astype(BF16), w_pb.astype(BF16), w_o.astype(BF16)
    w_down_b = w_down.astype(BF16)

    rows = lambda v: v.reshape(DEPTH, 1, -1)
    rope_gain = lambda v: rows(jnp.tile(v[:, M_NOPE:], (1, 2)))
    g_attn_r, g_ffn_r, b_conv_r = rows(g_attn), rows(g_ffn), rows(b_conv)
    g_qa_r, g_ka_r, g_mq_r, g_mkv_r = rows(g_qa), rows(g_ka), rows(g_mq_lat), rows(g_mkv_lat)
    g_qm_s = g_qm * (M_QK ** -0.5)
    gqn, gqr = rows(g_qm_s[:, :M_NOPE]), rope_gain(g_qm_s)
    gkn, gkr = rows(g_km[:, :M_NOPE]), rope_gain(g_km)

    xf = x.reshape(TOKENS, D_MODEL)
    for l in range(DEPTH):
        proj = _in_proj(l, xf, g_attn_r, mod, w_in_p)
        qa, ka, va, qi, ki2, wi, mqn, mkvn = _prep(
            l, proj, tabs_cos, tabs_sin, swaps, g_qa_r, g_ka_r, g_mq_r, g_mkv_r)
        mq, mk, mv = _mla_up(l, mqn, mkvn, proj, w_mq_p, w_mkv_b, tabs_cos, tabs_sin, swaps,
                             gqn, gqr, gkn, gkr)
        o_a = _dsa(qa, ka, va, qi, ki2, wi)
        o_b = _mla(mq, mk, mv)
        xf, h_ffn = _merge(l, o_a, o_b, proj, w_pa_b, w_pb_b, w_o_b, xf, mod, g_ffn_r)
        act = _ffn_up(l, h_ffn, w_up, w_conv, b_conv_r)
        xf = _ffn_down(l, act, w_down_b, xf, mod)
    return xf.reshape(BATCH, SEQ, D_MODEL)
```
